```python
import jax, jax.numpy as jnp
from jax import lax
import numpy as np

D_MODEL = 2048
BATCH = 8
SEQ = 4096
DEPTH = 4

CHUNK = 64
PLE_DIM = 256
EPS = 1e-6
MLA_HEADS = 16
Q_LORA = 512
KV_LORA = 512
QK_NOPE = 128
QK_ROPE = 64
V_DIM = 128
QK_DIM = QK_NOPE + QK_ROPE
ROPE_THETA = 10000.0
Q_BLOCK = 128
SSM_EXPAND = 2
D_INNER = SSM_EXPAND * D_MODEL
SSM_HEADDIM = 64
SSM_HEADS = D_INNER // SSM_HEADDIM
SSM_GROUPS = 8
SSM_STATE = 128
HEADS_PER_GROUP = SSM_HEADS // SSM_GROUPS
CONV_WIDTH = 4
CONV_DIM = D_INNER + 2 * SSM_GROUPS * SSM_STATE
SSD_CHUNK = CHUNK
D_FF = 4 * D_MODEL
N_BRANCH = 2
SPLIT_POINTS = (
    Q_LORA,
    Q_LORA + KV_LORA,
    Q_LORA + KV_LORA + QK_ROPE,
    Q_LORA + KV_LORA + QK_ROPE + D_INNER,
    Q_LORA + KV_LORA + QK_ROPE + D_INNER + CONV_DIM,
    Q_LORA + KV_LORA + QK_ROPE + D_INNER + CONV_DIM + SSM_HEADS,
)
D_IN_PROJ = Q_LORA + KV_LORA + QK_ROPE + D_INNER + CONV_DIM + SSM_HEADS + N_BRANCH * D_MODEL

kernel_name = 'hybrid_mla_ssd_parallel_gated_trunk'


def rms_norm(x, w):
    xf = x.astype(jnp.float32)
    y = xf * lax.rsqrt(jnp.mean(xf * xf, axis=-1, keepdims=True) + EPS)
    return (y * w.astype(jnp.float32)).astype(x.dtype)


def rope_tables(positions):
    inv_freq = 1.0 / (ROPE_THETA ** (jnp.arange(0, QK_ROPE, 2, dtype=jnp.float32) / QK_ROPE))
    ang = positions.astype(jnp.float32)[..., None] * inv_freq
    return jnp.cos(ang), jnp.sin(ang)


def apply_rope(x, cos, sin):
    xf = x.astype(jnp.float32)
    x1, x2 = jnp.split(xf, 2, axis=-1)
    c = cos[:, :, None, :]
    s = sin[:, :, None, :]
    return jnp.concatenate([x1 * c - x2 * s, x1 * s + x2 * c], axis=-1).astype(x.dtype)


def chunk_causal_attention(q, k, v):
    B, S, H, Dh = q.shape
    n_blk = S // Q_BLOCK
    scale = Dh ** -0.5
    qb = q.reshape(B, n_blk, Q_BLOCK, H, Dh).transpose(1, 0, 3, 2, 4)
    kh = k.transpose(0, 2, 1, 3)
    vh = v.transpose(0, 2, 1, 3)
    key_chunk = jnp.arange(S) // CHUNK

    def one_block(args):
        q_blk, blk = args
        s = jnp.einsum('bhqd,bhkd->bhqk', q_blk, kh, preferred_element_type=jnp.float32) * scale
        q_chunk = (blk * Q_BLOCK + jnp.arange(Q_BLOCK)) // CHUNK
        mask = key_chunk[None, :] <= q_chunk[:, None]
        s = jnp.where(mask, s, -jnp.inf)
        pr = jax.nn.softmax(s, axis=-1)
        return jnp.einsum('bhqk,bhkd->bhqd', pr.astype(vh.dtype), vh)

    o = lax.map(one_block, (qb, jnp.arange(n_blk)))
    return o.transpose(1, 0, 3, 2, 4).reshape(B, S, H, V_DIM)


def mla_branch(c_q, c_kv, k_r, cos, sin, q_a_norm_w, w_uq, kv_a_norm_w, w_ukv, q_norm_w, k_norm_w):
    B, S, _ = c_q.shape
    q = (rms_norm(c_q, q_a_norm_w) @ w_uq).reshape(B, S, MLA_HEADS, QK_DIM)
    kv = (rms_norm(c_kv, kv_a_norm_w) @ w_ukv).reshape(B, S, MLA_HEADS, QK_NOPE + V_DIM)
    k_nope, v = kv[..., :QK_NOPE], kv[..., QK_NOPE:]
    k_rope = jnp.broadcast_to(k_r[:, :, None, :], (B, S, MLA_HEADS, QK_ROPE))
    k = jnp.concatenate([k_nope, k_rope], axis=-1)
    q = rms_norm(q, q_norm_w)
    k = rms_norm(k, k_norm_w)
    q = jnp.concatenate([q[..., :QK_NOPE], apply_rope(q[..., QK_NOPE:], cos, sin)], axis=-1)
    k = jnp.concatenate([k[..., :QK_NOPE], apply_rope(k[..., QK_NOPE:], cos, sin)], axis=-1)
    o = chunk_causal_attention(q, k, v)
    return o.reshape(B, S, MLA_HEADS * V_DIM)


def causal_depthwise_conv(x, w, b):
    S = x.shape[1]
    xp = jnp.pad(x, ((0, 0), (CONV_WIDTH - 1, 0), (0, 0)))
    y = xp[:, 0:S] * w[0]
    for tap in range(1, CONV_WIDTH):
        y = y + xp[:, tap:tap + S] * w[tap]
    return y + b


def ssd_branch(xbc_raw, z, dt_raw, conv_w, conv_b, dt_bias, a_log, d_skip, ssm_norm_w):
    B, S, _ = z.shape
    G, R, P, N, T = SSM_GROUPS, HEADS_PER_GROUP, SSM_HEADDIM, SSM_STATE, SSD_CHUNK
    nc = S // T
    xbc = jax.nn.silu(causal_depthwise_conv(xbc_raw, conv_w, conv_b)).astype(jnp.float32)
    x5 = xbc[..., :D_INNER].reshape(B, nc, T, G, R, P)
    Bm = xbc[..., D_INNER:D_INNER + G * N].reshape(B, nc, T, G, N)
    Cm = xbc[..., D_INNER + G * N:].reshape(B, nc, T, G, N)
    dt = jax.nn.softplus(dt_raw.astype(jnp.float32) + dt_bias.astype(jnp.float32))
    A = -jnp.exp(a_log.astype(jnp.float32))
    dt5 = dt.reshape(B, nc, T, G, R)
    a_cum = jnp.cumsum((dt * A).reshape(B, nc, T, G, R), axis=2)
    seg = a_cum[:, :, :, None] - a_cum[:, :, None]
    tri = jnp.tril(jnp.ones((T, T), dtype=bool))
    decay = jnp.exp(jnp.where(tri[:, :, None, None], seg, -jnp.inf))
    cb = jnp.einsum('bctgn,bcsgn->bctsg', Cm, Bm)
    m = cb[..., None] * decay * dt5[:, :, None]
    y_diag = jnp.einsum('bctsgr,bcsgrp->bctgrp', m, x5)
    decay_to_end = jnp.exp(a_cum[:, :, -1:] - a_cum)
    states = jnp.einsum('bcsgn,bcsgr,bcsgrp->bcgrpn', Bm, decay_to_end * dt5, x5)
    chunk_decay = jnp.exp(a_cum[:, :, -1])

    def step(h, inp):
        st, dec = inp
        return h * dec[..., None, None] + st, h

    h0 = jnp.zeros((B, G, R, P, N), jnp.float32)
    _, prev = lax.scan(step, h0, (jnp.moveaxis(states, 1, 0), jnp.moveaxis(chunk_decay, 1, 0)))
    prev = jnp.moveaxis(prev, 0, 1)
    y_off = jnp.einsum('bctgn,bcgrpn,bctgr->bctgrp', Cm, prev, jnp.exp(a_cum))
    y = y_diag + y_off + d_skip.astype(jnp.float32).reshape(G, R)[..., None] * x5
    y = y.reshape(B, S, D_INNER) * jax.nn.silu(z.astype(jnp.float32))
    yg = y.reshape(B, S, G, D_INNER // G)
    yg = yg * lax.rsqrt(jnp.mean(yg * yg, axis=-1, keepdims=True) + EPS)
    y = yg.reshape(B, S, D_INNER) * ssm_norm_w.astype(jnp.float32)
    return y.astype(z.dtype)


def _fwd_setup_inputs(seed: int = 0) -> dict:
    key = jax.random.key(seed)
    ks = jax.random.split(key, 32)
    f32 = jnp.float32

    def nrm(k, shape, fan_in):
        return jax.random.normal(k, shape, f32) * (fan_in ** -0.5)

    def gain(k, dim):
        return 1.0 + 0.02 * jax.random.normal(k, (DEPTH, dim), f32)

    x = jax.random.normal(ks[0], (BATCH, SEQ, D_MODEL), f32)
    p = jax.random.normal(ks[1], (DEPTH, BATCH, SEQ, PLE_DIM), f32)
    start = jax.random.randint(ks[2], (BATCH, 1), 0, 16384, dtype=jnp.int32)
    positions = (start + jnp.arange(SEQ, dtype=jnp.int32)[None, :]).astype(jnp.int32)
    dt_init = jnp.exp(jax.random.uniform(ks[15], (DEPTH, SSM_HEADS), f32, np.log(1e-3), np.log(1e-1)))
    dt_bias = dt_init + jnp.log(-jnp.expm1(-dt_init))
    a_log = jnp.log(jax.random.uniform(ks[16], (DEPTH, SSM_HEADS), f32, 1.0, 16.0))
    return {
        'x': x,
        'p': p,
        'positions': positions,
        'norm_mix_w': gain(ks[3], D_MODEL),
        'w_in': nrm(ks[4], (DEPTH, D_MODEL, D_IN_PROJ), D_MODEL),
        'q_a_norm_w': gain(ks[5], Q_LORA),
        'w_uq': nrm(ks[6], (DEPTH, Q_LORA, MLA_HEADS * QK_DIM), Q_LORA),
        'kv_a_norm_w': gain(ks[7], KV_LORA),
        'w_ukv': nrm(ks[8], (DEPTH, KV_LORA, MLA_HEADS * (QK_NOPE + V_DIM)), KV_LORA),
        'q_norm_w': gain(ks[9], QK_DIM),
        'k_norm_w': gain(ks[10], QK_DIM),
        'w_o_mla': nrm(ks[11], (DEPTH, MLA_HEADS * V_DIM, D_MODEL), MLA_HEADS * V_DIM),
        'conv_w': nrm(ks[12], (DEPTH, CONV_WIDTH, CONV_DIM), CONV_WIDTH),
        'conv_b': 0.01 * jax.random.normal(ks[13], (DEPTH, CONV_DIM), f32),
        'dt_bias': dt_bias,
        'a_log': a_log,
        'd_skip': 1.0 + 0.1 * jax.random.normal(ks[17], (DEPTH, SSM_HEADS), f32),
        'ssm_norm_w': gain(ks[18], D_INNER),
        'w_o_ssm': nrm(ks[19], (DEPTH, D_INNER, D_MODEL), D_INNER),
        'w_out': nrm(ks[20], (DEPTH, D_MODEL, D_MODEL), D_MODEL),
        'norm_mlp_w': gain(ks[21], D_MODEL),
        'w_up': nrm(ks[22], (DEPTH, D_MODEL, D_FF), D_MODEL),
        'w_down': nrm(ks[23], (DEPTH, D_FF, D_MODEL), D_FF),
        'ple_norm_w': gain(ks[24], D_MODEL),
        'w_ple_gate': nrm(ks[25], (DEPTH, D_MODEL, D_MODEL), D_MODEL),
        'w_ple': nrm(ks[26], (DEPTH, PLE_DIM, D_MODEL), PLE_DIM),
    }


def _fwd_reference(x, p, positions, norm_mix_w, w_in, q_a_norm_w, w_uq, kv_a_norm_w, w_ukv,
              q_norm_w, k_norm_w, w_o_mla, conv_w, conv_b, dt_bias, a_log, d_skip,
              ssm_norm_w, w_o_ssm, w_out, norm_mlp_w, w_up, w_down, ple_norm_w,
              w_ple_gate, w_ple):
    B, S, _ = x.shape
    cos, sin = rope_tables(positions)
    for i in range(DEPTH):
        h = rms_norm(x, norm_mix_w[i])
        proj = h @ w_in[i]
        c_q, c_kv, k_r, z, xbc, dt_raw, gate_logits = jnp.split(proj, SPLIT_POINTS, axis=-1)
        y_a = mla_branch(c_q, c_kv, k_r, cos, sin, q_a_norm_w[i], w_uq[i], kv_a_norm_w[i],
                         w_ukv[i], q_norm_w[i], k_norm_w[i]) @ w_o_mla[i]
        y_b = ssd_branch(xbc, z, dt_raw, conv_w[i], conv_b[i], dt_bias[i], a_log[i],
                         d_skip[i], ssm_norm_w[i]) @ w_o_ssm[i]
        g = jax.nn.sigmoid(gate_logits.astype(jnp.float32)).reshape(B, S, N_BRANCH, D_MODEL)
        merged = (g[:, :, 0] * y_a + g[:, :, 1] * y_b).astype(x.dtype)
        x = x + (merged @ w_out[i]).astype(x.dtype)
        h2 = rms_norm(x, norm_mlp_w[i])
        x = x + (jnp.square(jax.nn.relu(h2 @ w_up[i])) @ w_down[i]).astype(x.dtype)
        ple_gate = jax.nn.sigmoid((rms_norm(x, ple_norm_w[i]) @ w_ple_gate[i]).astype(jnp.float32))
        x = x + ((p[i] @ w_ple[i]) * ple_gate).astype(x.dtype)
    return x


import jax as _jax
import jax.numpy as _jnp

TWIN_FORMAT = 'train_step'
FWD_PARAMS = ['x', 'p', 'positions', 'norm_mix_w', 'w_in', 'q_a_norm_w', 'w_uq', 'kv_a_norm_w', 'w_ukv', 'q_norm_w', 'k_norm_w', 'w_o_mla', 'conv_w', 'conv_b', 'dt_bias', 'a_log', 'd_skip', 'ssm_norm_w', 'w_o_ssm', 'w_out', 'norm_mlp_w', 'w_up', 'w_down', 'ple_norm_w', 'w_ple_gate', 'w_ple']
TWIN_WEIGHTS = ['norm_mix_w', 'w_in', 'q_a_norm_w', 'w_uq', 'kv_a_norm_w', 'w_ukv', 'q_norm_w', 'k_norm_w', 'w_o_mla', 'conv_w', 'conv_b', 'dt_bias', 'a_log', 'd_skip', 'ssm_norm_w', 'w_o_ssm', 'w_out', 'norm_mlp_w', 'w_up', 'w_down', 'ple_norm_w', 'w_ple_gate', 'w_ple']
TWIN_DIFF_INPUT = 'x'
TWIN_INPUTS = ['x', 'p', 'positions', 'norm_mix_w', 'w_in', 'q_a_norm_w', 'w_uq', 'kv_a_norm_w', 'w_ukv', 'q_norm_w', 'k_norm_w', 'w_o_mla', 'conv_w', 'conv_b', 'dt_bias', 'a_log', 'd_skip', 'ssm_norm_w', 'w_o_ssm', 'w_out', 'norm_mlp_w', 'w_up', 'w_down', 'ple_norm_w', 'w_ple_gate', 'w_ple', 'loss_target', 'm_norm_mix_w', 'm_w_in', 'm_q_a_norm_w', 'm_w_uq', 'm_kv_a_norm_w', 'm_w_ukv', 'm_q_norm_w', 'm_k_norm_w', 'm_w_o_mla', 'm_conv_w', 'm_conv_b', 'm_dt_bias', 'm_a_log', 'm_d_skip', 'm_ssm_norm_w', 'm_w_o_ssm', 'm_w_out', 'm_norm_mlp_w', 'm_w_up', 'm_w_down', 'm_ple_norm_w', 'm_w_ple_gate', 'm_w_ple', 'v_norm_mix_w', 'v_w_in', 'v_q_a_norm_w', 'v_w_uq', 'v_kv_a_norm_w', 'v_w_ukv', 'v_q_norm_w', 'v_k_norm_w', 'v_w_o_mla', 'v_conv_w', 'v_conv_b', 'v_dt_bias', 'v_a_log', 'v_d_skip', 'v_ssm_norm_w', 'v_w_o_ssm', 'v_w_out', 'v_norm_mlp_w', 'v_w_up', 'v_w_down', 'v_ple_norm_w', 'v_w_ple_gate', 'v_w_ple']
TWIN_OUTPUTS = ['loss', 'grad_x', 'grad_norm_mix_w', 'grad_w_in', 'grad_q_a_norm_w', 'grad_w_uq', 'grad_kv_a_norm_w', 'grad_w_ukv', 'grad_q_norm_w', 'grad_k_norm_w', 'grad_w_o_mla', 'grad_conv_w', 'grad_conv_b', 'grad_dt_bias', 'grad_a_log', 'grad_d_skip', 'grad_ssm_norm_w', 'grad_w_o_ssm', 'grad_w_out', 'grad_norm_mlp_w', 'grad_w_up', 'grad_w_down', 'grad_ple_norm_w', 'grad_w_ple_gate', 'grad_w_ple', 'delta_norm_mix_w', 'delta_w_in', 'delta_q_a_norm_w', 'delta_w_uq', 'delta_kv_a_norm_w', 'delta_w_ukv', 'delta_q_norm_w', 'delta_k_norm_w', 'delta_w_o_mla', 'delta_conv_w', 'delta_conv_b', 'delta_dt_bias', 'delta_a_log', 'delta_d_skip', 'delta_ssm_norm_w', 'delta_w_o_ssm', 'delta_w_out', 'delta_norm_mlp_w', 'delta_w_up', 'delta_w_down', 'delta_ple_norm_w', 'delta_w_ple_gate', 'delta_w_ple', 'new_m_norm_mix_w', 'new_m_w_in', 'new_m_q_a_norm_w', 'new_m_w_uq', 'new_m_kv_a_norm_w', 'new_m_w_ukv', 'new_m_q_norm_w', 'new_m_k_norm_w', 'new_m_w_o_mla', 'new_m_conv_w', 'new_m_conv_b', 'new_m_dt_bias', 'new_m_a_log', 'new_m_d_skip', 'new_m_ssm_norm_w', 'new_m_w_o_ssm', 'new_m_w_out', 'new_m_norm_mlp_w', 'new_m_w_up', 'new_m_w_down', 'new_m_ple_norm_w', 'new_m_w_ple_gate', 'new_m_w_ple', 'new_v_norm_mix_w', 'new_v_w_in', 'new_v_q_a_norm_w', 'new_v_w_uq', 'new_v_kv_a_norm_w', 'new_v_w_ukv', 'new_v_q_norm_w', 'new_v_k_norm_w', 'new_v_w_o_mla', 'new_v_conv_w', 'new_v_conv_b', 'new_v_dt_bias', 'new_v_a_log', 'new_v_d_skip', 'new_v_ssm_norm_w', 'new_v_w_o_ssm', 'new_v_w_out', 'new_v_norm_mlp_w', 'new_v_w_up', 'new_v_w_down', 'new_v_ple_norm_w', 'new_v_w_ple_gate', 'new_v_w_ple']
TWIN_LEAF_KINDS = {'loss': 'loss', 'grad_x': 'grad_x', 'grad_norm_mix_w': 'grad_w', 'grad_w_in': 'grad_w', 'grad_q_a_norm_w': 'grad_w', 'grad_w_uq': 'grad_w', 'grad_kv_a_norm_w': 'grad_w', 'grad_w_ukv': 'grad_w', 'grad_q_norm_w': 'grad_w', 'grad_k_norm_w': 'grad_w', 'grad_w_o_mla': 'grad_w', 'grad_conv_w': 'grad_w', 'grad_conv_b': 'grad_w', 'grad_dt_bias': 'grad_w', 'grad_a_log': 'grad_w', 'grad_d_skip': 'grad_w', 'grad_ssm_norm_w': 'grad_w', 'grad_w_o_ssm': 'grad_w', 'grad_w_out': 'grad_w', 'grad_norm_mlp_w': 'grad_w', 'grad_w_up': 'grad_w', 'grad_w_down': 'grad_w', 'grad_ple_norm_w': 'grad_w', 'grad_w_ple_gate': 'grad_w', 'grad_w_ple': 'grad_w', 'delta_norm_mix_w': 'delta_w', 'delta_w_in': 'delta_w', 'delta_q_a_norm_w': 'delta_w', 'delta_w_uq': 'delta_w', 'delta_kv_a_norm_w': 'delta_w', 'delta_w_ukv': 'delta_w', 'delta_q_norm_w': 'delta_w', 'delta_k_norm_w': 'delta_w', 'delta_w_o_mla': 'delta_w', 'delta_conv_w': 'delta_w', 'delta_conv_b': 'delta_w', 'delta_dt_bias': 'delta_w', 'delta_a_log': 'delta_w', 'delta_d_skip': 'delta_w', 'delta_ssm_norm_w': 'delta_w', 'delta_w_o_ssm': 'delta_w', 'delta_w_out': 'delta_w', 'delta_norm_mlp_w': 'delta_w', 'delta_w_up': 'delta_w', 'delta_w_down': 'delta_w', 'delta_ple_norm_w': 'delta_w', 'delta_w_ple_gate': 'delta_w', 'delta_w_ple': 'delta_w', 'new_m_norm_mix_w': 'new_m', 'new_m_w_in': 'new_m', 'new_m_q_a_norm_w': 'new_m', 'new_m_w_uq': 'new_m', 'new_m_kv_a_norm_w': 'new_m', 'new_m_w_ukv': 'new_m', 'new_m_q_norm_w': 'new_m', 'new_m_k_norm_w': 'new_m', 'new_m_w_o_mla': 'new_m', 'new_m_conv_w': 'new_m', 'new_m_conv_b': 'new_m', 'new_m_dt_bias': 'new_m', 'new_m_a_log': 'new_m', 'new_m_d_skip': 'new_m', 'new_m_ssm_norm_w': 'new_m', 'new_m_w_o_ssm': 'new_m', 'new_m_w_out': 'new_m', 'new_m_norm_mlp_w': 'new_m', 'new_m_w_up': 'new_m', 'new_m_w_down': 'new_m', 'new_m_ple_norm_w': 'new_m', 'new_m_w_ple_gate': 'new_m', 'new_m_w_ple': 'new_m', 'new_v_norm_mix_w': 'new_v', 'new_v_w_in': 'new_v', 'new_v_q_a_norm_w': 'new_v', 'new_v_w_uq': 'new_v', 'new_v_kv_a_norm_w': 'new_v', 'new_v_w_ukv': 'new_v', 'new_v_q_norm_w': 'new_v', 'new_v_k_norm_w': 'new_v', 'new_v_w_o_mla': 'new_v', 'new_v_conv_w': 'new_v', 'new_v_conv_b': 'new_v', 'new_v_dt_bias': 'new_v', 'new_v_a_log': 'new_v', 'new_v_d_skip': 'new_v', 'new_v_ssm_norm_w': 'new_v', 'new_v_w_o_ssm': 'new_v', 'new_v_w_out': 'new_v', 'new_v_norm_mlp_w': 'new_v', 'new_v_w_up': 'new_v', 'new_v_w_down': 'new_v', 'new_v_ple_norm_w': 'new_v', 'new_v_w_ple_gate': 'new_v', 'new_v_w_ple': 'new_v'}


def _forward(args):
    return _fwd_reference(*[args[k] for k in FWD_PARAMS])


def _output_shape():
    def fwd():
        inp = _fwd_setup_inputs(0)
        return _fwd_reference(*[inp[k] for k in FWD_PARAMS])
    out = _jax.eval_shape(fwd)
    return out.shape, out.dtype

N_MICROBATCH = 1
ADAM_LR = 0.001
ADAM_B1 = 0.9
ADAM_B2 = 0.999
ADAM_EPS = 1e-08
ADAM_WD = 0.01
ADAM_STEP = 10
PER_EXAMPLE_BATCH_AXIS = {'x': 0, 'p': 1, 'positions': 0, 'loss_target': 0}
SHARED_INPUTS = []
_WEIGHT_DTYPES = {'norm_mix_w': _jnp.float32, 'w_in': _jnp.float32, 'q_a_norm_w': _jnp.float32, 'w_uq': _jnp.float32, 'kv_a_norm_w': _jnp.float32, 'w_ukv': _jnp.float32, 'q_norm_w': _jnp.float32, 'k_norm_w': _jnp.float32, 'w_o_mla': _jnp.float32, 'conv_w': _jnp.float32, 'conv_b': _jnp.float32, 'dt_bias': _jnp.float32, 'a_log': _jnp.float32, 'd_skip': _jnp.float32, 'ssm_norm_w': _jnp.float32, 'w_o_ssm': _jnp.float32, 'w_out': _jnp.float32, 'norm_mlp_w': _jnp.float32, 'w_up': _jnp.float32, 'w_down': _jnp.float32, 'ple_norm_w': _jnp.float32, 'w_ple_gate': _jnp.float32, 'w_ple': _jnp.float32}
MOMENT_SCALE = {'norm_mix_w': 4.727560e+00, 'w_in': 1.696946e+00, 'q_a_norm_w': 2.057183e-01, 'w_uq': 8.587630e-02, 'kv_a_norm_w': 8.126969e+00, 'w_ukv': 2.598889e+00, 'q_norm_w': 3.827459e-01, 'k_norm_w': 3.856464e-01, 'w_o_mla': 3.595574e+00, 'conv_w': 1.761021e+00, 'conv_b': 4.375510e+00, 'dt_bias': 1.477902e+00, 'a_log': 7.785407e+00, 'd_skip': 8.309388e+00, 'ssm_norm_w': 5.272290e+00, 'w_o_ssm': 4.046005e+00, 'w_out': 5.232778e+00, 'norm_mlp_w': 4.949501e+01, 'w_up': 3.671568e+00, 'w_down': 1.383146e+01, 'ple_norm_w': 7.354161e-01, 'w_ple_gate': 5.475867e-01, 'w_ple': 3.780090e-01}


def _to_microbatches(a, axis):
    t = _jnp.moveaxis(a, axis, 0)
    t = t.reshape((N_MICROBATCH, t.shape[0] // N_MICROBATCH) + t.shape[1:])
    return _jnp.moveaxis(t, 1, axis + 1)


def setup_inputs(seed: int = 0) -> dict:
    inp = _fwd_setup_inputs(seed)
    key = _jax.random.fold_in(_jax.random.key(seed), 7919)
    shape, _ = _output_shape()
    out = dict(inp)
    out["loss_target"] = _jax.random.normal(_jax.random.fold_in(key, 0), shape, _jnp.float32)
    for i, name in enumerate(TWIN_WEIGHTS):
        w = inp[name].astype(_jnp.float32)
        if MOMENT_SCALE is None:
            s = _jnp.sqrt(_jnp.mean(_jnp.square(w)) + 1e-30)
        else:
            s = MOMENT_SCALE[name]
        km, kv = _jax.random.split(_jax.random.fold_in(key, i + 1))
        out[name] = w
        out["m_" + name] = s * _jax.random.normal(km, w.shape, _jnp.float32)
        out["v_" + name] = (s * s) * _jax.random.uniform(kv, w.shape, _jnp.float32, 0.5, 1.5)
    if N_MICROBATCH > 1:
        for name, axis in PER_EXAMPLE_BATCH_AXIS.items():
            out[name] = _to_microbatches(out[name], axis)
    return {'x': out['x'], 'p': out['p'], 'positions': out['positions'], 'norm_mix_w': out['norm_mix_w'], 'w_in': out['w_in'], 'q_a_norm_w': out['q_a_norm_w'], 'w_uq': out['w_uq'], 'kv_a_norm_w': out['kv_a_norm_w'], 'w_ukv': out['w_ukv'], 'q_norm_w': out['q_norm_w'], 'k_norm_w': out['k_norm_w'], 'w_o_mla': out['w_o_mla'], 'conv_w': out['conv_w'], 'conv_b': out['conv_b'], 'dt_bias': out['dt_bias'], 'a_log': out['a_log'], 'd_skip': out['d_skip'], 'ssm_norm_w': out['ssm_norm_w'], 'w_o_ssm': out['w_o_ssm'], 'w_out': out['w_out'], 'norm_mlp_w': out['norm_mlp_w'], 'w_up': out['w_up'], 'w_down': out['w_down'], 'ple_norm_w': out['ple_norm_w'], 'w_ple_gate': out['w_ple_gate'], 'w_ple': out['w_ple'], 'loss_target': out['loss_target'], 'm_norm_mix_w': out['m_norm_mix_w'], 'm_w_in': out['m_w_in'], 'm_q_a_norm_w': out['m_q_a_norm_w'], 'm_w_uq': out['m_w_uq'], 'm_kv_a_norm_w': out['m_kv_a_norm_w'], 'm_w_ukv': out['m_w_ukv'], 'm_q_norm_w': out['m_q_norm_w'], 'm_k_norm_w': out['m_k_norm_w'], 'm_w_o_mla': out['m_w_o_mla'], 'm_conv_w': out['m_conv_w'], 'm_conv_b': out['m_conv_b'], 'm_dt_bias': out['m_dt_bias'], 'm_a_log': out['m_a_log'], 'm_d_skip': out['m_d_skip'], 'm_ssm_norm_w': out['m_ssm_norm_w'], 'm_w_o_ssm': out['m_w_o_ssm'], 'm_w_out': out['m_w_out'], 'm_norm_mlp_w': out['m_norm_mlp_w'], 'm_w_up': out['m_w_up'], 'm_w_down': out['m_w_down'], 'm_ple_norm_w': out['m_ple_norm_w'], 'm_w_ple_gate': out['m_w_ple_gate'], 'm_w_ple': out['m_w_ple'], 'v_norm_mix_w': out['v_norm_mix_w'], 'v_w_in': out['v_w_in'], 'v_q_a_norm_w': out['v_q_a_norm_w'], 'v_w_uq': out['v_w_uq'], 'v_kv_a_norm_w': out['v_kv_a_norm_w'], 'v_w_ukv': out['v_w_ukv'], 'v_q_norm_w': out['v_q_norm_w'], 'v_k_norm_w': out['v_k_norm_w'], 'v_w_o_mla': out['v_w_o_mla'], 'v_conv_w': out['v_conv_w'], 'v_conv_b': out['v_conv_b'], 'v_dt_bias': out['v_dt_bias'], 'v_a_log': out['v_a_log'], 'v_d_skip': out['v_d_skip'], 'v_ssm_norm_w': out['v_ssm_norm_w'], 'v_w_o_ssm': out['v_w_o_ssm'], 'v_w_out': out['v_w_out'], 'v_norm_mlp_w': out['v_norm_mlp_w'], 'v_w_up': out['v_w_up'], 'v_w_down': out['v_w_down'], 'v_ple_norm_w': out['v_ple_norm_w'], 'v_w_ple_gate': out['v_w_ple_gate'], 'v_w_ple': out['v_w_ple']}


def _loss(weights, diff, rest, loss_target):
    with _jax.named_scope("forward"):
        args = {**rest, TWIN_DIFF_INPUT: diff, **{k: w.astype(_WEIGHT_DTYPES[k]) for k, w in weights.items()}}
        y = _forward(args)
    with _jax.named_scope("loss_head"):
        err = _jnp.square(y.astype(_jnp.float32) - loss_target)
        return 0.5 * _jnp.sum(_jnp.mean(err, axis=-1)) if err.ndim else 0.5 * err


def _adamw(w, g, m, v):
    m = ADAM_B1 * m + (1.0 - ADAM_B1) * g
    v = ADAM_B2 * v + (1.0 - ADAM_B2) * _jnp.square(g)
    m_hat = m / (1.0 - ADAM_B1 ** ADAM_STEP)
    v_hat = v / (1.0 - ADAM_B2 ** ADAM_STEP)
    delta = -ADAM_LR * (m_hat / (_jnp.sqrt(v_hat) + ADAM_EPS) + ADAM_WD * w)
    return delta, m, v


def reference(x, p, positions, norm_mix_w, w_in, q_a_norm_w, w_uq, kv_a_norm_w, w_ukv, q_norm_w, k_norm_w, w_o_mla, conv_w, conv_b, dt_bias, a_log, d_skip, ssm_norm_w, w_o_ssm, w_out, norm_mlp_w, w_up, w_down, ple_norm_w, w_ple_gate, w_ple, loss_target, m_norm_mix_w, m_w_in, m_q_a_norm_w, m_w_uq, m_kv_a_norm_w, m_w_ukv, m_q_norm_w, m_k_norm_w, m_w_o_mla, m_conv_w, m_conv_b, m_dt_bias, m_a_log, m_d_skip, m_ssm_norm_w, m_w_o_ssm, m_w_out, m_norm_mlp_w, m_w_up, m_w_down, m_ple_norm_w, m_w_ple_gate, m_w_ple, v_norm_mix_w, v_w_in, v_q_a_norm_w, v_w_uq, v_kv_a_norm_w, v_w_ukv, v_q_norm_w, v_k_norm_w, v_w_o_mla, v_conv_w, v_conv_b, v_dt_bias, v_a_log, v_d_skip, v_ssm_norm_w, v_w_o_ssm, v_w_out, v_norm_mlp_w, v_w_up, v_w_down, v_ple_norm_w, v_w_ple_gate, v_w_ple):
    given = dict(x=x, p=p, positions=positions, norm_mix_w=norm_mix_w, w_in=w_in, q_a_norm_w=q_a_norm_w, w_uq=w_uq, kv_a_norm_w=kv_a_norm_w, w_ukv=w_ukv, q_norm_w=q_norm_w, k_norm_w=k_norm_w, w_o_mla=w_o_mla, conv_w=conv_w, conv_b=conv_b, dt_bias=dt_bias, a_log=a_log, d_skip=d_skip, ssm_norm_w=ssm_norm_w, w_o_ssm=w_o_ssm, w_out=w_out, norm_mlp_w=norm_mlp_w, w_up=w_up, w_down=w_down, ple_norm_w=ple_norm_w, w_ple_gate=w_ple_gate, w_ple=w_ple, loss_target=loss_target, m_norm_mix_w=m_norm_mix_w, m_w_in=m_w_in, m_q_a_norm_w=m_q_a_norm_w, m_w_uq=m_w_uq, m_kv_a_norm_w=m_kv_a_norm_w, m_w_ukv=m_w_ukv, m_q_norm_w=m_q_norm_w, m_k_norm_w=m_k_norm_w, m_w_o_mla=m_w_o_mla, m_conv_w=m_conv_w, m_conv_b=m_conv_b, m_dt_bias=m_dt_bias, m_a_log=m_a_log, m_d_skip=m_d_skip, m_ssm_norm_w=m_ssm_norm_w, m_w_o_ssm=m_w_o_ssm, m_w_out=m_w_out, m_norm_mlp_w=m_norm_mlp_w, m_w_up=m_w_up, m_w_down=m_w_down, m_ple_norm_w=m_ple_norm_w, m_w_ple_gate=m_w_ple_gate, m_w_ple=m_w_ple, v_norm_mix_w=v_norm_mix_w, v_w_in=v_w_in, v_q_a_norm_w=v_q_a_norm_w, v_w_uq=v_w_uq, v_kv_a_norm_w=v_kv_a_norm_w, v_w_ukv=v_w_ukv, v_q_norm_w=v_q_norm_w, v_k_norm_w=v_k_norm_w, v_w_o_mla=v_w_o_mla, v_conv_w=v_conv_w, v_conv_b=v_conv_b, v_dt_bias=v_dt_bias, v_a_log=v_a_log, v_d_skip=v_d_skip, v_ssm_norm_w=v_ssm_norm_w, v_w_o_ssm=v_w_o_ssm, v_w_out=v_w_out, v_norm_mlp_w=v_norm_mlp_w, v_w_up=v_w_up, v_w_down=v_w_down, v_ple_norm_w=v_ple_norm_w, v_w_ple_gate=v_w_ple_gate, v_w_ple=v_w_ple)
    weights = {n: given[n] for n in TWIN_WEIGHTS}
    shared = {n: given[n] for n in SHARED_INPUTS}
    per_example = {n: given[n] for n in ['x', 'p', 'positions']}
    grad_fn = _jax.value_and_grad(_loss, argnums=(0, 1))

    def one_microbatch(ex, loss_target):
        ex = dict(ex)
        diff = ex.pop(TWIN_DIFF_INPUT)
        return grad_fn(weights, diff, {**shared, **ex}, loss_target)

    if N_MICROBATCH == 1:
        loss, (grad_w, grad_x) = one_microbatch(per_example, given["loss_target"])
    else:
        def body(carry, xs):
            loss_sum, grad_sum = carry
            l_k, (gw_k, gx_k) = one_microbatch(xs[0], xs[1])
            with _jax.named_scope("update"):
                return (loss_sum + l_k, _jax.tree.map(_jnp.add, grad_sum, gw_k)), gx_k

        init = (_jnp.zeros((), _jnp.float32), _jax.tree.map(_jnp.zeros_like, weights))
        (loss, grad_w), grad_x = _jax.lax.scan(body, init, (per_example, given["loss_target"]))
    with _jax.named_scope("update"):
        delta_w, new_m, new_v = {}, {}, {}
        for n in TWIN_WEIGHTS:
            delta_w[n], new_m[n], new_v[n] = _adamw(weights[n], grad_w[n], given["m_" + n], given["v_" + n])
    return (loss, grad_x, *[grad_w[n] for n in TWIN_WEIGHTS], *[delta_w[n] for n in TWIN_WEIGHTS],
            *[new_m[n] for n in TWIN_WEIGHTS], *[new_v[n] for n in TWIN_WEIGHTS])
```

```python
import functools

import jax
import jax.numpy as jnp
from jax import lax
from jax.experimental import pallas as pl
from jax.experimental.pallas import tpu as pltpu

F32 = jnp.float32
BF16 = jnp.bfloat16
HI = lax.Precision.HIGHEST

EPS = 1e-6
MLA_HEADS = 16
QK_NOPE = 128
QK_ROPE = 64
QK_DIM = QK_NOPE + QK_ROPE
V_DIM = 128
ROPE_THETA = 10000.0
ATT_CHUNK = 64
SSM_GROUPS = 8
SSM_HEADDIM = 64
SSM_STATE = 128
CONV_WIDTH = 4
ADAM_LR = 0.001
ADAM_B1 = 0.9
ADAM_B2 = 0.999
ADAM_EPS = 1e-08
ADAM_WD = 0.01
ADAM_STEP = 10

V7X_VMEM_LIMIT = 56 * 1024 * 1024


def _params(sem=None, **kw):
    return pltpu.CompilerParams(dimension_semantics=sem, vmem_limit_bytes=V7X_VMEM_LIMIT, **kw)


def _pick(n, prefs):
    for t in prefs:
        if n % t == 0:
            return t
    return n


def matmul(a, b, *, ta=False, tb=False, out_dtype=F32, name):
    m, k = (a.shape[1], a.shape[0]) if ta else a.shape
    k2, n = (b.shape[1], b.shape[0]) if tb else b.shape
    assert k == k2, (a.shape, b.shape, ta, tb)
    tm = _pick(m, (512, 256, 128))
    tn = _pick(n, (512, 256, 128))
    tk = _pick(k, (1024, 512, 256, 128))
    nk = k // tk
    dn = (((0 if ta else 1,), (1 if tb else 0,)), ((), ()))

    def body(a_ref, b_ref, o_ref, acc_ref):
        kk = pl.program_id(2)
        part = lax.dot_general(a_ref[...].astype(BF16), b_ref[...].astype(BF16), dn, preferred_element_type=F32)

        @pl.when(kk == 0)
        def _():
            acc_ref[...] = part

        @pl.when(kk > 0)
        def _():
            acc_ref[...] += part

        @pl.when(kk == nk - 1)
        def _():
            o_ref[...] = acc_ref[...].astype(o_ref.dtype)

    a_spec = pl.BlockSpec((tk, tm), lambda i, j, kk: (kk, i)) if ta else pl.BlockSpec((tm, tk), lambda i, j, kk: (i, kk))
    b_spec = pl.BlockSpec((tn, tk), lambda i, j, kk: (j, kk)) if tb else pl.BlockSpec((tk, tn), lambda i, j, kk: (kk, j))
    return pl.pallas_call(
        body,
        name=name,
        grid=(m // tm, n // tn, nk),
        in_specs=[a_spec, b_spec],
        out_specs=pl.BlockSpec((tm, tn), lambda i, j, kk: (i, j)),
        out_shape=jax.ShapeDtypeStruct((m, n), out_dtype),
        scratch_shapes=[pltpu.VMEM((tm, tn), F32)],
        compiler_params=_params(("parallel", "parallel", "arbitrary")),
    )(a, b)


def linear(a, w, sink, *, name):
    @jax.custom_vjp
    def op(a, w, sink):
        return matmul(a, w, name=name + "_fwd")

    def fwd(a, w, sink):
        return op(a, w, sink), (a, w)

    def bwd(res, ct):
        a, w = res
        da = matmul(ct, w, tb=True, name=name + "_bwd_da")
        dw = matmul(a, ct, ta=True, name=name + "_bwd_dw")
        return da.astype(a.dtype), jnp.zeros_like(w), dw

    op.defvjp(fwd, bwd)
    return op(a, w, sink)


def _tiled_call(fn, tiled, whole, tile, name, n_acc):
    rows = tiled[0].shape[0]
    assert rows % tile == 0
    t_avals = [jax.ShapeDtypeStruct((tile,) + a.shape[1:], a.dtype) for a in tiled]
    w_avals = [jax.ShapeDtypeStruct(a.shape, a.dtype) for a in whole]
    outs = jax.eval_shape(fn, *t_avals, *w_avals)
    n_in = len(tiled) + len(whole)
    n_t = len(outs) - n_acc

    def body(*refs):
        res = fn(*[r[...] for r in refs[:n_in]])
        o_refs = refs[n_in:]
        for r, v in zip(o_refs[:n_t], res[:n_t]):
            r[...] = v.astype(r.dtype)
        if n_acc:
            first = pl.program_id(0) == 0

            @pl.when(first)
            def _():
                for r, v in zip(o_refs[n_t:], res[n_t:]):
                    r[...] = v.astype(F32)

            @pl.when(jnp.logical_not(first))
            def _():
                for r, v in zip(o_refs[n_t:], res[n_t:]):
                    r[...] += v.astype(F32)

    def tspec(a):
        nd = len(a.shape)
        return pl.BlockSpec((tile,) + tuple(a.shape[1:]), lambda i, nd=nd: (i,) + (0,) * (nd - 1))

    def wspec(a):
        nd = len(a.shape)
        return pl.BlockSpec(tuple(a.shape), lambda i, nd=nd: (0,) * nd)

    out_shape = [jax.ShapeDtypeStruct((rows,) + o.shape[1:], o.dtype) for o in outs[:n_t]]
    out_shape += [jax.ShapeDtypeStruct(o.shape, F32) for o in outs[n_t:]]
    out_specs = [tspec(o) for o in out_shape[:n_t]] + [wspec(o) for o in out_shape[n_t:]]
    return pl.pallas_call(
        body,
        name=name,
        grid=(rows // tile,),
        in_specs=[tspec(a) for a in tiled] + [wspec(a) for a in whole],
        out_specs=out_specs,
        out_shape=out_shape,
        compiler_params=_params(("arbitrary",) if n_acc else ("parallel",)),
    )(*tiled, *whole)


def rowwise(f, rows, consts, params, *, name, tile, tables=()):
    rows, consts, tables, params = tuple(rows), tuple(consts), tuple(tables), tuple(params)
    nr, nc, ntab, npar = len(rows), len(consts), len(tables), len(params)

    @jax.custom_vjp
    def op(rows, consts, tables, params):
        return tuple(_tiled_call(f, rows + consts, tables + params, tile, name + "_fwd", 0))

    def fwd(rows, consts, tables, params):
        return op(rows, consts, tables, params), (rows, consts, tables, params)

    def bwd(res, cts):
        rows, consts, tables, params = res
        ncts = len(cts)

        def g(*args):
            r = args[:nr]
            c = args[nr:nr + nc]
            ct = args[nr + nc:nr + nc + ncts]
            tab = args[nr + nc + ncts:nr + nc + ncts + ntab]
            p = args[nr + nc + ncts + ntab:]
            _, vjp = jax.vjp(lambda *rp: f(*rp[:nr], *c, *tab, *rp[nr:]), *r, *p)
            return tuple(vjp(tuple(ct)))

        outs = _tiled_call(g, rows + consts + tuple(cts), tables + params, tile, name + "_bwd", npar)
        d_rows = tuple(o.astype(r.dtype) for o, r in zip(outs[:nr], rows))
        d_params = tuple(o.astype(p.dtype) for o, p in zip(outs[nr:], params))
        zeros = lambda xs: tuple(jnp.zeros_like(a) for a in xs)
        return d_rows, zeros(consts), zeros(tables), d_params

    op.defvjp(fwd, bwd)
    return op(rows, consts, tables, params)


ATT_TILE = 256
_NT = (((1,), (1,)), ((), ()))
_TN = (((0,), (0,)), ((), ()))


def _chunk_mask(row0, col0, shape):
    r = (row0 + lax.broadcasted_iota(jnp.int32, shape, 0)) // ATT_CHUNK
    c = (col0 + lax.broadcasted_iota(jnp.int32, shape, 1)) // ATT_CHUNK
    return c <= r


def _attention_fwd(q, k, v, name):
    h, s, dq = q.shape
    dv = v.shape[-1]
    t = min(ATT_TILE, s)
    scale = dq ** -0.5

    def body(q_ref, k_ref, v_ref, o_ref, lse_ref):
        i = pl.program_id(1)
        qb = q_ref[...].astype(BF16)

        def block(j, carry, masked):
            m, l, acc = carry
            off = pl.multiple_of(j * t, t)
            kj = k_ref[pl.ds(off, t), :].astype(BF16)
            vj = v_ref[pl.ds(off, t), :].astype(BF16)
            sc = lax.dot_general(qb, kj, _NT, preferred_element_type=F32) * scale
            if masked:
                sc = jnp.where(_chunk_mask(i * t, j * t, sc.shape), sc, -jnp.inf)
            m_new = jnp.maximum(m, jnp.max(sc, axis=1, keepdims=True))
            p = jnp.exp(sc - m_new)
            alpha = jnp.exp(m - m_new)
            l = alpha * l + jnp.sum(p, axis=1, keepdims=True)
            acc = alpha * acc + jnp.dot(p.astype(BF16), vj, preferred_element_type=F32)
            return m_new, l, acc

        init = (jnp.full((t, 1), -jnp.inf, F32), jnp.zeros((t, 1), F32), jnp.zeros((t, dv), F32))
        carry = lax.fori_loop(0, i, lambda j, c: block(j, c, False), init)
        m, l, acc = block(i, carry, True)
        o_ref[...] = acc / l
        lse_ref[...] = m + jnp.log(l)

    return pl.pallas_call(
        body,
        name=name,
        grid=(h, s // t),
        in_specs=[
            pl.BlockSpec((None, t, dq), lambda hh, i: (hh, i, 0)),
            pl.BlockSpec((None, s, dq), lambda hh, i: (hh, 0, 0)),
            pl.BlockSpec((None, s, dv), lambda hh, i: (hh, 0, 0)),
        ],
        out_specs=[
            pl.BlockSpec((None, t, dv), lambda hh, i: (hh, i, 0)),
            pl.BlockSpec((None, t, 1), lambda hh, i: (hh, i, 0)),
        ],
        out_shape=[jax.ShapeDtypeStruct((h, s, dv), F32), jax.ShapeDtypeStruct((h, s, 1), F32)],
        compiler_params=_params(("parallel", "parallel")),
    )(q, k, v)


def _attention_bwd_dq(q, k, v, o, lse, do, name):
    h, s, dq = q.shape
    dv = v.shape[-1]
    t = min(ATT_TILE, s)
    scale = dq ** -0.5

    def body(q_ref, k_ref, v_ref, o_ref, lse_ref, do_ref, dq_ref, delta_ref):
        i = pl.program_id(1)
        qb = q_ref[...].astype(BF16)
        dof = do_ref[...]
        dob = dof.astype(BF16)
        lse_v = lse_ref[...]
        delta = jnp.sum(dof * o_ref[...], axis=1, keepdims=True)
        delta_ref[...] = delta

        def block(j, acc, masked):
            off = pl.multiple_of(j * t, t)
            kj = k_ref[pl.ds(off, t), :].astype(BF16)
            vj = v_ref[pl.ds(off, t), :].astype(BF16)
            sc = lax.dot_general(qb, kj, _NT, preferred_element_type=F32) * scale
            p = jnp.exp(sc - lse_v)
            if masked:
                p = jnp.where(_chunk_mask(i * t, j * t, sc.shape), p, 0.0)
            dp = lax.dot_general(dob, vj, _NT, preferred_element_type=F32)
            ds = p * (dp - delta) * scale
            return acc + jnp.dot(ds.astype(BF16), kj, preferred_element_type=F32)

        acc = lax.fori_loop(0, i, lambda j, c: block(j, c, False), jnp.zeros((t, dq), F32))
        dq_ref[...] = block(i, acc, True)

    tile = lambda d: pl.BlockSpec((None, t, d), lambda hh, i: (hh, i, 0))
    whole = lambda d: pl.BlockSpec((None, s, d), lambda hh, i: (hh, 0, 0))
    return pl.pallas_call(
        body,
        name=name,
        grid=(h, s // t),
        in_specs=[tile(dq), whole(dq), whole(dv), tile(dv), tile(1), tile(dv)],
        out_specs=[tile(dq), tile(1)],
        out_shape=[jax.ShapeDtypeStruct((h, s, dq), F32), jax.ShapeDtypeStruct((h, s, 1), F32)],
        compiler_params=_params(("parallel", "parallel")),
    )(q, k, v, o, lse, do)


def _attention_bwd_dkv(q, k, v, lse, delta, do, name):
    h, s, dq = q.shape
    dv = v.shape[-1]
    t = min(ATT_TILE, s)
    n = s // t
    scale = dq ** -0.5

    def body(q_ref, k_ref, v_ref, lse_ref, delta_ref, do_ref, dk_ref, dv_ref):
        j = pl.program_id(1)
        kb = k_ref[...].astype(BF16)
        vb = v_ref[...].astype(BF16)

        def block(i, carry, masked):
            dk, dvv = carry
            off = pl.multiple_of(i * t, t)
            qi = q_ref[pl.ds(off, t), :].astype(BF16)
            doi = do_ref[pl.ds(off, t), :].astype(BF16)
            sc = lax.dot_general(qi, kb, _NT, preferred_element_type=F32) * scale
            p = jnp.exp(sc - lse_ref[pl.ds(off, t), :])
            if masked:
                p = jnp.where(_chunk_mask(i * t, j * t, sc.shape), p, 0.0)
            dp = lax.dot_general(doi, vb, _NT, preferred_element_type=F32)
            ds = p * (dp - delta_ref[pl.ds(off, t), :]) * scale
            dvv = dvv + lax.dot_general(p.astype(BF16), doi, _TN, preferred_element_type=F32)
            dk = dk + lax.dot_general(ds.astype(BF16), qi, _TN, preferred_element_type=F32)
            return dk, dvv

        carry = block(j, (jnp.zeros((t, dq), F32), jnp.zeros((t, dv), F32)), True)
        dk, dvv = lax.fori_loop(j + 1, n, lambda i, c: block(i, c, False), carry)
        dk_ref[...] = dk
        dv_ref[...] = dvv

    tile = lambda d: pl.BlockSpec((None, t, d), lambda hh, j: (hh, j, 0))
    whole = lambda d: pl.BlockSpec((None, s, d), lambda hh, j: (hh, 0, 0))
    return pl.pallas_call(
        body,
        name=name,
        grid=(h, n),
        in_specs=[whole(dq), tile(dq), tile(dv), whole(1), whole(1), whole(dv)],
        out_specs=[tile(dq), tile(dv)],
        out_shape=[jax.ShapeDtypeStruct((h, s, dq), F32), jax.ShapeDtypeStruct((h, s, dv), F32)],
        compiler_params=_params(("parallel", "parallel")),
    )(q, k, v, lse, delta, do)


def attention(q, k, v, *, name):
    @jax.custom_vjp
    def op(q, k, v):
        return _attention_fwd(q, k, v, name + "_fwd")[0]

    def fwd(q, k, v):
        o, lse = _attention_fwd(q, k, v, name + "_fwd")
        return o, (q, k, v, o, lse)

    def bwd(res, do):
        q, k, v, o, lse = res
        dq, delta = _attention_bwd_dq(q, k, v, o, lse, do, name + "_bwd_dq")
        dk, dv = _attention_bwd_dkv(q, k, v, lse, delta, do, name + "_bwd_dkv")
        return dq, dk, dv

    op.defvjp(fwd, bwd)
    return op(q, k, v)


CONV_HALO = 8


def _conv_tiles(s, c):
    return min(512, s), _pick(c, (512, 256, 128))


def _conv_fwd(x, w, name):
    s, c = x.shape
    ts, tc = _conv_tiles(s, c)
    nb = ts // CONV_HALO

    def body(xc_ref, xp_ref, w_ref, o_ref):
        t = pl.program_id(1)
        prev = jnp.where(t > 0, xp_ref[...], 0.0)
        xe = jnp.concatenate([prev, xc_ref[...]], axis=0)
        wv = w_ref[...]
        acc = jnp.zeros((ts, tc), F32)
        for tap in range(CONV_WIDTH):
            k = CONV_WIDTH - 1 - tap
            sh = xe if k == 0 else pltpu.roll(xe, k, axis=0)
            acc = acc + sh[CONV_HALO:, :] * wv[tap:tap + 1, :]
        o_ref[...] = acc

    return pl.pallas_call(
        body,
        name=name,
        grid=(c // tc, s // ts),
        in_specs=[
            pl.BlockSpec((ts, tc), lambda ci, t: (t, ci)),
            pl.BlockSpec((CONV_HALO, tc), lambda ci, t: (jnp.maximum(t * nb - 1, 0), ci)),
            pl.BlockSpec((CONV_WIDTH, tc), lambda ci, t: (0, ci)),
        ],
        out_specs=pl.BlockSpec((ts, tc), lambda ci, t: (t, ci)),
        out_shape=jax.ShapeDtypeStruct((s, c), F32),
        compiler_params=_params(("parallel", "parallel")),
    )(x, x, w)


def _conv_bwd(x, w, dy, name):
    s, c = x.shape
    ts, tc = _conv_tiles(s, c)
    nb = ts // CONV_HALO
    nt = s // ts

    def body(xc_ref, xp_ref, w_ref, dc_ref, dn_ref, dx_ref, dw_ref):
        t = pl.program_id(1)
        prev = jnp.where(t > 0, xp_ref[...], 0.0)
        xe = jnp.concatenate([prev, xc_ref[...]], axis=0)
        dcur = dc_ref[...]
        nxt = jnp.where(t < nt - 1, dn_ref[...], 0.0)
        de = jnp.concatenate([dcur, nxt], axis=0)
        wv = w_ref[...]
        dx = jnp.zeros((ts, tc), F32)
        dw = jnp.zeros((CONV_WIDTH, tc), F32)
        tap_row = lax.broadcasted_iota(jnp.int32, (CONV_WIDTH, tc), 0)
        for tap in range(CONV_WIDTH):
            k = CONV_WIDTH - 1 - tap
            dsh = de if k == 0 else pltpu.roll(de, ts + CONV_HALO - k, axis=0)
            dx = dx + dsh[:ts, :] * wv[tap:tap + 1, :]
            xsh = xe if k == 0 else pltpu.roll(xe, k, axis=0)
            dwt = jnp.sum(xsh[CONV_HALO:, :] * dcur, axis=0, keepdims=True)
            dw = jnp.where(tap_row == tap, dwt, dw)
        dx_ref[...] = dx

        @pl.when(t == 0)
        def _():
            dw_ref[...] = dw

        @pl.when(t > 0)
        def _():
            dw_ref[...] += dw

    return pl.pallas_call(
        body,
        name=name,
        grid=(c // tc, nt),
        in_specs=[
            pl.BlockSpec((ts, tc), lambda ci, t: (t, ci)),
            pl.BlockSpec((CONV_HALO, tc), lambda ci, t: (jnp.maximum(t * nb - 1, 0), ci)),
            pl.BlockSpec((CONV_WIDTH, tc), lambda ci, t: (0, ci)),
            pl.BlockSpec((ts, tc), lambda ci, t: (t, ci)),
            pl.BlockSpec((CONV_HALO, tc), lambda ci, t: (jnp.minimum((t + 1) * nb, s // CONV_HALO - 1), ci)),
        ],
        out_specs=[
            pl.BlockSpec((ts, tc), lambda ci, t: (t, ci)),
            pl.BlockSpec((CONV_WIDTH, tc), lambda ci, t: (0, ci)),
        ],
        out_shape=[jax.ShapeDtypeStruct((s, c), F32), jax.ShapeDtypeStruct((CONV_WIDTH, c), F32)],
        compiler_params=_params(("parallel", "arbitrary")),
    )(x, x, w, dy, dy)


def causal_conv(x, w, *, name):
    @jax.custom_vjp
    def op(x, w):
        return _conv_fwd(x, w, name + "_fwd")

    def fwd(x, w):
        return op(x, w), (x, w)

    def bwd(res, dy):
        x, w = res
        dx, dw = _conv_bwd(x, w, dy, name + "_bwd")
        return dx, dw

    op.defvjp(fwd, bwd)
    return op(x, w)


SSD_T = 128
SSD_R = 8
SSD_GW = SSD_R * SSM_HEADDIM


def _ssd_consts(t):
    r = lax.broadcasted_iota(jnp.int32, (t, t), 0)
    c = lax.broadcasted_iota(jnp.int32, (t, t), 1)
    tril = (c <= r).astype(F32)
    triu = (r <= c).astype(F32)
    head_of_lane = lax.broadcasted_iota(jnp.int32, (SSD_R, SSD_GW), 1) // SSM_HEADDIM
    expand = (head_of_lane == lax.broadcasted_iota(jnp.int32, (SSD_R, SSD_GW), 0)).astype(F32)
    return c <= r, tril, triu, expand


def _hi(a, b):
    return jnp.dot(a, b, precision=HI, preferred_element_type=F32)


def _hi_nt(a, b):
    return lax.dot_general(a, b, _NT, precision=HI, preferred_element_type=F32)


def _bdot(a, b, dn=None):
    if dn is None:
        return jnp.dot(a.astype(BF16), b.astype(BF16), preferred_element_type=F32)
    return lax.dot_general(a.astype(BF16), b.astype(BF16), dn, preferred_element_type=F32)


def _ssd_chunk_common(x_ref, b_ref, c_ref, dtc_ref, dtr_ref, alc_ref, alr_ref, t):
    mask, tril, triu, expand = _ssd_consts(t)
    x, bm, cm = x_ref[...], b_ref[...], c_ref[...]
    dtc, dtr = dtc_ref[...], dtr_ref[...]
    neg_a_c = -jnp.exp(alc_ref[...])
    neg_a_r = -jnp.exp(alr_ref[...])
    acum_c = _hi(tril, dtc * neg_a_c)
    acum_r = _hi(dtr * neg_a_r, triu)
    s_cb = _bdot(cm, bm, _NT)
    return mask, tril, triu, expand, x, bm, cm, dtc, dtr, neg_a_c, neg_a_r, acum_c, acum_r, s_cb


def _head_decay(mask, acum_c, acum_r, h):
    seg = acum_c[:, h:h + 1] - acum_r[h:h + 1, :]
    return jnp.exp(jnp.where(mask, seg, -jnp.inf))


def _ssd_fwd(xbc, dtc, dtr, alc, alr, dexp, name):
    s = xbc.shape[0]
    g = SSM_GROUPS
    t = min(SSD_T, s)
    nc = s // t
    n = SSM_STATE
    xblocks = (g * SSD_GW) // n

    def body(x_ref, b_ref, c_ref, dtc_ref, dtr_ref, alc_ref, alr_ref, d_ref, y_ref, hs_ref, h_scr):
        ci = pl.program_id(1)

        @pl.when(ci == 0)
        def _():
            h_scr[...] = jnp.zeros_like(h_scr)

        (mask, tril, triu, expand, x, bm, cm, dtc_v, dtr_v, _, _, acum_c, acum_r, s_cb) = _ssd_chunk_common(
            x_ref, b_ref, c_ref, dtc_ref, dtr_ref, alc_ref, alr_ref, t)
        hst = h_scr[...]
        hs_ref[...] = hst
        ch = _bdot(cm, hst)
        y = _hi(jnp.exp(acum_c), expand) * ch + d_ref[...] * x
        half = lax.broadcasted_iota(jnp.int32, (t, 2 * SSM_HEADDIM), 1) // SSM_HEADDIM
        parts = []
        for j in range(SSD_R // 2):
            xp = x[:, j * 128:(j + 1) * 128]
            acc = jnp.zeros((t, 128), F32)
            for hh in range(2):
                h = 2 * j + hh
                m = s_cb * _head_decay(mask, acum_c, acum_r, h) * dtr_v[h:h + 1, :]
                acc = acc + _bdot(m, jnp.where(half == hh, xp, 0.0))
            parts.append(acc)
        y_ref[...] = y + jnp.concatenate(parts, axis=1)
        last = acum_c[t - 1:t, :]
        w_c = jnp.exp(last - acum_c) * dtc_v
        dec = _hi(jnp.broadcast_to(jnp.exp(last), (SSD_R, SSD_R)), expand)[0:1, :]
        h_scr[...] = dec * hst + _bdot(bm, _hi(w_c, expand) * x, _TN)

    return pl.pallas_call(
        body,
        name=name,
        grid=(g, nc),
        in_specs=[
            pl.BlockSpec((t, SSD_GW), lambda gi, ci: (ci, gi)),
            pl.BlockSpec((t, n), lambda gi, ci: (ci, xblocks + gi)),
            pl.BlockSpec((t, n), lambda gi, ci: (ci, xblocks + g + gi)),
            pl.BlockSpec((None, t, SSD_R), lambda gi, ci: (gi, ci, 0)),
            pl.BlockSpec((None, SSD_R, t), lambda gi, ci: (gi, 0, ci)),
            pl.BlockSpec((None, 1, SSD_R), lambda gi, ci: (gi, 0, 0)),
            pl.BlockSpec((None, SSD_R, 1), lambda gi, ci: (gi, 0, 0)),
            pl.BlockSpec((None, 1, SSD_GW), lambda gi, ci: (gi, 0, 0)),
        ],
        out_specs=[
            pl.BlockSpec((t, SSD_GW), lambda gi, ci: (ci, gi)),
            pl.BlockSpec((None, None, n, SSD_GW), lambda gi, ci: (gi, ci, 0, 0)),
        ],
        out_shape=[jax.ShapeDtypeStruct((s, g * SSD_GW), F32), jax.ShapeDtypeStruct((g, nc, n, SSD_GW), F32)],
        scratch_shapes=[pltpu.VMEM((n, SSD_GW), F32)],
        compiler_params=_params(("parallel", "arbitrary")),
    )(xbc, xbc, xbc, dtc, dtr, alc, alr, dexp)


def _ssd_bwd(xbc, dtc, dtr, alc, alr, dexp, hs, dy, name):
    s = xbc.shape[0]
    g = SSM_GROUPS
    t = min(SSD_T, s)
    nc = s // t
    n = SSM_STATE
    xblocks = (g * SSD_GW) // n

    def body(x_ref, b_ref, c_ref, dtc_ref, dtr_ref, alc_ref, alr_ref, d_ref, hs_ref, dy_ref,
             dx_ref, db_ref, dc_ref, ddtc_ref, ddtr_ref, dalc_ref, dalr_ref, dd_ref, dh_scr):
        ci = pl.program_id(1)

        @pl.when(ci == 0)
        def _():
            dh_scr[...] = jnp.zeros_like(dh_scr)

        (mask, tril, triu, expand, x, bm, cm, dtc_v, dtr_v, neg_a_c, neg_a_r, acum_c, acum_r, s_cb) = _ssd_chunk_common(
            x_ref, b_ref, c_ref, dtc_ref, dtr_ref, alc_ref, alr_ref, t)
        hst = hs_ref[...]
        dhn = dh_scr[...]
        dy = dy_ref[...]
        ch = _bdot(cm, hst)
        scale_full = _hi(jnp.exp(acum_c), expand)
        sdy = scale_full * dy
        d_c = _bdot(sdy, hst, _NT)
        dh_prev = _bdot(cm, sdy, _TN)
        dacum_c = _hi_nt(sdy * ch, expand)
        dx = d_ref[...] * dy
        dd = jnp.sum(dy * x, axis=0, keepdims=True)
        last = acum_c[t - 1:t, :]
        e_last = jnp.exp(last)
        dec = _hi(jnp.broadcast_to(e_last, (SSD_R, SSD_R)), expand)[0:1, :]
        dh_prev = dh_prev + dec * dhn
        ddec = jnp.sum(hst * dhn, axis=0, keepdims=True)
        dlast = _hi_nt(jnp.broadcast_to(ddec, (SSD_R, SSD_GW)), expand)[0:1, :] * e_last
        w_e = jnp.exp(last - acum_c)
        w_c = w_e * dtc_v
        wfull = _hi(w_c, expand)
        z = _bdot(bm, dhn)
        dx = dx + wfull * z
        dw_c = _hi_nt(x * z, expand)
        ddt_c = dw_c * w_e
        q_c = dw_c * w_c
        dacum_c = dacum_c - q_c
        dlast = dlast + jnp.sum(q_c, axis=0, keepdims=True)
        d_b = _bdot(wfull * x, dhn, _NT)
        half = lax.broadcasted_iota(jnp.int32, (t, 2 * SSM_HEADDIM), 1) // SSM_HEADDIM
        lane8 = lax.broadcasted_iota(jnp.int32, (t, SSD_R), 1)
        row8 = lax.broadcasted_iota(jnp.int32, (SSD_R, t), 0)
        ds_cb = jnp.zeros((t, t), F32)
        dacum_r = jnp.zeros((SSD_R, t), F32)
        ddt_r = jnp.zeros((SSD_R, t), F32)
        parts = []
        for j in range(SSD_R // 2):
            xp = x[:, j * 128:(j + 1) * 128]
            dyp = dy[:, j * 128:(j + 1) * 128]
            dxp = jnp.zeros((t, 128), F32)
            for hh in range(2):
                h = 2 * j + hh
                dts = dtr_v[h:h + 1, :]
                decay = _head_decay(mask, acum_c, acum_r, h)
                sl = s_cb * decay
                m = sl * dts
                xm = jnp.where(half == hh, xp, 0.0)
                dym = jnp.where(half == hh, dyp, 0.0)
                dxp = dxp + _bdot(m, dym, _TN)
                dm = _bdot(dym, xm, _NT)
                ds_cb = ds_cb + dm * decay * dts
                q = dm * m
                dacum_c = dacum_c + jnp.where(lane8 == h, jnp.sum(q, axis=1, keepdims=True), 0.0)
                dacum_r = dacum_r - jnp.where(row8 == h, jnp.sum(q, axis=0, keepdims=True), 0.0)
                ddt_r = ddt_r + jnp.where(row8 == h, jnp.sum(dm * sl, axis=0, keepdims=True), 0.0)
            parts.append(dxp)
        dx_ref[...] = dx + jnp.concatenate(parts, axis=1)
        dc_ref[...] = d_c + _bdot(ds_cb, bm)
        db_ref[...] = d_b + _bdot(ds_cb, cm, _TN)
        row_t = lax.broadcasted_iota(jnp.int32, (t, SSD_R), 0)
        dacum_c = dacum_c + jnp.where(row_t == t - 1, dlast, 0.0)
        da_c = _hi(triu, dacum_c)
        da_r = _hi(dacum_r, tril)
        ddtc_ref[...] = ddt_c + da_c * neg_a_c
        ddtr_ref[...] = ddt_r + da_r * neg_a_r
        dal_c = jnp.sum(da_c * dtc_v, axis=0, keepdims=True) * neg_a_c
        dal_r = jnp.sum(da_r * dtr_v, axis=1, keepdims=True) * neg_a_r
        dh_scr[...] = dh_prev

        @pl.when(ci == 0)
        def _():
            dalc_ref[...] = dal_c
            dalr_ref[...] = dal_r
            dd_ref[...] = dd

        @pl.when(ci > 0)
        def _():
            dalc_ref[...] += dal_c
            dalr_ref[...] += dal_r
            dd_ref[...] += dd

    rev = lambda ci: nc - 1 - ci
    return pl.pallas_call(
        body,
        name=name,
        grid=(g, nc),
        in_specs=[
            pl.BlockSpec((t, SSD_GW), lambda gi, ci: (rev(ci), gi)),
            pl.BlockSpec((t, n), lambda gi, ci: (rev(ci), xblocks + gi)),
            pl.BlockSpec((t, n), lambda gi, ci: (rev(ci), xblocks + g + gi)),
            pl.BlockSpec((None, t, SSD_R), lambda gi, ci: (gi, rev(ci), 0)),
            pl.BlockSpec((None, SSD_R, t), lambda gi, ci: (gi, 0, rev(ci))),
            pl.BlockSpec((None, 1, SSD_R), lambda gi, ci: (gi, 0, 0)),
            pl.BlockSpec((None, SSD_R, 1), lambda gi, ci: (gi, 0, 0)),
            pl.BlockSpec((None, 1, SSD_GW), lambda gi, ci: (gi, 0, 0)),
            pl.BlockSpec((None, None, n, SSD_GW), lambda gi, ci: (gi, rev(ci), 0, 0)),
            pl.BlockSpec((t, SSD_GW), lambda gi, ci: (rev(ci), gi)),
        ],
        out_specs=[
            pl.BlockSpec((t, SSD_GW), lambda gi, ci: (rev(ci), gi)),
            pl.BlockSpec((t, n), lambda gi, ci: (rev(ci), gi)),
            pl.BlockSpec((t, n), lambda gi, ci: (rev(ci), gi)),
            pl.BlockSpec((None, t, SSD_R), lambda gi, ci: (gi, rev(ci), 0)),
            pl.BlockSpec((None, SSD_R, t), lambda gi, ci: (gi, 0, rev(ci))),
            pl.BlockSpec((None, 1, SSD_R), lambda gi, ci: (gi, 0, 0)),
            pl.BlockSpec((None, SSD_R, 1), lambda gi, ci: (gi, 0, 0)),
            pl.BlockSpec((None, 1, SSD_GW), lambda gi, ci: (gi, 0, 0)),
        ],
        out_shape=[
            jax.ShapeDtypeStruct((s, g * SSD_GW), F32),
            jax.ShapeDtypeStruct((s, g * n), F32),
            jax.ShapeDtypeStruct((s, g * n), F32),
            jax.ShapeDtypeStruct((g, s, SSD_R), F32),
            jax.ShapeDtypeStruct((g, SSD_R, s), F32),
            jax.ShapeDtypeStruct((g, 1, SSD_R), F32),
            jax.ShapeDtypeStruct((g, SSD_R, 1), F32),
            jax.ShapeDtypeStruct((g, 1, SSD_GW), F32),
        ],
        scratch_shapes=[pltpu.VMEM((n, SSD_GW), F32)],
        compiler_params=_params(("parallel", "arbitrary")),
    )(xbc, xbc, xbc, dtc, dtr, alc, alr, dexp, hs, dy)


def ssd_core(xbc, dtc, dtr, alc, alr, dexp, *, name):
    @jax.custom_vjp
    def op(xbc, dtc, dtr, alc, alr, dexp):
        return _ssd_fwd(xbc, dtc, dtr, alc, alr, dexp, name + "_fwd")[0]

    def fwd(xbc, dtc, dtr, alc, alr, dexp):
        y, hs = _ssd_fwd(xbc, dtc, dtr, alc, alr, dexp, name + "_fwd")
        return y, (xbc, dtc, dtr, alc, alr, dexp, hs)

    def bwd(res, dy):
        xbc, dtc, dtr, alc, alr, dexp, hs = res
        dx, db, dc, ddtc, ddtr, dalc, dalr, dd = _ssd_bwd(xbc, dtc, dtr, alc, alr, dexp, hs, dy, name + "_bwd")
        return jnp.concatenate([dx, db, dc], axis=1), ddtc, ddtr, dalc, dalr, dd

    op.defvjp(fwd, bwd)
    return op(xbc, dtc, dtr, alc, alr, dexp)


def gate_norm(y, z, w):
    return (rms_norm(y * (z * jax.nn.sigmoid(z)), w),)


def ssd_branch(xbc, z, dt_raw, conv_w, conv_b, dt_bias, a_log, d_skip, norm_w, *, name):
    s = xbc.shape[0]
    g = SSM_GROUPS
    conv = causal_conv(xbc, conv_w, name=name + "_conv")
    (xc,) = rowwise(lambda c, b: ((c + b) * jax.nn.sigmoid(c + b),), (conv,), (), (conv_b[None, :],),
                    name=name + "_silu", tile=min(256, s))
    (dt,) = rowwise(lambda r, b: (jax.nn.softplus(r + b),), (dt_raw,), (), (dt_bias[None, :],),
                    name=name + "_dt", tile=min(512, s))
    dt3 = dt.reshape(s, g, SSD_R)
    y = ssd_core(xc, dt3.transpose(1, 0, 2), dt3.transpose(1, 2, 0), a_log.reshape(g, 1, SSD_R),
                 a_log.reshape(g, SSD_R, 1), jnp.repeat(d_skip, SSM_HEADDIM).reshape(g, 1, SSD_GW), name=name + "_core")
    (out,) = rowwise(gate_norm, (y.reshape(s, g, SSD_GW), z.reshape(s, g, SSD_GW)), (),
                     (norm_w.reshape(g, SSD_GW),), name=name + "_gate", tile=min(128, s))
    return out.reshape(s, g * SSD_GW)


def rms_norm(x, w):
    return x * lax.rsqrt(jnp.mean(x * x, axis=-1, keepdims=True) + EPS) * w


def rope_matrix():
    half = QK_ROPE // 2
    j = jnp.arange(QK_DIM)
    src = jnp.where(j < QK_NOPE + half, j + half, j - half)
    sign = jnp.where(j < QK_NOPE, 0.0, jnp.where(j < QK_NOPE + half, -1.0, 1.0))
    return (jnp.arange(QK_DIM)[:, None] == src[None, :]).astype(F32) * sign[None, :]


def rope_tables_full(positions):
    inv_freq = 1.0 / (ROPE_THETA ** (jnp.arange(0, QK_ROPE, 2, dtype=F32) / QK_ROPE))
    ang = positions.astype(F32)[:, None] * inv_freq
    s = positions.shape[0]
    cos = jnp.concatenate([jnp.ones((s, QK_NOPE), F32), jnp.cos(ang), jnp.cos(ang)], axis=-1)
    sin = jnp.concatenate([jnp.zeros((s, QK_NOPE), F32), jnp.sin(ang), jnp.sin(ang)], axis=-1)
    return cos[:, None, :], sin[:, None, :]


def head_norm_rope(x, cos_full, sin_full, rot, w):
    t, h, d = x.shape
    y = rms_norm(x, w)
    partner = jnp.dot(y.reshape(t * h, d), rot, precision=HI, preferred_element_type=F32).reshape(t, h, d)
    return (y * cos_full + partner * sin_full,)


def _norm(x, w, *, name, tile=256):
    (y,) = rowwise(lambda x, w: (rms_norm(x, w),), (x,), (), (w[None, :],), name=name, tile=min(tile, x.shape[0]))
    return y


Q_LORA = 512
KV_LORA = 512
W_IN_PIECES = ("cq", "ckv", "kr", "z", "xbc", "dt", "ga", "gb")


def w_in_widths(d_model):
    d_inner = 2 * d_model
    conv_dim = d_inner + 2 * SSM_GROUPS * SSM_STATE
    return (Q_LORA, KV_LORA, QK_ROPE, d_inner, conv_dim, d_inner // SSM_HEADDIM, d_model, d_model)


def split_w_in(w_in):
    widths = w_in_widths(w_in.shape[0])
    offs = [0]
    for wd in widths:
        offs.append(offs[-1] + wd)
    pc = {n: w_in[:, offs[j]:offs[j + 1]] for j, n in enumerate(W_IN_PIECES)}
    pc["kd"] = jnp.concatenate([pc.pop("kr"), pc.pop("dt")], axis=1)
    return pc


def join_w_in_grads(g):
    kr, dt = g["kd"][:, :QK_ROPE], g["kd"][:, QK_ROPE:]
    return jnp.concatenate([g["cq"], g["ckv"], kr, g["z"], g["xbc"], dt, g["ga"], g["gb"]], axis=1)


LINEAR_NAMES = ("cq", "ckv", "kd", "z", "xbc", "ga", "gb", "w_uq", "w_ukv", "w_o_mla", "w_o_ssm", "w_out", "w_up",
                "w_down", "w_ple_gate", "w_ple")


def layer_forward(x, p_i, cos_full, sin_full, rot, wb, sm, sinks):
    s, d = x.shape
    lin = lambda a, n: linear(a, wb[n], sinks[n], name="lin_" + n)
    h = _norm(x, sm["norm_mix_w"], name="norm_mix")
    c_q, c_kv, kd = lin(h, "cq"), lin(h, "ckv"), lin(h, "kd")
    z, xbc, g_a, g_b = lin(h, "z"), lin(h, "xbc"), lin(h, "ga"), lin(h, "gb")
    k_r, dt_raw = kd[:, :QK_ROPE], kd[:, QK_ROPE:]
    q = lin(_norm(c_q, sm["q_a_norm_w"], name="norm_qa"), "w_uq").reshape(s, MLA_HEADS, QK_DIM)
    kv = lin(_norm(c_kv, sm["kv_a_norm_w"], name="norm_kva"), "w_ukv").reshape(s, MLA_HEADS, QK_NOPE + V_DIM)
    k = jnp.concatenate([kv[..., :QK_NOPE], jnp.broadcast_to(k_r[:, None, :], (s, MLA_HEADS, QK_ROPE))], axis=-1)
    v = kv[..., QK_NOPE:]
    tq = min(128, s)
    (q,) = rowwise(head_norm_rope, (q,), (cos_full, sin_full), (sm["q_norm_w"][None, :],), tables=(rot,), name="q_rope", tile=tq)
    (k,) = rowwise(head_norm_rope, (k,), (cos_full, sin_full), (sm["k_norm_w"][None, :],), tables=(rot,), name="k_rope", tile=tq)
    hm = lambda a: a.transpose(1, 0, 2)
    o = hm(attention(hm(q), hm(k), hm(v), name="attn")).reshape(s, MLA_HEADS * V_DIM)
    y_a = lin(o, "w_o_mla")
    y_ssd = ssd_branch(xbc, z, dt_raw, sm["conv_w"], sm["conv_b"], sm["dt_bias"], sm["a_log"], sm["d_skip"],
                       sm["ssm_norm_w"], name="ssd")
    y_b = lin(y_ssd, "w_o_ssm")
    sig = jax.nn.sigmoid
    tr = min(256, s)
    (merged,) = rowwise(lambda ga, gb, ya, yb: (sig(ga) * ya + sig(gb) * yb,), (g_a, g_b, y_a, y_b), (), (),
                        name="merge", tile=tr)
    x = x + lin(merged, "w_out")
    up = lin(_norm(x, sm["norm_mlp_w"], name="norm_mlp"), "w_up")
    (act,) = rowwise(lambda u: (jnp.square(jnp.maximum(u, 0.0)),), (up,), (), (), name="relu2", tile=tr)
    x = x + lin(act, "w_down")
    pg = lin(_norm(x, sm["ple_norm_w"], name="norm_ple"), "w_ple_gate")
    pe = lin(p_i, "w_ple")
    (x,) = rowwise(lambda x, pe, pg: (x + pe * sig(pg),), (x, pe, pg), (), (), name="ple_add", tile=tr)
    return x


def loss_and_cotangent(y, target):
    s, d = y.shape

    def f(y, t):
        e = y - t
        return e * (1.0 / d), 0.5 * jnp.sum(jnp.sum(e * e, axis=1, keepdims=True) * (1.0 / d), axis=0, keepdims=True)

    dy, part = _tiled_call(f, (y, target), (), min(256, s), "loss", 1)
    return dy, part[0, 0]


ADAM_BLOCK_ELEMS = 256 * 1024


def adamw(w, g, m, v, *, name):
    rows, cols = w.shape
    budget = max(8, ADAM_BLOCK_ELEMS // cols)
    tile = _pick(rows, tuple(t for t in (512, 256, 128, 64, 32, 16, 8) if t <= budget))

    def f(w, g, m, v):
        m = ADAM_B1 * m + (1.0 - ADAM_B1) * g
        v = ADAM_B2 * v + (1.0 - ADAM_B2) * jnp.square(g)
        m_hat = m / (1.0 - ADAM_B1 ** ADAM_STEP)
        v_hat = v / (1.0 - ADAM_B2 ** ADAM_STEP)
        delta = -ADAM_LR * (m_hat / (jnp.sqrt(v_hat) + ADAM_EPS) + ADAM_WD * w)
        return delta, m, v

    return _tiled_call(f, (w, g, m, v), (), tile, name, 0)


MESH_ID = pl.DeviceIdType.MESH
N_CHIPS = 4
LANES = 1024
_ANY = pl.BlockSpec(memory_space=pl.ANY)


def _place():
    return lax.axis_index("x"), lax.axis_index("y"), lax.axis_index("c")


def _other_chips(x, y):
    return [(1 - x, y), (x, 1 - y), (1 - x, 1 - y)]


def _rcopy(src, dst, send_sem, recv_sem, device):
    return pltpu.make_async_remote_copy(src_ref=src, dst_ref=dst, send_sem=send_sem, recv_sem=recv_sem,
                                        device_id=device, device_id_type=MESH_ID)


def gather_shards(shard, *, name):
    rows, cols = shard.shape
    half = rows // 2

    def body(src, out, send_sems, recv_sems, local_sem):
        x, y, c = _place()
        sibling = (x, y, 1 - c)
        chips = _other_chips(x, y)

        def part(px, py, h):
            return out.at[2 * px + py, pl.ds(h * half, half), :]

        mine = pltpu.make_async_copy(src, out.at[2 * x + y], local_sem)
        mine.start()
        first = [_rcopy(src.at[pl.ds(c * half, half), :], part(x, y, c), send_sems.at[j], recv_sems.at[j], (*chip, c))
                 for j, chip in enumerate(chips)]
        for cp in first:
            cp.start()
        passed = [_rcopy(part(*chip, c), part(*chip, c), send_sems.at[3 + j], recv_sems.at[3 + j], sibling)
                  for j, chip in enumerate(chips)]
        for j, chip in enumerate(chips):
            _rcopy(part(*chip, c), part(*chip, c), send_sems.at[j], recv_sems.at[j], (*chip, c)).wait_recv()
            passed[j].start()
        for j, chip in enumerate(chips):
            _rcopy(part(*chip, 1 - c), part(*chip, 1 - c), send_sems.at[3 + j], recv_sems.at[3 + j], sibling).wait_recv()
        for cp in first + passed:
            cp.wait_send()
        mine.wait()

    return pl.pallas_call(
        body,
        name=name,
        in_specs=[_ANY],
        out_specs=_ANY,
        out_shape=jax.ShapeDtypeStruct((N_CHIPS, rows, cols), shard.dtype),
        scratch_shapes=[pltpu.SemaphoreType.DMA((6,)), pltpu.SemaphoreType.DMA((6,)), pltpu.SemaphoreType.DMA],
        compiler_params=pltpu.CompilerParams(has_side_effects=True),
    )(shard)


def sibling_take_half(g, *, name):
    n, rows, cols = g.shape
    half = rows // 2

    def body(g_ref, a_ref, send_sem, recv_sem):
        x, y, c = _place()
        cp = _rcopy(g_ref.at[:, pl.ds((1 - c) * half, half), :], a_ref, send_sem, recv_sem, (x, y, 1 - c))
        cp.start()
        cp.wait()

    return pl.pallas_call(
        body,
        name=name,
        in_specs=[_ANY],
        out_specs=_ANY,
        out_shape=jax.ShapeDtypeStruct((n, half, cols), g.dtype),
        scratch_shapes=[pltpu.SemaphoreType.DMA, pltpu.SemaphoreType.DMA],
        compiler_params=pltpu.CompilerParams(has_side_effects=True),
    )(g)


def add_own_half(g, a, *, name):
    n, rows, cols = g.shape
    half = rows // 2
    tile = _pick(half, (512, 256, 128, 64, 32, 16, 8))
    nb = half // tile

    def body(c_ref, g_ref, a_ref, o_ref):
        o_ref[...] = g_ref[...] + a_ref[...]

    return pl.pallas_call(
        body,
        name=name,
        grid_spec=pltpu.PrefetchScalarGridSpec(
            num_scalar_prefetch=1,
            grid=(n, nb),
            in_specs=[
                pl.BlockSpec((None, tile, cols), lambda j, i, c_ref: (j, c_ref[0] * nb + i, 0)),
                pl.BlockSpec((None, tile, cols), lambda j, i, c_ref: (j, i, 0)),
            ],
            out_specs=pl.BlockSpec((None, tile, cols), lambda j, i, c_ref: (j, i, 0)),
        ),
        out_shape=jax.ShapeDtypeStruct((n, half, cols), g.dtype),
        compiler_params=_params(("parallel", "parallel")),
    )(lax.axis_index("c").astype(jnp.int32).reshape(1), g, a)


def exchange_chip_slots(p, *, name):
    n, rows, cols = p.shape

    def body(p_ref, b_ref, send_sems, recv_sems, local_sem):
        x, y, c = _place()
        me = 2 * x + y
        chips = _other_chips(x, y)
        mine = pltpu.make_async_copy(p_ref.at[me], b_ref.at[me], local_sem)
        mine.start()
        sends = [_rcopy(p_ref.at[2 * cx + cy], b_ref.at[me], send_sems.at[j], recv_sems.at[j], (cx, cy, c))
                 for j, (cx, cy) in enumerate(chips)]
        for cp in sends:
            cp.start()
        for j, (cx, cy) in enumerate(chips):
            _rcopy(p_ref.at[me], b_ref.at[2 * cx + cy], send_sems.at[j], recv_sems.at[j], (cx, cy, c)).wait_recv()
        for cp in sends:
            cp.wait_send()
        mine.wait()

    return pl.pallas_call(
        body,
        name=name,
        in_specs=[_ANY],
        out_specs=_ANY,
        out_shape=jax.ShapeDtypeStruct((n, rows, cols), p.dtype),
        scratch_shapes=[pltpu.SemaphoreType.DMA((3,)), pltpu.SemaphoreType.DMA((3,)), pltpu.SemaphoreType.DMA],
        compiler_params=pltpu.CompilerParams(has_side_effects=True),
    )(p)


def sum_slots(b, *, name):
    n, rows, cols = b.shape
    tile = _pick(rows, (512, 256, 128, 64, 32, 16, 8))

    def body(b_ref, o_ref):
        acc = b_ref[0]
        for j in range(1, n):
            acc = acc + b_ref[j]
        o_ref[...] = acc

    return pl.pallas_call(
        body,
        name=name,
        grid=(rows // tile,),
        in_specs=[pl.BlockSpec((n, tile, cols), lambda i: (0, i, 0))],
        out_specs=pl.BlockSpec((tile, cols), lambda i: (i, 0)),
        out_shape=jax.ShapeDtypeStruct((rows, cols), b.dtype),
        compiler_params=_params(("parallel",)),
    )(b)


def sibling_join_halves(f, *, name):
    half, cols = f.shape

    def body(f_ref, r_ref, send_sem, recv_sem, local_sem):
        x, y, c = _place()
        mine = pltpu.make_async_copy(f_ref, r_ref.at[pl.ds(c * half, half), :], local_sem)
        mine.start()
        cp = _rcopy(f_ref, r_ref.at[pl.ds(c * half, half), :], send_sem, recv_sem, (x, y, 1 - c))
        cp.start()
        _rcopy(f_ref, r_ref.at[pl.ds((1 - c) * half, half), :], send_sem, recv_sem, (x, y, 1 - c)).wait_recv()
        cp.wait_send()
        mine.wait()

    return pl.pallas_call(
        body,
        name=name,
        in_specs=[_ANY],
        out_specs=_ANY,
        out_shape=jax.ShapeDtypeStruct((2 * half, cols), f.dtype),
        scratch_shapes=[pltpu.SemaphoreType.DMA, pltpu.SemaphoreType.DMA, pltpu.SemaphoreType.DMA],
        compiler_params=pltpu.CompilerParams(has_side_effects=True),
    )(f)


def reduce_to_owner(g, *, name):
    a = sibling_take_half(g, name=name + "_pair")
    p = add_own_half(g, a, name=name + "_pair_add")
    b = exchange_chip_slots(p, name=name + "_chips")
    f = sum_slots(b, name=name + "_chips_add")
    return sibling_join_halves(f, name=name + "_join")


def allreduce_small(v, *, name):
    rows, cols = v.shape

    def body(v_ref, o_ref, buf, send_sems, recv_sems):
        x, y, c = _place()
        me = 4 * x + 2 * y + c
        buf[me] = v_ref[...]
        copies = []
        for k in range(1, 8):
            bx, by, bc = (k >> 2) & 1, (k >> 1) & 1, k & 1
            peer = (x if bx == 0 else 1 - x, y if by == 0 else 1 - y, c if bc == 0 else 1 - c)
            copies.append(_rcopy(v_ref, buf.at[me], send_sems.at[k - 1], recv_sems.at[k - 1], peer))
        for cp in copies:
            cp.start()
        for k in range(1, 8):
            bx, by, bc = (k >> 2) & 1, (k >> 1) & 1, k & 1
            px, py, pc = (x if bx == 0 else 1 - x, y if by == 0 else 1 - y, c if bc == 0 else 1 - c)
            _rcopy(v_ref, buf.at[4 * px + 2 * py + pc], send_sems.at[k - 1], recv_sems.at[k - 1], (px, py, pc)).wait_recv()
        for cp in copies:
            cp.wait_send()
        acc = buf[0]
        for j in range(1, 8):
            acc = acc + buf[j]
        o_ref[...] = acc

    return pl.pallas_call(
        body,
        name=name,
        in_specs=[pl.BlockSpec(memory_space=pltpu.VMEM)],
        out_specs=pl.BlockSpec(memory_space=pltpu.VMEM),
        out_shape=jax.ShapeDtypeStruct((rows, cols), v.dtype),
        scratch_shapes=[pltpu.VMEM((8, rows, cols), v.dtype), pltpu.SemaphoreType.DMA((7,)), pltpu.SemaphoreType.DMA((7,))],
        compiler_params=pltpu.CompilerParams(has_side_effects=True, vmem_limit_bytes=V7X_VMEM_LIMIT),
    )(v)


BIG = (("w_in", 1), ("w_uq", 1), ("w_ukv", 1), ("w_o_mla", 0), ("w_o_ssm", 0), ("w_out", 0), ("w_up", 1),
       ("w_down", 0), ("w_ple_gate", 0), ("w_ple", 1))
SHARDED = BIG + (("conv_w", 1),)
SMALL = ("norm_mix_w", "q_a_norm_w", "kv_a_norm_w", "q_norm_w", "k_norm_w", "conv_b", "dt_bias", "a_log", "d_skip",
         "ssm_norm_w", "norm_mlp_w", "ple_norm_w")
WEIGHTS = ("norm_mix_w", "w_in", "q_a_norm_w", "w_uq", "kv_a_norm_w", "w_ukv", "q_norm_w", "k_norm_w", "w_o_mla", "conv_w",
           "conv_b", "dt_bias", "a_log", "d_skip", "ssm_norm_w", "w_o_ssm", "w_out", "norm_mlp_w", "w_up", "w_down",
           "ple_norm_w", "w_ple_gate", "w_ple")


def _to_rows(flat, cols, row_multiple):
    n = flat.shape[-1]
    rows = -(-n // (cols * row_multiple)) * row_multiple
    pad = [(0, 0)] * (flat.ndim - 1) + [(0, rows * cols - n)]
    return jnp.pad(flat, pad).reshape(flat.shape[:-1] + (rows, cols))


def pack_weight_shards(a, i):
    parts = [a[n][i].astype(BF16).reshape(-1) for n, _ in BIG]
    parts.append(lax.bitcast_convert_type(a["conv_w"][i], BF16).reshape(-1))
    return _to_rows(jnp.concatenate(parts), LANES, 32)


def unpack_gathered(g, shard_shapes):
    g = g.reshape(N_CHIPS, -1)
    off, out = 0, {}
    for n, ax in BIG:
        r, c = shard_shapes[n]
        blk = g[:, off:off + r * c].reshape(N_CHIPS, r, c)
        off += r * c
        out[n] = blk.reshape(N_CHIPS * r, c) if ax == 0 else blk.transpose(1, 0, 2).reshape(r, N_CHIPS * c)
    r, c = shard_shapes["conv_w"]
    blk = lax.bitcast_convert_type(g[:, off:off + 2 * r * c].reshape(N_CHIPS, r, c, 2), F32)
    out["conv_w"] = blk.transpose(1, 0, 2).reshape(r, N_CHIPS * c)
    return out


def pack_grads(full, shard_shapes):
    parts = []
    for n, ax in SHARDED:
        r, c = shard_shapes[n]
        blk = full[n].reshape(N_CHIPS, r, c) if ax == 0 else full[n].reshape(r, N_CHIPS, c).transpose(1, 0, 2)
        parts.append(blk.reshape(N_CHIPS, r * c))
    return _to_rows(jnp.concatenate(parts, axis=1), LANES, 16)


def unpack_reduced(red, shard_shapes):
    red = red.reshape(-1)
    off, out = 0, {}
    for n, _ in SHARDED:
        r, c = shard_shapes[n]
        out[n] = red[off:off + r * c].reshape(r, c)
        off += r * c
    return out


def kernel(x, p, positions, norm_mix_w, w_in, q_a_norm_w, w_uq, kv_a_norm_w, w_ukv, q_norm_w, k_norm_w, w_o_mla, conv_w, conv_b, dt_bias, a_log, d_skip, ssm_norm_w, w_o_ssm, w_out, norm_mlp_w, w_up, w_down, ple_norm_w, w_ple_gate, w_ple, loss_target, m_norm_mix_w, m_w_in, m_q_a_norm_w, m_w_uq, m_kv_a_norm_w, m_w_ukv, m_q_norm_w, m_k_norm_w, m_w_o_mla, m_conv_w, m_conv_b, m_dt_bias, m_a_log, m_d_skip, m_ssm_norm_w, m_w_o_ssm, m_w_out, m_norm_mlp_w, m_w_up, m_w_down, m_ple_norm_w, m_w_ple_gate, m_w_ple, v_norm_mix_w, v_w_in, v_q_a_norm_w, v_w_uq, v_kv_a_norm_w, v_w_ukv, v_q_norm_w, v_k_norm_w, v_w_o_mla, v_conv_w, v_conv_b, v_dt_bias, v_a_log, v_d_skip, v_ssm_norm_w, v_w_o_ssm, v_w_out, v_norm_mlp_w, v_w_up, v_w_down, v_ple_norm_w, v_w_ple_gate, v_w_ple):
    a = dict(locals())
    x, p, pos, target = a["x"][0], a["p"][:, 0], a["positions"][0], a["loss_target"][0]
    depth = a["w_in"].shape[0]
    shard_shapes = {n: tuple(a[n].shape[1:]) for n, _ in SHARDED}
    cos_full, sin_full = rope_tables_full(pos)
    rot = rope_matrix()

    wb, small, sinks = [], [], []
    for i in range(depth):
        full = unpack_gathered(gather_shards(pack_weight_shards(a, i), name="gather_weights"), shard_shapes)
        w_i = split_w_in(full["w_in"])
        w_i.update({n: full[n] for n, _ in BIG if n != "w_in"})
        wb.append(w_i)
        sm_i = {n: a[n][i] for n in SMALL}
        sm_i["conv_w"] = full["conv_w"]
        small.append(sm_i)
        sinks.append({n: jnp.zeros(w_i[n].shape, F32) for n in LINEAR_NAMES})

    def forward(x, small, sinks):
        for i in range(depth):
            x = layer_forward(x, p[i], cos_full, sin_full, rot, wb[i], small[i], sinks[i])
        return x

    y, vjp = jax.vjp(forward, x, small, sinks)
    dy, loss_part = loss_and_cotangent(y, target)
    dx, d_small, d_sinks = vjp(dy)
    loss = lax.psum(loss_part, ("x", "y", "c"))

    per_layer = []
    for i in range(depth):
        g_i = {n: d_sinks[i][n] for n, _ in BIG if n != "w_in"}
        g_i["w_in"] = join_w_in_grads(d_sinks[i])
        g_i["conv_w"] = d_small[i]["conv_w"]
        per_layer.append(unpack_reduced(reduce_to_owner(pack_grads(g_i, shard_shapes), name="reduce_grads"), shard_shapes))
    grads = {n: jnp.stack([per_layer[i][n] for i in range(depth)]) for n, _ in SHARDED}

    flat = jnp.concatenate([d_small[i][n].reshape(-1) for i in range(depth) for n in SMALL])
    n_small = flat.shape[0]
    red = allreduce_small(_to_rows(flat, 128, 8), name="reduce_small").reshape(-1)[:n_small]
    per = n_small // depth
    off = 0
    for n in SMALL:
        width = a[n].shape[-1]
        grads[n] = jnp.stack([red[i * per + off:i * per + off + width] for i in range(depth)])
        off += width

    deltas, new_m, new_v = {}, {}, {}
    two_d = lambda t: t.reshape(-1, t.shape[-1])
    for n in WEIGHTS:
        d, m, v = adamw(two_d(a[n]), two_d(grads[n]), two_d(a["m_" + n]), two_d(a["v_" + n]), name="adamw")
        deltas[n], new_m[n], new_v[n] = d.reshape(a[n].shape), m.reshape(a[n].shape), v.reshape(a[n].shape)

    return (loss, dx[None], *[grads[n].reshape(a[n].shape) for n in WEIGHTS], *[deltas[n] for n in WEIGHTS],
            *[new_m[n] for n in WEIGHTS], *[new_v[n] for n in WEIGHTS])
```

```python
import functools

import jax
import jax.numpy as jnp
from jax import lax
from jax.experimental import pallas as pl
from jax.experimental.pallas import tpu as pltpu

F32 = jnp.float32
BF16 = jnp.bfloat16
HI = lax.Precision.HIGHEST

EPS = 1e-6
MLA_HEADS = 16
QK_NOPE = 128
QK_ROPE = 64
QK_DIM = QK_NOPE + QK_ROPE
V_DIM = 128
ROPE_THETA = 10000.0
ATT_CHUNK = 64
SSM_GROUPS = 8
SSM_HEADDIM = 64
SSM_STATE = 128
CONV_WIDTH = 4
ADAM_LR = 0.001
ADAM_B1 = 0.9
ADAM_B2 = 0.999
ADAM_EPS = 1e-08
ADAM_WD = 0.01
ADAM_STEP = 10

V7X_VMEM_LIMIT = 56 * 1024 * 1024


def _params(sem=None, **kw):
    return pltpu.CompilerParams(dimension_semantics=sem, vmem_limit_bytes=V7X_VMEM_LIMIT, **kw)


def _pick(n, prefs):
    for t in prefs:
        if n % t == 0:
            return t
    return n


def matmul(a, b, *, ta=False, tb=False, out_blocks=0, out_dtype=F32, name):
    m, k = (a.shape[1], a.shape[0]) if ta else a.shape
    blocked = b.ndim == 3
    if blocked:
        nb, rows, c = b.shape
        k2, n = (nb * c, rows) if tb else (rows, nb * c)
    else:
        k2, n = (b.shape[1], b.shape[0]) if tb else b.shape
    assert k == k2, (a.shape, b.shape, ta, tb)
    n_unit = n // out_blocks if out_blocks else (c if blocked and not tb else n)
    k_unit = c if blocked and tb else k
    tm = _pick(m, (1024, 512, 256, 128))
    tn = _pick(n_unit, (1024, 512, 256, 128))
    tk = _pick(k_unit, (512, 256, 128))
    nk = k // tk
    dn = (((0 if ta else 1,), (1 if tb else 0,)), ((), ()))

    def body(a_ref, b_ref, o_ref, acc_ref):
        kk = pl.program_id(2)
        part = lax.dot_general(a_ref[...].astype(BF16), b_ref[...].astype(BF16), dn, preferred_element_type=F32)

        @pl.when(kk == 0)
        def _():
            acc_ref[...] = part

        @pl.when(kk > 0)
        def _():
            acc_ref[...] += part

        @pl.when(kk == nk - 1)
        def _():
            o_ref[...] = acc_ref[...].astype(o_ref.dtype)

    a_spec = pl.BlockSpec((tk, tm), lambda i, j, kk: (kk, i)) if ta else pl.BlockSpec((tm, tk), lambda i, j, kk: (i, kk))
    if not blocked:
        b_spec = pl.BlockSpec((tn, tk), lambda i, j, kk: (j, kk)) if tb else pl.BlockSpec((tk, tn), lambda i, j, kk: (kk, j))
    elif tb:
        kb = c // tk
        b_spec = pl.BlockSpec((None, tn, tk), lambda i, j, kk: (kk // kb, j, kk % kb))
    else:
        cb = c // tn
        b_spec = pl.BlockSpec((None, tk, tn), lambda i, j, kk: (j // cb, kk, j % cb))
    if out_blocks:
        ob = n_unit // tn
        out_spec = pl.BlockSpec((None, tm, tn), lambda i, j, kk: (j // ob, i, j % ob))
        out_shape = jax.ShapeDtypeStruct((out_blocks, m, n_unit), out_dtype)
    else:
        out_spec = pl.BlockSpec((tm, tn), lambda i, j, kk: (i, j))
        out_shape = jax.ShapeDtypeStruct((m, n), out_dtype)
    return pl.pallas_call(
        body,
        name=name,
        grid=(m // tm, n // tn, nk),
        in_specs=[a_spec, b_spec],
        out_specs=out_spec,
        out_shape=out_shape,
        scratch_shapes=[pltpu.VMEM((tm, tn), F32)],
        compiler_params=_params(("parallel", "parallel", "arbitrary")),
    )(a, b)


def linear(a, w, sink, *, name):
    @jax.custom_vjp
    def op(a, w, sink):
        return matmul(a, w, name=name + "_fwd")

    def fwd(a, w, sink):
        return op(a, w, sink), (a, w)

    def bwd(res, ct):
        a, w = res
        da = matmul(ct, w, tb=True, name=name + "_bwd_da")
        dw = matmul(a, ct, ta=True, out_blocks=w.shape[0] if w.ndim == 3 else 0, name=name + "_bwd_dw")
        return da.astype(a.dtype), jnp.zeros_like(w), dw

    op.defvjp(fwd, bwd)
    return op(a, w, sink)


def _tiled_call(fn, tiled, whole, tile, name, n_acc):
    rows = tiled[0].shape[0]
    assert rows % tile == 0
    t_avals = [jax.ShapeDtypeStruct((tile,) + a.shape[1:], a.dtype) for a in tiled]
    w_avals = [jax.ShapeDtypeStruct(a.shape, a.dtype) for a in whole]
    outs = jax.eval_shape(fn, *t_avals, *w_avals)
    n_in = len(tiled) + len(whole)
    n_t = len(outs) - n_acc

    def body(*refs):
        res = fn(*[r[...] for r in refs[:n_in]])
        o_refs = refs[n_in:]
        for r, v in zip(o_refs[:n_t], res[:n_t]):
            r[...] = v.astype(r.dtype)
        if n_acc:
            first = pl.program_id(0) == 0

            @pl.when(first)
            def _():
                for r, v in zip(o_refs[n_t:], res[n_t:]):
                    r[...] = v.astype(F32)

            @pl.when(jnp.logical_not(first))
            def _():
                for r, v in zip(o_refs[n_t:], res[n_t:]):
                    r[...] += v.astype(F32)

    def tspec(a):
        nd = len(a.shape)
        return pl.BlockSpec((tile,) + tuple(a.shape[1:]), lambda i, nd=nd: (i,) + (0,) * (nd - 1))

    def wspec(a):
        nd = len(a.shape)
        return pl.BlockSpec(tuple(a.shape), lambda i, nd=nd: (0,) * nd)

    out_shape = [jax.ShapeDtypeStruct((rows,) + o.shape[1:], o.dtype) for o in outs[:n_t]]
    out_shape += [jax.ShapeDtypeStruct(o.shape, F32) for o in outs[n_t:]]
    out_specs = [tspec(o) for o in out_shape[:n_t]] + [wspec(o) for o in out_shape[n_t:]]
    return pl.pallas_call(
        body,
        name=name,
        grid=(rows // tile,),
        in_specs=[tspec(a) for a in tiled] + [wspec(a) for a in whole],
        out_specs=out_specs,
        out_shape=out_shape,
        compiler_params=_params(("arbitrary",) if n_acc else ("parallel",)),
    )(*tiled, *whole)


def rowwise(f, rows, consts, params, *, name, tile, tables=()):
    rows, consts, tables, params = tuple(rows), tuple(consts), tuple(tables), tuple(params)
    nr, nc, ntab, npar = len(rows), len(consts), len(tables), len(params)

    @jax.custom_vjp
    def op(rows, consts, tables, params):
        return tuple(_tiled_call(f, rows + consts, tables + params, tile, name + "_fwd", 0))

    def fwd(rows, consts, tables, params):
        return op(rows, consts, tables, params), (rows, consts, tables, params)

    def bwd(res, cts):
        rows, consts, tables, params = res
        ncts = len(cts)

        def g(*args):
            r = args[:nr]
            c = args[nr:nr + nc]
            ct = args[nr + nc:nr + nc + ncts]
            tab = args[nr + nc + ncts:nr + nc + ncts + ntab]
            p = args[nr + nc + ncts + ntab:]
            _, vjp = jax.vjp(lambda *rp: f(*rp[:nr], *c, *tab, *rp[nr:]), *r, *p)
            return tuple(vjp(tuple(ct)))

        outs = _tiled_call(g, rows + consts + tuple(cts), tables + params, tile, name + "_bwd", npar)
        d_rows = tuple(o.astype(r.dtype) for o, r in zip(outs[:nr], rows))
        d_params = tuple(o.astype(p.dtype) for o, p in zip(outs[nr:], params))
        zeros = lambda xs: tuple(jnp.zeros_like(a) for a in xs)
        return d_rows, zeros(consts), zeros(tables), d_params

    op.defvjp(fwd, bwd)
    return op(rows, consts, tables, params)


ATT_TILE = 512
_NT = (((1,), (1,)), ((), ()))
_TN = (((0,), (0,)), ((), ()))


def _chunk_mask(row0, col0, shape):
    r = (row0 + lax.broadcasted_iota(jnp.int32, shape, 0)) // ATT_CHUNK
    c = (col0 + lax.broadcasted_iota(jnp.int32, shape, 1)) // ATT_CHUNK
    return c <= r


def _attention_fwd(q, k, v, name):
    h, s, dq = q.shape
    dv = v.shape[-1]
    t = min(ATT_TILE, s)
    scale = dq ** -0.5

    def body(q_ref, k_ref, v_ref, o_ref, lse_ref, k_scr, v_scr):
        i = pl.program_id(1)

        @pl.when(i == 0)
        def _():
            k_scr[...] = k_ref[...].astype(BF16)
            v_scr[...] = v_ref[...].astype(BF16)

        qb = q_ref[...].astype(BF16)

        def block(j, carry, masked):
            m, l, acc = carry
            off = pl.multiple_of(j * t, t)
            kj = k_scr[pl.ds(off, t), :]
            vj = v_scr[pl.ds(off, t), :]
            sc = lax.dot_general(qb, kj, _NT, preferred_element_type=F32) * scale
            if masked:
                sc = jnp.where(_chunk_mask(i * t, j * t, sc.shape), sc, -jnp.inf)
            m_new = jnp.maximum(m, jnp.max(sc, axis=1, keepdims=True))
            p = jnp.exp(sc - m_new)
            alpha = jnp.exp(m - m_new)
            l = alpha * l + jnp.sum(p, axis=1, keepdims=True)
            acc = alpha * acc + jnp.dot(p.astype(BF16), vj, preferred_element_type=F32)
            return m_new, l, acc

        init = (jnp.full((t, 1), -jnp.inf, F32), jnp.zeros((t, 1), F32), jnp.zeros((t, dv), F32))
        carry = lax.fori_loop(0, i, lambda j, c: block(j, c, False), init)
        m, l, acc = block(i, carry, True)
        o_ref[...] = acc / l
        lse_ref[...] = m + jnp.log(l)

    return pl.pallas_call(
        body,
        name=name,
        grid=(h, s // t),
        in_specs=[
            pl.BlockSpec((None, t, dq), lambda hh, i: (hh, i, 0)),
            pl.BlockSpec((None, s, dq), lambda hh, i: (hh, 0, 0)),
            pl.BlockSpec((None, s, dv), lambda hh, i: (hh, 0, 0)),
        ],
        out_specs=[
            pl.BlockSpec((None, t, dv), lambda hh, i: (hh, i, 0)),
            pl.BlockSpec((None, t, 1), lambda hh, i: (hh, i, 0)),
        ],
        out_shape=[jax.ShapeDtypeStruct((h, s, dv), F32), jax.ShapeDtypeStruct((h, s, 1), F32)],
        scratch_shapes=[pltpu.VMEM((s, dq), BF16), pltpu.VMEM((s, dv), BF16)],
        compiler_params=_params(("parallel", "arbitrary")),
    )(q, k, v)


def _attention_bwd_dq(q, k, v, o, lse, do, name):
    h, s, dq = q.shape
    dv = v.shape[-1]
    t = min(ATT_TILE, s)
    scale = dq ** -0.5

    def body(q_ref, k_ref, v_ref, o_ref, lse_ref, do_ref, dq_ref, delta_ref, k_scr, v_scr):
        i = pl.program_id(1)

        @pl.when(i == 0)
        def _():
            k_scr[...] = k_ref[...].astype(BF16)
            v_scr[...] = v_ref[...].astype(BF16)

        qb = q_ref[...].astype(BF16)
        dof = do_ref[...]
        dob = dof.astype(BF16)
        lse_v = lse_ref[...]
        delta = jnp.sum(dof * o_ref[...], axis=1, keepdims=True)
        delta_ref[...] = delta

        def block(j, acc, masked):
            off = pl.multiple_of(j * t, t)
            kj = k_scr[pl.ds(off, t), :]
            vj = v_scr[pl.ds(off, t), :]
            sc = lax.dot_general(qb, kj, _NT, preferred_element_type=F32) * scale
            p = jnp.exp(sc - lse_v)
            if masked:
                p = jnp.where(_chunk_mask(i * t, j * t, sc.shape), p, 0.0)
            dp = lax.dot_general(dob, vj, _NT, preferred_element_type=F32)
            ds = p * (dp - delta) * scale
            return acc + jnp.dot(ds.astype(BF16), kj, preferred_element_type=F32)

        acc = lax.fori_loop(0, i, lambda j, c: block(j, c, False), jnp.zeros((t, dq), F32))
        dq_ref[...] = block(i, acc, True)

    tile = lambda d: pl.BlockSpec((None, t, d), lambda hh, i: (hh, i, 0))
    whole = lambda d: pl.BlockSpec((None, s, d), lambda hh, i: (hh, 0, 0))
    return pl.pallas_call(
        body,
        name=name,
        grid=(h, s // t),
        in_specs=[tile(dq), whole(dq), whole(dv), tile(dv), tile(1), tile(dv)],
        out_specs=[tile(dq), tile(1)],
        out_shape=[jax.ShapeDtypeStruct((h, s, dq), F32), jax.ShapeDtypeStruct((h, s, 1), F32)],
        scratch_shapes=[pltpu.VMEM((s, dq), BF16), pltpu.VMEM((s, dv), BF16)],
        compiler_params=_params(("parallel", "arbitrary")),
    )(q, k, v, o, lse, do)


def _attention_bwd_dkv(q, k, v, lse, delta, do, name):
    h, s, dq = q.shape
    dv = v.shape[-1]
    t = min(ATT_TILE, s)
    n = s // t
    scale = dq ** -0.5

    def body(q_ref, k_ref, v_ref, lse_ref, delta_ref, do_ref, dk_ref, dv_ref, q_scr, do_scr):
        j = pl.program_id(1)

        @pl.when(j == 0)
        def _():
            q_scr[...] = q_ref[...].astype(BF16)
            do_scr[...] = do_ref[...].astype(BF16)

        kb = k_ref[...].astype(BF16)
        vb = v_ref[...].astype(BF16)

        def block(i, carry, masked):
            dk, dvv = carry
            off = pl.multiple_of(i * t, t)
            qi = q_scr[pl.ds(off, t), :]
            doi = do_scr[pl.ds(off, t), :]
            sc = lax.dot_general(qi, kb, _NT, preferred_element_type=F32) * scale
            p = jnp.exp(sc - lse_ref[pl.ds(off, t), :])
            if masked:
                p = jnp.where(_chunk_mask(i * t, j * t, sc.shape), p, 0.0)
            dp = lax.dot_general(doi, vb, _NT, preferred_element_type=F32)
            ds = p * (dp - delta_ref[pl.ds(off, t), :]) * scale
            dvv = dvv + lax.dot_general(p.astype(BF16), doi, _TN, preferred_element_type=F32)
            dk = dk + lax.dot_general(ds.astype(BF16), qi, _TN, preferred_element_type=F32)
            return dk, dvv

        carry = block(j, (jnp.zeros((t, dq), F32), jnp.zeros((t, dv), F32)), True)
        dk, dvv = lax.fori_loop(j + 1, n, lambda i, c: block(i, c, False), carry)
        dk_ref[...] = dk
        dv_ref[...] = dvv

    tile = lambda d: pl.BlockSpec((None, t, d), lambda hh, j: (hh, j, 0))
    whole = lambda d: pl.BlockSpec((None, s, d), lambda hh, j: (hh, 0, 0))
    return pl.pallas_call(
        body,
        name=name,
        grid=(h, n),
        in_specs=[whole(dq), tile(dq), tile(dv), whole(1), whole(1), whole(dv)],
        out_specs=[tile(dq), tile(dv)],
        out_shape=[jax.ShapeDtypeStruct((h, s, dq), F32), jax.ShapeDtypeStruct((h, s, dv), F32)],
        scratch_shapes=[pltpu.VMEM((s, dq), BF16), pltpu.VMEM((s, dv), BF16)],
        compiler_params=_params(("parallel", "arbitrary")),
    )(q, k, v, lse, delta, do)


def attention(q, k, v, *, name):
    @jax.custom_vjp
    def op(q, k, v):
        return _attention_fwd(q, k, v, name + "_fwd")[0]

    def fwd(q, k, v):
        o, lse = _attention_fwd(q, k, v, name + "_fwd")
        return o, (q, k, v, o, lse)

    def bwd(res, do):
        q, k, v, o, lse = res
        dq, delta = _attention_bwd_dq(q, k, v, o, lse, do, name + "_bwd_dq")
        dk, dv = _attention_bwd_dkv(q, k, v, lse, delta, do, name + "_bwd_dkv")
        return dq, dk, dv

    op.defvjp(fwd, bwd)
    return op(q, k, v)


CONV_HALO = 8


def _conv_tiles(s, c):
    return min(512, s), _pick(c, (512, 256, 128))


def _conv_fwd(x, w, name):
    s, c = x.shape
    ts, tc = _conv_tiles(s, c)
    nb = ts // CONV_HALO

    def body(xc_ref, xp_ref, w_ref, o_ref):
        t = pl.program_id(1)
        prev = jnp.where(t > 0, xp_ref[...], 0.0)
        xe = jnp.concatenate([prev, xc_ref[...]], axis=0)
        wv = w_ref[...]
        acc = jnp.zeros((ts, tc), F32)
        for tap in range(CONV_WIDTH):
            k = CONV_WIDTH - 1 - tap
            sh = xe if k == 0 else pltpu.roll(xe, k, axis=0)
            acc = acc + sh[CONV_HALO:, :] * wv[tap:tap + 1, :]
        o_ref[...] = acc

    return pl.pallas_call(
        body,
        name=name,
        grid=(c // tc, s // ts),
        in_specs=[
            pl.BlockSpec((ts, tc), lambda ci, t: (t, ci)),
            pl.BlockSpec((CONV_HALO, tc), lambda ci, t: (jnp.maximum(t * nb - 1, 0), ci)),
            pl.BlockSpec((CONV_WIDTH, tc), lambda ci, t: (0, ci)),
        ],
        out_specs=pl.BlockSpec((ts, tc), lambda ci, t: (t, ci)),
        out_shape=jax.ShapeDtypeStruct((s, c), F32),
        compiler_params=_params(("parallel", "parallel")),
    )(x, x, w)


def _conv_bwd(x, w, dy, name):
    s, c = x.shape
    ts, tc = _conv_tiles(s, c)
    nb = ts // CONV_HALO
    nt = s // ts

    def body(xc_ref, xp_ref, w_ref, dc_ref, dn_ref, dx_ref, dw_ref):
        t = pl.program_id(1)
        prev = jnp.where(t > 0, xp_ref[...], 0.0)
        xe = jnp.concatenate([prev, xc_ref[...]], axis=0)
        dcur = dc_ref[...]
        nxt = jnp.where(t < nt - 1, dn_ref[...], 0.0)
        de = jnp.concatenate([dcur, nxt], axis=0)
        wv = w_ref[...]
        dx = jnp.zeros((ts, tc), F32)
        dw = jnp.zeros((CONV_WIDTH, tc), F32)
        tap_row = lax.broadcasted_iota(jnp.int32, (CONV_WIDTH, tc), 0)
        for tap in range(CONV_WIDTH):
            k = CONV_WIDTH - 1 - tap
            dsh = de if k == 0 else pltpu.roll(de, ts + CONV_HALO - k, axis=0)
            dx = dx + dsh[:ts, :] * wv[tap:tap + 1, :]
            xsh = xe if k == 0 else pltpu.roll(xe, k, axis=0)
            dwt = jnp.sum(xsh[CONV_HALO:, :] * dcur, axis=0, keepdims=True)
            dw = jnp.where(tap_row == tap, dwt, dw)
        dx_ref[...] = dx

        @pl.when(t == 0)
        def _():
            dw_ref[...] = dw

        @pl.when(t > 0)
        def _():
            dw_ref[...] += dw

    return pl.pallas_call(
        body,
        name=name,
        grid=(c // tc, nt),
        in_specs=[
            pl.BlockSpec((ts, tc), lambda ci, t: (t, ci)),
            pl.BlockSpec((CONV_HALO, tc), lambda ci, t: (jnp.maximum(t * nb - 1, 0), ci)),
            pl.BlockSpec((CONV_WIDTH, tc), lambda ci, t: (0, ci)),
            pl.BlockSpec((ts, tc), lambda ci, t: (t, ci)),
            pl.BlockSpec((CONV_HALO, tc), lambda ci, t: (jnp.minimum((t + 1) * nb, s // CONV_HALO - 1), ci)),
        ],
        out_specs=[
            pl.BlockSpec((ts, tc), lambda ci, t: (t, ci)),
            pl.BlockSpec((CONV_WIDTH, tc), lambda ci, t: (0, ci)),
        ],
        out_shape=[jax.ShapeDtypeStruct((s, c), F32), jax.ShapeDtypeStruct((CONV_WIDTH, c), F32)],
        compiler_params=_params(("parallel", "arbitrary")),
    )(x, x, w, dy, dy)


def causal_conv(x, w, *, name):
    @jax.custom_vjp
    def op(x, w):
        return _conv_fwd(x, w, name + "_fwd")

    def fwd(x, w):
        return op(x, w), (x, w)

    def bwd(res, dy):
        x, w = res
        dx, dw = _conv_bwd(x, w, dy, name + "_bwd")
        return dx, dw

    op.defvjp(fwd, bwd)
    return op(x, w)


SSD_T = 128
SSD_R = 8
SSD_GW = SSD_R * SSM_HEADDIM


def _ssd_consts(t):
    r = lax.broadcasted_iota(jnp.int32, (t, t), 0)
    c = lax.broadcasted_iota(jnp.int32, (t, t), 1)
    tril = (c <= r).astype(F32)
    triu = (r <= c).astype(F32)
    head_of_lane = lax.broadcasted_iota(jnp.int32, (SSD_R, SSD_GW), 1) // SSM_HEADDIM
    expand = (head_of_lane == lax.broadcasted_iota(jnp.int32, (SSD_R, SSD_GW), 0)).astype(F32)
    return c <= r, tril, triu, expand


def _hi(a, b):
    return jnp.dot(a, b, precision=HI, preferred_element_type=F32)


def _hi_nt(a, b):
    return lax.dot_general(a, b, _NT, precision=HI, preferred_element_type=F32)


def _bdot(a, b, dn=None):
    if dn is None:
        return jnp.dot(a.astype(BF16), b.astype(BF16), preferred_element_type=F32)
    return lax.dot_general(a.astype(BF16), b.astype(BF16), dn, preferred_element_type=F32)


def _ssd_chunk_common(x_ref, b_ref, c_ref, dtc_ref, dtr_ref, alc_ref, alr_ref, t):
    mask, tril, triu, expand = _ssd_consts(t)
    x, bm, cm = x_ref[...], b_ref[...], c_ref[...]
    dtc, dtr = dtc_ref[...], dtr_ref[...]
    neg_a_c = -jnp.exp(alc_ref[...])
    neg_a_r = -jnp.exp(alr_ref[...])
    acum_c = _hi(tril, dtc * neg_a_c)
    acum_r = _hi(dtr * neg_a_r, triu)
    s_cb = _bdot(cm, bm, _NT)
    return mask, tril, triu, expand, x, bm, cm, dtc, dtr, neg_a_c, neg_a_r, acum_c, acum_r, s_cb


def _head_decay(mask, acum_c, acum_r, h):
    seg = acum_c[:, h:h + 1] - acum_r[h:h + 1, :]
    return jnp.exp(jnp.where(mask, seg, -jnp.inf))


def _ssd_fwd(xbc, dtc, dtr, alc, alr, dexp, name):
    s = xbc.shape[0]
    g = SSM_GROUPS
    t = min(SSD_T, s)
    nc = s // t
    n = SSM_STATE
    xblocks = (g * SSD_GW) // n

    def body(x_ref, b_ref, c_ref, dtc_ref, dtr_ref, alc_ref, alr_ref, d_ref, y_ref, hs_ref, h_scr):
        ci = pl.program_id(1)

        @pl.when(ci == 0)
        def _():
            h_scr[...] = jnp.zeros_like(h_scr)

        (mask, tril, triu, expand, x, bm, cm, dtc_v, dtr_v, _, _, acum_c, acum_r, s_cb) = _ssd_chunk_common(
            x_ref, b_ref, c_ref, dtc_ref, dtr_ref, alc_ref, alr_ref, t)
        hst = h_scr[...]
        hs_ref[...] = hst
        ch = _bdot(cm, hst)
        y = _hi(jnp.exp(acum_c), expand) * ch + d_ref[...] * x
        half = lax.broadcasted_iota(jnp.int32, (t, 2 * SSM_HEADDIM), 1) // SSM_HEADDIM
        parts = []
        for j in range(SSD_R // 2):
            xp = x[:, j * 128:(j + 1) * 128]
            acc = jnp.zeros((t, 128), F32)
            for hh in range(2):
                h = 2 * j + hh
                m = s_cb * _head_decay(mask, acum_c, acum_r, h) * dtr_v[h:h + 1, :]
                acc = acc + _bdot(m, jnp.where(half == hh, xp, 0.0))
            parts.append(acc)
        y_ref[...] = y + jnp.concatenate(parts, axis=1)
        last = acum_c[t - 1:t, :]
        w_c = jnp.exp(last - acum_c) * dtc_v
        dec = _hi(jnp.broadcast_to(jnp.exp(last), (SSD_R, SSD_R)), expand)[0:1, :]
        h_scr[...] = dec * hst + _bdot(bm, _hi(w_c, expand) * x, _TN)

    return pl.pallas_call(
        body,
        name=name,
        grid=(g, nc),
        in_specs=[
            pl.BlockSpec((t, SSD_GW), lambda gi, ci: (ci, gi)),
            pl.BlockSpec((t, n), lambda gi, ci: (ci, xblocks + gi)),
            pl.BlockSpec((t, n), lambda gi, ci: (ci, xblocks + g + gi)),
            pl.BlockSpec((None, t, SSD_R), lambda gi, ci: (gi, ci, 0)),
            pl.BlockSpec((None, SSD_R, t), lambda gi, ci: (gi, 0, ci)),
            pl.BlockSpec((None, 1, SSD_R), lambda gi, ci: (gi, 0, 0)),
            pl.BlockSpec((None, SSD_R, 1), lambda gi, ci: (gi, 0, 0)),
            pl.BlockSpec((None, 1, SSD_GW), lambda gi, ci: (gi, 0, 0)),
        ],
        out_specs=[
            pl.BlockSpec((t, SSD_GW), lambda gi, ci: (ci, gi)),
            pl.BlockSpec((None, None, n, SSD_GW), lambda gi, ci: (gi, ci, 0, 0)),
        ],
        out_shape=[jax.ShapeDtypeStruct((s, g * SSD_GW), F32), jax.ShapeDtypeStruct((g, nc, n, SSD_GW), F32)],
        scratch_shapes=[pltpu.VMEM((n, SSD_GW), F32)],
        compiler_params=_params(("parallel", "arbitrary")),
    )(xbc, xbc, xbc, dtc, dtr, alc, alr, dexp)


def _ssd_bwd(xbc, dtc, dtr, alc, alr, dexp, hs, dy, name):
    s = xbc.shape[0]
    g = SSM_GROUPS
    t = min(SSD_T, s)
    nc = s // t
    n = SSM_STATE
    xblocks = (g * SSD_GW) // n

    def body(x_ref, b_ref, c_ref, dtc_ref, dtr_ref, alc_ref, alr_ref, d_ref, hs_ref, dy_ref,
             dx_ref, db_ref, dc_ref, ddtc_ref, ddtr_ref, dalc_ref, dalr_ref, dd_ref, dh_scr):
        ci = pl.program_id(1)

        @pl.when(ci == 0)
        def _():
            dh_scr[...] = jnp.zeros_like(dh_scr)

        (mask, tril, triu, expand, x, bm, cm, dtc_v, dtr_v, neg_a_c, neg_a_r, acum_c, acum_r, s_cb) = _ssd_chunk_common(
            x_ref, b_ref, c_ref, dtc_ref, dtr_ref, alc_ref, alr_ref, t)
        hst = hs_ref[...]
        dhn = dh_scr[...]
        dy = dy_ref[...]
        ch = _bdot(cm, hst)
        scale_full = _hi(jnp.exp(acum_c), expand)
        sdy = scale_full * dy
        d_c = _bdot(sdy, hst, _NT)
        dh_prev = _bdot(cm, sdy, _TN)
        dacum_c = _hi_nt(sdy * ch, expand)
        dx = d_ref[...] * dy
        dd = jnp.sum(dy * x, axis=0, keepdims=True)
        last = acum_c[t - 1:t, :]
        e_last = jnp.exp(last)
        dec = _hi(jnp.broadcast_to(e_last, (SSD_R, SSD_R)), expand)[0:1, :]
        dh_prev = dh_prev + dec * dhn
        ddec = jnp.sum(hst * dhn, axis=0, keepdims=True)
        dlast = _hi_nt(jnp.broadcast_to(ddec, (SSD_R, SSD_GW)), expand)[0:1, :] * e_last
        w_e = jnp.exp(last - acum_c)
        w_c = w_e * dtc_v
        wfull = _hi(w_c, expand)
        z = _bdot(bm, dhn)
        dx = dx + wfull * z
        dw_c = _hi_nt(x * z, expand)
        ddt_c = dw_c * w_e
        q_c = dw_c * w_c
        dacum_c = dacum_c - q_c
        dlast = dlast + jnp.sum(q_c, axis=0, keepdims=True)
        d_b = _bdot(wfull * x, dhn, _NT)
        half = lax.broadcasted_iota(jnp.int32, (t, 2 * SSM_HEADDIM), 1) // SSM_HEADDIM
        lane8 = lax.broadcasted_iota(jnp.int32, (t, SSD_R), 1)
        row8 = lax.broadcasted_iota(jnp.int32, (SSD_R, t), 0)
        ds_cb = jnp.zeros((t, t), F32)
        dacum_r = jnp.zeros((SSD_R, t), F32)
        ddt_r = jnp.zeros((SSD_R, t), F32)
        parts = []
        for j in range(SSD_R // 2):
            xp = x[:, j * 128:(j + 1) * 128]
            dyp = dy[:, j * 128:(j + 1) * 128]
            dxp = jnp.zeros((t, 128), F32)
            for hh in range(2):
                h = 2 * j + hh
                dts = dtr_v[h:h + 1, :]
                decay = _head_decay(mask, acum_c, acum_r, h)
                sl = s_cb * decay
                m = sl * dts
                xm = jnp.where(half == hh, xp, 0.0)
                dym = jnp.where(half == hh, dyp, 0.0)
                dxp = dxp + _bdot(m, dym, _TN)
                dm = _bdot(dym, xm, _NT)
                ds_cb = ds_cb + dm * decay * dts
                q = dm * m
                dacum_c = dacum_c + jnp.where(lane8 == h, jnp.sum(q, axis=1, keepdims=True), 0.0)
                dacum_r = dacum_r - jnp.where(row8 == h, jnp.sum(q, axis=0, keepdims=True), 0.0)
                ddt_r = ddt_r + jnp.where(row8 == h, jnp.sum(dm * sl, axis=0, keepdims=True), 0.0)
            parts.append(dxp)
        dx_ref[...] = dx + jnp.concatenate(parts, axis=1)
        dc_ref[...] = d_c + _bdot(ds_cb, bm)
        db_ref[...] = d_b + _bdot(ds_cb, cm, _TN)
        row_t = lax.broadcasted_iota(jnp.int32, (t, SSD_R), 0)
        dacum_c = dacum_c + jnp.where(row_t == t - 1, dlast, 0.0)
        da_c = _hi(triu, dacum_c)
        da_r = _hi(dacum_r, tril)
        ddtc_ref[...] = ddt_c + da_c * neg_a_c
        ddtr_ref[...] = ddt_r + da_r * neg_a_r
        dal_c = jnp.sum(da_c * dtc_v, axis=0, keepdims=True) * neg_a_c
        dal_r = jnp.sum(da_r * dtr_v, axis=1, keepdims=True) * neg_a_r
        dh_scr[...] = dh_prev

        @pl.when(ci == 0)
        def _():
            dalc_ref[...] = dal_c
            dalr_ref[...] = dal_r
            dd_ref[...] = dd

        @pl.when(ci > 0)
        def _():
            dalc_ref[...] += dal_c
            dalr_ref[...] += dal_r
            dd_ref[...] += dd

    rev = lambda ci: nc - 1 - ci
    return pl.pallas_call(
        body,
        name=name,
        grid=(g, nc),
        in_specs=[
            pl.BlockSpec((t, SSD_GW), lambda gi, ci: (rev(ci), gi)),
            pl.BlockSpec((t, n), lambda gi, ci: (rev(ci), xblocks + gi)),
            pl.BlockSpec((t, n), lambda gi, ci: (rev(ci), xblocks + g + gi)),
            pl.BlockSpec((None, t, SSD_R), lambda gi, ci: (gi, rev(ci), 0)),
            pl.BlockSpec((None, SSD_R, t), lambda gi, ci: (gi, 0, rev(ci))),
            pl.BlockSpec((None, 1, SSD_R), lambda gi, ci: (gi, 0, 0)),
            pl.BlockSpec((None, SSD_R, 1), lambda gi, ci: (gi, 0, 0)),
            pl.BlockSpec((None, 1, SSD_GW), lambda gi, ci: (gi, 0, 0)),
            pl.BlockSpec((None, None, n, SSD_GW), lambda gi, ci: (gi, rev(ci), 0, 0)),
            pl.BlockSpec((t, SSD_GW), lambda gi, ci: (rev(ci), gi)),
        ],
        out_specs=[
            pl.BlockSpec((t, SSD_GW), lambda gi, ci: (rev(ci), gi)),
            pl.BlockSpec((t, n), lambda gi, ci: (rev(ci), gi)),
            pl.BlockSpec((t, n), lambda gi, ci: (rev(ci), gi)),
            pl.BlockSpec((None, t, SSD_R), lambda gi, ci: (gi, rev(ci), 0)),
            pl.BlockSpec((None, SSD_R, t), lambda gi, ci: (gi, 0, rev(ci))),
            pl.BlockSpec((None, 1, SSD_R), lambda gi, ci: (gi, 0, 0)),
            pl.BlockSpec((None, SSD_R, 1), lambda gi, ci: (gi, 0, 0)),
            pl.BlockSpec((None, 1, SSD_GW), lambda gi, ci: (gi, 0, 0)),
        ],
        out_shape=[
            jax.ShapeDtypeStruct((s, g * SSD_GW), F32),
            jax.ShapeDtypeStruct((s, g * n), F32),
            jax.ShapeDtypeStruct((s, g * n), F32),
            jax.ShapeDtypeStruct((g, s, SSD_R), F32),
            jax.ShapeDtypeStruct((g, SSD_R, s), F32),
            jax.ShapeDtypeStruct((g, 1, SSD_R), F32),
            jax.ShapeDtypeStruct((g, SSD_R, 1), F32),
            jax.ShapeDtypeStruct((g, 1, SSD_GW), F32),
        ],
        scratch_shapes=[pltpu.VMEM((n, SSD_GW), F32)],
        compiler_params=_params(("parallel", "arbitrary")),
    )(xbc, xbc, xbc, dtc, dtr, alc, alr, dexp, hs, dy)


def ssd_core(xbc, dtc, dtr, alc, alr, dexp, *, name):
    @jax.custom_vjp
    def op(xbc, dtc, dtr, alc, alr, dexp):
        return _ssd_fwd(xbc, dtc, dtr, alc, alr, dexp, name + "_fwd")[0]

    def fwd(xbc, dtc, dtr, alc, alr, dexp):
        y, hs = _ssd_fwd(xbc, dtc, dtr, alc, alr, dexp, name + "_fwd")
        return y, (xbc, dtc, dtr, alc, alr, dexp, hs)

    def bwd(res, dy):
        xbc, dtc, dtr, alc, alr, dexp, hs = res
        dx, db, dc, ddtc, ddtr, dalc, dalr, dd = _ssd_bwd(xbc, dtc, dtr, alc, alr, dexp, hs, dy, name + "_bwd")
        return jnp.concatenate([dx, db, dc], axis=1), ddtc, ddtr, dalc, dalr, dd

    op.defvjp(fwd, bwd)
    return op(xbc, dtc, dtr, alc, alr, dexp)


def gate_norm(y, z, w):
    return (rms_norm(y * (z * jax.nn.sigmoid(z)), w),)


def ssd_branch(xbc, z, dt_raw, conv_w, conv_b, dt_bias, a_log, d_skip, norm_w, *, name):
    s = xbc.shape[0]
    g = SSM_GROUPS
    conv = causal_conv(xbc, conv_w, name=name + "_conv")
    (xc,) = rowwise(lambda c, b: ((c + b) * jax.nn.sigmoid(c + b),), (conv,), (), (conv_b[None, :],),
                    name=name + "_silu", tile=min(256, s))
    (dt,) = rowwise(lambda r, b: (jax.nn.softplus(r + b),), (dt_raw,), (), (dt_bias[None, :],),
                    name=name + "_dt", tile=min(512, s))
    dt3 = dt.reshape(s, g, SSD_R)
    y = ssd_core(xc, dt3.transpose(1, 0, 2), dt3.transpose(1, 2, 0), a_log.reshape(g, 1, SSD_R),
                 a_log.reshape(g, SSD_R, 1), jnp.repeat(d_skip, SSM_HEADDIM).reshape(g, 1, SSD_GW), name=name + "_core")
    (out,) = rowwise(gate_norm, (y.reshape(s, g, SSD_GW), z.reshape(s, g, SSD_GW)), (),
                     (norm_w.reshape(g, SSD_GW),), name=name + "_gate", tile=min(128, s))
    return out.reshape(s, g * SSD_GW)


def rms_norm(x, w):
    return x * lax.rsqrt(jnp.mean(x * x, axis=-1, keepdims=True) + EPS) * w


def rope_matrix():
    half = QK_ROPE // 2
    j = jnp.arange(QK_DIM)
    src = jnp.where(j < QK_NOPE + half, j + half, j - half)
    sign = jnp.where(j < QK_NOPE, 0.0, jnp.where(j < QK_NOPE + half, -1.0, 1.0))
    return (jnp.arange(QK_DIM)[:, None] == src[None, :]).astype(F32) * sign[None, :]


def rope_tables_full(positions):
    inv_freq = 1.0 / (ROPE_THETA ** (jnp.arange(0, QK_ROPE, 2, dtype=F32) / QK_ROPE))
    ang = positions.astype(F32)[:, None] * inv_freq
    s = positions.shape[0]
    cos = jnp.concatenate([jnp.ones((s, QK_NOPE), F32), jnp.cos(ang), jnp.cos(ang)], axis=-1)
    sin = jnp.concatenate([jnp.zeros((s, QK_NOPE), F32), jnp.sin(ang), jnp.sin(ang)], axis=-1)
    return cos[:, None, :], sin[:, None, :]


def head_norm_rope(x, cos_full, sin_full, rot, w):
    t, h, d = x.shape
    y = rms_norm(x, w)
    partner = jnp.dot(y.reshape(t * h, d), rot, precision=HI, preferred_element_type=F32).reshape(t, h, d)
    return (y * cos_full + partner * sin_full,)


def _norm(x, w, *, name, tile=256):
    (y,) = rowwise(lambda x, w: (rms_norm(x, w),), (x,), (), (w[None, :],), name=name, tile=min(tile, x.shape[0]))
    return y


Q_LORA = 512
KV_LORA = 512
W_IN_PIECES = ("cq", "ckv", "kr", "z", "xbc", "dt", "ga", "gb")


def w_in_widths(d_model):
    d_inner = 2 * d_model
    conv_dim = d_inner + 2 * SSM_GROUPS * SSM_STATE
    return (Q_LORA, KV_LORA, QK_ROPE, d_inner, conv_dim, d_inner // SSM_HEADDIM, d_model, d_model)


LINEAR_NAMES = ("cq", "ckv", "kd", "z", "xbc", "ga", "gb", "w_uq", "w_ukv", "w_o_mla", "w_o_ssm", "w_out", "w_up",
                "w_down", "w_ple_gate", "w_ple")


def layer_forward(x, p_i, cos_full, sin_full, rot, wb, sm, sinks):
    s, d = x.shape
    lin = lambda a, n: linear(a, wb[n], sinks[n], name="lin_" + n)
    h = _norm(x, sm["norm_mix_w"], name="norm_mix")
    c_q, c_kv, kd = lin(h, "cq"), lin(h, "ckv"), lin(h, "kd")
    z, xbc, g_a, g_b = lin(h, "z"), lin(h, "xbc"), lin(h, "ga"), lin(h, "gb")
    k_r, dt_raw = kd[:, :QK_ROPE], kd[:, QK_ROPE:]
    q = lin(_norm(c_q, sm["q_a_norm_w"], name="norm_qa"), "w_uq").reshape(s, MLA_HEADS, QK_DIM)
    kv = lin(_norm(c_kv, sm["kv_a_norm_w"], name="norm_kva"), "w_ukv").reshape(s, MLA_HEADS, QK_NOPE + V_DIM)
    k = jnp.concatenate([kv[..., :QK_NOPE], jnp.broadcast_to(k_r[:, None, :], (s, MLA_HEADS, QK_ROPE))], axis=-1)
    v = kv[..., QK_NOPE:]
    tq = min(128, s)
    (q,) = rowwise(head_norm_rope, (q,), (cos_full, sin_full), (sm["q_norm_w"][None, :],), tables=(rot,), name="q_rope", tile=tq)
    (k,) = rowwise(head_norm_rope, (k,), (cos_full, sin_full), (sm["k_norm_w"][None, :],), tables=(rot,), name="k_rope", tile=tq)
    hm = lambda a: a.transpose(1, 0, 2)
    o = hm(attention(hm(q), hm(k), hm(v), name="attn")).reshape(s, MLA_HEADS * V_DIM)
    y_a = lin(o, "w_o_mla")
    y_ssd = ssd_branch(xbc, z, dt_raw, sm["conv_w"], sm["conv_b"], sm["dt_bias"], sm["a_log"], sm["d_skip"],
                       sm["ssm_norm_w"], name="ssd")
    y_b = lin(y_ssd, "w_o_ssm")
    sig = jax.nn.sigmoid
    tr = min(256, s)
    (merged,) = rowwise(lambda ga, gb, ya, yb: (sig(ga) * ya + sig(gb) * yb,), (g_a, g_b, y_a, y_b), (), (),
                        name="merge", tile=tr)
    x = x + lin(merged, "w_out")
    up = lin(_norm(x, sm["norm_mlp_w"], name="norm_mlp"), "w_up")
    (act,) = rowwise(lambda u: (jnp.square(jnp.maximum(u, 0.0)),), (up,), (), (), name="relu2", tile=tr)
    x = x + lin(act, "w_down")
    pg = lin(_norm(x, sm["ple_norm_w"], name="norm_ple"), "w_ple_gate")
    pe = lin(p_i, "w_ple")
    (x,) = rowwise(lambda x, pe, pg: (x + pe * sig(pg),), (x, pe, pg), (), (), name="ple_add", tile=tr)
    return x


def loss_and_cotangent(y, target):
    s, d = y.shape

    def f(y, t):
        e = y - t
        return e * (1.0 / d), 0.5 * jnp.sum(jnp.sum(e * e, axis=1, keepdims=True) * (1.0 / d), axis=0, keepdims=True)

    dy, part = _tiled_call(f, (y, target), (), min(256, s), "loss", 1)
    return dy, part[0, 0]


ADAM_BLOCK_ELEMS = 256 * 1024


def adamw(w, g, m, v, *, name):
    rows, cols = w.shape
    budget = max(8, ADAM_BLOCK_ELEMS // cols)
    tile = _pick(rows, tuple(t for t in (512, 256, 128, 64, 32, 16, 8) if t <= budget))

    def f(w, g, m, v):
        m = ADAM_B1 * m + (1.0 - ADAM_B1) * g
        v = ADAM_B2 * v + (1.0 - ADAM_B2) * jnp.square(g)
        m_hat = m / (1.0 - ADAM_B1 ** ADAM_STEP)
        v_hat = v / (1.0 - ADAM_B2 ** ADAM_STEP)
        delta = -ADAM_LR * (m_hat / (jnp.sqrt(v_hat) + ADAM_EPS) + ADAM_WD * w)
        return delta, m, v

    return _tiled_call(f, (w, g, m, v), (), tile, name, 0)


MESH_ID = pl.DeviceIdType.MESH
N_CHIPS = 4
_ANY = pl.BlockSpec(memory_space=pl.ANY)


def _place():
    return lax.axis_index("x"), lax.axis_index("y"), lax.axis_index("c")


def _other_chips(x, y):
    return [(1 - x, y), (x, 1 - y), (1 - x, 1 - y)]


def _rcopy(src, dst, send_sem, recv_sem, device):
    return pltpu.make_async_remote_copy(src_ref=src, dst_ref=dst, send_sem=send_sem, recv_sem=recv_sem,
                                        device_id=device, device_id_type=MESH_ID)


def _sems(n, k):
    return pltpu.SemaphoreType.DMA((n, k))


def _comm_call(body, name, ins, out_shapes, scratch):
    return pl.pallas_call(
        body,
        name=name,
        in_specs=[_ANY] * len(ins),
        out_specs=[_ANY] * len(out_shapes),
        out_shape=out_shapes,
        scratch_shapes=scratch,
        compiler_params=pltpu.CompilerParams(has_side_effects=True),
    )(*ins)


SPLIT_ROWS = 32


def gather_shards(shards, *, name):
    n = len(shards)
    split = [s.shape[0] % SPLIT_ROWS == 0 for s in shards]

    def body(*refs):
        srcs, outs = refs[:n], refs[n:2 * n]
        send_sems, recv_sems, local_sems = refs[2 * n:]
        x, y, c = _place()
        sibling = (x, y, 1 - c)
        chips = _other_chips(x, y)
        me = 2 * x + y

        def part(t, slot, h):
            if not split[t]:
                return outs[t].at[slot]
            half = shards[t].shape[0] // 2
            return outs[t].at[slot, pl.ds(h * half, half), :]

        def own(t):
            if not split[t]:
                return srcs[t]
            half = shards[t].shape[0] // 2
            return srcs[t].at[pl.ds(c * half, half), :]

        mine = [pltpu.make_async_copy(srcs[t], outs[t].at[me], local_sems.at[t]) for t in range(n)]
        for cp in mine:
            cp.start()
        sent = []
        for t in range(n):
            for j, chip in enumerate(chips):
                sent.append(_rcopy(own(t), part(t, me, c), send_sems.at[t, j], recv_sems.at[t, j], (*chip, c)))
                sent[-1].start()
        for t in range(n):
            for j, (cx, cy) in enumerate(chips):
                got = part(t, 2 * cx + cy, c)
                _rcopy(got, got, send_sems.at[t, j], recv_sems.at[t, j], (cx, cy, c)).wait_recv()
                if split[t]:
                    sent.append(_rcopy(got, got, send_sems.at[t, 3 + j], recv_sems.at[t, 3 + j], sibling))
                    sent[-1].start()
        for t in range(n):
            if split[t]:
                for j, (cx, cy) in enumerate(chips):
                    got = part(t, 2 * cx + cy, 1 - c)
                    _rcopy(got, got, send_sems.at[t, 3 + j], recv_sems.at[t, 3 + j], sibling).wait_recv()
        for cp in sent:
            cp.wait_send()
        for cp in mine:
            cp.wait()

    out_shapes = [jax.ShapeDtypeStruct((N_CHIPS,) + s.shape, s.dtype) for s in shards]
    return _comm_call(body, name, shards, out_shapes, [_sems(n, 6), _sems(n, 6), pltpu.SemaphoreType.DMA((n,))])


def sibling_take_half(gs, *, name):
    n = len(gs)

    def body(*refs):
        g_refs, a_refs = refs[:n], refs[n:2 * n]
        send_sems, recv_sems = refs[2 * n:]
        x, y, c = _place()
        copies = []
        for t in range(n):
            half = gs[t].shape[1] // 2
            copies.append(_rcopy(g_refs[t].at[:, pl.ds((1 - c) * half, half), :], a_refs[t], send_sems.at[t, 0],
                                 recv_sems.at[t, 0], (x, y, 1 - c)))
            copies[-1].start()
        for cp in copies:
            cp.wait()

    out_shapes = [jax.ShapeDtypeStruct((g.shape[0], g.shape[1] // 2, g.shape[2]), g.dtype) for g in gs]
    return _comm_call(body, name, gs, out_shapes, [_sems(n, 1), _sems(n, 1)])


ELEMWISE_BLOCK_ELEMS = 256 * 1024


def _row_tile(rows, cols):
    budget = max(16, ELEMWISE_BLOCK_ELEMS // cols)
    return _pick(rows, tuple(t for t in (1024, 512, 256, 128, 64, 32, 16) if t <= budget))


def _core_and_chip():
    x, y, c = _place()
    return jnp.stack([c, 2 * x + y]).astype(jnp.int32)


def pair_add(g, a, *, name):
    n, rows, cols = g.shape
    half = rows // 2
    tile = _row_tile(half, cols)
    nb = half // tile

    def body(who_ref, g_ref, a_ref, o_ref):
        o_ref[...] = (g_ref[...] + a_ref[...]).astype(o_ref.dtype)

    return pl.pallas_call(
        body,
        name=name,
        grid_spec=pltpu.PrefetchScalarGridSpec(
            num_scalar_prefetch=1,
            grid=(n, nb),
            in_specs=[
                pl.BlockSpec((None, tile, cols), lambda j, i, who: (j, who[0] * nb + i, 0)),
                pl.BlockSpec((None, tile, cols), lambda j, i, who: (j, i, 0)),
            ],
            out_specs=pl.BlockSpec((None, tile, cols), lambda j, i, who: (j, i, 0)),
        ),
        out_shape=jax.ShapeDtypeStruct((n, half, cols), BF16),
        compiler_params=_params(("parallel", "parallel")),
    )(_core_and_chip(), g, a)


def exchange_chip_slots(ps, *, name):
    n = len(ps)

    def body(*refs):
        p_refs, b_refs = refs[:n], refs[n:2 * n]
        send_sems, recv_sems, local_sems = refs[2 * n:]
        x, y, c = _place()
        me = 2 * x + y
        chips = _other_chips(x, y)
        mine = [pltpu.make_async_copy(p_refs[t].at[me], b_refs[t].at[me], local_sems.at[t]) for t in range(n)]
        for cp in mine:
            cp.start()
        sends = []
        for t in range(n):
            for j, (cx, cy) in enumerate(chips):
                sends.append(_rcopy(p_refs[t].at[2 * cx + cy], b_refs[t].at[me], send_sems.at[t, j], recv_sems.at[t, j],
                                    (cx, cy, c)))
                sends[-1].start()
        for t in range(n):
            for j, (cx, cy) in enumerate(chips):
                got = b_refs[t].at[2 * cx + cy]
                _rcopy(got, got, send_sems.at[t, j], recv_sems.at[t, j], (cx, cy, c)).wait_recv()
        for cp in sends:
            cp.wait_send()
        for cp in mine:
            cp.wait()

    out_shapes = [jax.ShapeDtypeStruct(p.shape, p.dtype) for p in ps]
    return _comm_call(body, name, ps, out_shapes, [_sems(n, 3), _sems(n, 3), pltpu.SemaphoreType.DMA((n,))])


def chips_add(g, a, b, *, name):
    n, rows, cols = g.shape
    half = rows // 2
    tile = _row_tile(half, cols)
    nb = half // tile

    def body(who_ref, g_ref, a_ref, b_ref, o_ref):
        me = who_ref[1]
        own = g_ref[...] + a_ref[...]
        acc = None
        for j in range(n):
            term = jnp.where(me == j, own, b_ref[j].astype(F32))
            acc = term if acc is None else acc + term
        o_ref[...] = acc

    return pl.pallas_call(
        body,
        name=name,
        grid_spec=pltpu.PrefetchScalarGridSpec(
            num_scalar_prefetch=1,
            grid=(nb,),
            in_specs=[
                pl.BlockSpec((None, tile, cols), lambda i, who: (who[1], who[0] * nb + i, 0)),
                pl.BlockSpec((None, tile, cols), lambda i, who: (who[1], i, 0)),
                pl.BlockSpec((n, tile, cols), lambda i, who: (0, i, 0)),
            ],
            out_specs=pl.BlockSpec((tile, cols), lambda i, who: (i, 0)),
        ),
        out_shape=jax.ShapeDtypeStruct((half, cols), F32),
        compiler_params=_params(("parallel",)),
    )(_core_and_chip(), g, a, b)


def sibling_join_halves(fs, *, name):
    n = len(fs)

    def body(*refs):
        f_refs, r_refs = refs[:n], refs[n:2 * n]
        send_sems, recv_sems, local_sems = refs[2 * n:]
        x, y, c = _place()
        sibling = (x, y, 1 - c)
        copies = []
        for t in range(n):
            half = fs[t].shape[0]
            dst = r_refs[t].at[pl.ds(c * half, half), :]
            copies.append(pltpu.make_async_copy(f_refs[t], dst, local_sems.at[t]))
            copies[-1].start()
            copies.append(_rcopy(f_refs[t], dst, send_sems.at[t, 0], recv_sems.at[t, 0], sibling))
            copies[-1].start()
        for t in range(n):
            half = fs[t].shape[0]
            got = r_refs[t].at[pl.ds((1 - c) * half, half), :]
            _rcopy(got, got, send_sems.at[t, 0], recv_sems.at[t, 0], sibling).wait_recv()
        for t in range(n):
            copies[2 * t].wait()
            copies[2 * t + 1].wait_send()

    out_shapes = [jax.ShapeDtypeStruct((2 * f.shape[0], f.shape[1]), f.dtype) for f in fs]
    return _comm_call(body, name, fs, out_shapes, [_sems(n, 1), _sems(n, 1), pltpu.SemaphoreType.DMA((n,))])


def reduce_to_owner(gs, *, name):
    gs = list(gs)
    a = sibling_take_half(gs, name=name + "_pair")
    p = [pair_add(g, ai, name=name + "_pair_add") for g, ai in zip(gs, a)]
    b = exchange_chip_slots(p, name=name + "_chips")
    f = [chips_add(g, ai, bi, name=name + "_chips_add") for g, ai, bi in zip(gs, a, b)]
    return sibling_join_halves(f, name=name + "_join")


def allreduce_small(v, *, name):
    rows, cols = v.shape

    def body(v_ref, o_ref, buf, send_sems, recv_sems):
        x, y, c = _place()
        me = 4 * x + 2 * y + c
        buf[me] = v_ref[...]
        copies = []
        for k in range(1, 8):
            bx, by, bc = (k >> 2) & 1, (k >> 1) & 1, k & 1
            peer = (x if bx == 0 else 1 - x, y if by == 0 else 1 - y, c if bc == 0 else 1 - c)
            copies.append(_rcopy(v_ref, buf.at[me], send_sems.at[k - 1], recv_sems.at[k - 1], peer))
        for cp in copies:
            cp.start()
        for k in range(1, 8):
            bx, by, bc = (k >> 2) & 1, (k >> 1) & 1, k & 1
            px, py, pc = (x if bx == 0 else 1 - x, y if by == 0 else 1 - y, c if bc == 0 else 1 - c)
            _rcopy(v_ref, buf.at[4 * px + 2 * py + pc], send_sems.at[k - 1], recv_sems.at[k - 1], (px, py, pc)).wait_recv()
        for cp in copies:
            cp.wait_send()
        acc = buf[0]
        for j in range(1, 8):
            acc = acc + buf[j]
        o_ref[...] = acc

    return pl.pallas_call(
        body,
        name=name,
        in_specs=[pl.BlockSpec(memory_space=pltpu.VMEM)],
        out_specs=pl.BlockSpec(memory_space=pltpu.VMEM),
        out_shape=jax.ShapeDtypeStruct((rows, cols), v.dtype),
        scratch_shapes=[pltpu.VMEM((8, rows, cols), v.dtype), pltpu.SemaphoreType.DMA((7,)), pltpu.SemaphoreType.DMA((7,))],
        compiler_params=pltpu.CompilerParams(has_side_effects=True, vmem_limit_bytes=V7X_VMEM_LIMIT),
    )(v)


BIG = (("w_in", 1), ("w_uq", 1), ("w_ukv", 1), ("w_o_mla", 0), ("w_o_ssm", 0), ("w_out", 0), ("w_up", 1),
       ("w_down", 0), ("w_ple_gate", 0), ("w_ple", 1))
SHARDED = BIG + (("conv_w", 1),)
SMALL = ("norm_mix_w", "q_a_norm_w", "kv_a_norm_w", "q_norm_w", "k_norm_w", "conv_b", "dt_bias", "a_log", "d_skip",
         "ssm_norm_w", "norm_mlp_w", "ple_norm_w")
WEIGHTS = ("norm_mix_w", "w_in", "q_a_norm_w", "w_uq", "kv_a_norm_w", "w_ukv", "q_norm_w", "k_norm_w", "w_o_mla", "conv_w",
           "conv_b", "dt_bias", "a_log", "d_skip", "ssm_norm_w", "w_o_ssm", "w_out", "norm_mlp_w", "w_up", "w_down",
           "ple_norm_w", "w_ple_gate", "w_ple")


def _to_rows(flat, cols, row_multiple):
    n = flat.shape[-1]
    rows = -(-n // (cols * row_multiple)) * row_multiple
    pad = [(0, 0)] * (flat.ndim - 1) + [(0, rows * cols - n)]
    return jnp.pad(flat, pad).reshape(flat.shape[:-1] + (rows, cols))


def _w_in_ranges(d_model):
    out, lo = {}, 0
    for n, wd in zip(W_IN_PIECES, w_in_widths(d_model)):
        out[n] = (lo, lo + wd)
        lo += wd
    return out


def w_in_pieces(w3):
    _, k, c = w3.shape
    pc = {}
    for n, (lo, hi) in _w_in_ranges(k).items():
        cuts = [w3[j][:, max(lo, j * c) - j * c:min(hi, (j + 1) * c) - j * c]
                for j in range(N_CHIPS) if max(lo, j * c) < min(hi, (j + 1) * c)]
        pc[n] = cuts[0] if len(cuts) == 1 else jnp.concatenate(cuts, axis=1)
    pc["kd"] = jnp.concatenate([pc.pop("kr"), pc.pop("dt")], axis=1)
    return pc


def w_in_shard_grads(g, k, c):
    g = dict(g)
    g["kr"], g["dt"] = g["kd"][:, :QK_ROPE], g["kd"][:, QK_ROPE:]
    shards = []
    for j in range(N_CHIPS):
        cuts = []
        for n, (lo, hi) in _w_in_ranges(k).items():
            a, b = max(lo, j * c), min(hi, (j + 1) * c)
            if a < b:
                cuts.append(g[n][:, a - lo:b - lo])
        shards.append(jnp.concatenate(cuts, axis=1))
    return jnp.stack(shards)


def kernel(x, p, positions, norm_mix_w, w_in, q_a_norm_w, w_uq, kv_a_norm_w, w_ukv, q_norm_w, k_norm_w, w_o_mla, conv_w, conv_b, dt_bias, a_log, d_skip, ssm_norm_w, w_o_ssm, w_out, norm_mlp_w, w_up, w_down, ple_norm_w, w_ple_gate, w_ple, loss_target, m_norm_mix_w, m_w_in, m_q_a_norm_w, m_w_uq, m_kv_a_norm_w, m_w_ukv, m_q_norm_w, m_k_norm_w, m_w_o_mla, m_conv_w, m_conv_b, m_dt_bias, m_a_log, m_d_skip, m_ssm_norm_w, m_w_o_ssm, m_w_out, m_norm_mlp_w, m_w_up, m_w_down, m_ple_norm_w, m_w_ple_gate, m_w_ple, v_norm_mix_w, v_w_in, v_q_a_norm_w, v_w_uq, v_kv_a_norm_w, v_w_ukv, v_q_norm_w, v_k_norm_w, v_w_o_mla, v_conv_w, v_conv_b, v_dt_bias, v_a_log, v_d_skip, v_ssm_norm_w, v_w_o_ssm, v_w_out, v_norm_mlp_w, v_w_up, v_w_down, v_ple_norm_w, v_w_ple_gate, v_w_ple):
    a = dict(locals())
    x, p, pos, target = a["x"][0], a["p"][:, 0], a["positions"][0], a["loss_target"][0]
    depth = a["w_in"].shape[0]
    shard_shapes = {n: tuple(a[n].shape[1:]) for n, _ in SHARDED}
    cos_full, sin_full = rope_tables_full(pos)
    rot = rope_matrix()

    wb, small, sinks = [], [], []
    for i in range(depth):
        got = gather_shards([a[n][i].astype(BF16) for n, _ in BIG] + [a["conv_w"][i]], name="gather_weights")
        full = dict(zip([n for n, _ in SHARDED], got))
        w_i = w_in_pieces(full["w_in"])
        for n, ax in BIG[1:]:
            w_i[n] = full[n] if ax == 1 else full[n].reshape((-1, full[n].shape[-1]))
        wb.append(w_i)
        sm_i = {n: a[n][i] for n in SMALL}
        sm_i["conv_w"] = full["conv_w"].transpose(1, 0, 2).reshape(CONV_WIDTH, -1)
        small.append(sm_i)
        sinks.append({n: jnp.zeros(w_i[n].shape, F32) for n in LINEAR_NAMES})

    def forward(x, small, sinks):
        for i in range(depth):
            x = layer_forward(x, p[i], cos_full, sin_full, rot, wb[i], small[i], sinks[i])
        return x

    y, vjp = jax.vjp(forward, x, small, sinks)
    dy, loss_part = loss_and_cotangent(y, target)
    dx, d_small, d_sinks = vjp(dy)
    loss = lax.psum(loss_part, ("x", "y", "c"))

    per_layer = []
    for i in range(depth):
        g_i = [w_in_shard_grads(d_sinks[i], *shard_shapes["w_in"])]
        g_i += [d_sinks[i][n].reshape((N_CHIPS,) + shard_shapes[n]) for n, _ in BIG[1:]]
        per_layer.append(reduce_to_owner(g_i, name="reduce_grads"))
    grads = {n: jnp.stack([per_layer[i][t] for i in range(depth)]) for t, (n, _) in enumerate(BIG)}

    small_names = SMALL + ("conv_w",)
    flat = jnp.concatenate([d_small[i][n].reshape(-1) for i in range(depth) for n in small_names])
    n_small = flat.shape[0]
    red = allreduce_small(_to_rows(flat, 128, 8), name="reduce_small").reshape(-1)[:n_small]
    per = n_small // depth
    off = 0
    for n in SMALL:
        width = a[n].shape[-1]
        grads[n] = jnp.stack([red[i * per + off:i * per + off + width] for i in range(depth)])
        off += width
    conv_c = shard_shapes["conv_w"][1]
    conv_full = jnp.stack([red[i * per + off:i * per + off + CONV_WIDTH * N_CHIPS * conv_c] for i in range(depth)])
    chip = 2 * lax.axis_index("x") + lax.axis_index("y")
    grads["conv_w"] = lax.dynamic_index_in_dim(conv_full.reshape(depth, CONV_WIDTH, N_CHIPS, conv_c), chip, axis=2,
                                               keepdims=False)

    deltas, new_m, new_v = {}, {}, {}
    two_d = lambda t: t.reshape(-1, t.shape[-1])
    for n in WEIGHTS:
        d, m, v = adamw(two_d(a[n]), two_d(grads[n]), two_d(a["m_" + n]), two_d(a["v_" + n]), name="adamw")
        deltas[n], new_m[n], new_v[n] = d.reshape(a[n].shape), m.reshape(a[n].shape), v.reshape(a[n].shape)

    return (loss, dx[None], *[grads[n].reshape(a[n].shape) for n in WEIGHTS], *[deltas[n] for n in WEIGHTS],
            *[new_m[n] for n in WEIGHTS], *[new_v[n] for n in WEIGHTS])
```

```python
import functools

import jax
import jax.numpy as jnp
from jax import lax
from jax.experimental import pallas as pl
from jax.experimental.pallas import tpu as pltpu

F32 = jnp.float32
BF16 = jnp.bfloat16
HI = lax.Precision.HIGHEST

EPS = 1e-6
MLA_HEADS = 16
QK_NOPE = 128
QK_ROPE = 64
QK_DIM = QK_NOPE + QK_ROPE
V_DIM = 128
ROPE_THETA = 10000.0
ATT_CHUNK = 64
SSM_GROUPS = 8
SSM_HEADDIM = 64
SSM_STATE = 128
CONV_WIDTH = 4
ADAM_LR = 0.001
ADAM_B1 = 0.9
ADAM_B2 = 0.999
ADAM_EPS = 1e-08
ADAM_WD = 0.01
ADAM_STEP = 10

V7X_VMEM_LIMIT = 56 * 1024 * 1024


def _params(sem=None, **kw):
    return pltpu.CompilerParams(dimension_semantics=sem, vmem_limit_bytes=V7X_VMEM_LIMIT, **kw)


def _pick(n, prefs):
    for t in prefs:
        if n % t == 0:
            return t
    return n


MATMUL_OPERAND_BYTES = 24 * 1024 * 1024


def matmul(a, b, *, ta=False, tb=False, out_blocks=0, out_dtype=F32, name):
    m, k = (a.shape[1], a.shape[0]) if ta else a.shape
    blocked = b.ndim == 3
    if blocked:
        nb, rows, c = b.shape
        k2, n = (nb * c, rows) if tb else (rows, nb * c)
    else:
        k2, n = (b.shape[1], b.shape[0]) if tb else b.shape
    assert k == k2, (a.shape, b.shape, ta, tb)
    n_unit = n // out_blocks if out_blocks else (c if blocked and not tb else n)
    k_unit = c if blocked and tb else k
    tm = _pick(m, (1024, 512, 256, 128))
    tn = _pick(n_unit, (1024, 512, 256, 128))
    in_bytes = tm * a.dtype.itemsize + tn * b.dtype.itemsize
    tk = _pick(k_unit, tuple(t for t in (2048, 1024, 512, 256, 128) if 2 * t * in_bytes <= MATMUL_OPERAND_BYTES))
    nk = k // tk
    dn = (((0 if ta else 1,), (1 if tb else 0,)), ((), ()))

    def body(a_ref, b_ref, o_ref, *acc):
        part = lambda: lax.dot_general(a_ref[...].astype(BF16), b_ref[...].astype(BF16), dn, preferred_element_type=F32)
        if nk == 1:
            o_ref[...] = part().astype(o_ref.dtype)
            return
        (acc_ref,) = acc
        kk = pl.program_id(2)

        @pl.when(kk == 0)
        def _():
            acc_ref[...] = jnp.zeros_like(acc_ref)

        acc_ref[...] += part()

        @pl.when(kk == nk - 1)
        def _():
            o_ref[...] = acc_ref[...].astype(o_ref.dtype)

    a_spec = pl.BlockSpec((tk, tm), lambda i, j, kk: (kk, i)) if ta else pl.BlockSpec((tm, tk), lambda i, j, kk: (i, kk))
    if not blocked:
        b_spec = pl.BlockSpec((tn, tk), lambda i, j, kk: (j, kk)) if tb else pl.BlockSpec((tk, tn), lambda i, j, kk: (kk, j))
    elif tb:
        kb = c // tk
        b_spec = pl.BlockSpec((None, tn, tk), lambda i, j, kk: (kk // kb, j, kk % kb))
    else:
        cb = c // tn
        b_spec = pl.BlockSpec((None, tk, tn), lambda i, j, kk: (j // cb, kk, j % cb))
    if out_blocks:
        ob = n_unit // tn
        out_spec = pl.BlockSpec((None, tm, tn), lambda i, j, kk: (j // ob, i, j % ob))
        out_shape = jax.ShapeDtypeStruct((out_blocks, m, n_unit), out_dtype)
    else:
        out_spec = pl.BlockSpec((tm, tn), lambda i, j, kk: (i, j))
        out_shape = jax.ShapeDtypeStruct((m, n), out_dtype)
    return pl.pallas_call(
        body,
        name=name,
        grid=(m // tm, n // tn, nk),
        in_specs=[a_spec, b_spec],
        out_specs=out_spec,
        out_shape=out_shape,
        scratch_shapes=[pltpu.VMEM((tm, tn), F32)] if nk > 1 else [],
        compiler_params=_params(("parallel", "parallel", "arbitrary")),
    )(a, b)


def linear(a, w, sink, *, name):
    @jax.custom_vjp
    def op(a, w, sink):
        return matmul(a, w, name=name + "_fwd")

    def fwd(a, w, sink):
        return op(a, w, sink), (a, w)

    def bwd(res, ct):
        a, w = res
        da = matmul(ct, w, tb=True, name=name + "_bwd_da")
        dw = matmul(a, ct, ta=True, out_blocks=w.shape[0] if w.ndim == 3 else 0, name=name + "_bwd_dw")
        return da.astype(a.dtype), jnp.zeros_like(w), dw

    op.defvjp(fwd, bwd)
    return op(a, w, sink)


def _tiled_call(fn, tiled, whole, tile, name, n_acc):
    rows = tiled[0].shape[0]
    assert rows % tile == 0
    t_avals = [jax.ShapeDtypeStruct((tile,) + a.shape[1:], a.dtype) for a in tiled]
    w_avals = [jax.ShapeDtypeStruct(a.shape, a.dtype) for a in whole]
    outs = jax.eval_shape(fn, *t_avals, *w_avals)
    n_in = len(tiled) + len(whole)
    n_t = len(outs) - n_acc

    def body(*refs):
        res = fn(*[r[...] for r in refs[:n_in]])
        o_refs = refs[n_in:]
        for r, v in zip(o_refs[:n_t], res[:n_t]):
            r[...] = v.astype(r.dtype)
        if n_acc:
            first = pl.program_id(0) == 0

            @pl.when(first)
            def _():
                for r, v in zip(o_refs[n_t:], res[n_t:]):
                    r[...] = v.astype(F32)

            @pl.when(jnp.logical_not(first))
            def _():
                for r, v in zip(o_refs[n_t:], res[n_t:]):
                    r[...] += v.astype(F32)

    def tspec(a):
        nd = len(a.shape)
        return pl.BlockSpec((tile,) + tuple(a.shape[1:]), lambda i, nd=nd: (i,) + (0,) * (nd - 1))

    def wspec(a):
        nd = len(a.shape)
        return pl.BlockSpec(tuple(a.shape), lambda i, nd=nd: (0,) * nd)

    out_shape = [jax.ShapeDtypeStruct((rows,) + o.shape[1:], o.dtype) for o in outs[:n_t]]
    out_shape += [jax.ShapeDtypeStruct(o.shape, F32) for o in outs[n_t:]]
    out_specs = [tspec(o) for o in out_shape[:n_t]] + [wspec(o) for o in out_shape[n_t:]]
    return pl.pallas_call(
        body,
        name=name,
        grid=(rows // tile,),
        in_specs=[tspec(a) for a in tiled] + [wspec(a) for a in whole],
        out_specs=out_specs,
        out_shape=out_shape,
        compiler_params=_params(("arbitrary",) if n_acc else ("parallel",)),
    )(*tiled, *whole)


def rowwise(f, rows, consts, params, *, name, tile, tables=()):
    rows, consts, tables, params = tuple(rows), tuple(consts), tuple(tables), tuple(params)
    nr, nc, ntab, npar = len(rows), len(consts), len(tables), len(params)

    @jax.custom_vjp
    def op(rows, consts, tables, params):
        return tuple(_tiled_call(f, rows + consts, tables + params, tile, name + "_fwd", 0))

    def fwd(rows, consts, tables, params):
        return op(rows, consts, tables, params), (rows, consts, tables, params)

    def bwd(res, cts):
        rows, consts, tables, params = res
        ncts = len(cts)

        def g(*args):
            r = args[:nr]
            c = args[nr:nr + nc]
            ct = args[nr + nc:nr + nc + ncts]
            tab = args[nr + nc + ncts:nr + nc + ncts + ntab]
            p = args[nr + nc + ncts + ntab:]
            _, vjp = jax.vjp(lambda *rp: f(*rp[:nr], *c, *tab, *rp[nr:]), *r, *p)
            return tuple(vjp(tuple(ct)))

        outs = _tiled_call(g, rows + consts + tuple(cts), tables + params, tile, name + "_bwd", npar)
        d_rows = tuple(o.astype(r.dtype) for o, r in zip(outs[:nr], rows))
        d_params = tuple(o.astype(p.dtype) for o, p in zip(outs[nr:], params))
        zeros = lambda xs: tuple(jnp.zeros_like(a) for a in xs)
        return d_rows, zeros(consts), zeros(tables), d_params

    op.defvjp(fwd, bwd)
    return op(rows, consts, tables, params)


ATT_TILE = 512
_NT = (((1,), (1,)), ((), ()))
_TN = (((0,), (0,)), ((), ()))


def _chunk_mask(row0, col0, shape):
    r = (row0 + lax.broadcasted_iota(jnp.int32, shape, 0)) // ATT_CHUNK
    c = (col0 + lax.broadcasted_iota(jnp.int32, shape, 1)) // ATT_CHUNK
    return c <= r


def _attention_fwd(q, k, v, name):
    h, s, dq = q.shape
    dv = v.shape[-1]
    t = min(ATT_TILE, s)
    scale = dq ** -0.5

    def body(q_ref, k_ref, v_ref, o_ref, lse_ref, k_scr, v_scr):
        i = pl.program_id(1)

        @pl.when(i == 0)
        def _():
            k_scr[...] = k_ref[...].astype(BF16)
            v_scr[...] = v_ref[...].astype(BF16)

        qb = q_ref[...].astype(BF16)

        def block(j, carry, masked):
            m, l, acc = carry
            off = pl.multiple_of(j * t, t)
            kj = k_scr[pl.ds(off, t), :]
            vj = v_scr[pl.ds(off, t), :]
            sc = lax.dot_general(qb, kj, _NT, preferred_element_type=F32) * scale
            if masked:
                sc = jnp.where(_chunk_mask(i * t, j * t, sc.shape), sc, -jnp.inf)
            m_new = jnp.maximum(m, jnp.max(sc, axis=1, keepdims=True))
            p = jnp.exp(sc - m_new)
            alpha = jnp.exp(m - m_new)
            l = alpha * l + jnp.sum(p, axis=1, keepdims=True)
            acc = alpha * acc + jnp.dot(p.astype(BF16), vj, preferred_element_type=F32)
            return m_new, l, acc

        init = (jnp.full((t, 1), -jnp.inf, F32), jnp.zeros((t, 1), F32), jnp.zeros((t, dv), F32))
        carry = lax.fori_loop(0, i, lambda j, c: block(j, c, False), init)
        m, l, acc = block(i, carry, True)
        o_ref[...] = acc / l
        lse_ref[...] = m + jnp.log(l)

    return pl.pallas_call(
        body,
        name=name,
        grid=(h, s // t),
        in_specs=[
            pl.BlockSpec((None, t, dq), lambda hh, i: (hh, i, 0)),
            pl.BlockSpec((None, s, dq), lambda hh, i: (hh, 0, 0)),
            pl.BlockSpec((None, s, dv), lambda hh, i: (hh, 0, 0)),
        ],
        out_specs=[
            pl.BlockSpec((None, t, dv), lambda hh, i: (hh, i, 0)),
            pl.BlockSpec((None, t, 1), lambda hh, i: (hh, i, 0)),
        ],
        out_shape=[jax.ShapeDtypeStruct((h, s, dv), F32), jax.ShapeDtypeStruct((h, s, 1), F32)],
        scratch_shapes=[pltpu.VMEM((s, dq), BF16), pltpu.VMEM((s, dv), BF16)],
        compiler_params=_params(("parallel", "arbitrary")),
    )(q, k, v)


def _attention_bwd_dq(q, k, v, o, lse, do, name):
    h, s, dq = q.shape
    dv = v.shape[-1]
    t = min(ATT_TILE, s)
    scale = dq ** -0.5

    def body(q_ref, k_ref, v_ref, o_ref, lse_ref, do_ref, dq_ref, delta_ref, k_scr, v_scr):
        i = pl.program_id(1)

        @pl.when(i == 0)
        def _():
            k_scr[...] = k_ref[...].astype(BF16)
            v_scr[...] = v_ref[...].astype(BF16)

        qb = q_ref[...].astype(BF16)
        dof = do_ref[...]
        dob = dof.astype(BF16)
        lse_v = lse_ref[...]
        delta = jnp.sum(dof * o_ref[...], axis=1, keepdims=True)
        delta_ref[...] = delta

        def block(j, acc, masked):
            off = pl.multiple_of(j * t, t)
            kj = k_scr[pl.ds(off, t), :]
            vj = v_scr[pl.ds(off, t), :]
            sc = lax.dot_general(qb, kj, _NT, preferred_element_type=F32) * scale
            p = jnp.exp(sc - lse_v)
            if masked:
                p = jnp.where(_chunk_mask(i * t, j * t, sc.shape), p, 0.0)
            dp = lax.dot_general(dob, vj, _NT, preferred_element_type=F32)
            ds = p * (dp - delta) * scale
            return acc + jnp.dot(ds.astype(BF16), kj, preferred_element_type=F32)

        acc = lax.fori_loop(0, i, lambda j, c: block(j, c, False), jnp.zeros((t, dq), F32))
        dq_ref[...] = block(i, acc, True)

    tile = lambda d: pl.BlockSpec((None, t, d), lambda hh, i: (hh, i, 0))
    whole = lambda d: pl.BlockSpec((None, s, d), lambda hh, i: (hh, 0, 0))
    return pl.pallas_call(
        body,
        name=name,
        grid=(h, s // t),
        in_specs=[tile(dq), whole(dq), whole(dv), tile(dv), tile(1), tile(dv)],
        out_specs=[tile(dq), tile(1)],
        out_shape=[jax.ShapeDtypeStruct((h, s, dq), F32), jax.ShapeDtypeStruct((h, s, 1), F32)],
        scratch_shapes=[pltpu.VMEM((s, dq), BF16), pltpu.VMEM((s, dv), BF16)],
        compiler_params=_params(("parallel", "arbitrary")),
    )(q, k, v, o, lse, do)


def _attention_bwd_dkv(q, k, v, lse, delta, do, name):
    h, s, dq = q.shape
    dv = v.shape[-1]
    t = min(ATT_TILE, s)
    n = s // t
    scale = dq ** -0.5

    def body(q_ref, k_ref, v_ref, lse_ref, delta_ref, do_ref, dk_ref, dv_ref, q_scr, do_scr):
        j = pl.program_id(1)

        @pl.when(j == 0)
        def _():
            q_scr[...] = q_ref[...].astype(BF16)
            do_scr[...] = do_ref[...].astype(BF16)

        kb = k_ref[...].astype(BF16)
        vb = v_ref[...].astype(BF16)

        def block(i, carry, masked):
            dk, dvv = carry
            off = pl.multiple_of(i * t, t)
            qi = q_scr[pl.ds(off, t), :]
            doi = do_scr[pl.ds(off, t), :]
            sc = lax.dot_general(qi, kb, _NT, preferred_element_type=F32) * scale
            p = jnp.exp(sc - lse_ref[pl.ds(off, t), :])
            if masked:
                p = jnp.where(_chunk_mask(i * t, j * t, sc.shape), p, 0.0)
            dp = lax.dot_general(doi, vb, _NT, preferred_element_type=F32)
            ds = p * (dp - delta_ref[pl.ds(off, t), :]) * scale
            dvv = dvv + lax.dot_general(p.astype(BF16), doi, _TN, preferred_element_type=F32)
            dk = dk + lax.dot_general(ds.astype(BF16), qi, _TN, preferred_element_type=F32)
            return dk, dvv

        carry = block(j, (jnp.zeros((t, dq), F32), jnp.zeros((t, dv), F32)), True)
        dk, dvv = lax.fori_loop(j + 1, n, lambda i, c: block(i, c, False), carry)
        dk_ref[...] = dk
        dv_ref[...] = dvv

    tile = lambda d: pl.BlockSpec((None, t, d), lambda hh, j: (hh, j, 0))
    whole = lambda d: pl.BlockSpec((None, s, d), lambda hh, j: (hh, 0, 0))
    return pl.pallas_call(
        body,
        name=name,
        grid=(h, n),
        in_specs=[whole(dq), tile(dq), tile(dv), whole(1), whole(1), whole(dv)],
        out_specs=[tile(dq), tile(dv)],
        out_shape=[jax.ShapeDtypeStruct((h, s, dq), F32), jax.ShapeDtypeStruct((h, s, dv), F32)],
        scratch_shapes=[pltpu.VMEM((s, dq), BF16), pltpu.VMEM((s, dv), BF16)],
        compiler_params=_params(("parallel", "arbitrary")),
    )(q, k, v, lse, delta, do)


def attention(q, k, v, *, name):
    @jax.custom_vjp
    def op(q, k, v):
        return _attention_fwd(q, k, v, name + "_fwd")[0]

    def fwd(q, k, v):
        o, lse = _attention_fwd(q, k, v, name + "_fwd")
        return o, (q, k, v, o, lse)

    def bwd(res, do):
        q, k, v, o, lse = res
        dq, delta = _attention_bwd_dq(q, k, v, o, lse, do, name + "_bwd_dq")
        dk, dv = _attention_bwd_dkv(q, k, v, lse, delta, do, name + "_bwd_dkv")
        return dq, dk, dv

    op.defvjp(fwd, bwd)
    return op(q, k, v)


CONV_HALO = 8


def _conv_tiles(s, c):
    return min(512, s), _pick(c, (512, 256, 128))


def _conv_fwd(x, w, name):
    s, c = x.shape
    ts, tc = _conv_tiles(s, c)
    nb = ts // CONV_HALO

    def body(xc_ref, xp_ref, w_ref, o_ref):
        t = pl.program_id(1)
        prev = jnp.where(t > 0, xp_ref[...], 0.0)
        xe = jnp.concatenate([prev, xc_ref[...]], axis=0)
        wv = w_ref[...]
        acc = jnp.zeros((ts, tc), F32)
        for tap in range(CONV_WIDTH):
            k = CONV_WIDTH - 1 - tap
            sh = xe if k == 0 else pltpu.roll(xe, k, axis=0)
            acc = acc + sh[CONV_HALO:, :] * wv[tap:tap + 1, :]
        o_ref[...] = acc

    return pl.pallas_call(
        body,
        name=name,
        grid=(c // tc, s // ts),
        in_specs=[
            pl.BlockSpec((ts, tc), lambda ci, t: (t, ci)),
            pl.BlockSpec((CONV_HALO, tc), lambda ci, t: (jnp.maximum(t * nb - 1, 0), ci)),
            pl.BlockSpec((CONV_WIDTH, tc), lambda ci, t: (0, ci)),
        ],
        out_specs=pl.BlockSpec((ts, tc), lambda ci, t: (t, ci)),
        out_shape=jax.ShapeDtypeStruct((s, c), F32),
        compiler_params=_params(("parallel", "parallel")),
    )(x, x, w)


def _conv_bwd(x, w, dy, name):
    s, c = x.shape
    ts, tc = _conv_tiles(s, c)
    nb = ts // CONV_HALO
    nt = s // ts

    def body(xc_ref, xp_ref, w_ref, dc_ref, dn_ref, dx_ref, dw_ref):
        t = pl.program_id(1)
        prev = jnp.where(t > 0, xp_ref[...], 0.0)
        xe = jnp.concatenate([prev, xc_ref[...]], axis=0)
        dcur = dc_ref[...]
        nxt = jnp.where(t < nt - 1, dn_ref[...], 0.0)
        de = jnp.concatenate([dcur, nxt], axis=0)
        wv = w_ref[...]
        dx = jnp.zeros((ts, tc), F32)
        dw = jnp.zeros((CONV_WIDTH, tc), F32)
        tap_row = lax.broadcasted_iota(jnp.int32, (CONV_WIDTH, tc), 0)
        for tap in range(CONV_WIDTH):
            k = CONV_WIDTH - 1 - tap
            dsh = de if k == 0 else pltpu.roll(de, ts + CONV_HALO - k, axis=0)
            dx = dx + dsh[:ts, :] * wv[tap:tap + 1, :]
            xsh = xe if k == 0 else pltpu.roll(xe, k, axis=0)
            dwt = jnp.sum(xsh[CONV_HALO:, :] * dcur, axis=0, keepdims=True)
            dw = jnp.where(tap_row == tap, dwt, dw)
        dx_ref[...] = dx

        @pl.when(t == 0)
        def _():
            dw_ref[...] = dw

        @pl.when(t > 0)
        def _():
            dw_ref[...] += dw

    return pl.pallas_call(
        body,
        name=name,
        grid=(c // tc, nt),
        in_specs=[
            pl.BlockSpec((ts, tc), lambda ci, t: (t, ci)),
            pl.BlockSpec((CONV_HALO, tc), lambda ci, t: (jnp.maximum(t * nb - 1, 0), ci)),
            pl.BlockSpec((CONV_WIDTH, tc), lambda ci, t: (0, ci)),
            pl.BlockSpec((ts, tc), lambda ci, t: (t, ci)),
            pl.BlockSpec((CONV_HALO, tc), lambda ci, t: (jnp.minimum((t + 1) * nb, s // CONV_HALO - 1), ci)),
        ],
        out_specs=[
            pl.BlockSpec((ts, tc), lambda ci, t: (t, ci)),
            pl.BlockSpec((CONV_WIDTH, tc), lambda ci, t: (0, ci)),
        ],
        out_shape=[jax.ShapeDtypeStruct((s, c), F32), jax.ShapeDtypeStruct((CONV_WIDTH, c), F32)],
        compiler_params=_params(("parallel", "arbitrary")),
    )(x, x, w, dy, dy)


def causal_conv(x, w, *, name):
    @jax.custom_vjp
    def op(x, w):
        return _conv_fwd(x, w, name + "_fwd")

    def fwd(x, w):
        return op(x, w), (x, w)

    def bwd(res, dy):
        x, w = res
        dx, dw = _conv_bwd(x, w, dy, name + "_bwd")
        return dx, dw

    op.defvjp(fwd, bwd)
    return op(x, w)


SSD_T = 128
SSD_R = 8
SSD_GW = SSD_R * SSM_HEADDIM


def _ssd_consts(t):
    r = lax.broadcasted_iota(jnp.int32, (t, t), 0)
    c = lax.broadcasted_iota(jnp.int32, (t, t), 1)
    tril = (c <= r).astype(F32)
    triu = (r <= c).astype(F32)
    head_of_lane = lax.broadcasted_iota(jnp.int32, (SSD_R, SSD_GW), 1) // SSM_HEADDIM
    expand = (head_of_lane == lax.broadcasted_iota(jnp.int32, (SSD_R, SSD_GW), 0)).astype(F32)
    return c <= r, tril, triu, expand


def _hi(a, b):
    return jnp.dot(a, b, precision=HI, preferred_element_type=F32)


def _hi_nt(a, b):
    return lax.dot_general(a, b, _NT, precision=HI, preferred_element_type=F32)


def _bdot(a, b, dn=None):
    if dn is None:
        return jnp.dot(a.astype(BF16), b.astype(BF16), preferred_element_type=F32)
    return lax.dot_general(a.astype(BF16), b.astype(BF16), dn, preferred_element_type=F32)


def _ssd_chunk_common(x_ref, b_ref, c_ref, dtc_ref, dtr_ref, alc_ref, alr_ref, t):
    mask, tril, triu, expand = _ssd_consts(t)
    x, bm, cm = x_ref[...], b_ref[...], c_ref[...]
    dtc, dtr = dtc_ref[...], dtr_ref[...]
    neg_a_c = -jnp.exp(alc_ref[...])
    neg_a_r = -jnp.exp(alr_ref[...])
    acum_c = _hi(tril, dtc * neg_a_c)
    acum_r = _hi(dtr * neg_a_r, triu)
    s_cb = _bdot(cm, bm, _NT)
    return mask, tril, triu, expand, x, bm, cm, dtc, dtr, neg_a_c, neg_a_r, acum_c, acum_r, s_cb


def _head_decay(mask, acum_c, acum_r, h):
    seg = acum_c[:, h:h + 1] - acum_r[h:h + 1, :]
    return jnp.exp(jnp.where(mask, seg, -jnp.inf))


def _ssd_fwd(xbc, dtc, dtr, alc, alr, dexp, name):
    s = xbc.shape[0]
    g = SSM_GROUPS
    t = min(SSD_T, s)
    nc = s // t
    n = SSM_STATE
    xblocks = (g * SSD_GW) // n

    def body(x_ref, b_ref, c_ref, dtc_ref, dtr_ref, alc_ref, alr_ref, d_ref, y_ref, hs_ref, h_scr):
        ci = pl.program_id(1)

        @pl.when(ci == 0)
        def _():
            h_scr[...] = jnp.zeros_like(h_scr)

        (mask, tril, triu, expand, x, bm, cm, dtc_v, dtr_v, _, _, acum_c, acum_r, s_cb) = _ssd_chunk_common(
            x_ref, b_ref, c_ref, dtc_ref, dtr_ref, alc_ref, alr_ref, t)
        hst = h_scr[...]
        hs_ref[...] = hst
        ch = _bdot(cm, hst)
        y = _hi(jnp.exp(acum_c), expand) * ch + d_ref[...] * x
        half = lax.broadcasted_iota(jnp.int32, (t, 2 * SSM_HEADDIM), 1) // SSM_HEADDIM
        parts = []
        for j in range(SSD_R // 2):
            xp = x[:, j * 128:(j + 1) * 128]
            acc = jnp.zeros((t, 128), F32)
            for hh in range(2):
                h = 2 * j + hh
                m = s_cb * _head_decay(mask, acum_c, acum_r, h) * dtr_v[h:h + 1, :]
                acc = acc + _bdot(m, jnp.where(half == hh, xp, 0.0))
            parts.append(acc)
        y_ref[...] = y + jnp.concatenate(parts, axis=1)
        last = acum_c[t - 1:t, :]
        w_c = jnp.exp(last - acum_c) * dtc_v
        dec = _hi(jnp.broadcast_to(jnp.exp(last), (SSD_R, SSD_R)), expand)[0:1, :]
        h_scr[...] = dec * hst + _bdot(bm, _hi(w_c, expand) * x, _TN)

    return pl.pallas_call(
        body,
        name=name,
        grid=(g, nc),
        in_specs=[
            pl.BlockSpec((t, SSD_GW), lambda gi, ci: (ci, gi)),
            pl.BlockSpec((t, n), lambda gi, ci: (ci, xblocks + gi)),
            pl.BlockSpec((t, n), lambda gi, ci: (ci, xblocks + g + gi)),
            pl.BlockSpec((None, t, SSD_R), lambda gi, ci: (gi, ci, 0)),
            pl.BlockSpec((None, SSD_R, t), lambda gi, ci: (gi, 0, ci)),
            pl.BlockSpec((None, 1, SSD_R), lambda gi, ci: (gi, 0, 0)),
            pl.BlockSpec((None, SSD_R, 1), lambda gi, ci: (gi, 0, 0)),
            pl.BlockSpec((None, 1, SSD_GW), lambda gi, ci: (gi, 0, 0)),
        ],
        out_specs=[
            pl.BlockSpec((t, SSD_GW), lambda gi, ci: (ci, gi)),
            pl.BlockSpec((None, None, n, SSD_GW), lambda gi, ci: (gi, ci, 0, 0)),
        ],
        out_shape=[jax.ShapeDtypeStruct((s, g * SSD_GW), F32), jax.ShapeDtypeStruct((g, nc, n, SSD_GW), F32)],
        scratch_shapes=[pltpu.VMEM((n, SSD_GW), F32)],
        compiler_params=_params(("parallel", "arbitrary")),
    )(xbc, xbc, xbc, dtc, dtr, alc, alr, dexp)


def _ssd_bwd(xbc, dtc, dtr, alc, alr, dexp, hs, dy, name):
    s = xbc.shape[0]
    g = SSM_GROUPS
    t = min(SSD_T, s)
    nc = s // t
    n = SSM_STATE
    xblocks = (g * SSD_GW) // n

    def body(x_ref, b_ref, c_ref, dtc_ref, dtr_ref, alc_ref, alr_ref, d_ref, hs_ref, dy_ref,
             dx_ref, db_ref, dc_ref, ddtc_ref, ddtr_ref, dalc_ref, dalr_ref, dd_ref, dh_scr):
        ci = pl.program_id(1)

        @pl.when(ci == 0)
        def _():
            dh_scr[...] = jnp.zeros_like(dh_scr)

        (mask, tril, triu, expand, x, bm, cm, dtc_v, dtr_v, neg_a_c, neg_a_r, acum_c, acum_r, s_cb) = _ssd_chunk_common(
            x_ref, b_ref, c_ref, dtc_ref, dtr_ref, alc_ref, alr_ref, t)
        hst = hs_ref[...]
        dhn = dh_scr[...]
        dy = dy_ref[...]
        ch = _bdot(cm, hst)
        scale_full = _hi(jnp.exp(acum_c), expand)
        sdy = scale_full * dy
        d_c = _bdot(sdy, hst, _NT)
        dh_prev = _bdot(cm, sdy, _TN)
        dacum_c = _hi_nt(sdy * ch, expand)
        dx = d_ref[...] * dy
        dd = jnp.sum(dy * x, axis=0, keepdims=True)
        last = acum_c[t - 1:t, :]
        e_last = jnp.exp(last)
        dec = _hi(jnp.broadcast_to(e_last, (SSD_R, SSD_R)), expand)[0:1, :]
        dh_prev = dh_prev + dec * dhn
        ddec = jnp.sum(hst * dhn, axis=0, keepdims=True)
        dlast = _hi_nt(jnp.broadcast_to(ddec, (SSD_R, SSD_GW)), expand)[0:1, :] * e_last
        w_e = jnp.exp(last - acum_c)
        w_c = w_e * dtc_v
        wfull = _hi(w_c, expand)
        z = _bdot(bm, dhn)
        dx = dx + wfull * z
        dw_c = _hi_nt(x * z, expand)
        ddt_c = dw_c * w_e
        q_c = dw_c * w_c
        dacum_c = dacum_c - q_c
        dlast = dlast + jnp.sum(q_c, axis=0, keepdims=True)
        d_b = _bdot(wfull * x, dhn, _NT)
        half = lax.broadcasted_iota(jnp.int32, (t, 2 * SSM_HEADDIM), 1) // SSM_HEADDIM
        lane8 = lax.broadcasted_iota(jnp.int32, (t, SSD_R), 1)
        row8 = lax.broadcasted_iota(jnp.int32, (SSD_R, t), 0)
        ds_cb = jnp.zeros((t, t), F32)
        dacum_r = jnp.zeros((SSD_R, t), F32)
        ddt_r = jnp.zeros((SSD_R, t), F32)
        parts = []
        for j in range(SSD_R // 2):
            xp = x[:, j * 128:(j + 1) * 128]
            dyp = dy[:, j * 128:(j + 1) * 128]
            dxp = jnp.zeros((t, 128), F32)
            for hh in range(2):
                h = 2 * j + hh
                dts = dtr_v[h:h + 1, :]
                decay = _head_decay(mask, acum_c, acum_r, h)
                sl = s_cb * decay
                m = sl * dts
                xm = jnp.where(half == hh, xp, 0.0)
                dym = jnp.where(half == hh, dyp, 0.0)
                dxp = dxp + _bdot(m, dym, _TN)
                dm = _bdot(dym, xm, _NT)
                ds_cb = ds_cb + dm * decay * dts
                q = dm * m
                dacum_c = dacum_c + jnp.where(lane8 == h, jnp.sum(q, axis=1, keepdims=True), 0.0)
                dacum_r = dacum_r - jnp.where(row8 == h, jnp.sum(q, axis=0, keepdims=True), 0.0)
                ddt_r = ddt_r + jnp.where(row8 == h, jnp.sum(dm * sl, axis=0, keepdims=True), 0.0)
            parts.append(dxp)
        dx_ref[...] = dx + jnp.concatenate(parts, axis=1)
        dc_ref[...] = d_c + _bdot(ds_cb, bm)
        db_ref[...] = d_b + _bdot(ds_cb, cm, _TN)
        row_t = lax.broadcasted_iota(jnp.int32, (t, SSD_R), 0)
        dacum_c = dacum_c + jnp.where(row_t == t - 1, dlast, 0.0)
        da_c = _hi(triu, dacum_c)
        da_r = _hi(dacum_r, tril)
        ddtc_ref[...] = ddt_c + da_c * neg_a_c
        ddtr_ref[...] = ddt_r + da_r * neg_a_r
        dal_c = jnp.sum(da_c * dtc_v, axis=0, keepdims=True) * neg_a_c
        dal_r = jnp.sum(da_r * dtr_v, axis=1, keepdims=True) * neg_a_r
        dh_scr[...] = dh_prev

        @pl.when(ci == 0)
        def _():
            dalc_ref[...] = dal_c
            dalr_ref[...] = dal_r
            dd_ref[...] = dd

        @pl.when(ci > 0)
        def _():
            dalc_ref[...] += dal_c
            dalr_ref[...] += dal_r
            dd_ref[...] += dd

    rev = lambda ci: nc - 1 - ci
    return pl.pallas_call(
        body,
        name=name,
        grid=(g, nc),
        in_specs=[
            pl.BlockSpec((t, SSD_GW), lambda gi, ci: (rev(ci), gi)),
            pl.BlockSpec((t, n), lambda gi, ci: (rev(ci), xblocks + gi)),
            pl.BlockSpec((t, n), lambda gi, ci: (rev(ci), xblocks + g + gi)),
            pl.BlockSpec((None, t, SSD_R), lambda gi, ci: (gi, rev(ci), 0)),
            pl.BlockSpec((None, SSD_R, t), lambda gi, ci: (gi, 0, rev(ci))),
            pl.BlockSpec((None, 1, SSD_R), lambda gi, ci: (gi, 0, 0)),
            pl.BlockSpec((None, SSD_R, 1), lambda gi, ci: (gi, 0, 0)),
            pl.BlockSpec((None, 1, SSD_GW), lambda gi, ci: (gi, 0, 0)),
            pl.BlockSpec((None, None, n, SSD_GW), lambda gi, ci: (gi, rev(ci), 0, 0)),
            pl.BlockSpec((t, SSD_GW), lambda gi, ci: (rev(ci), gi)),
        ],
        out_specs=[
            pl.BlockSpec((t, SSD_GW), lambda gi, ci: (rev(ci), gi)),
            pl.BlockSpec((t, n), lambda gi, ci: (rev(ci), gi)),
            pl.BlockSpec((t, n), lambda gi, ci: (rev(ci), gi)),
            pl.BlockSpec((None, t, SSD_R), lambda gi, ci: (gi, rev(ci), 0)),
            pl.BlockSpec((None, SSD_R, t), lambda gi, ci: (gi, 0, rev(ci))),
            pl.BlockSpec((None, 1, SSD_R), lambda gi, ci: (gi, 0, 0)),
            pl.BlockSpec((None, SSD_R, 1), lambda gi, ci: (gi, 0, 0)),
            pl.BlockSpec((None, 1, SSD_GW), lambda gi, ci: (gi, 0, 0)),
        ],
        out_shape=[
            jax.ShapeDtypeStruct((s, g * SSD_GW), F32),
            jax.ShapeDtypeStruct((s, g * n), F32),
            jax.ShapeDtypeStruct((s, g * n), F32),
            jax.ShapeDtypeStruct((g, s, SSD_R), F32),
            jax.ShapeDtypeStruct((g, SSD_R, s), F32),
            jax.ShapeDtypeStruct((g, 1, SSD_R), F32),
            jax.ShapeDtypeStruct((g, SSD_R, 1), F32),
            jax.ShapeDtypeStruct((g, 1, SSD_GW), F32),
        ],
        scratch_shapes=[pltpu.VMEM((n, SSD_GW), F32)],
        compiler_params=_params(("parallel", "arbitrary")),
    )(xbc, xbc, xbc, dtc, dtr, alc, alr, dexp, hs, dy)


def ssd_core(xbc, dtc, dtr, alc, alr, dexp, *, name):
    @jax.custom_vjp
    def op(xbc, dtc, dtr, alc, alr, dexp):
        return _ssd_fwd(xbc, dtc, dtr, alc, alr, dexp, name + "_fwd")[0]

    def fwd(xbc, dtc, dtr, alc, alr, dexp):
        y, hs = _ssd_fwd(xbc, dtc, dtr, alc, alr, dexp, name + "_fwd")
        return y, (xbc, dtc, dtr, alc, alr, dexp, hs)

    def bwd(res, dy):
        xbc, dtc, dtr, alc, alr, dexp, hs = res
        dx, db, dc, ddtc, ddtr, dalc, dalr, dd = _ssd_bwd(xbc, dtc, dtr, alc, alr, dexp, hs, dy, name + "_bwd")
        return jnp.concatenate([dx, db, dc], axis=1), ddtc, ddtr, dalc, dalr, dd

    op.defvjp(fwd, bwd)
    return op(xbc, dtc, dtr, alc, alr, dexp)


def gate_norm(y, z, w):
    return (rms_norm(y * (z * jax.nn.sigmoid(z)), w),)


def ssd_branch(xbc, z, dt_raw, conv_w, conv_b, dt_bias, a_log, d_skip, norm_w, *, name):
    s = xbc.shape[0]
    g = SSM_GROUPS
    conv = causal_conv(xbc, conv_w, name=name + "_conv")
    (xc,) = rowwise(lambda c, b: ((c + b) * jax.nn.sigmoid(c + b),), (conv,), (), (conv_b[None, :],),
                    name=name + "_silu", tile=min(256, s))
    (dt,) = rowwise(lambda r, b: (jax.nn.softplus(r + b),), (dt_raw,), (), (dt_bias[None, :],),
                    name=name + "_dt", tile=min(512, s))
    dt3 = dt.reshape(s, g, SSD_R)
    y = ssd_core(xc, dt3.transpose(1, 0, 2), dt3.transpose(1, 2, 0), a_log.reshape(g, 1, SSD_R),
                 a_log.reshape(g, SSD_R, 1), jnp.repeat(d_skip, SSM_HEADDIM).reshape(g, 1, SSD_GW), name=name + "_core")
    (out,) = rowwise(gate_norm, (y.reshape(s, g, SSD_GW), z.reshape(s, g, SSD_GW)), (),
                     (norm_w.reshape(g, SSD_GW),), name=name + "_gate", tile=min(128, s))
    return out.reshape(s, g * SSD_GW)


def rms_norm(x, w):
    return x * lax.rsqrt(jnp.mean(x * x, axis=-1, keepdims=True) + EPS) * w


def rope_matrix():
    half = QK_ROPE // 2
    j = jnp.arange(QK_DIM)
    src = jnp.where(j < QK_NOPE + half, j + half, j - half)
    sign = jnp.where(j < QK_NOPE, 0.0, jnp.where(j < QK_NOPE + half, -1.0, 1.0))
    return (jnp.arange(QK_DIM)[:, None] == src[None, :]).astype(F32) * sign[None, :]


def rope_tables_full(positions):
    inv_freq = 1.0 / (ROPE_THETA ** (jnp.arange(0, QK_ROPE, 2, dtype=F32) / QK_ROPE))
    ang = positions.astype(F32)[:, None] * inv_freq
    s = positions.shape[0]
    cos = jnp.concatenate([jnp.ones((s, QK_NOPE), F32), jnp.cos(ang), jnp.cos(ang)], axis=-1)
    sin = jnp.concatenate([jnp.zeros((s, QK_NOPE), F32), jnp.sin(ang), jnp.sin(ang)], axis=-1)
    return cos[:, None, :], sin[:, None, :]


def head_norm_rope(x, cos_full, sin_full, rot, w):
    t, h, d = x.shape
    y = rms_norm(x, w)
    partner = jnp.dot(y.reshape(t * h, d), rot, precision=HI, preferred_element_type=F32).reshape(t, h, d)
    return (y * cos_full + partner * sin_full,)


def _norm(x, w, *, name, tile=256):
    (y,) = rowwise(lambda x, w: (rms_norm(x, w),), (x,), (), (w[None, :],), name=name, tile=min(tile, x.shape[0]))
    return y


Q_LORA = 512
KV_LORA = 512
W_IN_PIECES = ("cq", "ckv", "kr", "z", "xbc", "dt", "ga", "gb")


def w_in_widths(d_model):
    d_inner = 2 * d_model
    conv_dim = d_inner + 2 * SSM_GROUPS * SSM_STATE
    return (Q_LORA, KV_LORA, QK_ROPE, d_inner, conv_dim, d_inner // SSM_HEADDIM, d_model, d_model)


LINEAR_NAMES = ("cq", "ckv", "kd", "z", "xbc", "ga", "gb", "w_uq", "w_ukv", "w_o_mla", "w_o_ssm", "w_out", "w_up",
                "w_down", "w_ple_gate", "w_ple")


def layer_forward(x, p_i, cos_full, sin_full, rot, wb, sm, sinks):
    s, d = x.shape
    lin = lambda a, n: linear(a, wb[n], sinks[n], name="lin_" + n)
    h = _norm(x, sm["norm_mix_w"], name="norm_mix")
    c_q, c_kv, kd = lin(h, "cq"), lin(h, "ckv"), lin(h, "kd")
    z, xbc, g_a, g_b = lin(h, "z"), lin(h, "xbc"), lin(h, "ga"), lin(h, "gb")
    k_r, dt_raw = kd[:, :QK_ROPE], kd[:, QK_ROPE:]
    q = lin(_norm(c_q, sm["q_a_norm_w"], name="norm_qa"), "w_uq").reshape(s, MLA_HEADS, QK_DIM)
    kv = lin(_norm(c_kv, sm["kv_a_norm_w"], name="norm_kva"), "w_ukv").reshape(s, MLA_HEADS, QK_NOPE + V_DIM)
    k = jnp.concatenate([kv[..., :QK_NOPE], jnp.broadcast_to(k_r[:, None, :], (s, MLA_HEADS, QK_ROPE))], axis=-1)
    v = kv[..., QK_NOPE:]
    tq = min(128, s)
    (q,) = rowwise(head_norm_rope, (q,), (cos_full, sin_full), (sm["q_norm_w"][None, :],), tables=(rot,), name="q_rope", tile=tq)
    (k,) = rowwise(head_norm_rope, (k,), (cos_full, sin_full), (sm["k_norm_w"][None, :],), tables=(rot,), name="k_rope", tile=tq)
    hm = lambda a: a.transpose(1, 0, 2)
    o = hm(attention(hm(q), hm(k), hm(v), name="attn")).reshape(s, MLA_HEADS * V_DIM)
    y_a = lin(o, "w_o_mla")
    y_ssd = ssd_branch(xbc, z, dt_raw, sm["conv_w"], sm["conv_b"], sm["dt_bias"], sm["a_log"], sm["d_skip"],
                       sm["ssm_norm_w"], name="ssd")
    y_b = lin(y_ssd, "w_o_ssm")
    sig = jax.nn.sigmoid
    tr = min(256, s)
    (merged,) = rowwise(lambda ga, gb, ya, yb: (sig(ga) * ya + sig(gb) * yb,), (g_a, g_b, y_a, y_b), (), (),
                        name="merge", tile=tr)
    x = x + lin(merged, "w_out")
    up = lin(_norm(x, sm["norm_mlp_w"], name="norm_mlp"), "w_up")
    (act,) = rowwise(lambda u: (jnp.square(jnp.maximum(u, 0.0)),), (up,), (), (), name="relu2", tile=tr)
    x = x + lin(act, "w_down")
    pg = lin(_norm(x, sm["ple_norm_w"], name="norm_ple"), "w_ple_gate")
    pe = lin(p_i, "w_ple")
    (x,) = rowwise(lambda x, pe, pg: (x + pe * sig(pg),), (x, pe, pg), (), (), name="ple_add", tile=tr)
    return x


def loss_and_cotangent(y, target):
    s, d = y.shape

    def f(y, t):
        e = y - t
        return e * (1.0 / d), 0.5 * jnp.sum(jnp.sum(e * e, axis=1, keepdims=True) * (1.0 / d), axis=0, keepdims=True)

    dy, part = _tiled_call(f, (y, target), (), min(256, s), "loss", 1)
    return dy, part[0, 0]


ADAM_BLOCK_ELEMS = 256 * 1024


def adamw(w, g, m, v, *, name):
    rows, cols = w.shape
    budget = max(8, ADAM_BLOCK_ELEMS // cols)
    tile = _pick(rows, tuple(t for t in (512, 256, 128, 64, 32, 16, 8) if t <= budget))

    def f(w, g, m, v):
        m = ADAM_B1 * m + (1.0 - ADAM_B1) * g
        v = ADAM_B2 * v + (1.0 - ADAM_B2) * jnp.square(g)
        m_hat = m / (1.0 - ADAM_B1 ** ADAM_STEP)
        v_hat = v / (1.0 - ADAM_B2 ** ADAM_STEP)
        delta = -ADAM_LR * (m_hat / (jnp.sqrt(v_hat) + ADAM_EPS) + ADAM_WD * w)
        return delta, m, v

    return _tiled_call(f, (w, g, m, v), (), tile, name, 0)


MESH_ID = pl.DeviceIdType.MESH
N_CHIPS = 4
_ANY = pl.BlockSpec(memory_space=pl.ANY)


def _place():
    return lax.axis_index("x"), lax.axis_index("y"), lax.axis_index("c")


def _other_chips(x, y):
    return [(1 - x, y), (x, 1 - y), (1 - x, 1 - y)]


def _rcopy(src, dst, send_sem, recv_sem, device):
    return pltpu.make_async_remote_copy(src_ref=src, dst_ref=dst, send_sem=send_sem, recv_sem=recv_sem,
                                        device_id=device, device_id_type=MESH_ID)


def _sems(n, k):
    return pltpu.SemaphoreType.DMA((n, k))


def _comm_call(body, name, ins, out_shapes, scratch, aliases=None):
    return pl.pallas_call(
        body,
        name=name,
        in_specs=[_ANY] * len(ins),
        out_specs=[_ANY] * len(out_shapes),
        out_shape=out_shapes,
        scratch_shapes=scratch,
        input_output_aliases=aliases or {},
        compiler_params=pltpu.CompilerParams(has_side_effects=True),
    )(*ins)


SPLIT_ROWS = 32


def gather_shards(shards, *, name):
    n = len(shards)
    split = [s.shape[0] % SPLIT_ROWS == 0 for s in shards]

    def body(*refs):
        srcs, outs = refs[:n], refs[n:2 * n]
        send_sems, recv_sems = refs[2 * n:]
        x, y, c = _place()
        sibling = (x, y, 1 - c)
        chips = _other_chips(x, y)
        me = 2 * x + y

        def part(t, slot, h):
            if not split[t]:
                return outs[t].at[slot]
            half = shards[t].shape[0] // 2
            return outs[t].at[slot, pl.ds(h * half, half), :]

        def own(t):
            if not split[t]:
                return srcs[t]
            half = shards[t].shape[0] // 2
            return srcs[t].at[pl.ds(c * half, half), :]

        sent = []
        for t in range(n):
            for j, chip in enumerate(chips):
                sent.append(_rcopy(own(t), part(t, me, c), send_sems.at[t, j], recv_sems.at[t, j], (*chip, c)))
                sent[-1].start()
        for t in range(n):
            sent.append(_rcopy(srcs[t], outs[t].at[me], send_sems.at[t, 6], recv_sems.at[t, 6], sibling))
            sent[-1].start()
        for t in range(n):
            for j, (cx, cy) in enumerate(chips):
                got = part(t, 2 * cx + cy, c)
                _rcopy(got, got, send_sems.at[t, j], recv_sems.at[t, j], (cx, cy, c)).wait_recv()
                if split[t]:
                    sent.append(_rcopy(got, got, send_sems.at[t, 3 + j], recv_sems.at[t, 3 + j], sibling))
                    sent[-1].start()
        for t in range(n):
            if split[t]:
                for j, (cx, cy) in enumerate(chips):
                    got = part(t, 2 * cx + cy, 1 - c)
                    _rcopy(got, got, send_sems.at[t, 3 + j], recv_sems.at[t, 3 + j], sibling).wait_recv()
        for t in range(n):
            _rcopy(srcs[t], outs[t].at[me], send_sems.at[t, 6], recv_sems.at[t, 6], sibling).wait_recv()
        for cp in sent:
            cp.wait_send()

    out_shapes = [jax.ShapeDtypeStruct((N_CHIPS,) + s.shape, s.dtype) for s in shards]
    return _comm_call(body, name, shards, out_shapes, [_sems(n, 7), _sems(n, 7)])


def sibling_take_half(gs, *, name):
    n = len(gs)

    def body(*refs):
        g_refs, a_refs = refs[:n], refs[n:2 * n]
        send_sems, recv_sems = refs[2 * n:]
        x, y, c = _place()
        copies = []
        for t in range(n):
            half = gs[t].shape[1] // 2
            copies.append(_rcopy(g_refs[t].at[:, pl.ds((1 - c) * half, half), :], a_refs[t], send_sems.at[t, 0],
                                 recv_sems.at[t, 0], (x, y, 1 - c)))
            copies[-1].start()
        for cp in copies:
            cp.wait()

    out_shapes = [jax.ShapeDtypeStruct((g.shape[0], g.shape[1] // 2, g.shape[2]), g.dtype) for g in gs]
    return _comm_call(body, name, gs, out_shapes, [_sems(n, 1), _sems(n, 1)])


ELEMWISE_BLOCK_ELEMS = 256 * 1024


def _row_tile(rows, cols):
    budget = max(16, ELEMWISE_BLOCK_ELEMS // cols)
    return _pick(rows, tuple(t for t in (1024, 512, 256, 128, 64, 32, 16) if t <= budget))


def _core_and_chip():
    x, y, c = _place()
    return jnp.stack([c, 2 * x + y]).astype(jnp.int32)


def pair_add(g, a, *, name):
    n, rows, cols = g.shape
    half = rows // 2
    tile = _row_tile(half, cols)
    nb = half // tile

    def body(who_ref, g_ref, a_ref, o_ref):
        o_ref[...] = (g_ref[...] + a_ref[...]).astype(o_ref.dtype)

    return pl.pallas_call(
        body,
        name=name,
        grid_spec=pltpu.PrefetchScalarGridSpec(
            num_scalar_prefetch=1,
            grid=(n, nb),
            in_specs=[
                pl.BlockSpec((None, tile, cols), lambda j, i, who: (j, who[0] * nb + i, 0)),
                pl.BlockSpec((None, tile, cols), lambda j, i, who: (j, i, 0)),
            ],
            out_specs=pl.BlockSpec((None, tile, cols), lambda j, i, who: (j, i, 0)),
        ),
        out_shape=jax.ShapeDtypeStruct((n, half, cols), BF16),
        compiler_params=_params(("parallel", "parallel")),
    )(_core_and_chip(), g, a)


def exchange_chip_slots(ps, *, name):
    n = len(ps)

    def body(*refs):
        p_refs, b_refs = refs[:n], refs[n:2 * n]
        send_sems, recv_sems = refs[2 * n:]
        x, y, c = _place()
        me = 2 * x + y
        chips = _other_chips(x, y)
        sends = []
        for t in range(n):
            for j, (cx, cy) in enumerate(chips):
                sends.append(_rcopy(p_refs[t].at[2 * cx + cy], b_refs[t].at[me], send_sems.at[t, j], recv_sems.at[t, j],
                                    (cx, cy, c)))
                sends[-1].start()
        for t in range(n):
            for j, (cx, cy) in enumerate(chips):
                got = b_refs[t].at[2 * cx + cy]
                _rcopy(got, got, send_sems.at[t, j], recv_sems.at[t, j], (cx, cy, c)).wait_recv()
        for cp in sends:
            cp.wait_send()

    out_shapes = [jax.ShapeDtypeStruct(p.shape, p.dtype) for p in ps]
    return _comm_call(body, name, ps, out_shapes, [_sems(n, 3), _sems(n, 3)])


def chips_add(g, a, b, *, name):
    n, rows, cols = g.shape
    half = rows // 2
    tile = _row_tile(half, cols)
    nb = half // tile

    def body(who_ref, g_ref, a_ref, *rest):
        o_ref = rest[-1]
        acc = g_ref[...] + a_ref[...]
        for b_ref in rest[:-1]:
            acc = acc + b_ref[...].astype(F32)
        o_ref[...] = acc

    other = lambda k: pl.BlockSpec((None, tile, cols), lambda i, who, k=k: ((who[1] + k) % n, i, 0))
    return pl.pallas_call(
        body,
        name=name,
        grid_spec=pltpu.PrefetchScalarGridSpec(
            num_scalar_prefetch=1,
            grid=(nb,),
            in_specs=[
                pl.BlockSpec((None, tile, cols), lambda i, who: (who[1], who[0] * nb + i, 0)),
                pl.BlockSpec((None, tile, cols), lambda i, who: (who[1], i, 0)),
            ] + [other(k) for k in range(1, n)],
            out_specs=pl.BlockSpec((tile, cols), lambda i, who: (who[0] * nb + i, 0)),
        ),
        out_shape=jax.ShapeDtypeStruct((rows, cols), F32),
        compiler_params=_params(("parallel",)),
    )(_core_and_chip(), g, a, *([b] * (n - 1)))


def sibling_join_halves(rs, *, name):
    n = len(rs)

    def body(*refs):
        r_refs = refs[n:2 * n]
        send_sems, recv_sems = refs[2 * n:]
        x, y, c = _place()
        sibling = (x, y, 1 - c)
        copies = []
        for t in range(n):
            half = rs[t].shape[0] // 2
            mine = r_refs[t].at[pl.ds(c * half, half), :]
            copies.append(_rcopy(mine, mine, send_sems.at[t, 0], recv_sems.at[t, 0], sibling))
            copies[-1].start()
        for t in range(n):
            half = rs[t].shape[0] // 2
            got = r_refs[t].at[pl.ds((1 - c) * half, half), :]
            _rcopy(got, got, send_sems.at[t, 0], recv_sems.at[t, 0], sibling).wait_recv()
        for cp in copies:
            cp.wait_send()

    out_shapes = [jax.ShapeDtypeStruct(r.shape, r.dtype) for r in rs]
    return _comm_call(body, name, rs, out_shapes, [_sems(n, 1), _sems(n, 1)], aliases={t: t for t in range(n)})


def reduce_to_owner(gs, *, name):
    gs = list(gs)
    a = sibling_take_half(gs, name=name + "_pair")
    p = [pair_add(g, ai, name=name + "_pair_add") for g, ai in zip(gs, a)]
    b = exchange_chip_slots(p, name=name + "_chips")
    f = [chips_add(g, ai, bi, name=name + "_chips_add") for g, ai, bi in zip(gs, a, b)]
    return sibling_join_halves(f, name=name + "_join")


def allreduce_small(v, *, name):
    rows, cols = v.shape

    def body(v_ref, o_ref, buf, send_sems, recv_sems):
        x, y, c = _place()
        me = 4 * x + 2 * y + c
        buf[me] = v_ref[...]
        copies = []
        for k in range(1, 8):
            bx, by, bc = (k >> 2) & 1, (k >> 1) & 1, k & 1
            peer = (x if bx == 0 else 1 - x, y if by == 0 else 1 - y, c if bc == 0 else 1 - c)
            copies.append(_rcopy(v_ref, buf.at[me], send_sems.at[k - 1], recv_sems.at[k - 1], peer))
        for cp in copies:
            cp.start()
        for k in range(1, 8):
            bx, by, bc = (k >> 2) & 1, (k >> 1) & 1, k & 1
            px, py, pc = (x if bx == 0 else 1 - x, y if by == 0 else 1 - y, c if bc == 0 else 1 - c)
            _rcopy(v_ref, buf.at[4 * px + 2 * py + pc], send_sems.at[k - 1], recv_sems.at[k - 1], (px, py, pc)).wait_recv()
        for cp in copies:
            cp.wait_send()
        acc = buf[0]
        for j in range(1, 8):
            acc = acc + buf[j]
        o_ref[...] = acc

    return pl.pallas_call(
        body,
        name=name,
        in_specs=[pl.BlockSpec(memory_space=pltpu.VMEM)],
        out_specs=pl.BlockSpec(memory_space=pltpu.VMEM),
        out_shape=jax.ShapeDtypeStruct((rows, cols), v.dtype),
        scratch_shapes=[pltpu.VMEM((8, rows, cols), v.dtype), pltpu.SemaphoreType.DMA((7,)), pltpu.SemaphoreType.DMA((7,))],
        compiler_params=pltpu.CompilerParams(has_side_effects=True, vmem_limit_bytes=V7X_VMEM_LIMIT),
    )(v)


BIG = (("w_in", 1), ("w_uq", 1), ("w_ukv", 1), ("w_o_mla", 0), ("w_o_ssm", 0), ("w_out", 0), ("w_up", 1),
       ("w_down", 0), ("w_ple_gate", 0), ("w_ple", 1))
SHARDED = BIG + (("conv_w", 1),)
SMALL = ("norm_mix_w", "q_a_norm_w", "kv_a_norm_w", "q_norm_w", "k_norm_w", "conv_b", "dt_bias", "a_log", "d_skip",
         "ssm_norm_w", "norm_mlp_w", "ple_norm_w")
WEIGHTS = ("norm_mix_w", "w_in", "q_a_norm_w", "w_uq", "kv_a_norm_w", "w_ukv", "q_norm_w", "k_norm_w", "w_o_mla", "conv_w",
           "conv_b", "dt_bias", "a_log", "d_skip", "ssm_norm_w", "w_o_ssm", "w_out", "norm_mlp_w", "w_up", "w_down",
           "ple_norm_w", "w_ple_gate", "w_ple")


def _to_rows(flat, cols, row_multiple):
    n = flat.shape[-1]
    rows = -(-n // (cols * row_multiple)) * row_multiple
    pad = [(0, 0)] * (flat.ndim - 1) + [(0, rows * cols - n)]
    return jnp.pad(flat, pad).reshape(flat.shape[:-1] + (rows, cols))


def _w_in_ranges(d_model):
    out, lo = {}, 0
    for n, wd in zip(W_IN_PIECES, w_in_widths(d_model)):
        out[n] = (lo, lo + wd)
        lo += wd
    return out


def w_in_pieces(w3):
    _, k, c = w3.shape
    pc = {}
    for n, (lo, hi) in _w_in_ranges(k).items():
        cuts = [w3[j][:, max(lo, j * c) - j * c:min(hi, (j + 1) * c) - j * c]
                for j in range(N_CHIPS) if max(lo, j * c) < min(hi, (j + 1) * c)]
        pc[n] = cuts[0] if len(cuts) == 1 else jnp.concatenate(cuts, axis=1)
    pc["kd"] = jnp.concatenate([pc.pop("kr"), pc.pop("dt")], axis=1)
    return pc


def w_in_shard_grads(g, k, c):
    g = dict(g)
    g["kr"], g["dt"] = g["kd"][:, :QK_ROPE], g["kd"][:, QK_ROPE:]
    shards = []
    for j in range(N_CHIPS):
        cuts = []
        for n, (lo, hi) in _w_in_ranges(k).items():
            a, b = max(lo, j * c), min(hi, (j + 1) * c)
            if a < b:
                cuts.append(g[n][:, a - lo:b - lo])
        shards.append(jnp.concatenate(cuts, axis=1))
    return jnp.stack(shards)


def kernel(x, p, positions, norm_mix_w, w_in, q_a_norm_w, w_uq, kv_a_norm_w, w_ukv, q_norm_w, k_norm_w, w_o_mla, conv_w, conv_b, dt_bias, a_log, d_skip, ssm_norm_w, w_o_ssm, w_out, norm_mlp_w, w_up, w_down, ple_norm_w, w_ple_gate, w_ple, loss_target, m_norm_mix_w, m_w_in, m_q_a_norm_w, m_w_uq, m_kv_a_norm_w, m_w_ukv, m_q_norm_w, m_k_norm_w, m_w_o_mla, m_conv_w, m_conv_b, m_dt_bias, m_a_log, m_d_skip, m_ssm_norm_w, m_w_o_ssm, m_w_out, m_norm_mlp_w, m_w_up, m_w_down, m_ple_norm_w, m_w_ple_gate, m_w_ple, v_norm_mix_w, v_w_in, v_q_a_norm_w, v_w_uq, v_kv_a_norm_w, v_w_ukv, v_q_norm_w, v_k_norm_w, v_w_o_mla, v_conv_w, v_conv_b, v_dt_bias, v_a_log, v_d_skip, v_ssm_norm_w, v_w_o_ssm, v_w_out, v_norm_mlp_w, v_w_up, v_w_down, v_ple_norm_w, v_w_ple_gate, v_w_ple):
    a = dict(locals())
    x, p, pos, target = a["x"][0], a["p"][:, 0], a["positions"][0], a["loss_target"][0]
    depth = a["w_in"].shape[0]
    shard_shapes = {n: tuple(a[n].shape[1:]) for n, _ in SHARDED}
    cos_full, sin_full = rope_tables_full(pos)
    rot = rope_matrix()

    wb, small, sinks = [], [], []
    for i in range(depth):
        got = gather_shards([a[n][i].astype(BF16) for n, _ in BIG] + [a["conv_w"][i]], name="gather_weights")
        full = dict(zip([n for n, _ in SHARDED], got))
        w_i = w_in_pieces(full["w_in"])
        for n, ax in BIG[1:]:
            w_i[n] = full[n] if ax == 1 else full[n].reshape((-1, full[n].shape[-1]))
        wb.append(w_i)
        sm_i = {n: a[n][i] for n in SMALL}
        sm_i["conv_w"] = full["conv_w"].transpose(1, 0, 2).reshape(CONV_WIDTH, -1)
        small.append(sm_i)
        sinks.append({n: jnp.zeros(w_i[n].shape, F32) for n in LINEAR_NAMES})

    def forward(x, small, sinks):
        for i in range(depth):
            x = layer_forward(x, p[i], cos_full, sin_full, rot, wb[i], small[i], sinks[i])
        return x

    y, vjp = jax.vjp(forward, x, small, sinks)
    dy, loss_part = loss_and_cotangent(y, target)
    dx, d_small, d_sinks = vjp(dy)
    loss = lax.psum(loss_part, ("x", "y", "c"))

    per_layer = []
    for i in range(depth):
        g_i = [w_in_shard_grads(d_sinks[i], *shard_shapes["w_in"])]
        g_i += [d_sinks[i][n].reshape((N_CHIPS,) + shard_shapes[n]) for n, _ in BIG[1:]]
        per_layer.append(reduce_to_owner(g_i, name="reduce_grads"))
    grads = {n: jnp.stack([per_layer[i][t] for i in range(depth)]) for t, (n, _) in enumerate(BIG)}

    small_names = SMALL + ("conv_w",)
    flat = jnp.concatenate([d_small[i][n].reshape(-1) for i in range(depth) for n in small_names])
    n_small = flat.shape[0]
    red = allreduce_small(_to_rows(flat, 128, 8), name="reduce_small").reshape(-1)[:n_small]
    per = n_small // depth
    off = 0
    for n in SMALL:
        width = a[n].shape[-1]
        grads[n] = jnp.stack([red[i * per + off:i * per + off + width] for i in range(depth)])
        off += width
    conv_c = shard_shapes["conv_w"][1]
    conv_full = jnp.stack([red[i * per + off:i * per + off + CONV_WIDTH * N_CHIPS * conv_c] for i in range(depth)])
    chip = 2 * lax.axis_index("x") + lax.axis_index("y")
    grads["conv_w"] = lax.dynamic_index_in_dim(conv_full.reshape(depth, CONV_WIDTH, N_CHIPS, conv_c), chip, axis=2,
                                               keepdims=False)

    deltas, new_m, new_v = {}, {}, {}
    two_d = lambda t: t.reshape(-1, t.shape[-1])
    for n in WEIGHTS:
        d, m, v = adamw(two_d(a[n]), two_d(grads[n]), two_d(a["m_" + n]), two_d(a["v_" + n]), name="adamw")
        deltas[n], new_m[n], new_v[n] = d.reshape(a[n].shape), m.reshape(a[n].shape), v.reshape(a[n].shape)

    return (loss, dx[None], *[grads[n].reshape(a[n].shape) for n in WEIGHTS], *[deltas[n] for n in WEIGHTS],
            *[new_m[n] for n in WEIGHTS], *[new_v[n] for n in WEIGHTS])
```

```python
import functools

import jax
import jax.numpy as jnp
from jax import lax
from jax.experimental import pallas as pl
from jax.experimental.pallas import tpu as pltpu

F32 = jnp.float32
BF16 = jnp.bfloat16
HI = lax.Precision.HIGHEST

EPS = 1e-6
MLA_HEADS = 16
QK_NOPE = 128
QK_ROPE = 64
QK_DIM = QK_NOPE + QK_ROPE
V_DIM = 128
ROPE_THETA = 10000.0
ATT_CHUNK = 64
SSM_GROUPS = 8
SSM_HEADDIM = 64
SSM_STATE = 128
CONV_WIDTH = 4
ADAM_LR = 0.001
ADAM_B1 = 0.9
ADAM_B2 = 0.999
ADAM_EPS = 1e-08
ADAM_WD = 0.01
ADAM_STEP = 10

V7X_VMEM_LIMIT = 56 * 1024 * 1024


def _params(sem=None, **kw):
    return pltpu.CompilerParams(dimension_semantics=sem, vmem_limit_bytes=V7X_VMEM_LIMIT, **kw)


def _pick(n, prefs):
    for t in prefs:
        if n % t == 0:
            return t
    return n


MATMUL_OPERAND_BYTES = 24 * 1024 * 1024


def matmul(a, b, *, ta=False, tb=False, out_blocks=0, out_dtype=F32, name):
    m, k = (a.shape[1], a.shape[0]) if ta else a.shape
    blocked = b.ndim == 3
    if blocked:
        nb, rows, c = b.shape
        k2, n = (nb * c, rows) if tb else (rows, nb * c)
    else:
        k2, n = (b.shape[1], b.shape[0]) if tb else b.shape
    assert k == k2, (a.shape, b.shape, ta, tb)
    n_unit = n // out_blocks if out_blocks else (c if blocked and not tb else n)
    k_unit = c if blocked and tb else k
    tm = _pick(m, (1024, 512, 256, 128))
    tn = _pick(n_unit, (1024, 512, 256, 128))
    in_bytes = tm * a.dtype.itemsize + tn * b.dtype.itemsize
    tk = _pick(k_unit, tuple(t for t in (2048, 1024, 512, 256, 128) if 2 * t * in_bytes <= MATMUL_OPERAND_BYTES))
    nk = k // tk
    dn = (((0 if ta else 1,), (1 if tb else 0,)), ((), ()))

    def body(a_ref, b_ref, o_ref, *acc):
        part = lambda: lax.dot_general(a_ref[...].astype(BF16), b_ref[...].astype(BF16), dn, preferred_element_type=F32)
        if nk == 1:
            o_ref[...] = part().astype(o_ref.dtype)
            return
        (acc_ref,) = acc
        kk = pl.program_id(2)

        @pl.when(kk == 0)
        def _():
            acc_ref[...] = jnp.zeros_like(acc_ref)

        acc_ref[...] += part()

        @pl.when(kk == nk - 1)
        def _():
            o_ref[...] = acc_ref[...].astype(o_ref.dtype)

    a_spec = pl.BlockSpec((tk, tm), lambda i, j, kk: (kk, i)) if ta else pl.BlockSpec((tm, tk), lambda i, j, kk: (i, kk))
    if not blocked:
        b_spec = pl.BlockSpec((tn, tk), lambda i, j, kk: (j, kk)) if tb else pl.BlockSpec((tk, tn), lambda i, j, kk: (kk, j))
    elif tb:
        kb = c // tk
        b_spec = pl.BlockSpec((None, tn, tk), lambda i, j, kk: (kk // kb, j, kk % kb))
    else:
        cb = c // tn
        b_spec = pl.BlockSpec((None, tk, tn), lambda i, j, kk: (j // cb, kk, j % cb))
    if out_blocks:
        ob = n_unit // tn
        out_spec = pl.BlockSpec((None, tm, tn), lambda i, j, kk: (j // ob, i, j % ob))
        out_shape = jax.ShapeDtypeStruct((out_blocks, m, n_unit), out_dtype)
    else:
        out_spec = pl.BlockSpec((tm, tn), lambda i, j, kk: (i, j))
        out_shape = jax.ShapeDtypeStruct((m, n), out_dtype)
    return pl.pallas_call(
        body,
        name=name,
        grid=(m // tm, n // tn, nk),
        in_specs=[a_spec, b_spec],
        out_specs=out_spec,
        out_shape=out_shape,
        scratch_shapes=[pltpu.VMEM((tm, tn), F32)] if nk > 1 else [],
        compiler_params=_params(("parallel", "parallel", "arbitrary")),
    )(a, b)


def linear(a, w, sink, *, name):
    @jax.custom_vjp
    def op(a, w, sink):
        return matmul(a, w, name=name + "_fwd")

    def fwd(a, w, sink):
        return op(a, w, sink), (a, w)

    def bwd(res, ct):
        a, w = res
        da = matmul(ct, w, tb=True, name=name + "_bwd_da")
        dw = matmul(a, ct, ta=True, out_blocks=w.shape[0] if w.ndim == 3 else 0, name=name + "_bwd_dw")
        return da.astype(a.dtype), jnp.zeros_like(w), dw

    op.defvjp(fwd, bwd)
    return op(a, w, sink)


def _tiled_call(fn, tiled, whole, tile, name, n_acc):
    rows = tiled[0].shape[0]
    assert rows % tile == 0
    t_avals = [jax.ShapeDtypeStruct((tile,) + a.shape[1:], a.dtype) for a in tiled]
    w_avals = [jax.ShapeDtypeStruct(a.shape, a.dtype) for a in whole]
    outs = jax.eval_shape(fn, *t_avals, *w_avals)
    n_in = len(tiled) + len(whole)
    n_t = len(outs) - n_acc

    def body(*refs):
        res = fn(*[r[...] for r in refs[:n_in]])
        o_refs = refs[n_in:]
        for r, v in zip(o_refs[:n_t], res[:n_t]):
            r[...] = v.astype(r.dtype)
        if n_acc:
            first = pl.program_id(0) == 0

            @pl.when(first)
            def _():
                for r, v in zip(o_refs[n_t:], res[n_t:]):
                    r[...] = v.astype(F32)

            @pl.when(jnp.logical_not(first))
            def _():
                for r, v in zip(o_refs[n_t:], res[n_t:]):
                    r[...] += v.astype(F32)

    def tspec(a):
        nd = len(a.shape)
        return pl.BlockSpec((tile,) + tuple(a.shape[1:]), lambda i, nd=nd: (i,) + (0,) * (nd - 1))

    def wspec(a):
        nd = len(a.shape)
        return pl.BlockSpec(tuple(a.shape), lambda i, nd=nd: (0,) * nd)

    out_shape = [jax.ShapeDtypeStruct((rows,) + o.shape[1:], o.dtype) for o in outs[:n_t]]
    out_shape += [jax.ShapeDtypeStruct(o.shape, F32) for o in outs[n_t:]]
    out_specs = [tspec(o) for o in out_shape[:n_t]] + [wspec(o) for o in out_shape[n_t:]]
    return pl.pallas_call(
        body,
        name=name,
        grid=(rows // tile,),
        in_specs=[tspec(a) for a in tiled] + [wspec(a) for a in whole],
        out_specs=out_specs,
        out_shape=out_shape,
        compiler_params=_params(("arbitrary",) if n_acc else ("parallel",)),
    )(*tiled, *whole)


def rowwise(f, rows, consts, params, *, name, tile, tables=()):
    rows, consts, tables, params = tuple(rows), tuple(consts), tuple(tables), tuple(params)
    nr, nc, ntab, npar = len(rows), len(consts), len(tables), len(params)

    @jax.custom_vjp
    def op(rows, consts, tables, params):
        return tuple(_tiled_call(f, rows + consts, tables + params, tile, name + "_fwd", 0))

    def fwd(rows, consts, tables, params):
        return op(rows, consts, tables, params), (rows, consts, tables, params)

    def bwd(res, cts):
        rows, consts, tables, params = res
        ncts = len(cts)

        def g(*args):
            r = args[:nr]
            c = args[nr:nr + nc]
            ct = args[nr + nc:nr + nc + ncts]
            tab = args[nr + nc + ncts:nr + nc + ncts + ntab]
            p = args[nr + nc + ncts + ntab:]
            _, vjp = jax.vjp(lambda *rp: f(*rp[:nr], *c, *tab, *rp[nr:]), *r, *p)
            return tuple(vjp(tuple(ct)))

        outs = _tiled_call(g, rows + consts + tuple(cts), tables + params, tile, name + "_bwd", npar)
        d_rows = tuple(o.astype(r.dtype) for o, r in zip(outs[:nr], rows))
        d_params = tuple(o.astype(p.dtype) for o, p in zip(outs[nr:], params))
        zeros = lambda xs: tuple(jnp.zeros_like(a) for a in xs)
        return d_rows, zeros(consts), zeros(tables), d_params

    op.defvjp(fwd, bwd)
    return op(rows, consts, tables, params)


ATT_TILE = 512
LOG2E = 1.4426950408889634
HOSTED_IN_DQ = 4
_NT = (((1,), (1,)), ((), ()))
_TN = (((0,), (0,)), ((), ()))


def _chunk_mask(row0, col0, shape):
    r = (row0 + lax.broadcasted_iota(jnp.int32, shape, 0)) // ATT_CHUNK
    c = (col0 + lax.broadcasted_iota(jnp.int32, shape, 1)) // ATT_CHUNK
    return c <= r


def _attention_fwd(q, k, v, name):
    h, s, dq = q.shape
    dv = v.shape[-1]
    t = min(ATT_TILE, s)
    scale = dq ** -0.5

    def body(q_ref, k_ref, v_ref, o_ref, lse_ref, k_scr, v_scr):
        i = pl.program_id(1)

        @pl.when(i == 0)
        def _():
            k_scr[...] = k_ref[...].astype(BF16)
            v_scr[...] = v_ref[...].astype(BF16)

        qb = (q_ref[...] * (scale * LOG2E)).astype(BF16)

        def block(j, carry, masked):
            m, l, acc = carry
            off = pl.multiple_of(j * t, t)
            kj = k_scr[pl.ds(off, t), :]
            vj = v_scr[pl.ds(off, t), :]
            sc = lax.dot_general(qb, kj, _NT, preferred_element_type=F32)
            if masked:
                sc = jnp.where(_chunk_mask(i * t, j * t, sc.shape), sc, -jnp.inf)
            m_new = jnp.maximum(m, jnp.max(sc, axis=1, keepdims=True))
            p = jnp.exp2(sc - m_new)
            alpha = jnp.exp2(m - m_new)
            l = alpha * l + jnp.sum(p, axis=1, keepdims=True)
            acc = alpha * acc + jnp.dot(p.astype(BF16), vj, preferred_element_type=F32)
            return m_new, l, acc

        init = (jnp.full((t, 1), -jnp.inf, F32), jnp.zeros((t, 1), F32), jnp.zeros((t, dv), F32))
        carry = lax.fori_loop(0, i, lambda j, c: block(j, c, False), init)
        m, l, acc = block(i, carry, True)
        o_ref[...] = acc / l
        lse_ref[...] = m + jnp.log2(l)

    return pl.pallas_call(
        body,
        name=name,
        grid=(h, s // t),
        in_specs=[
            pl.BlockSpec((None, t, dq), lambda hh, i: (hh, i, 0)),
            pl.BlockSpec((None, s, dq), lambda hh, i: (hh, 0, 0)),
            pl.BlockSpec((None, s, dv), lambda hh, i: (hh, 0, 0)),
        ],
        out_specs=[
            pl.BlockSpec((None, t, dv), lambda hh, i: (hh, i, 0)),
            pl.BlockSpec((None, t, 1), lambda hh, i: (hh, i, 0)),
        ],
        out_shape=[jax.ShapeDtypeStruct((h, s, dv), F32), jax.ShapeDtypeStruct((h, s, 1), F32)],
        scratch_shapes=[pltpu.VMEM((s, dq), BF16), pltpu.VMEM((s, dv), BF16)],
        compiler_params=_params(("parallel", "arbitrary")),
    )(q, k, v)


def _hosted_exchange(p_refs, b_refs, send_sems, recv_sems, first, last):
    x, y, c = _place()
    me = 2 * x + y
    chips = _other_chips(x, y)
    n = len(p_refs)

    def sends():
        return [_rcopy(p_refs[t].at[2 * cx + cy], b_refs[t].at[me], send_sems.at[t, j], recv_sems.at[t, j], (cx, cy, c))
                for t in range(n) for j, (cx, cy) in enumerate(chips)]

    @pl.when(first)
    def _():
        for cp in sends():
            cp.start()

    @pl.when(last)
    def _():
        for t in range(n):
            for j, (cx, cy) in enumerate(chips):
                got = b_refs[t].at[2 * cx + cy]
                _rcopy(got, got, send_sems.at[t, j], recv_sems.at[t, j], (cx, cy, c)).wait_recv()
        for cp in sends():
            cp.wait_send()


def _host_args(hosted):
    n = len(hosted)
    if not n:
        return [], [], [], []
    any_spec = pl.BlockSpec(memory_space=pl.ANY)
    shapes = [jax.ShapeDtypeStruct(p.shape, p.dtype) for p in hosted]
    return [any_spec] * n, [any_spec] * n, shapes, [pltpu.SemaphoreType.DMA((n, 3)), pltpu.SemaphoreType.DMA((n, 3))]


def _attention_bwd_dq(q, k, v, o, lse, do, name, hosted=()):
    h, s, dq = q.shape
    dv = v.shape[-1]
    t = min(ATT_TILE, s)
    scale = dq ** -0.5
    nh = len(hosted)

    def body(*refs):
        q_ref, k_ref, v_ref, o_ref, lse_ref, do_ref = refs[:6]
        p_refs = refs[6:6 + nh]
        dq_ref, delta_ref = refs[6 + nh:8 + nh]
        b_refs = refs[8 + nh:8 + 2 * nh]
        k_scr, v_scr = refs[8 + 2 * nh:10 + 2 * nh]
        hh, i = pl.program_id(0), pl.program_id(1)
        if nh:
            _hosted_exchange(p_refs, b_refs, refs[-2], refs[-1], jnp.logical_and(hh == 0, i == 0),
                             jnp.logical_and(hh == h - 1, i == s // t - 1))

        @pl.when(i == 0)
        def _():
            k_scr[...] = k_ref[...].astype(BF16)
            v_scr[...] = v_ref[...].astype(BF16)

        qb = (q_ref[...] * (scale * LOG2E)).astype(BF16)
        dof = do_ref[...]
        dob = dof.astype(BF16)
        lse_v = lse_ref[...]
        delta = jnp.sum(dof * o_ref[...], axis=1, keepdims=True)
        delta_ref[...] = delta

        def block(j, acc, masked):
            off = pl.multiple_of(j * t, t)
            kj = k_scr[pl.ds(off, t), :]
            vj = v_scr[pl.ds(off, t), :]
            sc = lax.dot_general(qb, kj, _NT, preferred_element_type=F32)
            p = jnp.exp2(sc - lse_v)
            if masked:
                p = jnp.where(_chunk_mask(i * t, j * t, sc.shape), p, 0.0)
            dp = lax.dot_general(dob, vj, _NT, preferred_element_type=F32)
            ds = p * (dp - delta)
            return acc + jnp.dot(ds.astype(BF16), kj, preferred_element_type=F32)

        acc = lax.fori_loop(0, i, lambda j, c: block(j, c, False), jnp.zeros((t, dq), F32))
        dq_ref[...] = block(i, acc, True) * scale

    tile = lambda d: pl.BlockSpec((None, t, d), lambda hh, i: (hh, i, 0))
    whole = lambda d: pl.BlockSpec((None, s, d), lambda hh, i: (hh, 0, 0))
    h_in, h_out, h_shapes, h_scratch = _host_args(hosted)
    return pl.pallas_call(
        body,
        name=name,
        grid=(h, s // t),
        in_specs=[tile(dq), whole(dq), whole(dv), tile(dv), tile(1), tile(dv)] + h_in,
        out_specs=[tile(dq), tile(1)] + h_out,
        out_shape=[jax.ShapeDtypeStruct((h, s, dq), F32), jax.ShapeDtypeStruct((h, s, 1), F32)] + h_shapes,
        scratch_shapes=[pltpu.VMEM((s, dq), BF16), pltpu.VMEM((s, dv), BF16)] + h_scratch,
        compiler_params=_params(("arbitrary", "arbitrary"), has_side_effects=bool(nh)),
    )(q, k, v, o, lse, do, *hosted)


def _attention_bwd_dkv(q, k, v, lse, delta, do, name, hosted=()):
    h, s, dq = q.shape
    dv = v.shape[-1]
    t = min(ATT_TILE, s)
    n = s // t
    scale = dq ** -0.5
    nh = len(hosted)

    def body(*refs):
        q_ref, k_ref, v_ref, lse_ref, delta_ref, do_ref = refs[:6]
        p_refs = refs[6:6 + nh]
        dk_ref, dv_ref = refs[6 + nh:8 + nh]
        b_refs = refs[8 + nh:8 + 2 * nh]
        q_scr, do_scr = refs[8 + 2 * nh:10 + 2 * nh]
        hh, j = pl.program_id(0), pl.program_id(1)
        if nh:
            _hosted_exchange(p_refs, b_refs, refs[-2], refs[-1], jnp.logical_and(hh == 0, j == 0),
                             jnp.logical_and(hh == h - 1, j == n - 1))

        @pl.when(j == 0)
        def _():
            q_scr[...] = (q_ref[...] * (scale * LOG2E)).astype(BF16)
            do_scr[...] = do_ref[...].astype(BF16)

        kb = k_ref[...].astype(BF16)
        vb = v_ref[...].astype(BF16)

        def block(i, carry, masked):
            dk, dvv = carry
            off = pl.multiple_of(i * t, t)
            qi = q_scr[pl.ds(off, t), :]
            doi = do_scr[pl.ds(off, t), :]
            sc = lax.dot_general(qi, kb, _NT, preferred_element_type=F32)
            p = jnp.exp2(sc - lse_ref[pl.ds(off, t), :])
            if masked:
                p = jnp.where(_chunk_mask(i * t, j * t, sc.shape), p, 0.0)
            dp = lax.dot_general(doi, vb, _NT, preferred_element_type=F32)
            ds = p * (dp - delta_ref[pl.ds(off, t), :])
            dvv = dvv + lax.dot_general(p.astype(BF16), doi, _TN, preferred_element_type=F32)
            dk = dk + lax.dot_general(ds.astype(BF16), qi, _TN, preferred_element_type=F32)
            return dk, dvv

        carry = block(j, (jnp.zeros((t, dq), F32), jnp.zeros((t, dv), F32)), True)
        dk, dvv = lax.fori_loop(j + 1, n, lambda i, c: block(i, c, False), carry)
        dk_ref[...] = dk * (1.0 / LOG2E)
        dv_ref[...] = dvv

    tile = lambda d: pl.BlockSpec((None, t, d), lambda hh, j: (hh, j, 0))
    whole = lambda d: pl.BlockSpec((None, s, d), lambda hh, j: (hh, 0, 0))
    h_in, h_out, h_shapes, h_scratch = _host_args(hosted)
    return pl.pallas_call(
        body,
        name=name,
        grid=(h, n),
        in_specs=[whole(dq), tile(dq), tile(dv), whole(1), whole(1), whole(dv)] + h_in,
        out_specs=[tile(dq), tile(dv)] + h_out,
        out_shape=[jax.ShapeDtypeStruct((h, s, dq), F32), jax.ShapeDtypeStruct((h, s, dv), F32)] + h_shapes,
        scratch_shapes=[pltpu.VMEM((s, dq), BF16), pltpu.VMEM((s, dv), BF16)] + h_scratch,
        compiler_params=_params(("arbitrary", "arbitrary"), has_side_effects=bool(nh)),
    )(q, k, v, lse, delta, do, *hosted)


def attention(q, k, v, tok=(), *, name):
    tok = tuple(tok)

    @jax.custom_vjp
    def op(q, k, v, tok):
        return _attention_fwd(q, k, v, name + "_fwd")[0], tok

    def fwd(q, k, v, tok):
        o, lse = _attention_fwd(q, k, v, name + "_fwd")
        return (o, tok), (q, k, v, o, lse)

    def bwd(res, cts):
        q, k, v, o, lse = res
        do, payload = cts
        first = tuple(payload[:HOSTED_IN_DQ])
        rest = tuple(payload[HOSTED_IN_DQ:])
        dq, delta, *got_a = _attention_bwd_dq(q, k, v, o, lse, do, name + "_bwd_dq" + ("_host" if first else ""), first)
        dk, dv, *got_b = _attention_bwd_dkv(q, k, v, lse, delta, do, name + "_bwd_dkv" + ("_host" if rest else ""), rest)
        return dq, dk, dv, tuple(got_a) + tuple(got_b)

    op.defvjp(fwd, bwd)
    return op(q, k, v, tok)


CONV_HALO = 8


def _conv_tiles(s, c):
    return min(512, s), _pick(c, (512, 256, 128))


def _conv_fwd(x, w, name):
    s, c = x.shape
    ts, tc = _conv_tiles(s, c)
    nb = ts // CONV_HALO

    def body(xc_ref, xp_ref, w_ref, o_ref):
        t = pl.program_id(1)
        prev = jnp.where(t > 0, xp_ref[...], 0.0)
        xe = jnp.concatenate([prev, xc_ref[...]], axis=0)
        wv = w_ref[...]
        acc = jnp.zeros((ts, tc), F32)
        for tap in range(CONV_WIDTH):
            k = CONV_WIDTH - 1 - tap
            sh = xe if k == 0 else pltpu.roll(xe, k, axis=0)
            acc = acc + sh[CONV_HALO:, :] * wv[tap:tap + 1, :]
        o_ref[...] = acc

    return pl.pallas_call(
        body,
        name=name,
        grid=(c // tc, s // ts),
        in_specs=[
            pl.BlockSpec((ts, tc), lambda ci, t: (t, ci)),
            pl.BlockSpec((CONV_HALO, tc), lambda ci, t: (jnp.maximum(t * nb - 1, 0), ci)),
            pl.BlockSpec((CONV_WIDTH, tc), lambda ci, t: (0, ci)),
        ],
        out_specs=pl.BlockSpec((ts, tc), lambda ci, t: (t, ci)),
        out_shape=jax.ShapeDtypeStruct((s, c), F32),
        compiler_params=_params(("parallel", "parallel")),
    )(x, x, w)


def _conv_bwd(x, w, dy, name):
    s, c = x.shape
    ts, tc = _conv_tiles(s, c)
    nb = ts // CONV_HALO
    nt = s // ts

    def body(xc_ref, xp_ref, w_ref, dc_ref, dn_ref, dx_ref, dw_ref):
        t = pl.program_id(1)
        prev = jnp.where(t > 0, xp_ref[...], 0.0)
        xe = jnp.concatenate([prev, xc_ref[...]], axis=0)
        dcur = dc_ref[...]
        nxt = jnp.where(t < nt - 1, dn_ref[...], 0.0)
        de = jnp.concatenate([dcur, nxt], axis=0)
        wv = w_ref[...]
        dx = jnp.zeros((ts, tc), F32)
        dw = jnp.zeros((CONV_WIDTH, tc), F32)
        tap_row = lax.broadcasted_iota(jnp.int32, (CONV_WIDTH, tc), 0)
        for tap in range(CONV_WIDTH):
            k = CONV_WIDTH - 1 - tap
            dsh = de if k == 0 else pltpu.roll(de, ts + CONV_HALO - k, axis=0)
            dx = dx + dsh[:ts, :] * wv[tap:tap + 1, :]
            xsh = xe if k == 0 else pltpu.roll(xe, k, axis=0)
            dwt = jnp.sum(xsh[CONV_HALO:, :] * dcur, axis=0, keepdims=True)
            dw = jnp.where(tap_row == tap, dwt, dw)
        dx_ref[...] = dx

        @pl.when(t == 0)
        def _():
            dw_ref[...] = dw

        @pl.when(t > 0)
        def _():
            dw_ref[...] += dw

    return pl.pallas_call(
        body,
        name=name,
        grid=(c // tc, nt),
        in_specs=[
            pl.BlockSpec((ts, tc), lambda ci, t: (t, ci)),
            pl.BlockSpec((CONV_HALO, tc), lambda ci, t: (jnp.maximum(t * nb - 1, 0), ci)),
            pl.BlockSpec((CONV_WIDTH, tc), lambda ci, t: (0, ci)),
            pl.BlockSpec((ts, tc), lambda ci, t: (t, ci)),
            pl.BlockSpec((CONV_HALO, tc), lambda ci, t: (jnp.minimum((t + 1) * nb, s // CONV_HALO - 1), ci)),
        ],
        out_specs=[
            pl.BlockSpec((ts, tc), lambda ci, t: (t, ci)),
            pl.BlockSpec((CONV_WIDTH, tc), lambda ci, t: (0, ci)),
        ],
        out_shape=[jax.ShapeDtypeStruct((s, c), F32), jax.ShapeDtypeStruct((CONV_WIDTH, c), F32)],
        compiler_params=_params(("parallel", "arbitrary")),
    )(x, x, w, dy, dy)


def causal_conv(x, w, *, name):
    @jax.custom_vjp
    def op(x, w):
        return _conv_fwd(x, w, name + "_fwd")

    def fwd(x, w):
        return op(x, w), (x, w)

    def bwd(res, dy):
        x, w = res
        dx, dw = _conv_bwd(x, w, dy, name + "_bwd")
        return dx, dw

    op.defvjp(fwd, bwd)
    return op(x, w)


SSD_T = 128
SSD_R = 8
SSD_GW = SSD_R * SSM_HEADDIM


def _ssd_consts(t):
    r = lax.broadcasted_iota(jnp.int32, (t, t), 0)
    c = lax.broadcasted_iota(jnp.int32, (t, t), 1)
    tril = (c <= r).astype(F32)
    triu = (r <= c).astype(F32)
    head_of_lane = lax.broadcasted_iota(jnp.int32, (SSD_R, SSD_GW), 1) // SSM_HEADDIM
    expand = (head_of_lane == lax.broadcasted_iota(jnp.int32, (SSD_R, SSD_GW), 0)).astype(F32)
    return c <= r, tril, triu, expand


def _hi(a, b):
    return jnp.dot(a, b, precision=HI, preferred_element_type=F32)


def _hi_nt(a, b):
    return lax.dot_general(a, b, _NT, precision=HI, preferred_element_type=F32)


def _bdot(a, b, dn=None):
    if dn is None:
        return jnp.dot(a.astype(BF16), b.astype(BF16), preferred_element_type=F32)
    return lax.dot_general(a.astype(BF16), b.astype(BF16), dn, preferred_element_type=F32)


def _ssd_chunk_common(x_ref, b_ref, c_ref, dtc_ref, dtr_ref, alc_ref, alr_ref, t):
    mask, tril, triu, expand = _ssd_consts(t)
    x, bm, cm = x_ref[...], b_ref[...], c_ref[...]
    dtc, dtr = dtc_ref[...], dtr_ref[...]
    neg_a_c = -jnp.exp(alc_ref[...])
    neg_a_r = -jnp.exp(alr_ref[...])
    acum_c = _hi(tril, dtc * neg_a_c)
    acum_r = _hi(dtr * neg_a_r, triu)
    s_cb = _bdot(cm, bm, _NT)
    return mask, tril, triu, expand, x, bm, cm, dtc, dtr, neg_a_c, neg_a_r, acum_c, acum_r, s_cb


def _head_decay(mask, acum_c, acum_r, h):
    seg = acum_c[:, h:h + 1] - acum_r[h:h + 1, :]
    return jnp.exp(jnp.where(mask, seg, -jnp.inf))


def _ssd_fwd(xbc, dtc, dtr, alc, alr, dexp, name):
    s = xbc.shape[0]
    g = SSM_GROUPS
    t = min(SSD_T, s)
    nc = s // t
    n = SSM_STATE
    xblocks = (g * SSD_GW) // n

    def body(x_ref, b_ref, c_ref, dtc_ref, dtr_ref, alc_ref, alr_ref, d_ref, y_ref, hs_ref, h_scr):
        ci = pl.program_id(1)

        @pl.when(ci == 0)
        def _():
            h_scr[...] = jnp.zeros_like(h_scr)

        (mask, tril, triu, expand, x, bm, cm, dtc_v, dtr_v, _, _, acum_c, acum_r, s_cb) = _ssd_chunk_common(
            x_ref, b_ref, c_ref, dtc_ref, dtr_ref, alc_ref, alr_ref, t)
        hst = h_scr[...]
        hs_ref[...] = hst
        ch = _bdot(cm, hst)
        y = _hi(jnp.exp(acum_c), expand) * ch + d_ref[...] * x
        half = lax.broadcasted_iota(jnp.int32, (t, 2 * SSM_HEADDIM), 1) // SSM_HEADDIM
        parts = []
        for j in range(SSD_R // 2):
            xp = x[:, j * 128:(j + 1) * 128]
            acc = jnp.zeros((t, 128), F32)
            for hh in range(2):
                h = 2 * j + hh
                m = s_cb * _head_decay(mask, acum_c, acum_r, h) * dtr_v[h:h + 1, :]
                acc = acc + _bdot(m, jnp.where(half == hh, xp, 0.0))
            parts.append(acc)
        y_ref[...] = y + jnp.concatenate(parts, axis=1)
        last = acum_c[t - 1:t, :]
        w_c = jnp.exp(last - acum_c) * dtc_v
        dec = _hi(jnp.broadcast_to(jnp.exp(last), (SSD_R, SSD_R)), expand)[0:1, :]
        h_scr[...] = dec * hst + _bdot(bm, _hi(w_c, expand) * x, _TN)

    return pl.pallas_call(
        body,
        name=name,
        grid=(g, nc),
        in_specs=[
            pl.BlockSpec((t, SSD_GW), lambda gi, ci: (ci, gi)),
            pl.BlockSpec((t, n), lambda gi, ci: (ci, xblocks + gi)),
            pl.BlockSpec((t, n), lambda gi, ci: (ci, xblocks + g + gi)),
            pl.BlockSpec((None, t, SSD_R), lambda gi, ci: (gi, ci, 0)),
            pl.BlockSpec((None, SSD_R, t), lambda gi, ci: (gi, 0, ci)),
            pl.BlockSpec((None, 1, SSD_R), lambda gi, ci: (gi, 0, 0)),
            pl.BlockSpec((None, SSD_R, 1), lambda gi, ci: (gi, 0, 0)),
            pl.BlockSpec((None, 1, SSD_GW), lambda gi, ci: (gi, 0, 0)),
        ],
        out_specs=[
            pl.BlockSpec((t, SSD_GW), lambda gi, ci: (ci, gi)),
            pl.BlockSpec((None, None, n, SSD_GW), lambda gi, ci: (gi, ci, 0, 0)),
        ],
        out_shape=[jax.ShapeDtypeStruct((s, g * SSD_GW), F32), jax.ShapeDtypeStruct((g, nc, n, SSD_GW), F32)],
        scratch_shapes=[pltpu.VMEM((n, SSD_GW), F32)],
        compiler_params=_params(("parallel", "arbitrary")),
    )(xbc, xbc, xbc, dtc, dtr, alc, alr, dexp)


def _ssd_bwd(xbc, dtc, dtr, alc, alr, dexp, hs, dy, name):
    s = xbc.shape[0]
    g = SSM_GROUPS
    t = min(SSD_T, s)
    nc = s // t
    n = SSM_STATE
    xblocks = (g * SSD_GW) // n

    def body(x_ref, b_ref, c_ref, dtc_ref, dtr_ref, alc_ref, alr_ref, d_ref, hs_ref, dy_ref,
             dx_ref, db_ref, dc_ref, ddtc_ref, ddtr_ref, dalc_ref, dalr_ref, dd_ref, dh_scr):
        ci = pl.program_id(1)

        @pl.when(ci == 0)
        def _():
            dh_scr[...] = jnp.zeros_like(dh_scr)

        (mask, tril, triu, expand, x, bm, cm, dtc_v, dtr_v, neg_a_c, neg_a_r, acum_c, acum_r, s_cb) = _ssd_chunk_common(
            x_ref, b_ref, c_ref, dtc_ref, dtr_ref, alc_ref, alr_ref, t)
        hst = hs_ref[...]
        dhn = dh_scr[...]
        dy = dy_ref[...]
        ch = _bdot(cm, hst)
        scale_full = _hi(jnp.exp(acum_c), expand)
        sdy = scale_full * dy
        d_c = _bdot(sdy, hst, _NT)
        dh_prev = _bdot(cm, sdy, _TN)
        dacum_c = _hi_nt(sdy * ch, expand)
        dx = d_ref[...] * dy
        dd = jnp.sum(dy * x, axis=0, keepdims=True)
        last = acum_c[t - 1:t, :]
        e_last = jnp.exp(last)
        dec = _hi(jnp.broadcast_to(e_last, (SSD_R, SSD_R)), expand)[0:1, :]
        dh_prev = dh_prev + dec * dhn
        ddec = jnp.sum(hst * dhn, axis=0, keepdims=True)
        dlast = _hi_nt(jnp.broadcast_to(ddec, (SSD_R, SSD_GW)), expand)[0:1, :] * e_last
        w_e = jnp.exp(last - acum_c)
        w_c = w_e * dtc_v
        wfull = _hi(w_c, expand)
        z = _bdot(bm, dhn)
        dx = dx + wfull * z
        dw_c = _hi_nt(x * z, expand)
        ddt_c = dw_c * w_e
        q_c = dw_c * w_c
        dacum_c = dacum_c - q_c
        dlast = dlast + jnp.sum(q_c, axis=0, keepdims=True)
        d_b = _bdot(wfull * x, dhn, _NT)
        half = lax.broadcasted_iota(jnp.int32, (t, 2 * SSM_HEADDIM), 1) // SSM_HEADDIM
        lane8 = lax.broadcasted_iota(jnp.int32, (t, SSD_R), 1)
        row8 = lax.broadcasted_iota(jnp.int32, (SSD_R, t), 0)
        ds_cb = jnp.zeros((t, t), F32)
        dacum_r = jnp.zeros((SSD_R, t), F32)
        ddt_r = jnp.zeros((SSD_R, t), F32)
        parts = []
        for j in range(SSD_R // 2):
            xp = x[:, j * 128:(j + 1) * 128]
            dyp = dy[:, j * 128:(j + 1) * 128]
            dxp = jnp.zeros((t, 128), F32)
            for hh in range(2):
                h = 2 * j + hh
                dts = dtr_v[h:h + 1, :]
                decay = _head_decay(mask, acum_c, acum_r, h)
                sl = s_cb * decay
                m = sl * dts
                xm = jnp.where(half == hh, xp, 0.0)
                dym = jnp.where(half == hh, dyp, 0.0)
                dxp = dxp + _bdot(m, dym, _TN)
                dm = _bdot(dym, xm, _NT)
                ds_cb = ds_cb + dm * decay * dts
                q = dm * m
                dacum_c = dacum_c + jnp.where(lane8 == h, jnp.sum(q, axis=1, keepdims=True), 0.0)
                dacum_r = dacum_r - jnp.where(row8 == h, jnp.sum(q, axis=0, keepdims=True), 0.0)
                ddt_r = ddt_r + jnp.where(row8 == h, jnp.sum(dm * sl, axis=0, keepdims=True), 0.0)
            parts.append(dxp)
        dx_ref[...] = dx + jnp.concatenate(parts, axis=1)
        dc_ref[...] = d_c + _bdot(ds_cb, bm)
        db_ref[...] = d_b + _bdot(ds_cb, cm, _TN)
        row_t = lax.broadcasted_iota(jnp.int32, (t, SSD_R), 0)
        dacum_c = dacum_c + jnp.where(row_t == t - 1, dlast, 0.0)
        da_c = _hi(triu, dacum_c)
        da_r = _hi(dacum_r, tril)
        ddtc_ref[...] = ddt_c + da_c * neg_a_c
        ddtr_ref[...] = ddt_r + da_r * neg_a_r
        dal_c = jnp.sum(da_c * dtc_v, axis=0, keepdims=True) * neg_a_c
        dal_r = jnp.sum(da_r * dtr_v, axis=1, keepdims=True) * neg_a_r
        dh_scr[...] = dh_prev

        @pl.when(ci == 0)
        def _():
            dalc_ref[...] = dal_c
            dalr_ref[...] = dal_r
            dd_ref[...] = dd

        @pl.when(ci > 0)
        def _():
            dalc_ref[...] += dal_c
            dalr_ref[...] += dal_r
            dd_ref[...] += dd

    rev = lambda ci: nc - 1 - ci
    return pl.pallas_call(
        body,
        name=name,
        grid=(g, nc),
        in_specs=[
            pl.BlockSpec((t, SSD_GW), lambda gi, ci: (rev(ci), gi)),
            pl.BlockSpec((t, n), lambda gi, ci: (rev(ci), xblocks + gi)),
            pl.BlockSpec((t, n), lambda gi, ci: (rev(ci), xblocks + g + gi)),
            pl.BlockSpec((None, t, SSD_R), lambda gi, ci: (gi, rev(ci), 0)),
            pl.BlockSpec((None, SSD_R, t), lambda gi, ci: (gi, 0, rev(ci))),
            pl.BlockSpec((None, 1, SSD_R), lambda gi, ci: (gi, 0, 0)),
            pl.BlockSpec((None, SSD_R, 1), lambda gi, ci: (gi, 0, 0)),
            pl.BlockSpec((None, 1, SSD_GW), lambda gi, ci: (gi, 0, 0)),
            pl.BlockSpec((None, None, n, SSD_GW), lambda gi, ci: (gi, rev(ci), 0, 0)),
            pl.BlockSpec((t, SSD_GW), lambda gi, ci: (rev(ci), gi)),
        ],
        out_specs=[
            pl.BlockSpec((t, SSD_GW), lambda gi, ci: (rev(ci), gi)),
            pl.BlockSpec((t, n), lambda gi, ci: (rev(ci), gi)),
            pl.BlockSpec((t, n), lambda gi, ci: (rev(ci), gi)),
            pl.BlockSpec((None, t, SSD_R), lambda gi, ci: (gi, rev(ci), 0)),
            pl.BlockSpec((None, SSD_R, t), lambda gi, ci: (gi, 0, rev(ci))),
            pl.BlockSpec((None, 1, SSD_R), lambda gi, ci: (gi, 0, 0)),
            pl.BlockSpec((None, SSD_R, 1), lambda gi, ci: (gi, 0, 0)),
            pl.BlockSpec((None, 1, SSD_GW), lambda gi, ci: (gi, 0, 0)),
        ],
        out_shape=[
            jax.ShapeDtypeStruct((s, g * SSD_GW), F32),
            jax.ShapeDtypeStruct((s, g * n), F32),
            jax.ShapeDtypeStruct((s, g * n), F32),
            jax.ShapeDtypeStruct((g, s, SSD_R), F32),
            jax.ShapeDtypeStruct((g, SSD_R, s), F32),
            jax.ShapeDtypeStruct((g, 1, SSD_R), F32),
            jax.ShapeDtypeStruct((g, SSD_R, 1), F32),
            jax.ShapeDtypeStruct((g, 1, SSD_GW), F32),
        ],
        scratch_shapes=[pltpu.VMEM((n, SSD_GW), F32)],
        compiler_params=_params(("parallel", "arbitrary")),
    )(xbc, xbc, xbc, dtc, dtr, alc, alr, dexp, hs, dy)


def ssd_core(xbc, dtc, dtr, alc, alr, dexp, *, name):
    @jax.custom_vjp
    def op(xbc, dtc, dtr, alc, alr, dexp):
        return _ssd_fwd(xbc, dtc, dtr, alc, alr, dexp, name + "_fwd")[0]

    def fwd(xbc, dtc, dtr, alc, alr, dexp):
        y, hs = _ssd_fwd(xbc, dtc, dtr, alc, alr, dexp, name + "_fwd")
        return y, (xbc, dtc, dtr, alc, alr, dexp, hs)

    def bwd(res, dy):
        xbc, dtc, dtr, alc, alr, dexp, hs = res
        dx, db, dc, ddtc, ddtr, dalc, dalr, dd = _ssd_bwd(xbc, dtc, dtr, alc, alr, dexp, hs, dy, name + "_bwd")
        return jnp.concatenate([dx, db, dc], axis=1), ddtc, ddtr, dalc, dalr, dd

    op.defvjp(fwd, bwd)
    return op(xbc, dtc, dtr, alc, alr, dexp)


def gate_norm(y, z, w):
    return (rms_norm(y * (z * jax.nn.sigmoid(z)), w),)


def ssd_branch(xbc, z, dt_raw, conv_w, conv_b, dt_bias, a_log, d_skip, norm_w, *, name):
    s = xbc.shape[0]
    g = SSM_GROUPS
    conv = causal_conv(xbc, conv_w, name=name + "_conv")
    (xc,) = rowwise(lambda c, b: ((c + b) * jax.nn.sigmoid(c + b),), (conv,), (), (conv_b[None, :],),
                    name=name + "_silu", tile=min(256, s))
    (dt,) = rowwise(lambda r, b: (jax.nn.softplus(r + b),), (dt_raw,), (), (dt_bias[None, :],),
                    name=name + "_dt", tile=min(512, s))
    dt3 = dt.reshape(s, g, SSD_R)
    y = ssd_core(xc, dt3.transpose(1, 0, 2), dt3.transpose(1, 2, 0), a_log.reshape(g, 1, SSD_R),
                 a_log.reshape(g, SSD_R, 1), jnp.repeat(d_skip, SSM_HEADDIM).reshape(g, 1, SSD_GW), name=name + "_core")
    (out,) = rowwise(gate_norm, (y.reshape(s, g, SSD_GW), z.reshape(s, g, SSD_GW)), (),
                     (norm_w.reshape(g, SSD_GW),), name=name + "_gate", tile=min(128, s))
    return out.reshape(s, g * SSD_GW)


def rms_norm(x, w):
    return x * lax.rsqrt(jnp.mean(x * x, axis=-1, keepdims=True) + EPS) * w


def rope_matrix():
    half = QK_ROPE // 2
    j = jnp.arange(QK_DIM)
    src = jnp.where(j < QK_NOPE + half, j + half, j - half)
    sign = jnp.where(j < QK_NOPE, 0.0, jnp.where(j < QK_NOPE + half, -1.0, 1.0))
    return (jnp.arange(QK_DIM)[:, None] == src[None, :]).astype(F32) * sign[None, :]


def rope_tables_full(positions):
    inv_freq = 1.0 / (ROPE_THETA ** (jnp.arange(0, QK_ROPE, 2, dtype=F32) / QK_ROPE))
    ang = positions.astype(F32)[:, None] * inv_freq
    s = positions.shape[0]
    cos = jnp.concatenate([jnp.ones((s, QK_NOPE), F32), jnp.cos(ang), jnp.cos(ang)], axis=-1)
    sin = jnp.concatenate([jnp.zeros((s, QK_NOPE), F32), jnp.sin(ang), jnp.sin(ang)], axis=-1)
    return cos[:, None, :], sin[:, None, :]


def head_norm_rope(x, cos_full, sin_full, rot, w):
    t, h, d = x.shape
    y = rms_norm(x, w)
    partner = jnp.dot(y.reshape(t * h, d), rot, precision=HI, preferred_element_type=F32).reshape(t, h, d)
    return (y * cos_full + partner * sin_full,)


def _norm(x, w, *, name, tile=256):
    (y,) = rowwise(lambda x, w: (rms_norm(x, w),), (x,), (), (w[None, :],), name=name, tile=min(tile, x.shape[0]))
    return y


Q_LORA = 512
KV_LORA = 512
W_IN_PIECES = ("cq", "ckv", "kr", "z", "xbc", "dt", "ga", "gb")


def w_in_widths(d_model):
    d_inner = 2 * d_model
    conv_dim = d_inner + 2 * SSM_GROUPS * SSM_STATE
    return (Q_LORA, KV_LORA, QK_ROPE, d_inner, conv_dim, d_inner // SSM_HEADDIM, d_model, d_model)


LINEAR_NAMES = ("cq", "ckv", "kd", "z", "xbc", "ga", "gb", "w_uq", "w_ukv", "w_o_mla", "w_o_ssm", "w_out", "w_up",
                "w_down", "w_ple_gate", "w_ple")


def layer_forward(x, p_i, cos_full, sin_full, rot, wb, sm, sinks, tok=()):
    s, d = x.shape
    lin = lambda a, n: linear(a, wb[n], sinks[n], name="lin_" + n)
    h = _norm(x, sm["norm_mix_w"], name="norm_mix")
    c_q, c_kv, kd = lin(h, "cq"), lin(h, "ckv"), lin(h, "kd")
    z, xbc, g_a, g_b = lin(h, "z"), lin(h, "xbc"), lin(h, "ga"), lin(h, "gb")
    k_r, dt_raw = kd[:, :QK_ROPE], kd[:, QK_ROPE:]
    q = lin(_norm(c_q, sm["q_a_norm_w"], name="norm_qa"), "w_uq").reshape(s, MLA_HEADS, QK_DIM)
    kv = lin(_norm(c_kv, sm["kv_a_norm_w"], name="norm_kva"), "w_ukv").reshape(s, MLA_HEADS, QK_NOPE + V_DIM)
    k = jnp.concatenate([kv[..., :QK_NOPE], jnp.broadcast_to(k_r[:, None, :], (s, MLA_HEADS, QK_ROPE))], axis=-1)
    v = kv[..., QK_NOPE:]
    tq = min(128, s)
    (q,) = rowwise(head_norm_rope, (q,), (cos_full, sin_full), (sm["q_norm_w"][None, :],), tables=(rot,), name="q_rope", tile=tq)
    (k,) = rowwise(head_norm_rope, (k,), (cos_full, sin_full), (sm["k_norm_w"][None, :],), tables=(rot,), name="k_rope", tile=tq)
    hm = lambda a: a.transpose(1, 0, 2)
    o, tok = attention(hm(q), hm(k), hm(v), tok, name="attn")
    o = hm(o).reshape(s, MLA_HEADS * V_DIM)
    y_a = lin(o, "w_o_mla")
    y_ssd = ssd_branch(xbc, z, dt_raw, sm["conv_w"], sm["conv_b"], sm["dt_bias"], sm["a_log"], sm["d_skip"],
                       sm["ssm_norm_w"], name="ssd")
    y_b = lin(y_ssd, "w_o_ssm")
    sig = jax.nn.sigmoid
    tr = min(256, s)
    (merged,) = rowwise(lambda ga, gb, ya, yb: (sig(ga) * ya + sig(gb) * yb,), (g_a, g_b, y_a, y_b), (), (),
                        name="merge", tile=tr)
    x = x + lin(merged, "w_out")
    up = lin(_norm(x, sm["norm_mlp_w"], name="norm_mlp"), "w_up")
    (act,) = rowwise(lambda u: (jnp.square(jnp.maximum(u, 0.0)),), (up,), (), (), name="relu2", tile=tr)
    x = x + lin(act, "w_down")
    pg = lin(_norm(x, sm["ple_norm_w"], name="norm_ple"), "w_ple_gate")
    pe = lin(p_i, "w_ple")
    (x,) = rowwise(lambda x, pe, pg: (x + pe * sig(pg),), (x, pe, pg), (), (), name="ple_add", tile=tr)
    return x, tok


def loss_and_cotangent(y, target):
    s, d = y.shape

    def f(y, t):
        e = y - t
        return e * (1.0 / d), 0.5 * jnp.sum(jnp.sum(e * e, axis=1, keepdims=True) * (1.0 / d), axis=0, keepdims=True)

    dy, part = _tiled_call(f, (y, target), (), min(256, s), "loss", 1)
    return dy, part[0, 0]


ADAM_BLOCK_ELEMS = 256 * 1024


def adamw(w, g, m, v, *, name):
    rows, cols = w.shape
    budget = max(8, ADAM_BLOCK_ELEMS // cols)
    tile = _pick(rows, tuple(t for t in (512, 256, 128, 64, 32, 16, 8) if t <= budget))

    def f(w, g, m, v):
        m = ADAM_B1 * m + (1.0 - ADAM_B1) * g
        v = ADAM_B2 * v + (1.0 - ADAM_B2) * jnp.square(g)
        m_hat = m / (1.0 - ADAM_B1 ** ADAM_STEP)
        v_hat = v / (1.0 - ADAM_B2 ** ADAM_STEP)
        delta = -ADAM_LR * (m_hat / (jnp.sqrt(v_hat) + ADAM_EPS) + ADAM_WD * w)
        return delta, m, v

    return _tiled_call(f, (w, g, m, v), (), tile, name, 0)


MESH_ID = pl.DeviceIdType.MESH
N_CHIPS = 4
_ANY = pl.BlockSpec(memory_space=pl.ANY)


def _place():
    return lax.axis_index("x"), lax.axis_index("y"), lax.axis_index("c")


def _other_chips(x, y):
    return [(1 - x, y), (x, 1 - y), (1 - x, 1 - y)]


def _rcopy(src, dst, send_sem, recv_sem, device):
    return pltpu.make_async_remote_copy(src_ref=src, dst_ref=dst, send_sem=send_sem, recv_sem=recv_sem,
                                        device_id=device, device_id_type=MESH_ID)


def _sems(n, k):
    return pltpu.SemaphoreType.DMA((n, k))


def _comm_call(body, name, ins, out_shapes, scratch, aliases=None):
    return pl.pallas_call(
        body,
        name=name,
        in_specs=[_ANY] * len(ins),
        out_specs=[_ANY] * len(out_shapes),
        out_shape=out_shapes,
        scratch_shapes=scratch,
        input_output_aliases=aliases or {},
        compiler_params=pltpu.CompilerParams(has_side_effects=True),
    )(*ins)


SPLIT_ROWS = 32


def gather_shards(shards, *, name):
    n = len(shards)
    split = [s.shape[0] % SPLIT_ROWS == 0 for s in shards]

    def body(*refs):
        srcs, outs = refs[:n], refs[n:2 * n]
        send_sems, recv_sems = refs[2 * n:]
        x, y, c = _place()
        sibling = (x, y, 1 - c)
        chips = _other_chips(x, y)
        me = 2 * x + y

        def part(t, slot, h):
            if not split[t]:
                return outs[t].at[slot]
            half = shards[t].shape[0] // 2
            return outs[t].at[slot, pl.ds(h * half, half), :]

        def own(t):
            if not split[t]:
                return srcs[t]
            half = shards[t].shape[0] // 2
            return srcs[t].at[pl.ds(c * half, half), :]

        sent = []
        for t in range(n):
            for j, chip in enumerate(chips):
                sent.append(_rcopy(own(t), part(t, me, c), send_sems.at[t, j], recv_sems.at[t, j], (*chip, c)))
                sent[-1].start()
        for t in range(n):
            sent.append(_rcopy(srcs[t], outs[t].at[me], send_sems.at[t, 6], recv_sems.at[t, 6], sibling))
            sent[-1].start()
        for t in range(n):
            for j, (cx, cy) in enumerate(chips):
                got = part(t, 2 * cx + cy, c)
                _rcopy(got, got, send_sems.at[t, j], recv_sems.at[t, j], (cx, cy, c)).wait_recv()
                if split[t]:
                    sent.append(_rcopy(got, got, send_sems.at[t, 3 + j], recv_sems.at[t, 3 + j], sibling))
                    sent[-1].start()
        for t in range(n):
            if split[t]:
                for j, (cx, cy) in enumerate(chips):
                    got = part(t, 2 * cx + cy, 1 - c)
                    _rcopy(got, got, send_sems.at[t, 3 + j], recv_sems.at[t, 3 + j], sibling).wait_recv()
        for t in range(n):
            _rcopy(srcs[t], outs[t].at[me], send_sems.at[t, 6], recv_sems.at[t, 6], sibling).wait_recv()
        for cp in sent:
            cp.wait_send()

    out_shapes = [jax.ShapeDtypeStruct((N_CHIPS,) + s.shape, s.dtype) for s in shards]
    return _comm_call(body, name, shards, out_shapes, [_sems(n, 7), _sems(n, 7)])


def sibling_take_half(gs, *, name):
    n = len(gs)

    def body(*refs):
        g_refs, a_refs = refs[:n], refs[n:2 * n]
        send_sems, recv_sems = refs[2 * n:]
        x, y, c = _place()
        copies = []
        for t in range(n):
            half = gs[t].shape[1] // 2
            copies.append(_rcopy(g_refs[t].at[:, pl.ds((1 - c) * half, half), :], a_refs[t], send_sems.at[t, 0],
                                 recv_sems.at[t, 0], (x, y, 1 - c)))
            copies[-1].start()
        for cp in copies:
            cp.wait()

    out_shapes = [jax.ShapeDtypeStruct((g.shape[0], g.shape[1] // 2, g.shape[2]), g.dtype) for g in gs]
    return _comm_call(body, name, gs, out_shapes, [_sems(n, 1), _sems(n, 1)])


ELEMWISE_BLOCK_ELEMS = 256 * 1024


def _row_tile(rows, cols):
    budget = max(16, ELEMWISE_BLOCK_ELEMS // cols)
    return _pick(rows, tuple(t for t in (1024, 512, 256, 128, 64, 32, 16) if t <= budget))


def _core_and_chip():
    x, y, c = _place()
    return jnp.stack([c, 2 * x + y]).astype(jnp.int32)


def pair_add(g, a, *, name):
    n, rows, cols = g.shape
    half = rows // 2
    tile = _row_tile(half, cols)
    nb = half // tile

    def body(who_ref, g_ref, a_ref, o_ref):
        o_ref[...] = (g_ref[...] + a_ref[...]).astype(o_ref.dtype)

    return pl.pallas_call(
        body,
        name=name,
        grid_spec=pltpu.PrefetchScalarGridSpec(
            num_scalar_prefetch=1,
            grid=(n, nb),
            in_specs=[
                pl.BlockSpec((None, tile, cols), lambda j, i, who: (j, who[0] * nb + i, 0)),
                pl.BlockSpec((None, tile, cols), lambda j, i, who: (j, i, 0)),
            ],
            out_specs=pl.BlockSpec((None, tile, cols), lambda j, i, who: (j, i, 0)),
        ),
        out_shape=jax.ShapeDtypeStruct((n, half, cols), BF16),
        compiler_params=_params(("parallel", "parallel")),
    )(_core_and_chip(), g, a)


def exchange_chip_slots(ps, *, name):
    n = len(ps)

    def body(*refs):
        p_refs, b_refs = refs[:n], refs[n:2 * n]
        send_sems, recv_sems = refs[2 * n:]
        x, y, c = _place()
        me = 2 * x + y
        chips = _other_chips(x, y)
        sends = []
        for t in range(n):
            for j, (cx, cy) in enumerate(chips):
                sends.append(_rcopy(p_refs[t].at[2 * cx + cy], b_refs[t].at[me], send_sems.at[t, j], recv_sems.at[t, j],
                                    (cx, cy, c)))
                sends[-1].start()
        for t in range(n):
            for j, (cx, cy) in enumerate(chips):
                got = b_refs[t].at[2 * cx + cy]
                _rcopy(got, got, send_sems.at[t, j], recv_sems.at[t, j], (cx, cy, c)).wait_recv()
        for cp in sends:
            cp.wait_send()

    out_shapes = [jax.ShapeDtypeStruct(p.shape, p.dtype) for p in ps]
    return _comm_call(body, name, ps, out_shapes, [_sems(n, 3), _sems(n, 3)])


def chips_add(g, a, b, *, name):
    n, rows, cols = g.shape
    half = rows // 2
    tile = _row_tile(half, cols)
    nb = half // tile

    def body(who_ref, g_ref, a_ref, *rest):
        o_ref = rest[-1]
        acc = g_ref[...] + a_ref[...]
        for b_ref in rest[:-1]:
            acc = acc + b_ref[...].astype(F32)
        o_ref[...] = acc

    other = lambda k: pl.BlockSpec((None, tile, cols), lambda i, who, k=k: ((who[1] + k) % n, i, 0))
    return pl.pallas_call(
        body,
        name=name,
        grid_spec=pltpu.PrefetchScalarGridSpec(
            num_scalar_prefetch=1,
            grid=(nb,),
            in_specs=[
                pl.BlockSpec((None, tile, cols), lambda i, who: (who[1], who[0] * nb + i, 0)),
                pl.BlockSpec((None, tile, cols), lambda i, who: (who[1], i, 0)),
            ] + [other(k) for k in range(1, n)],
            out_specs=pl.BlockSpec((tile, cols), lambda i, who: (who[0] * nb + i, 0)),
        ),
        out_shape=jax.ShapeDtypeStruct((rows, cols), F32),
        compiler_params=_params(("parallel",)),
    )(_core_and_chip(), g, a, *([b] * (n - 1)))


def sibling_join_halves(rs, *, name):
    n = len(rs)

    def body(*refs):
        r_refs = refs[n:2 * n]
        send_sems, recv_sems = refs[2 * n:]
        x, y, c = _place()
        sibling = (x, y, 1 - c)
        copies = []
        for t in range(n):
            half = rs[t].shape[0] // 2
            mine = r_refs[t].at[pl.ds(c * half, half), :]
            copies.append(_rcopy(mine, mine, send_sems.at[t, 0], recv_sems.at[t, 0], sibling))
            copies[-1].start()
        for t in range(n):
            half = rs[t].shape[0] // 2
            got = r_refs[t].at[pl.ds((1 - c) * half, half), :]
            _rcopy(got, got, send_sems.at[t, 0], recv_sems.at[t, 0], sibling).wait_recv()
        for cp in copies:
            cp.wait_send()

    out_shapes = [jax.ShapeDtypeStruct(r.shape, r.dtype) for r in rs]
    return _comm_call(body, name, rs, out_shapes, [_sems(n, 1), _sems(n, 1)], aliases={t: t for t in range(n)})


def reduce_to_owner(gs, *, name):
    gs, a, p = reduce_pairs(gs, name=name)
    return reduce_finish(gs, a, exchange_chip_slots(p, name=name + "_chips"), name=name)


def reduce_pairs(gs, *, name):
    gs = list(gs)
    a = sibling_take_half(gs, name=name + "_pair")
    p = [pair_add(g, ai, name=name + "_pair_add") for g, ai in zip(gs, a)]
    return gs, a, p


def reduce_finish(gs, a, b, *, name):
    f = [chips_add(g, ai, bi, name=name + "_chips_add") for g, ai, bi in zip(gs, a, b)]
    return sibling_join_halves(f, name=name + "_join")


def allreduce_small(v, *, name):
    rows, cols = v.shape

    def body(v_ref, o_ref, buf, send_sems, recv_sems):
        x, y, c = _place()
        me = 4 * x + 2 * y + c
        buf[me] = v_ref[...]
        copies = []
        for k in range(1, 8):
            bx, by, bc = (k >> 2) & 1, (k >> 1) & 1, k & 1
            peer = (x if bx == 0 else 1 - x, y if by == 0 else 1 - y, c if bc == 0 else 1 - c)
            copies.append(_rcopy(v_ref, buf.at[me], send_sems.at[k - 1], recv_sems.at[k - 1], peer))
        for cp in copies:
            cp.start()
        for k in range(1, 8):
            bx, by, bc = (k >> 2) & 1, (k >> 1) & 1, k & 1
            px, py, pc = (x if bx == 0 else 1 - x, y if by == 0 else 1 - y, c if bc == 0 else 1 - c)
            _rcopy(v_ref, buf.at[4 * px + 2 * py + pc], send_sems.at[k - 1], recv_sems.at[k - 1], (px, py, pc)).wait_recv()
        for cp in copies:
            cp.wait_send()
        acc = buf[0]
        for j in range(1, 8):
            acc = acc + buf[j]
        o_ref[...] = acc

    return pl.pallas_call(
        body,
        name=name,
        in_specs=[pl.BlockSpec(memory_space=pltpu.VMEM)],
        out_specs=pl.BlockSpec(memory_space=pltpu.VMEM),
        out_shape=jax.ShapeDtypeStruct((rows, cols), v.dtype),
        scratch_shapes=[pltpu.VMEM((8, rows, cols), v.dtype), pltpu.SemaphoreType.DMA((7,)), pltpu.SemaphoreType.DMA((7,))],
        compiler_params=pltpu.CompilerParams(has_side_effects=True, vmem_limit_bytes=V7X_VMEM_LIMIT),
    )(v)


BIG = (("w_in", 1), ("w_uq", 1), ("w_ukv", 1), ("w_o_mla", 0), ("w_o_ssm", 0), ("w_out", 0), ("w_up", 1),
       ("w_down", 0), ("w_ple_gate", 0), ("w_ple", 1))
SHARDED = BIG + (("conv_w", 1),)
SMALL = ("norm_mix_w", "q_a_norm_w", "kv_a_norm_w", "q_norm_w", "k_norm_w", "conv_b", "dt_bias", "a_log", "d_skip",
         "ssm_norm_w", "norm_mlp_w", "ple_norm_w")
WEIGHTS = ("norm_mix_w", "w_in", "q_a_norm_w", "w_uq", "kv_a_norm_w", "w_ukv", "q_norm_w", "k_norm_w", "w_o_mla", "conv_w",
           "conv_b", "dt_bias", "a_log", "d_skip", "ssm_norm_w", "w_o_ssm", "w_out", "norm_mlp_w", "w_up", "w_down",
           "ple_norm_w", "w_ple_gate", "w_ple")


def _to_rows(flat, cols, row_multiple):
    n = flat.shape[-1]
    rows = -(-n // (cols * row_multiple)) * row_multiple
    pad = [(0, 0)] * (flat.ndim - 1) + [(0, rows * cols - n)]
    return jnp.pad(flat, pad).reshape(flat.shape[:-1] + (rows, cols))


def _w_in_ranges(d_model):
    out, lo = {}, 0
    for n, wd in zip(W_IN_PIECES, w_in_widths(d_model)):
        out[n] = (lo, lo + wd)
        lo += wd
    return out


def w_in_pieces(w3):
    _, k, c = w3.shape
    pc = {}
    for n, (lo, hi) in _w_in_ranges(k).items():
        cuts = [w3[j][:, max(lo, j * c) - j * c:min(hi, (j + 1) * c) - j * c]
                for j in range(N_CHIPS) if max(lo, j * c) < min(hi, (j + 1) * c)]
        pc[n] = cuts[0] if len(cuts) == 1 else jnp.concatenate(cuts, axis=1)
    pc["kd"] = jnp.concatenate([pc.pop("kr"), pc.pop("dt")], axis=1)
    return pc


def w_in_shard_grads(g, k, c):
    g = dict(g)
    g["kr"], g["dt"] = g["kd"][:, :QK_ROPE], g["kd"][:, QK_ROPE:]
    shards = []
    for j in range(N_CHIPS):
        cuts = []
        for n, (lo, hi) in _w_in_ranges(k).items():
            a, b = max(lo, j * c), min(hi, (j + 1) * c)
            if a < b:
                cuts.append(g[n][:, a - lo:b - lo])
        shards.append(jnp.concatenate(cuts, axis=1))
    return jnp.stack(shards)


def kernel(x, p, positions, norm_mix_w, w_in, q_a_norm_w, w_uq, kv_a_norm_w, w_ukv, q_norm_w, k_norm_w, w_o_mla, conv_w, conv_b, dt_bias, a_log, d_skip, ssm_norm_w, w_o_ssm, w_out, norm_mlp_w, w_up, w_down, ple_norm_w, w_ple_gate, w_ple, loss_target, m_norm_mix_w, m_w_in, m_q_a_norm_w, m_w_uq, m_kv_a_norm_w, m_w_ukv, m_q_norm_w, m_k_norm_w, m_w_o_mla, m_conv_w, m_conv_b, m_dt_bias, m_a_log, m_d_skip, m_ssm_norm_w, m_w_o_ssm, m_w_out, m_norm_mlp_w, m_w_up, m_w_down, m_ple_norm_w, m_w_ple_gate, m_w_ple, v_norm_mix_w, v_w_in, v_q_a_norm_w, v_w_uq, v_kv_a_norm_w, v_w_ukv, v_q_norm_w, v_k_norm_w, v_w_o_mla, v_conv_w, v_conv_b, v_dt_bias, v_a_log, v_d_skip, v_ssm_norm_w, v_w_o_ssm, v_w_out, v_norm_mlp_w, v_w_up, v_w_down, v_ple_norm_w, v_w_ple_gate, v_w_ple):
    a = dict(locals())
    x, p, pos, target = a["x"][0], a["p"][:, 0], a["positions"][0], a["loss_target"][0]
    depth = a["w_in"].shape[0]
    shard_shapes = {n: tuple(a[n].shape[1:]) for n, _ in SHARDED}
    cos_full, sin_full = rope_tables_full(pos)
    rot = rope_matrix()

    wb, small, sinks = [], [], []
    for i in range(depth):
        got = gather_shards([a[n][i].astype(BF16) for n, _ in BIG] + [a["conv_w"][i]], name="gather_weights")
        full = dict(zip([n for n, _ in SHARDED], got))
        w_i = w_in_pieces(full["w_in"])
        for n, ax in BIG[1:]:
            w_i[n] = full[n] if ax == 1 else full[n].reshape((-1, full[n].shape[-1]))
        wb.append(w_i)
        sm_i = {n: a[n][i] for n in SMALL}
        sm_i["conv_w"] = full["conv_w"].transpose(1, 0, 2).reshape(CONV_WIDTH, -1)
        small.append(sm_i)
        sinks.append({n: jnp.zeros(w_i[n].shape, F32) for n in LINEAR_NAMES})

    tok = tuple(jnp.zeros((N_CHIPS, shard_shapes[n][0] // 2, shard_shapes[n][1]), BF16) for n, _ in BIG)
    layer_vjps = []
    for i in range(depth):
        f_i = functools.partial(layer_forward, p_i=p[i], cos_full=cos_full, sin_full=sin_full, rot=rot, wb=wb[i])
        g_i = lambda x, sm, sk, tk, f=f_i: f(x, sm=sm, sinks=sk, tok=tk)
        (x, _), vjp_i = jax.vjp(g_i, x, small[i], sinks[i], tok if i < depth - 1 else ())
        layer_vjps.append(vjp_i)
    dx, loss_part = loss_and_cotangent(x, target)
    loss = lax.psum(loss_part, ("x", "y", "c"))

    per_layer, d_small, pending = [None] * depth, [None] * depth, None
    for i in reversed(range(depth)):
        payload = tuple(pending[2]) if pending is not None else ()
        dx, d_small[i], d_sinks_i, got = layer_vjps[i]((dx, payload))
        if pending is not None:
            per_layer[i + 1] = reduce_finish(pending[0], pending[1], list(got), name="reduce_grads")
        g_i = [w_in_shard_grads(d_sinks_i, *shard_shapes["w_in"])]
        g_i += [d_sinks_i[n].reshape((N_CHIPS,) + shard_shapes[n]) for n, _ in BIG[1:]]
        pending = reduce_pairs(g_i, name="reduce_grads")
    per_layer[0] = reduce_finish(pending[0], pending[1], exchange_chip_slots(pending[2], name="reduce_grads_chips"),
                                 name="reduce_grads")
    grads = {n: jnp.stack([per_layer[i][t] for i in range(depth)]) for t, (n, _) in enumerate(BIG)}

    small_names = SMALL + ("conv_w",)
    flat = jnp.concatenate([d_small[i][n].reshape(-1) for i in range(depth) for n in small_names])
    n_small = flat.shape[0]
    red = allreduce_small(_to_rows(flat, 128, 8), name="reduce_small").reshape(-1)[:n_small]
    per = n_small // depth
    off = 0
    for n in SMALL:
        width = a[n].shape[-1]
        grads[n] = jnp.stack([red[i * per + off:i * per + off + width] for i in range(depth)])
        off += width
    conv_c = shard_shapes["conv_w"][1]
    conv_full = jnp.stack([red[i * per + off:i * per + off + CONV_WIDTH * N_CHIPS * conv_c] for i in range(depth)])
    chip = 2 * lax.axis_index("x") + lax.axis_index("y")
    grads["conv_w"] = lax.dynamic_index_in_dim(conv_full.reshape(depth, CONV_WIDTH, N_CHIPS, conv_c), chip, axis=2,
                                               keepdims=False)

    deltas, new_m, new_v = {}, {}, {}
    two_d = lambda t: t.reshape(-1, t.shape[-1])
    for n in WEIGHTS:
        d, m, v = adamw(two_d(a[n]), two_d(grads[n]), two_d(a["m_" + n]), two_d(a["v_" + n]), name="adamw")
        deltas[n], new_m[n], new_v[n] = d.reshape(a[n].shape), m.reshape(a[n].shape), v.reshape(a[n].shape)

    return (loss, dx[None], *[grads[n].reshape(a[n].shape) for n in WEIGHTS], *[deltas[n] for n in WEIGHTS],
            *[new_m[n] for n in WEIGHTS], *[new_v[n] for n in WEIGHTS])
```

```python
import functools

import jax
import jax.numpy as jnp
from jax import lax
from jax.experimental import pallas as pl
from jax.experimental.pallas import tpu as pltpu

F32 = jnp.float32
BF16 = jnp.bfloat16
HI = lax.Precision.HIGHEST

EPS = 1e-6
MLA_HEADS = 16
QK_NOPE = 128
QK_ROPE = 64
QK_DIM = QK_NOPE + QK_ROPE
V_DIM = 128
ROPE_THETA = 10000.0
ATT_CHUNK = 64
SSM_GROUPS = 8
SSM_HEADDIM = 64
SSM_STATE = 128
CONV_WIDTH = 4
ADAM_LR = 0.001
ADAM_B1 = 0.9
ADAM_B2 = 0.999
ADAM_EPS = 1e-08
ADAM_WD = 0.01
ADAM_STEP = 10

V7X_VMEM_LIMIT = 56 * 1024 * 1024


def _params(sem=None, **kw):
    return pltpu.CompilerParams(dimension_semantics=sem, vmem_limit_bytes=V7X_VMEM_LIMIT, **kw)


def _pick(n, prefs):
    for t in prefs:
        if n % t == 0:
            return t
    return n


MATMUL_OPERAND_BYTES = 24 * 1024 * 1024


def matmul(a, b, *, ta=False, tb=False, out_blocks=0, out_dtype=F32, name):
    m, k = (a.shape[1], a.shape[0]) if ta else a.shape
    blocked = b.ndim == 3
    if blocked:
        nb, rows, c = b.shape
        k2, n = (nb * c, rows) if tb else (rows, nb * c)
    else:
        k2, n = (b.shape[1], b.shape[0]) if tb else b.shape
    assert k == k2, (a.shape, b.shape, ta, tb)
    n_unit = n // out_blocks if out_blocks else (c if blocked and not tb else n)
    k_unit = c if blocked and tb else k
    tm = _pick(m, (1024, 512, 256, 128))
    tn = _pick(n_unit, (1024, 512, 256, 128))
    in_bytes = tm * a.dtype.itemsize + tn * b.dtype.itemsize
    tk = _pick(k_unit, tuple(t for t in (2048, 1024, 512, 256, 128) if 2 * t * in_bytes <= MATMUL_OPERAND_BYTES))
    nk = k // tk
    dn = (((0 if ta else 1,), (1 if tb else 0,)), ((), ()))

    def body(a_ref, b_ref, o_ref, *acc):
        part = lambda: lax.dot_general(a_ref[...].astype(BF16), b_ref[...].astype(BF16), dn, preferred_element_type=F32)
        if nk == 1:
            o_ref[...] = part().astype(o_ref.dtype)
            return
        (acc_ref,) = acc
        kk = pl.program_id(2)

        @pl.when(kk == 0)
        def _():
            acc_ref[...] = jnp.zeros_like(acc_ref)

        acc_ref[...] += part()

        @pl.when(kk == nk - 1)
        def _():
            o_ref[...] = acc_ref[...].astype(o_ref.dtype)

    a_spec = pl.BlockSpec((tk, tm), lambda i, j, kk: (kk, i)) if ta else pl.BlockSpec((tm, tk), lambda i, j, kk: (i, kk))
    if not blocked:
        b_spec = pl.BlockSpec((tn, tk), lambda i, j, kk: (j, kk)) if tb else pl.BlockSpec((tk, tn), lambda i, j, kk: (kk, j))
    elif tb:
        kb = c // tk
        b_spec = pl.BlockSpec((None, tn, tk), lambda i, j, kk: (kk // kb, j, kk % kb))
    else:
        cb = c // tn
        b_spec = pl.BlockSpec((None, tk, tn), lambda i, j, kk: (j // cb, kk, j % cb))
    if out_blocks:
        ob = n_unit // tn
        out_spec = pl.BlockSpec((None, tm, tn), lambda i, j, kk: (j // ob, i, j % ob))
        out_shape = jax.ShapeDtypeStruct((out_blocks, m, n_unit), out_dtype)
    else:
        out_spec = pl.BlockSpec((tm, tn), lambda i, j, kk: (i, j))
        out_shape = jax.ShapeDtypeStruct((m, n), out_dtype)
    return pl.pallas_call(
        body,
        name=name,
        grid=(m // tm, n // tn, nk),
        in_specs=[a_spec, b_spec],
        out_specs=out_spec,
        out_shape=out_shape,
        scratch_shapes=[pltpu.VMEM((tm, tn), F32)] if nk > 1 else [],
        compiler_params=_params(("parallel", "parallel", "arbitrary")),
    )(a, b)


def linear(a, w, sink, *, name):
    @jax.custom_vjp
    def op(a, w, sink):
        return matmul(a, w, name=name + "_fwd")

    def fwd(a, w, sink):
        return op(a, w, sink), (a, w)

    def bwd(res, ct):
        a, w = res
        da = matmul(ct, w, tb=True, name=name + "_bwd_da")
        dw = matmul(a, ct, ta=True, out_blocks=w.shape[0] if w.ndim == 3 else 0, name=name + "_bwd_dw")
        return da.astype(a.dtype), jnp.zeros_like(w), dw

    op.defvjp(fwd, bwd)
    return op(a, w, sink)


def _tiled_call(fn, tiled, whole, tile, name, n_acc):
    rows = tiled[0].shape[0]
    assert rows % tile == 0
    t_avals = [jax.ShapeDtypeStruct((tile,) + a.shape[1:], a.dtype) for a in tiled]
    w_avals = [jax.ShapeDtypeStruct(a.shape, a.dtype) for a in whole]
    outs = jax.eval_shape(fn, *t_avals, *w_avals)
    n_in = len(tiled) + len(whole)
    n_t = len(outs) - n_acc

    def body(*refs):
        res = fn(*[r[...] for r in refs[:n_in]])
        o_refs = refs[n_in:]
        for r, v in zip(o_refs[:n_t], res[:n_t]):
            r[...] = v.astype(r.dtype)
        if n_acc:
            first = pl.program_id(0) == 0

            @pl.when(first)
            def _():
                for r, v in zip(o_refs[n_t:], res[n_t:]):
                    r[...] = v.astype(F32)

            @pl.when(jnp.logical_not(first))
            def _():
                for r, v in zip(o_refs[n_t:], res[n_t:]):
                    r[...] += v.astype(F32)

    def tspec(a):
        nd = len(a.shape)
        return pl.BlockSpec((tile,) + tuple(a.shape[1:]), lambda i, nd=nd: (i,) + (0,) * (nd - 1))

    def wspec(a):
        nd = len(a.shape)
        return pl.BlockSpec(tuple(a.shape), lambda i, nd=nd: (0,) * nd)

    out_shape = [jax.ShapeDtypeStruct((rows,) + o.shape[1:], o.dtype) for o in outs[:n_t]]
    out_shape += [jax.ShapeDtypeStruct(o.shape, F32) for o in outs[n_t:]]
    out_specs = [tspec(o) for o in out_shape[:n_t]] + [wspec(o) for o in out_shape[n_t:]]
    return pl.pallas_call(
        body,
        name=name,
        grid=(rows // tile,),
        in_specs=[tspec(a) for a in tiled] + [wspec(a) for a in whole],
        out_specs=out_specs,
        out_shape=out_shape,
        compiler_params=_params(("arbitrary",) if n_acc else ("parallel",)),
    )(*tiled, *whole)


def rowwise(f, rows, consts, params, *, name, tile, tables=()):
    rows, consts, tables, params = tuple(rows), tuple(consts), tuple(tables), tuple(params)
    nr, nc, ntab, npar = len(rows), len(consts), len(tables), len(params)

    @jax.custom_vjp
    def op(rows, consts, tables, params):
        return tuple(_tiled_call(f, rows + consts, tables + params, tile, name + "_fwd", 0))

    def fwd(rows, consts, tables, params):
        return op(rows, consts, tables, params), (rows, consts, tables, params)

    def bwd(res, cts):
        rows, consts, tables, params = res
        ncts = len(cts)

        def g(*args):
            r = args[:nr]
            c = args[nr:nr + nc]
            ct = args[nr + nc:nr + nc + ncts]
            tab = args[nr + nc + ncts:nr + nc + ncts + ntab]
            p = args[nr + nc + ncts + ntab:]
            _, vjp = jax.vjp(lambda *rp: f(*rp[:nr], *c, *tab, *rp[nr:]), *r, *p)
            return tuple(vjp(tuple(ct)))

        outs = _tiled_call(g, rows + consts + tuple(cts), tables + params, tile, name + "_bwd", npar)
        d_rows = tuple(o.astype(r.dtype) for o, r in zip(outs[:nr], rows))
        d_params = tuple(o.astype(p.dtype) for o, p in zip(outs[nr:], params))
        zeros = lambda xs: tuple(jnp.zeros_like(a) for a in xs)
        return d_rows, zeros(consts), zeros(tables), d_params

    op.defvjp(fwd, bwd)
    return op(rows, consts, tables, params)


ATT_TILE = 512
LOG2E = 1.4426950408889634
HOSTED_IN_DQ = 4
_NT = (((1,), (1,)), ((), ()))
_TN = (((0,), (0,)), ((), ()))


def _chunk_mask(row0, col0, shape):
    r = (row0 + lax.broadcasted_iota(jnp.int32, shape, 0)) // ATT_CHUNK
    c = (col0 + lax.broadcasted_iota(jnp.int32, shape, 1)) // ATT_CHUNK
    return c <= r


def _split_rows(shape):
    return shape[0] % SPLIT_ROWS == 0


def _hosted_gather(src_refs, out_refs, send_sems, recv_sems, first, last):
    x, y, c = _place()
    me = 2 * x + y
    chips = _other_chips(x, y)
    n = len(src_refs)

    def rows(t, ref, h):
        if not _split_rows(src_refs[t].shape):
            return ref
        half = src_refs[t].shape[0] // 2
        return ref.at[pl.ds(h * half, half), :]

    def sends():
        return [_rcopy(rows(t, src_refs[t], c), rows(t, out_refs[t].at[me], c), send_sems.at[t, j], recv_sems.at[t, j],
                       (*chip, c)) for t in range(n) for j, chip in enumerate(chips)]

    @pl.when(first)
    def _():
        for cp in sends():
            cp.start()

    @pl.when(last)
    def _():
        for t in range(n):
            for j, (cx, cy) in enumerate(chips):
                got = rows(t, out_refs[t].at[2 * cx + cy], c)
                _rcopy(got, got, send_sems.at[t, j], recv_sems.at[t, j], (cx, cy, c)).wait_recv()
        for cp in sends():
            cp.wait_send()


def _gather_host_args(shards):
    n = len(shards)
    if not n:
        return [], [], [], []
    any_spec = pl.BlockSpec(memory_space=pl.ANY)
    shapes = [jax.ShapeDtypeStruct((N_CHIPS,) + s.shape, s.dtype) for s in shards]
    return [any_spec] * n, [any_spec] * n, shapes, [pltpu.SemaphoreType.DMA((n, 3)), pltpu.SemaphoreType.DMA((n, 3))]


def _attention_fwd(q, k, v, name, shards=()):
    h, s, dq = q.shape
    dv = v.shape[-1]
    t = min(ATT_TILE, s)
    scale = dq ** -0.5
    ng = len(shards)

    def body(*refs):
        q_ref, k_ref, v_ref = refs[:3]
        o_ref, lse_ref = refs[3 + ng:5 + ng]
        k_scr, v_scr = refs[5 + 2 * ng:7 + 2 * ng]
        i = pl.program_id(1)
        if ng:
            hh = pl.program_id(0)
            _hosted_gather(refs[3:3 + ng], refs[5 + ng:5 + 2 * ng], refs[-2], refs[-1],
                           jnp.logical_and(hh == 0, i == 0), jnp.logical_and(hh == h - 1, i == s // t - 1))

        @pl.when(i == 0)
        def _():
            k_scr[...] = k_ref[...].astype(BF16)
            v_scr[...] = v_ref[...].astype(BF16)

        qb = (q_ref[...] * (scale * LOG2E)).astype(BF16)

        def block(j, carry, masked):
            m, l, acc = carry
            off = pl.multiple_of(j * t, t)
            kj = k_scr[pl.ds(off, t), :]
            vj = v_scr[pl.ds(off, t), :]
            sc = lax.dot_general(qb, kj, _NT, preferred_element_type=F32)
            if masked:
                sc = jnp.where(_chunk_mask(i * t, j * t, sc.shape), sc, -jnp.inf)
            m_new = jnp.maximum(m, jnp.max(sc, axis=1, keepdims=True))
            p = jnp.exp2(sc - m_new)
            alpha = jnp.exp2(m - m_new)
            l = alpha * l + jnp.sum(p, axis=1, keepdims=True)
            acc = alpha * acc + jnp.dot(p.astype(BF16), vj, preferred_element_type=F32)
            return m_new, l, acc

        init = (jnp.full((t, 1), -jnp.inf, F32), jnp.zeros((t, 1), F32), jnp.zeros((t, dv), F32))
        carry = lax.fori_loop(0, i, lambda j, c: block(j, c, False), init)
        m, l, acc = block(i, carry, True)
        o_ref[...] = acc / l
        lse_ref[...] = m + jnp.log2(l)

    g_in, g_out, g_shapes, g_scratch = _gather_host_args(shards)
    return pl.pallas_call(
        body,
        name=name,
        grid=(h, s // t),
        in_specs=[
            pl.BlockSpec((None, t, dq), lambda hh, i: (hh, i, 0)),
            pl.BlockSpec((None, s, dq), lambda hh, i: (hh, 0, 0)),
            pl.BlockSpec((None, s, dv), lambda hh, i: (hh, 0, 0)),
        ] + g_in,
        out_specs=[
            pl.BlockSpec((None, t, dv), lambda hh, i: (hh, i, 0)),
            pl.BlockSpec((None, t, 1), lambda hh, i: (hh, i, 0)),
        ] + g_out,
        out_shape=[jax.ShapeDtypeStruct((h, s, dv), F32), jax.ShapeDtypeStruct((h, s, 1), F32)] + g_shapes,
        scratch_shapes=[pltpu.VMEM((s, dq), BF16), pltpu.VMEM((s, dv), BF16)] + g_scratch,
        compiler_params=_params(("arbitrary", "arbitrary"), has_side_effects=bool(ng)),
    )(q, k, v, *shards)


def _hosted_exchange(p_refs, b_refs, send_sems, recv_sems, first, last):
    x, y, c = _place()
    me = 2 * x + y
    chips = _other_chips(x, y)
    n = len(p_refs)

    def sends():
        return [_rcopy(p_refs[t].at[2 * cx + cy], b_refs[t].at[me], send_sems.at[t, j], recv_sems.at[t, j], (cx, cy, c))
                for t in range(n) for j, (cx, cy) in enumerate(chips)]

    @pl.when(first)
    def _():
        for cp in sends():
            cp.start()

    @pl.when(last)
    def _():
        for t in range(n):
            for j, (cx, cy) in enumerate(chips):
                got = b_refs[t].at[2 * cx + cy]
                _rcopy(got, got, send_sems.at[t, j], recv_sems.at[t, j], (cx, cy, c)).wait_recv()
        for cp in sends():
            cp.wait_send()


def _host_args(hosted):
    n = len(hosted)
    if not n:
        return [], [], [], []
    any_spec = pl.BlockSpec(memory_space=pl.ANY)
    shapes = [jax.ShapeDtypeStruct(p.shape, p.dtype) for p in hosted]
    return [any_spec] * n, [any_spec] * n, shapes, [pltpu.SemaphoreType.DMA((n, 3)), pltpu.SemaphoreType.DMA((n, 3))]


def _attention_bwd_dq(q, k, v, o, lse, do, name, hosted=()):
    h, s, dq = q.shape
    dv = v.shape[-1]
    t = min(ATT_TILE, s)
    scale = dq ** -0.5
    nh = len(hosted)

    def body(*refs):
        q_ref, k_ref, v_ref, o_ref, lse_ref, do_ref = refs[:6]
        p_refs = refs[6:6 + nh]
        dq_ref, delta_ref = refs[6 + nh:8 + nh]
        b_refs = refs[8 + nh:8 + 2 * nh]
        k_scr, v_scr = refs[8 + 2 * nh:10 + 2 * nh]
        hh, i = pl.program_id(0), pl.program_id(1)
        if nh:
            _hosted_exchange(p_refs, b_refs, refs[-2], refs[-1], jnp.logical_and(hh == 0, i == 0),
                             jnp.logical_and(hh == h - 1, i == s // t - 1))

        @pl.when(i == 0)
        def _():
            k_scr[...] = k_ref[...].astype(BF16)
            v_scr[...] = v_ref[...].astype(BF16)

        qb = (q_ref[...] * (scale * LOG2E)).astype(BF16)
        dof = do_ref[...]
        dob = dof.astype(BF16)
        lse_v = lse_ref[...]
        delta = jnp.sum(dof * o_ref[...], axis=1, keepdims=True)
        delta_ref[...] = delta

        def block(j, acc, masked):
            off = pl.multiple_of(j * t, t)
            kj = k_scr[pl.ds(off, t), :]
            vj = v_scr[pl.ds(off, t), :]
            sc = lax.dot_general(qb, kj, _NT, preferred_element_type=F32)
            p = jnp.exp2(sc - lse_v)
            if masked:
                p = jnp.where(_chunk_mask(i * t, j * t, sc.shape), p, 0.0)
            dp = lax.dot_general(dob, vj, _NT, preferred_element_type=F32)
            ds = p * (dp - delta)
            return acc + jnp.dot(ds.astype(BF16), kj, preferred_element_type=F32)

        acc = lax.fori_loop(0, i, lambda j, c: block(j, c, False), jnp.zeros((t, dq), F32))
        dq_ref[...] = block(i, acc, True) * scale

    tile = lambda d: pl.BlockSpec((None, t, d), lambda hh, i: (hh, i, 0))
    whole = lambda d: pl.BlockSpec((None, s, d), lambda hh, i: (hh, 0, 0))
    h_in, h_out, h_shapes, h_scratch = _host_args(hosted)
    return pl.pallas_call(
        body,
        name=name,
        grid=(h, s // t),
        in_specs=[tile(dq), whole(dq), whole(dv), tile(dv), tile(1), tile(dv)] + h_in,
        out_specs=[tile(dq), tile(1)] + h_out,
        out_shape=[jax.ShapeDtypeStruct((h, s, dq), F32), jax.ShapeDtypeStruct((h, s, 1), F32)] + h_shapes,
        scratch_shapes=[pltpu.VMEM((s, dq), BF16), pltpu.VMEM((s, dv), BF16)] + h_scratch,
        compiler_params=_params(("arbitrary", "arbitrary"), has_side_effects=bool(nh)),
    )(q, k, v, o, lse, do, *hosted)


def _attention_bwd_dkv(q, k, v, lse, delta, do, name, hosted=()):
    h, s, dq = q.shape
    dv = v.shape[-1]
    t = min(ATT_TILE, s)
    n = s // t
    scale = dq ** -0.5
    nh = len(hosted)

    def body(*refs):
        q_ref, k_ref, v_ref, lse_ref, delta_ref, do_ref = refs[:6]
        p_refs = refs[6:6 + nh]
        dk_ref, dv_ref = refs[6 + nh:8 + nh]
        b_refs = refs[8 + nh:8 + 2 * nh]
        q_scr, do_scr = refs[8 + 2 * nh:10 + 2 * nh]
        hh, j = pl.program_id(0), pl.program_id(1)
        if nh:
            _hosted_exchange(p_refs, b_refs, refs[-2], refs[-1], jnp.logical_and(hh == 0, j == 0),
                             jnp.logical_and(hh == h - 1, j == n - 1))

        @pl.when(j == 0)
        def _():
            q_scr[...] = (q_ref[...] * (scale * LOG2E)).astype(BF16)
            do_scr[...] = do_ref[...].astype(BF16)

        kb = k_ref[...].astype(BF16)
        vb = v_ref[...].astype(BF16)

        def block(i, carry, masked):
            dk, dvv = carry
            off = pl.multiple_of(i * t, t)
            qi = q_scr[pl.ds(off, t), :]
            doi = do_scr[pl.ds(off, t), :]
            sc = lax.dot_general(qi, kb, _NT, preferred_element_type=F32)
            p = jnp.exp2(sc - lse_ref[pl.ds(off, t), :])
            if masked:
                p = jnp.where(_chunk_mask(i * t, j * t, sc.shape), p, 0.0)
            dp = lax.dot_general(doi, vb, _NT, preferred_element_type=F32)
            ds = p * (dp - delta_ref[pl.ds(off, t), :])
            dvv = dvv + lax.dot_general(p.astype(BF16), doi, _TN, preferred_element_type=F32)
            dk = dk + lax.dot_general(ds.astype(BF16), qi, _TN, preferred_element_type=F32)
            return dk, dvv

        carry = block(j, (jnp.zeros((t, dq), F32), jnp.zeros((t, dv), F32)), True)
        dk, dvv = lax.fori_loop(j + 1, n, lambda i, c: block(i, c, False), carry)
        dk_ref[...] = dk * (1.0 / LOG2E)
        dv_ref[...] = dvv

    tile = lambda d: pl.BlockSpec((None, t, d), lambda hh, j: (hh, j, 0))
    whole = lambda d: pl.BlockSpec((None, s, d), lambda hh, j: (hh, 0, 0))
    h_in, h_out, h_shapes, h_scratch = _host_args(hosted)
    return pl.pallas_call(
        body,
        name=name,
        grid=(h, n),
        in_specs=[whole(dq), tile(dq), tile(dv), whole(1), whole(1), whole(dv)] + h_in,
        out_specs=[tile(dq), tile(dv)] + h_out,
        out_shape=[jax.ShapeDtypeStruct((h, s, dq), F32), jax.ShapeDtypeStruct((h, s, dv), F32)] + h_shapes,
        scratch_shapes=[pltpu.VMEM((s, dq), BF16), pltpu.VMEM((s, dv), BF16)] + h_scratch,
        compiler_params=_params(("arbitrary", "arbitrary"), has_side_effects=bool(nh)),
    )(q, k, v, lse, delta, do, *hosted)


def attention(q, k, v, tok=(), shards=(), *, name):
    tok, shards = tuple(tok), tuple(shards)
    fwd_name = name + "_fwd" + ("_host" if shards else "")

    @jax.custom_vjp
    def op(q, k, v, tok, shards):
        o, _, *partial = _attention_fwd(q, k, v, fwd_name, shards)
        return o, tok, tuple(partial)

    def fwd(q, k, v, tok, shards):
        o, lse, *partial = _attention_fwd(q, k, v, fwd_name, shards)
        return (o, tok, tuple(partial)), (q, k, v, o, lse, shards)

    def bwd(res, cts):
        q, k, v, o, lse, shards = res
        do, payload, _ = cts
        first = tuple(payload[:HOSTED_IN_DQ])
        rest = tuple(payload[HOSTED_IN_DQ:])
        dq, delta, *got_a = _attention_bwd_dq(q, k, v, o, lse, do, name + "_bwd_dq" + ("_host" if first else ""), first)
        dk, dv, *got_b = _attention_bwd_dkv(q, k, v, lse, delta, do, name + "_bwd_dkv" + ("_host" if rest else ""), rest)
        return dq, dk, dv, tuple(got_a) + tuple(got_b), tuple(jnp.zeros_like(s) for s in shards)

    op.defvjp(fwd, bwd)
    return op(q, k, v, tok, shards)


CONV_HALO = 8


def _conv_tiles(s, c):
    return min(512, s), _pick(c, (512, 256, 128))


def _conv_fwd(x, w, name):
    s, c = x.shape
    ts, tc = _conv_tiles(s, c)
    nb = ts // CONV_HALO

    def body(xc_ref, xp_ref, w_ref, o_ref):
        t = pl.program_id(1)
        prev = jnp.where(t > 0, xp_ref[...], 0.0)
        xe = jnp.concatenate([prev, xc_ref[...]], axis=0)
        wv = w_ref[...]
        acc = jnp.zeros((ts, tc), F32)
        for tap in range(CONV_WIDTH):
            k = CONV_WIDTH - 1 - tap
            sh = xe if k == 0 else pltpu.roll(xe, k, axis=0)
            acc = acc + sh[CONV_HALO:, :] * wv[tap:tap + 1, :]
        o_ref[...] = acc

    return pl.pallas_call(
        body,
        name=name,
        grid=(c // tc, s // ts),
        in_specs=[
            pl.BlockSpec((ts, tc), lambda ci, t: (t, ci)),
            pl.BlockSpec((CONV_HALO, tc), lambda ci, t: (jnp.maximum(t * nb - 1, 0), ci)),
            pl.BlockSpec((CONV_WIDTH, tc), lambda ci, t: (0, ci)),
        ],
        out_specs=pl.BlockSpec((ts, tc), lambda ci, t: (t, ci)),
        out_shape=jax.ShapeDtypeStruct((s, c), F32),
        compiler_params=_params(("parallel", "parallel")),
    )(x, x, w)


def _conv_bwd(x, w, dy, name):
    s, c = x.shape
    ts, tc = _conv_tiles(s, c)
    nb = ts // CONV_HALO
    nt = s // ts

    def body(xc_ref, xp_ref, w_ref, dc_ref, dn_ref, dx_ref, dw_ref):
        t = pl.program_id(1)
        prev = jnp.where(t > 0, xp_ref[...], 0.0)
        xe = jnp.concatenate([prev, xc_ref[...]], axis=0)
        dcur = dc_ref[...]
        nxt = jnp.where(t < nt - 1, dn_ref[...], 0.0)
        de = jnp.concatenate([dcur, nxt], axis=0)
        wv = w_ref[...]
        dx = jnp.zeros((ts, tc), F32)
        dw = jnp.zeros((CONV_WIDTH, tc), F32)
        tap_row = lax.broadcasted_iota(jnp.int32, (CONV_WIDTH, tc), 0)
        for tap in range(CONV_WIDTH):
            k = CONV_WIDTH - 1 - tap
            dsh = de if k == 0 else pltpu.roll(de, ts + CONV_HALO - k, axis=0)
            dx = dx + dsh[:ts, :] * wv[tap:tap + 1, :]
            xsh = xe if k == 0 else pltpu.roll(xe, k, axis=0)
            dwt = jnp.sum(xsh[CONV_HALO:, :] * dcur, axis=0, keepdims=True)
            dw = jnp.where(tap_row == tap, dwt, dw)
        dx_ref[...] = dx

        @pl.when(t == 0)
        def _():
            dw_ref[...] = dw

        @pl.when(t > 0)
        def _():
            dw_ref[...] += dw

    return pl.pallas_call(
        body,
        name=name,
        grid=(c // tc, nt),
        in_specs=[
            pl.BlockSpec((ts, tc), lambda ci, t: (t, ci)),
            pl.BlockSpec((CONV_HALO, tc), lambda ci, t: (jnp.maximum(t * nb - 1, 0), ci)),
            pl.BlockSpec((CONV_WIDTH, tc), lambda ci, t: (0, ci)),
            pl.BlockSpec((ts, tc), lambda ci, t: (t, ci)),
            pl.BlockSpec((CONV_HALO, tc), lambda ci, t: (jnp.minimum((t + 1) * nb, s // CONV_HALO - 1), ci)),
        ],
        out_specs=[
            pl.BlockSpec((ts, tc), lambda ci, t: (t, ci)),
            pl.BlockSpec((CONV_WIDTH, tc), lambda ci, t: (0, ci)),
        ],
        out_shape=[jax.ShapeDtypeStruct((s, c), F32), jax.ShapeDtypeStruct((CONV_WIDTH, c), F32)],
        compiler_params=_params(("parallel", "arbitrary")),
    )(x, x, w, dy, dy)


def causal_conv(x, w, *, name):
    @jax.custom_vjp
    def op(x, w):
        return _conv_fwd(x, w, name + "_fwd")

    def fwd(x, w):
        return op(x, w), (x, w)

    def bwd(res, dy):
        x, w = res
        dx, dw = _conv_bwd(x, w, dy, name + "_bwd")
        return dx, dw

    op.defvjp(fwd, bwd)
    return op(x, w)


SSD_T = 128
SSD_R = 8
SSD_GW = SSD_R * SSM_HEADDIM


def _ssd_consts(t):
    r = lax.broadcasted_iota(jnp.int32, (t, t), 0)
    c = lax.broadcasted_iota(jnp.int32, (t, t), 1)
    tril = (c <= r).astype(F32)
    triu = (r <= c).astype(F32)
    head_of_lane = lax.broadcasted_iota(jnp.int32, (SSD_R, SSD_GW), 1) // SSM_HEADDIM
    expand = (head_of_lane == lax.broadcasted_iota(jnp.int32, (SSD_R, SSD_GW), 0)).astype(F32)
    return c <= r, tril, triu, expand


def _hi(a, b):
    return jnp.dot(a, b, precision=HI, preferred_element_type=F32)


def _hi_nt(a, b):
    return lax.dot_general(a, b, _NT, precision=HI, preferred_element_type=F32)


def _bdot(a, b, dn=None):
    if dn is None:
        return jnp.dot(a.astype(BF16), b.astype(BF16), preferred_element_type=F32)
    return lax.dot_general(a.astype(BF16), b.astype(BF16), dn, preferred_element_type=F32)


def _ssd_chunk_common(x_ref, b_ref, c_ref, dtc_ref, dtr_ref, alc_ref, alr_ref, t):
    mask, tril, triu, expand = _ssd_consts(t)
    x, bm, cm = x_ref[...], b_ref[...], c_ref[...]
    dtc, dtr = dtc_ref[...], dtr_ref[...]
    neg_a_c = -jnp.exp(alc_ref[...])
    neg_a_r = -jnp.exp(alr_ref[...])
    acum_c = _hi(tril, dtc * neg_a_c)
    acum_r = _hi(dtr * neg_a_r, triu)
    s_cb = _bdot(cm, bm, _NT)
    return mask, tril, triu, expand, x, bm, cm, dtc, dtr, neg_a_c, neg_a_r, acum_c, acum_r, s_cb


def _head_decay(mask, acum_c, acum_r, h):
    seg = acum_c[:, h:h + 1] - acum_r[h:h + 1, :]
    return jnp.exp(jnp.where(mask, seg, -jnp.inf))


def _ssd_fwd(xbc, dtc, dtr, alc, alr, dexp, name, shards=()):
    s = xbc.shape[0]
    g = SSM_GROUPS
    t = min(SSD_T, s)
    nc = s // t
    n = SSM_STATE
    xblocks = (g * SSD_GW) // n
    ng = len(shards)

    def body(*refs):
        x_ref, b_ref, c_ref, dtc_ref, dtr_ref, alc_ref, alr_ref, d_ref = refs[:8]
        y_ref, hs_ref = refs[8 + ng:10 + ng]
        h_scr = refs[10 + 2 * ng]
        ci = pl.program_id(1)
        if ng:
            gi = pl.program_id(0)
            _hosted_gather(refs[8:8 + ng], refs[10 + ng:10 + 2 * ng], refs[-2], refs[-1],
                           jnp.logical_and(gi == 0, ci == 0), jnp.logical_and(gi == g - 1, ci == nc - 1))

        @pl.when(ci == 0)
        def _():
            h_scr[...] = jnp.zeros_like(h_scr)

        (mask, tril, triu, expand, x, bm, cm, dtc_v, dtr_v, _, _, acum_c, acum_r, s_cb) = _ssd_chunk_common(
            x_ref, b_ref, c_ref, dtc_ref, dtr_ref, alc_ref, alr_ref, t)
        hst = h_scr[...]
        hs_ref[...] = hst
        ch = _bdot(cm, hst)
        y = _hi(jnp.exp(acum_c), expand) * ch + d_ref[...] * x
        half = lax.broadcasted_iota(jnp.int32, (t, 2 * SSM_HEADDIM), 1) // SSM_HEADDIM
        parts = []
        for j in range(SSD_R // 2):
            xp = x[:, j * 128:(j + 1) * 128]
            acc = jnp.zeros((t, 128), F32)
            for hh in range(2):
                h = 2 * j + hh
                m = s_cb * _head_decay(mask, acum_c, acum_r, h) * dtr_v[h:h + 1, :]
                acc = acc + _bdot(m, jnp.where(half == hh, xp, 0.0))
            parts.append(acc)
        y_ref[...] = y + jnp.concatenate(parts, axis=1)
        last = acum_c[t - 1:t, :]
        w_c = jnp.exp(last - acum_c) * dtc_v
        dec = _hi(jnp.broadcast_to(jnp.exp(last), (SSD_R, SSD_R)), expand)[0:1, :]
        h_scr[...] = dec * hst + _bdot(bm, _hi(w_c, expand) * x, _TN)

    g_in, g_out, g_shapes, g_scratch = _gather_host_args(shards)
    return pl.pallas_call(
        body,
        name=name,
        grid=(g, nc),
        in_specs=[
            pl.BlockSpec((t, SSD_GW), lambda gi, ci: (ci, gi)),
            pl.BlockSpec((t, n), lambda gi, ci: (ci, xblocks + gi)),
            pl.BlockSpec((t, n), lambda gi, ci: (ci, xblocks + g + gi)),
            pl.BlockSpec((None, t, SSD_R), lambda gi, ci: (gi, ci, 0)),
            pl.BlockSpec((None, SSD_R, t), lambda gi, ci: (gi, 0, ci)),
            pl.BlockSpec((None, 1, SSD_R), lambda gi, ci: (gi, 0, 0)),
            pl.BlockSpec((None, SSD_R, 1), lambda gi, ci: (gi, 0, 0)),
            pl.BlockSpec((None, 1, SSD_GW), lambda gi, ci: (gi, 0, 0)),
        ] + g_in,
        out_specs=[
            pl.BlockSpec((t, SSD_GW), lambda gi, ci: (ci, gi)),
            pl.BlockSpec((None, None, n, SSD_GW), lambda gi, ci: (gi, ci, 0, 0)),
        ] + g_out,
        out_shape=[jax.ShapeDtypeStruct((s, g * SSD_GW), F32), jax.ShapeDtypeStruct((g, nc, n, SSD_GW), F32)] + g_shapes,
        scratch_shapes=[pltpu.VMEM((n, SSD_GW), F32)] + g_scratch,
        compiler_params=_params(("arbitrary" if ng else "parallel", "arbitrary"), has_side_effects=bool(ng)),
    )(xbc, xbc, xbc, dtc, dtr, alc, alr, dexp, *shards)


def _ssd_bwd(xbc, dtc, dtr, alc, alr, dexp, hs, dy, name):
    s = xbc.shape[0]
    g = SSM_GROUPS
    t = min(SSD_T, s)
    nc = s // t
    n = SSM_STATE
    xblocks = (g * SSD_GW) // n

    def body(x_ref, b_ref, c_ref, dtc_ref, dtr_ref, alc_ref, alr_ref, d_ref, hs_ref, dy_ref,
             dx_ref, db_ref, dc_ref, ddtc_ref, ddtr_ref, dalc_ref, dalr_ref, dd_ref, dh_scr):
        ci = pl.program_id(1)

        @pl.when(ci == 0)
        def _():
            dh_scr[...] = jnp.zeros_like(dh_scr)

        (mask, tril, triu, expand, x, bm, cm, dtc_v, dtr_v, neg_a_c, neg_a_r, acum_c, acum_r, s_cb) = _ssd_chunk_common(
            x_ref, b_ref, c_ref, dtc_ref, dtr_ref, alc_ref, alr_ref, t)
        hst = hs_ref[...]
        dhn = dh_scr[...]
        dy = dy_ref[...]
        ch = _bdot(cm, hst)
        scale_full = _hi(jnp.exp(acum_c), expand)
        sdy = scale_full * dy
        d_c = _bdot(sdy, hst, _NT)
        dh_prev = _bdot(cm, sdy, _TN)
        dacum_c = _hi_nt(sdy * ch, expand)
        dx = d_ref[...] * dy
        dd = jnp.sum(dy * x, axis=0, keepdims=True)
        last = acum_c[t - 1:t, :]
        e_last = jnp.exp(last)
        dec = _hi(jnp.broadcast_to(e_last, (SSD_R, SSD_R)), expand)[0:1, :]
        dh_prev = dh_prev + dec * dhn
        ddec = jnp.sum(hst * dhn, axis=0, keepdims=True)
        dlast = _hi_nt(jnp.broadcast_to(ddec, (SSD_R, SSD_GW)), expand)[0:1, :] * e_last
        w_e = jnp.exp(last - acum_c)
        w_c = w_e * dtc_v
        wfull = _hi(w_c, expand)
        z = _bdot(bm, dhn)
        dx = dx + wfull * z
        dw_c = _hi_nt(x * z, expand)
        ddt_c = dw_c * w_e
        q_c = dw_c * w_c
        dacum_c = dacum_c - q_c
        dlast = dlast + jnp.sum(q_c, axis=0, keepdims=True)
        d_b = _bdot(wfull * x, dhn, _NT)
        half = lax.broadcasted_iota(jnp.int32, (t, 2 * SSM_HEADDIM), 1) // SSM_HEADDIM
        lane8 = lax.broadcasted_iota(jnp.int32, (t, SSD_R), 1)
        row8 = lax.broadcasted_iota(jnp.int32, (SSD_R, t), 0)
        ds_cb = jnp.zeros((t, t), F32)
        dacum_r = jnp.zeros((SSD_R, t), F32)
        ddt_r = jnp.zeros((SSD_R, t), F32)
        parts = []
        for j in range(SSD_R // 2):
            xp = x[:, j * 128:(j + 1) * 128]
            dyp = dy[:, j * 128:(j + 1) * 128]
            dxp = jnp.zeros((t, 128), F32)
            for hh in range(2):
                h = 2 * j + hh
                dts = dtr_v[h:h + 1, :]
                decay = _head_decay(mask, acum_c, acum_r, h)
                sl = s_cb * decay
                m = sl * dts
                xm = jnp.where(half == hh, xp, 0.0)
                dym = jnp.where(half == hh, dyp, 0.0)
                dxp = dxp + _bdot(m, dym, _TN)
                dm = _bdot(dym, xm, _NT)
                ds_cb = ds_cb + dm * decay * dts
                q = dm * m
                dacum_c = dacum_c + jnp.where(lane8 == h, jnp.sum(q, axis=1, keepdims=True), 0.0)
                dacum_r = dacum_r - jnp.where(row8 == h, jnp.sum(q, axis=0, keepdims=True), 0.0)
                ddt_r = ddt_r + jnp.where(row8 == h, jnp.sum(dm * sl, axis=0, keepdims=True), 0.0)
            parts.append(dxp)
        dx_ref[...] = dx + jnp.concatenate(parts, axis=1)
        dc_ref[...] = d_c + _bdot(ds_cb, bm)
        db_ref[...] = d_b + _bdot(ds_cb, cm, _TN)
        row_t = lax.broadcasted_iota(jnp.int32, (t, SSD_R), 0)
        dacum_c = dacum_c + jnp.where(row_t == t - 1, dlast, 0.0)
        da_c = _hi(triu, dacum_c)
        da_r = _hi(dacum_r, tril)
        ddtc_ref[...] = ddt_c + da_c * neg_a_c
        ddtr_ref[...] = ddt_r + da_r * neg_a_r
        dal_c = jnp.sum(da_c * dtc_v, axis=0, keepdims=True) * neg_a_c
        dal_r = jnp.sum(da_r * dtr_v, axis=1, keepdims=True) * neg_a_r
        dh_scr[...] = dh_prev

        @pl.when(ci == 0)
        def _():
            dalc_ref[...] = dal_c
            dalr_ref[...] = dal_r
            dd_ref[...] = dd

        @pl.when(ci > 0)
        def _():
            dalc_ref[...] += dal_c
            dalr_ref[...] += dal_r
            dd_ref[...] += dd

    rev = lambda ci: nc - 1 - ci
    return pl.pallas_call(
        body,
        name=name,
        grid=(g, nc),
        in_specs=[
            pl.BlockSpec((t, SSD_GW), lambda gi, ci: (rev(ci), gi)),
            pl.BlockSpec((t, n), lambda gi, ci: (rev(ci), xblocks + gi)),
            pl.BlockSpec((t, n), lambda gi, ci: (rev(ci), xblocks + g + gi)),
            pl.BlockSpec((None, t, SSD_R), lambda gi, ci: (gi, rev(ci), 0)),
            pl.BlockSpec((None, SSD_R, t), lambda gi, ci: (gi, 0, rev(ci))),
            pl.BlockSpec((None, 1, SSD_R), lambda gi, ci: (gi, 0, 0)),
            pl.BlockSpec((None, SSD_R, 1), lambda gi, ci: (gi, 0, 0)),
            pl.BlockSpec((None, 1, SSD_GW), lambda gi, ci: (gi, 0, 0)),
            pl.BlockSpec((None, None, n, SSD_GW), lambda gi, ci: (gi, rev(ci), 0, 0)),
            pl.BlockSpec((t, SSD_GW), lambda gi, ci: (rev(ci), gi)),
        ],
        out_specs=[
            pl.BlockSpec((t, SSD_GW), lambda gi, ci: (rev(ci), gi)),
            pl.BlockSpec((t, n), lambda gi, ci: (rev(ci), gi)),
            pl.BlockSpec((t, n), lambda gi, ci: (rev(ci), gi)),
            pl.BlockSpec((None, t, SSD_R), lambda gi, ci: (gi, rev(ci), 0)),
            pl.BlockSpec((None, SSD_R, t), lambda gi, ci: (gi, 0, rev(ci))),
            pl.BlockSpec((None, 1, SSD_R), lambda gi, ci: (gi, 0, 0)),
            pl.BlockSpec((None, SSD_R, 1), lambda gi, ci: (gi, 0, 0)),
            pl.BlockSpec((None, 1, SSD_GW), lambda gi, ci: (gi, 0, 0)),
        ],
        out_shape=[
            jax.ShapeDtypeStruct((s, g * SSD_GW), F32),
            jax.ShapeDtypeStruct((s, g * n), F32),
            jax.ShapeDtypeStruct((s, g * n), F32),
            jax.ShapeDtypeStruct((g, s, SSD_R), F32),
            jax.ShapeDtypeStruct((g, SSD_R, s), F32),
            jax.ShapeDtypeStruct((g, 1, SSD_R), F32),
            jax.ShapeDtypeStruct((g, SSD_R, 1), F32),
            jax.ShapeDtypeStruct((g, 1, SSD_GW), F32),
        ],
        scratch_shapes=[pltpu.VMEM((n, SSD_GW), F32)],
        compiler_params=_params(("parallel", "arbitrary")),
    )(xbc, xbc, xbc, dtc, dtr, alc, alr, dexp, hs, dy)


def ssd_core(xbc, dtc, dtr, alc, alr, dexp, shards=(), *, name):
    shards = tuple(shards)
    fwd_name = name + "_fwd" + ("_host" if shards else "")

    @jax.custom_vjp
    def op(xbc, dtc, dtr, alc, alr, dexp, shards):
        y, _, *partial = _ssd_fwd(xbc, dtc, dtr, alc, alr, dexp, fwd_name, shards)
        return y, tuple(partial)

    def fwd(xbc, dtc, dtr, alc, alr, dexp, shards):
        y, hs, *partial = _ssd_fwd(xbc, dtc, dtr, alc, alr, dexp, fwd_name, shards)
        return (y, tuple(partial)), (xbc, dtc, dtr, alc, alr, dexp, hs, shards)

    def bwd(res, cts):
        xbc, dtc, dtr, alc, alr, dexp, hs, shards = res
        dx, db, dc, ddtc, ddtr, dalc, dalr, dd = _ssd_bwd(xbc, dtc, dtr, alc, alr, dexp, hs, cts[0], name + "_bwd")
        return (jnp.concatenate([dx, db, dc], axis=1), ddtc, ddtr, dalc, dalr, dd,
                tuple(jnp.zeros_like(s) for s in shards))

    op.defvjp(fwd, bwd)
    return op(xbc, dtc, dtr, alc, alr, dexp, shards)


def gate_norm(y, z, w):
    return (rms_norm(y * (z * jax.nn.sigmoid(z)), w),)


def ssd_branch(xbc, z, dt_raw, conv_w, conv_b, dt_bias, a_log, d_skip, norm_w, shards=(), *, name):
    s = xbc.shape[0]
    g = SSM_GROUPS
    conv = causal_conv(xbc, conv_w, name=name + "_conv")
    (xc,) = rowwise(lambda c, b: ((c + b) * jax.nn.sigmoid(c + b),), (conv,), (), (conv_b[None, :],),
                    name=name + "_silu", tile=min(256, s))
    (dt,) = rowwise(lambda r, b: (jax.nn.softplus(r + b),), (dt_raw,), (), (dt_bias[None, :],),
                    name=name + "_dt", tile=min(512, s))
    dt3 = dt.reshape(s, g, SSD_R)
    y, partial = ssd_core(xc, dt3.transpose(1, 0, 2), dt3.transpose(1, 2, 0), a_log.reshape(g, 1, SSD_R),
                          a_log.reshape(g, SSD_R, 1), jnp.repeat(d_skip, SSM_HEADDIM).reshape(g, 1, SSD_GW), shards,
                          name=name + "_core")
    (out,) = rowwise(gate_norm, (y.reshape(s, g, SSD_GW), z.reshape(s, g, SSD_GW)), (),
                     (norm_w.reshape(g, SSD_GW),), name=name + "_gate", tile=min(128, s))
    return out.reshape(s, g * SSD_GW), partial


def rms_norm(x, w):
    return x * lax.rsqrt(jnp.mean(x * x, axis=-1, keepdims=True) + EPS) * w


def rope_matrix():
    half = QK_ROPE // 2
    j = jnp.arange(QK_DIM)
    src = jnp.where(j < QK_NOPE + half, j + half, j - half)
    sign = jnp.where(j < QK_NOPE, 0.0, jnp.where(j < QK_NOPE + half, -1.0, 1.0))
    return (jnp.arange(QK_DIM)[:, None] == src[None, :]).astype(F32) * sign[None, :]


def rope_tables_full(positions):
    inv_freq = 1.0 / (ROPE_THETA ** (jnp.arange(0, QK_ROPE, 2, dtype=F32) / QK_ROPE))
    ang = positions.astype(F32)[:, None] * inv_freq
    s = positions.shape[0]
    cos = jnp.concatenate([jnp.ones((s, QK_NOPE), F32), jnp.cos(ang), jnp.cos(ang)], axis=-1)
    sin = jnp.concatenate([jnp.zeros((s, QK_NOPE), F32), jnp.sin(ang), jnp.sin(ang)], axis=-1)
    return cos[:, None, :], sin[:, None, :]


def head_norm_rope(x, cos_full, sin_full, rot, w):
    t, h, d = x.shape
    y = rms_norm(x, w)
    partner = jnp.dot(y.reshape(t * h, d), rot, precision=HI, preferred_element_type=F32).reshape(t, h, d)
    return (y * cos_full + partner * sin_full,)


def _norm(x, w, *, name, tile=256):
    (y,) = rowwise(lambda x, w: (rms_norm(x, w),), (x,), (), (w[None, :],), name=name, tile=min(tile, x.shape[0]))
    return y


Q_LORA = 512
KV_LORA = 512
W_IN_PIECES = ("cq", "ckv", "kr", "z", "xbc", "dt", "ga", "gb")


def w_in_widths(d_model):
    d_inner = 2 * d_model
    conv_dim = d_inner + 2 * SSM_GROUPS * SSM_STATE
    return (Q_LORA, KV_LORA, QK_ROPE, d_inner, conv_dim, d_inner // SSM_HEADDIM, d_model, d_model)


LINEAR_NAMES = ("cq", "ckv", "kd", "z", "xbc", "ga", "gb", "w_uq", "w_ukv", "w_o_mla", "w_o_ssm", "w_out", "w_up",
                "w_down", "w_ple_gate", "w_ple")


GATHER_IN_ATTENTION = (0, 6)


def layer_forward(x, p_i, cos_full, sin_full, rot, wb, sm, sinks, tok=(), next_shards=()):
    s, d = x.shape
    in_att = [t for t in range(len(next_shards)) if t in GATHER_IN_ATTENTION]
    in_ssd = [t for t in range(len(next_shards)) if t not in GATHER_IN_ATTENTION]
    lin = lambda a, n: linear(a, wb[n], sinks[n], name="lin_" + n)
    h = _norm(x, sm["norm_mix_w"], name="norm_mix")
    c_q, c_kv, kd = lin(h, "cq"), lin(h, "ckv"), lin(h, "kd")
    z, xbc, g_a, g_b = lin(h, "z"), lin(h, "xbc"), lin(h, "ga"), lin(h, "gb")
    k_r, dt_raw = kd[:, :QK_ROPE], kd[:, QK_ROPE:]
    q = lin(_norm(c_q, sm["q_a_norm_w"], name="norm_qa"), "w_uq").reshape(s, MLA_HEADS, QK_DIM)
    kv = lin(_norm(c_kv, sm["kv_a_norm_w"], name="norm_kva"), "w_ukv").reshape(s, MLA_HEADS, QK_NOPE + V_DIM)
    k = jnp.concatenate([kv[..., :QK_NOPE], jnp.broadcast_to(k_r[:, None, :], (s, MLA_HEADS, QK_ROPE))], axis=-1)
    v = kv[..., QK_NOPE:]
    tq = min(128, s)
    (q,) = rowwise(head_norm_rope, (q,), (cos_full, sin_full), (sm["q_norm_w"][None, :],), tables=(rot,), name="q_rope", tile=tq)
    (k,) = rowwise(head_norm_rope, (k,), (cos_full, sin_full), (sm["k_norm_w"][None, :],), tables=(rot,), name="k_rope", tile=tq)
    hm = lambda a: a.transpose(1, 0, 2)
    o, tok, part_att = attention(hm(q), hm(k), hm(v), tok, [next_shards[t] for t in in_att], name="attn")
    o = hm(o).reshape(s, MLA_HEADS * V_DIM)
    y_a = lin(o, "w_o_mla")
    y_ssd, part_ssd = ssd_branch(xbc, z, dt_raw, sm["conv_w"], sm["conv_b"], sm["dt_bias"], sm["a_log"], sm["d_skip"],
                                 sm["ssm_norm_w"], [next_shards[t] for t in in_ssd], name="ssd")
    y_b = lin(y_ssd, "w_o_ssm")
    partial = [None] * len(next_shards)
    for t, buf in list(zip(in_att, part_att)) + list(zip(in_ssd, part_ssd)):
        partial[t] = buf
    sig = jax.nn.sigmoid
    tr = min(256, s)
    (merged,) = rowwise(lambda ga, gb, ya, yb: (sig(ga) * ya + sig(gb) * yb,), (g_a, g_b, y_a, y_b), (), (),
                        name="merge", tile=tr)
    x = x + lin(merged, "w_out")
    up = lin(_norm(x, sm["norm_mlp_w"], name="norm_mlp"), "w_up")
    (act,) = rowwise(lambda u: (jnp.square(jnp.maximum(u, 0.0)),), (up,), (), (), name="relu2", tile=tr)
    x = x + lin(act, "w_down")
    pg = lin(_norm(x, sm["ple_norm_w"], name="norm_ple"), "w_ple_gate")
    pe = lin(p_i, "w_ple")
    (x,) = rowwise(lambda x, pe, pg: (x + pe * sig(pg),), (x, pe, pg), (), (), name="ple_add", tile=tr)
    return (x, tok), partial


def loss_and_cotangent(y, target):
    s, d = y.shape

    def f(y, t):
        e = y - t
        return e * (1.0 / d), 0.5 * jnp.sum(jnp.sum(e * e, axis=1, keepdims=True) * (1.0 / d), axis=0, keepdims=True)

    dy, part = _tiled_call(f, (y, target), (), min(256, s), "loss", 1)
    return dy, part[0, 0]


ADAM_BLOCK_ELEMS = 256 * 1024


def adamw(w, g, m, v, *, name):
    rows, cols = w.shape
    budget = max(8, ADAM_BLOCK_ELEMS // cols)
    tile = _pick(rows, tuple(t for t in (512, 256, 128, 64, 32, 16, 8) if t <= budget))

    def f(w, g, m, v):
        m = ADAM_B1 * m + (1.0 - ADAM_B1) * g
        v = ADAM_B2 * v + (1.0 - ADAM_B2) * jnp.square(g)
        m_hat = m / (1.0 - ADAM_B1 ** ADAM_STEP)
        v_hat = v / (1.0 - ADAM_B2 ** ADAM_STEP)
        delta = -ADAM_LR * (m_hat / (jnp.sqrt(v_hat) + ADAM_EPS) + ADAM_WD * w)
        return delta, m, v

    return _tiled_call(f, (w, g, m, v), (), tile, name, 0)


MESH_ID = pl.DeviceIdType.MESH
N_CHIPS = 4
_ANY = pl.BlockSpec(memory_space=pl.ANY)


def _place():
    return lax.axis_index("x"), lax.axis_index("y"), lax.axis_index("c")


def _other_chips(x, y):
    return [(1 - x, y), (x, 1 - y), (1 - x, 1 - y)]


def _rcopy(src, dst, send_sem, recv_sem, device):
    return pltpu.make_async_remote_copy(src_ref=src, dst_ref=dst, send_sem=send_sem, recv_sem=recv_sem,
                                        device_id=device, device_id_type=MESH_ID)


def _sems(n, k):
    return pltpu.SemaphoreType.DMA((n, k))


def _comm_call(body, name, ins, out_shapes, scratch, aliases=None):
    return pl.pallas_call(
        body,
        name=name,
        in_specs=[_ANY] * len(ins),
        out_specs=[_ANY] * len(out_shapes),
        out_shape=out_shapes,
        scratch_shapes=scratch,
        input_output_aliases=aliases or {},
        compiler_params=pltpu.CompilerParams(has_side_effects=True),
    )(*ins)


SPLIT_ROWS = 32


def gather_shards(shards, *, name):
    n = len(shards)
    split = [s.shape[0] % SPLIT_ROWS == 0 for s in shards]

    def body(*refs):
        srcs, outs = refs[:n], refs[n:2 * n]
        send_sems, recv_sems = refs[2 * n:]
        x, y, c = _place()
        sibling = (x, y, 1 - c)
        chips = _other_chips(x, y)
        me = 2 * x + y

        def part(t, slot, h):
            if not split[t]:
                return outs[t].at[slot]
            half = shards[t].shape[0] // 2
            return outs[t].at[slot, pl.ds(h * half, half), :]

        def own(t):
            if not split[t]:
                return srcs[t]
            half = shards[t].shape[0] // 2
            return srcs[t].at[pl.ds(c * half, half), :]

        sent = []
        for t in range(n):
            for j, chip in enumerate(chips):
                sent.append(_rcopy(own(t), part(t, me, c), send_sems.at[t, j], recv_sems.at[t, j], (*chip, c)))
                sent[-1].start()
        for t in range(n):
            sent.append(_rcopy(srcs[t], outs[t].at[me], send_sems.at[t, 6], recv_sems.at[t, 6], sibling))
            sent[-1].start()
        for t in range(n):
            for j, (cx, cy) in enumerate(chips):
                got = part(t, 2 * cx + cy, c)
                _rcopy(got, got, send_sems.at[t, j], recv_sems.at[t, j], (cx, cy, c)).wait_recv()
                if split[t]:
                    sent.append(_rcopy(got, got, send_sems.at[t, 3 + j], recv_sems.at[t, 3 + j], sibling))
                    sent[-1].start()
        for t in range(n):
            if split[t]:
                for j, (cx, cy) in enumerate(chips):
                    got = part(t, 2 * cx + cy, 1 - c)
                    _rcopy(got, got, send_sems.at[t, 3 + j], recv_sems.at[t, 3 + j], sibling).wait_recv()
        for t in range(n):
            _rcopy(srcs[t], outs[t].at[me], send_sems.at[t, 6], recv_sems.at[t, 6], sibling).wait_recv()
        for cp in sent:
            cp.wait_send()

    out_shapes = [jax.ShapeDtypeStruct((N_CHIPS,) + s.shape, s.dtype) for s in shards]
    return _comm_call(body, name, shards, out_shapes, [_sems(n, 7), _sems(n, 7)])


def gather_finish(shards, partial, *, name):
    n = len(shards)
    split = [_split_rows(s.shape) for s in shards]

    def body(*refs):
        srcs, outs = refs[:n], refs[2 * n:3 * n]
        send_sems, recv_sems = refs[3 * n:]
        x, y, c = _place()
        sibling = (x, y, 1 - c)
        chips = _other_chips(x, y)
        me = 2 * x + y

        def part(t, slot, h):
            half = shards[t].shape[0] // 2
            return outs[t].at[slot, pl.ds(h * half, half), :]

        sent = []
        for t in range(n):
            if split[t]:
                for j, (cx, cy) in enumerate(chips):
                    got = part(t, 2 * cx + cy, c)
                    sent.append(_rcopy(got, got, send_sems.at[t, j], recv_sems.at[t, j], sibling))
                    sent[-1].start()
            sent.append(_rcopy(srcs[t], outs[t].at[me], send_sems.at[t, 3], recv_sems.at[t, 3], sibling))
            sent[-1].start()
        for t in range(n):
            if split[t]:
                for j, (cx, cy) in enumerate(chips):
                    got = part(t, 2 * cx + cy, 1 - c)
                    _rcopy(got, got, send_sems.at[t, j], recv_sems.at[t, j], sibling).wait_recv()
            _rcopy(srcs[t], outs[t].at[me], send_sems.at[t, 3], recv_sems.at[t, 3], sibling).wait_recv()
        for cp in sent:
            cp.wait_send()

    out_shapes = [jax.ShapeDtypeStruct(p.shape, p.dtype) for p in partial]
    return _comm_call(body, name, list(shards) + list(partial), out_shapes, [_sems(n, 4), _sems(n, 4)],
                      aliases={n + t: t for t in range(n)})


def sibling_take_half(gs, *, name):
    n = len(gs)

    def body(*refs):
        g_refs, a_refs = refs[:n], refs[n:2 * n]
        send_sems, recv_sems = refs[2 * n:]
        x, y, c = _place()
        copies = []
        for t in range(n):
            half = gs[t].shape[1] // 2
            copies.append(_rcopy(g_refs[t].at[:, pl.ds((1 - c) * half, half), :], a_refs[t], send_sems.at[t, 0],
                                 recv_sems.at[t, 0], (x, y, 1 - c)))
            copies[-1].start()
        for cp in copies:
            cp.wait()

    out_shapes = [jax.ShapeDtypeStruct((g.shape[0], g.shape[1] // 2, g.shape[2]), g.dtype) for g in gs]
    return _comm_call(body, name, gs, out_shapes, [_sems(n, 1), _sems(n, 1)])


ELEMWISE_BLOCK_ELEMS = 256 * 1024


def _row_tile(rows, cols):
    budget = max(16, ELEMWISE_BLOCK_ELEMS // cols)
    return _pick(rows, tuple(t for t in (1024, 512, 256, 128, 64, 32, 16) if t <= budget))


def _core_and_chip():
    x, y, c = _place()
    return jnp.stack([c, 2 * x + y]).astype(jnp.int32)


def pair_add(g, a, *, name):
    n, rows, cols = g.shape
    half = rows // 2
    tile = _row_tile(half, cols)
    nb = half // tile

    def body(who_ref, g_ref, a_ref, o_ref):
        o_ref[...] = (g_ref[...] + a_ref[...]).astype(o_ref.dtype)

    return pl.pallas_call(
        body,
        name=name,
        grid_spec=pltpu.PrefetchScalarGridSpec(
            num_scalar_prefetch=1,
            grid=(n, nb),
            in_specs=[
                pl.BlockSpec((None, tile, cols), lambda j, i, who: (j, who[0] * nb + i, 0)),
                pl.BlockSpec((None, tile, cols), lambda j, i, who: (j, i, 0)),
            ],
            out_specs=pl.BlockSpec((None, tile, cols), lambda j, i, who: (j, i, 0)),
        ),
        out_shape=jax.ShapeDtypeStruct((n, half, cols), BF16),
        compiler_params=_params(("parallel", "parallel")),
    )(_core_and_chip(), g, a)


def exchange_chip_slots(ps, *, name):
    n = len(ps)

    def body(*refs):
        p_refs, b_refs = refs[:n], refs[n:2 * n]
        send_sems, recv_sems = refs[2 * n:]
        x, y, c = _place()
        me = 2 * x + y
        chips = _other_chips(x, y)
        sends = []
        for t in range(n):
            for j, (cx, cy) in enumerate(chips):
                sends.append(_rcopy(p_refs[t].at[2 * cx + cy], b_refs[t].at[me], send_sems.at[t, j], recv_sems.at[t, j],
                                    (cx, cy, c)))
                sends[-1].start()
        for t in range(n):
            for j, (cx, cy) in enumerate(chips):
                got = b_refs[t].at[2 * cx + cy]
                _rcopy(got, got, send_sems.at[t, j], recv_sems.at[t, j], (cx, cy, c)).wait_recv()
        for cp in sends:
            cp.wait_send()

    out_shapes = [jax.ShapeDtypeStruct(p.shape, p.dtype) for p in ps]
    return _comm_call(body, name, ps, out_shapes, [_sems(n, 3), _sems(n, 3)])


def chips_add(g, a, b, *, name):
    n, rows, cols = g.shape
    half = rows // 2
    tile = _row_tile(half, cols)
    nb = half // tile

    def body(who_ref, g_ref, a_ref, *rest):
        o_ref = rest[-1]
        acc = g_ref[...] + a_ref[...]
        for b_ref in rest[:-1]:
            acc = acc + b_ref[...].astype(F32)
        o_ref[...] = acc

    other = lambda k: pl.BlockSpec((None, tile, cols), lambda i, who, k=k: ((who[1] + k) % n, i, 0))
    return pl.pallas_call(
        body,
        name=name,
        grid_spec=pltpu.PrefetchScalarGridSpec(
            num_scalar_prefetch=1,
            grid=(nb,),
            in_specs=[
                pl.BlockSpec((None, tile, cols), lambda i, who: (who[1], who[0] * nb + i, 0)),
                pl.BlockSpec((None, tile, cols), lambda i, who: (who[1], i, 0)),
            ] + [other(k) for k in range(1, n)],
            out_specs=pl.BlockSpec((tile, cols), lambda i, who: (who[0] * nb + i, 0)),
        ),
        out_shape=jax.ShapeDtypeStruct((rows, cols), F32),
        compiler_params=_params(("parallel",)),
    )(_core_and_chip(), g, a, *([b] * (n - 1)))


def sibling_join_halves(rs, *, name):
    n = len(rs)

    def body(*refs):
        r_refs = refs[n:2 * n]
        send_sems, recv_sems = refs[2 * n:]
        x, y, c = _place()
        sibling = (x, y, 1 - c)
        copies = []
        for t in range(n):
            half = rs[t].shape[0] // 2
            mine = r_refs[t].at[pl.ds(c * half, half), :]
            copies.append(_rcopy(mine, mine, send_sems.at[t, 0], recv_sems.at[t, 0], sibling))
            copies[-1].start()
        for t in range(n):
            half = rs[t].shape[0] // 2
            got = r_refs[t].at[pl.ds((1 - c) * half, half), :]
            _rcopy(got, got, send_sems.at[t, 0], recv_sems.at[t, 0], sibling).wait_recv()
        for cp in copies:
            cp.wait_send()

    out_shapes = [jax.ShapeDtypeStruct(r.shape, r.dtype) for r in rs]
    return _comm_call(body, name, rs, out_shapes, [_sems(n, 1), _sems(n, 1)], aliases={t: t for t in range(n)})


def reduce_to_owner(gs, *, name):
    gs, a, p = reduce_pairs(gs, name=name)
    return reduce_finish(gs, a, exchange_chip_slots(p, name=name + "_chips"), name=name)


def reduce_pairs(gs, *, name):
    gs = list(gs)
    a = sibling_take_half(gs, name=name + "_pair")
    p = [pair_add(g, ai, name=name + "_pair_add") for g, ai in zip(gs, a)]
    return gs, a, p


def reduce_finish(gs, a, b, *, name):
    f = [chips_add(g, ai, bi, name=name + "_chips_add") for g, ai, bi in zip(gs, a, b)]
    return sibling_join_halves(f, name=name + "_join")


def allreduce_small(v, *, name):
    rows, cols = v.shape

    def body(v_ref, o_ref, buf, send_sems, recv_sems):
        x, y, c = _place()
        me = 4 * x + 2 * y + c
        buf[me] = v_ref[...]
        copies = []
        for k in range(1, 8):
            bx, by, bc = (k >> 2) & 1, (k >> 1) & 1, k & 1
            peer = (x if bx == 0 else 1 - x, y if by == 0 else 1 - y, c if bc == 0 else 1 - c)
            copies.append(_rcopy(v_ref, buf.at[me], send_sems.at[k - 1], recv_sems.at[k - 1], peer))
        for cp in copies:
            cp.start()
        for k in range(1, 8):
            bx, by, bc = (k >> 2) & 1, (k >> 1) & 1, k & 1
            px, py, pc = (x if bx == 0 else 1 - x, y if by == 0 else 1 - y, c if bc == 0 else 1 - c)
            _rcopy(v_ref, buf.at[4 * px + 2 * py + pc], send_sems.at[k - 1], recv_sems.at[k - 1], (px, py, pc)).wait_recv()
        for cp in copies:
            cp.wait_send()
        acc = buf[0]
        for j in range(1, 8):
            acc = acc + buf[j]
        o_ref[...] = acc

    return pl.pallas_call(
        body,
        name=name,
        in_specs=[pl.BlockSpec(memory_space=pltpu.VMEM)],
        out_specs=pl.BlockSpec(memory_space=pltpu.VMEM),
        out_shape=jax.ShapeDtypeStruct((rows, cols), v.dtype),
        scratch_shapes=[pltpu.VMEM((8, rows, cols), v.dtype), pltpu.SemaphoreType.DMA((7,)), pltpu.SemaphoreType.DMA((7,))],
        compiler_params=pltpu.CompilerParams(has_side_effects=True, vmem_limit_bytes=V7X_VMEM_LIMIT),
    )(v)


BIG = (("w_in", 1), ("w_uq", 1), ("w_ukv", 1), ("w_o_mla", 0), ("w_o_ssm", 0), ("w_out", 0), ("w_up", 1),
       ("w_down", 0), ("w_ple_gate", 0), ("w_ple", 1))
SHARDED = BIG + (("conv_w", 1),)
SMALL = ("norm_mix_w", "q_a_norm_w", "kv_a_norm_w", "q_norm_w", "k_norm_w", "conv_b", "dt_bias", "a_log", "d_skip",
         "ssm_norm_w", "norm_mlp_w", "ple_norm_w")
WEIGHTS = ("norm_mix_w", "w_in", "q_a_norm_w", "w_uq", "kv_a_norm_w", "w_ukv", "q_norm_w", "k_norm_w", "w_o_mla", "conv_w",
           "conv_b", "dt_bias", "a_log", "d_skip", "ssm_norm_w", "w_o_ssm", "w_out", "norm_mlp_w", "w_up", "w_down",
           "ple_norm_w", "w_ple_gate", "w_ple")


def _to_rows(flat, cols, row_multiple):
    n = flat.shape[-1]
    rows = -(-n // (cols * row_multiple)) * row_multiple
    pad = [(0, 0)] * (flat.ndim - 1) + [(0, rows * cols - n)]
    return jnp.pad(flat, pad).reshape(flat.shape[:-1] + (rows, cols))


def _w_in_ranges(d_model):
    out, lo = {}, 0
    for n, wd in zip(W_IN_PIECES, w_in_widths(d_model)):
        out[n] = (lo, lo + wd)
        lo += wd
    return out


def w_in_pieces(w3):
    _, k, c = w3.shape
    pc = {}
    for n, (lo, hi) in _w_in_ranges(k).items():
        cuts = [w3[j][:, max(lo, j * c) - j * c:min(hi, (j + 1) * c) - j * c]
                for j in range(N_CHIPS) if max(lo, j * c) < min(hi, (j + 1) * c)]
        pc[n] = cuts[0] if len(cuts) == 1 else jnp.concatenate(cuts, axis=1)
    pc["kd"] = jnp.concatenate([pc.pop("kr"), pc.pop("dt")], axis=1)
    return pc


def w_in_shard_grads(g, k, c):
    g = dict(g)
    g["kr"], g["dt"] = g["kd"][:, :QK_ROPE], g["kd"][:, QK_ROPE:]
    shards = []
    for j in range(N_CHIPS):
        cuts = []
        for n, (lo, hi) in _w_in_ranges(k).items():
            a, b = max(lo, j * c), min(hi, (j + 1) * c)
            if a < b:
                cuts.append(g[n][:, a - lo:b - lo])
        shards.append(jnp.concatenate(cuts, axis=1))
    return jnp.stack(shards)


def kernel(x, p, positions, norm_mix_w, w_in, q_a_norm_w, w_uq, kv_a_norm_w, w_ukv, q_norm_w, k_norm_w, w_o_mla, conv_w, conv_b, dt_bias, a_log, d_skip, ssm_norm_w, w_o_ssm, w_out, norm_mlp_w, w_up, w_down, ple_norm_w, w_ple_gate, w_ple, loss_target, m_norm_mix_w, m_w_in, m_q_a_norm_w, m_w_uq, m_kv_a_norm_w, m_w_ukv, m_q_norm_w, m_k_norm_w, m_w_o_mla, m_conv_w, m_conv_b, m_dt_bias, m_a_log, m_d_skip, m_ssm_norm_w, m_w_o_ssm, m_w_out, m_norm_mlp_w, m_w_up, m_w_down, m_ple_norm_w, m_w_ple_gate, m_w_ple, v_norm_mix_w, v_w_in, v_q_a_norm_w, v_w_uq, v_kv_a_norm_w, v_w_ukv, v_q_norm_w, v_k_norm_w, v_w_o_mla, v_conv_w, v_conv_b, v_dt_bias, v_a_log, v_d_skip, v_ssm_norm_w, v_w_o_ssm, v_w_out, v_norm_mlp_w, v_w_up, v_w_down, v_ple_norm_w, v_w_ple_gate, v_w_ple):
    a = dict(locals())
    x, p, pos, target = a["x"][0], a["p"][:, 0], a["positions"][0], a["loss_target"][0]
    depth = a["w_in"].shape[0]
    shard_shapes = {n: tuple(a[n].shape[1:]) for n, _ in SHARDED}
    cos_full, sin_full = rope_tables_full(pos)
    rot = rope_matrix()

    shards = [[a[n][i].astype(BF16) for n, _ in BIG] + [a["conv_w"][i]] for i in range(depth)]
    tok = tuple(jnp.zeros((N_CHIPS, shard_shapes[n][0] // 2, shard_shapes[n][1]), BF16) for n, _ in BIG)
    layer_vjps = []
    got = gather_shards(shards[0], name="gather_weights")
    for i in range(depth):
        full = dict(zip([n for n, _ in SHARDED], got))
        w_i = w_in_pieces(full["w_in"])
        for n, ax in BIG[1:]:
            w_i[n] = full[n] if ax == 1 else full[n].reshape((-1, full[n].shape[-1]))
        sm_i = {n: a[n][i] for n in SMALL}
        sm_i["conv_w"] = full["conv_w"].transpose(1, 0, 2).reshape(CONV_WIDTH, -1)
        sinks_i = {n: jnp.zeros(w_i[n].shape, F32) for n in LINEAR_NAMES}
        last = i == depth - 1
        f_i = functools.partial(layer_forward, p_i=p[i], cos_full=cos_full, sin_full=sin_full, rot=rot, wb=w_i,
                                next_shards=() if last else shards[i + 1])
        g_i = lambda x, sm, sk, tk, f=f_i: f(x, sm=sm, sinks=sk, tok=tk)
        (x, _), vjp_i, partial = jax.vjp(g_i, x, sm_i, sinks_i, () if last else tok, has_aux=True)
        layer_vjps.append(vjp_i)
        if not last:
            got = gather_finish(shards[i + 1], partial, name="gather_finish")
    dx, loss_part = loss_and_cotangent(x, target)
    loss = lax.psum(loss_part, ("x", "y", "c"))

    per_layer, d_small, pending = [None] * depth, [None] * depth, None
    for i in reversed(range(depth)):
        payload = tuple(pending[2]) if pending is not None else ()
        dx, d_small[i], d_sinks_i, got = layer_vjps[i]((dx, payload))
        if pending is not None:
            per_layer[i + 1] = reduce_finish(pending[0], pending[1], list(got), name="reduce_grads")
        g_i = [w_in_shard_grads(d_sinks_i, *shard_shapes["w_in"])]
        g_i += [d_sinks_i[n].reshape((N_CHIPS,) + shard_shapes[n]) for n, _ in BIG[1:]]
        pending = reduce_pairs(g_i, name="reduce_grads")
    per_layer[0] = reduce_finish(pending[0], pending[1], exchange_chip_slots(pending[2], name="reduce_grads_chips"),
                                 name="reduce_grads")
    grads = {n: jnp.stack([per_layer[i][t] for i in range(depth)]) for t, (n, _) in enumerate(BIG)}

    small_names = SMALL + ("conv_w",)
    flat = jnp.concatenate([d_small[i][n].reshape(-1) for i in range(depth) for n in small_names])
    n_small = flat.shape[0]
    red = allreduce_small(_to_rows(flat, 128, 8), name="reduce_small").reshape(-1)[:n_small]
    per = n_small // depth
    off = 0
    for n in SMALL:
        width = a[n].shape[-1]
        grads[n] = jnp.stack([red[i * per + off:i * per + off + width] for i in range(depth)])
        off += width
    conv_c = shard_shapes["conv_w"][1]
    conv_full = jnp.stack([red[i * per + off:i * per + off + CONV_WIDTH * N_CHIPS * conv_c] for i in range(depth)])
    chip = 2 * lax.axis_index("x") + lax.axis_index("y")
    grads["conv_w"] = lax.dynamic_index_in_dim(conv_full.reshape(depth, CONV_WIDTH, N_CHIPS, conv_c), chip, axis=2,
                                               keepdims=False)

    deltas, new_m, new_v = {}, {}, {}
    two_d = lambda t: t.reshape(-1, t.shape[-1])
    for n in WEIGHTS:
        d, m, v = adamw(two_d(a[n]), two_d(grads[n]), two_d(a["m_" + n]), two_d(a["v_" + n]), name="adamw")
        deltas[n], new_m[n], new_v[n] = d.reshape(a[n].shape), m.reshape(a[n].shape), v.reshape(a[n].shape)

    return (loss, dx[None], *[grads[n].reshape(a[n].shape) for n in WEIGHTS], *[deltas[n] for n in WEIGHTS],
            *[new_m[n] for n in WEIGHTS], *[new_v[n] for n in WEIGHTS])
```

```python
import functools

import jax
import jax.numpy as jnp
from jax import lax
from jax.experimental import pallas as pl
from jax.experimental.pallas import tpu as pltpu

F32 = jnp.float32
BF16 = jnp.bfloat16
HI = lax.Precision.HIGHEST

EPS = 1e-6
MLA_HEADS = 16
QK_NOPE = 128
QK_ROPE = 64
QK_DIM = QK_NOPE + QK_ROPE
V_DIM = 128
ROPE_THETA = 10000.0
ATT_CHUNK = 64
SSM_GROUPS = 8
SSM_HEADDIM = 64
SSM_STATE = 128
CONV_WIDTH = 4
ADAM_LR = 0.001
ADAM_B1 = 0.9
ADAM_B2 = 0.999
ADAM_EPS = 1e-08
ADAM_WD = 0.01
ADAM_STEP = 10

V7X_VMEM_LIMIT = 56 * 1024 * 1024


def _params(sem=None, **kw):
    return pltpu.CompilerParams(dimension_semantics=sem, vmem_limit_bytes=V7X_VMEM_LIMIT, **kw)


def _pick(n, prefs):
    for t in prefs:
        if n % t == 0:
            return t
    return n


MATMUL_OPERAND_BYTES = 24 * 1024 * 1024


def matmul(a, b, *, ta=False, tb=False, out_blocks=0, out_dtype=F32, name):
    m, k = (a.shape[1], a.shape[0]) if ta else a.shape
    blocked = b.ndim == 3
    if blocked:
        nb, rows, c = b.shape
        k2, n = (nb * c, rows) if tb else (rows, nb * c)
    else:
        k2, n = (b.shape[1], b.shape[0]) if tb else b.shape
    assert k == k2, (a.shape, b.shape, ta, tb)
    n_unit = n // out_blocks if out_blocks else (c if blocked and not tb else n)
    k_unit = c if blocked and tb else k
    tm = _pick(m, (1024, 512, 256, 128))
    tn = _pick(n_unit, (1024, 512, 256, 128))
    in_bytes = tm * a.dtype.itemsize + tn * b.dtype.itemsize
    tk = _pick(k_unit, tuple(t for t in (2048, 1024, 512, 256, 128) if 2 * t * in_bytes <= MATMUL_OPERAND_BYTES))
    nk = k // tk
    dn = (((0 if ta else 1,), (1 if tb else 0,)), ((), ()))

    def body(a_ref, b_ref, o_ref, *acc):
        part = lambda: lax.dot_general(a_ref[...].astype(BF16), b_ref[...].astype(BF16), dn, preferred_element_type=F32)
        if nk == 1:
            o_ref[...] = part().astype(o_ref.dtype)
            return
        (acc_ref,) = acc
        kk = pl.program_id(2)

        @pl.when(kk == 0)
        def _():
            acc_ref[...] = jnp.zeros_like(acc_ref)

        acc_ref[...] += part()

        @pl.when(kk == nk - 1)
        def _():
            o_ref[...] = acc_ref[...].astype(o_ref.dtype)

    a_spec = pl.BlockSpec((tk, tm), lambda i, j, kk: (kk, i)) if ta else pl.BlockSpec((tm, tk), lambda i, j, kk: (i, kk))
    if not blocked:
        b_spec = pl.BlockSpec((tn, tk), lambda i, j, kk: (j, kk)) if tb else pl.BlockSpec((tk, tn), lambda i, j, kk: (kk, j))
    elif tb:
        kb = c // tk
        b_spec = pl.BlockSpec((None, tn, tk), lambda i, j, kk: (kk // kb, j, kk % kb))
    else:
        cb = c // tn
        b_spec = pl.BlockSpec((None, tk, tn), lambda i, j, kk: (j // cb, kk, j % cb))
    if out_blocks:
        ob = n_unit // tn
        out_spec = pl.BlockSpec((None, tm, tn), lambda i, j, kk: (j // ob, i, j % ob))
        out_shape = jax.ShapeDtypeStruct((out_blocks, m, n_unit), out_dtype)
    else:
        out_spec = pl.BlockSpec((tm, tn), lambda i, j, kk: (i, j))
        out_shape = jax.ShapeDtypeStruct((m, n), out_dtype)
    return pl.pallas_call(
        body,
        name=name,
        grid=(m // tm, n // tn, nk),
        in_specs=[a_spec, b_spec],
        out_specs=out_spec,
        out_shape=out_shape,
        scratch_shapes=[pltpu.VMEM((tm, tn), F32)] if nk > 1 else [],
        compiler_params=_params(("parallel", "parallel", "arbitrary")),
    )(a, b)


def linear(a, w, sink, *, name, out_dtype=BF16):
    @jax.custom_vjp
    def op(a, w, sink):
        return matmul(a, w, out_dtype=out_dtype, name=name + "_fwd")

    def fwd(a, w, sink):
        return op(a, w, sink), (a, w)

    def bwd(res, ct):
        a, w = res
        da = matmul(ct, w, tb=True, out_dtype=a.dtype, name=name + "_bwd_da")
        dw = matmul(a, ct, ta=True, out_blocks=w.shape[0] if w.ndim == 3 else 0, name=name + "_bwd_dw")
        return da, jnp.zeros_like(w), dw

    op.defvjp(fwd, bwd)
    return op(a, w, sink)


def _tiled_call(fn, tiled, whole, tile, name, n_acc):
    rows = tiled[0].shape[0]
    assert rows % tile == 0
    t_avals = [jax.ShapeDtypeStruct((tile,) + a.shape[1:], a.dtype) for a in tiled]
    w_avals = [jax.ShapeDtypeStruct(a.shape, a.dtype) for a in whole]
    outs = jax.eval_shape(fn, *t_avals, *w_avals)
    n_in = len(tiled) + len(whole)
    n_t = len(outs) - n_acc

    def body(*refs):
        res = fn(*[r[...] for r in refs[:n_in]])
        o_refs = refs[n_in:]
        for r, v in zip(o_refs[:n_t], res[:n_t]):
            r[...] = v.astype(r.dtype)
        if n_acc:
            first = pl.program_id(0) == 0

            @pl.when(first)
            def _():
                for r, v in zip(o_refs[n_t:], res[n_t:]):
                    r[...] = v.astype(F32)

            @pl.when(jnp.logical_not(first))
            def _():
                for r, v in zip(o_refs[n_t:], res[n_t:]):
                    r[...] += v.astype(F32)

    def tspec(a):
        nd = len(a.shape)
        return pl.BlockSpec((tile,) + tuple(a.shape[1:]), lambda i, nd=nd: (i,) + (0,) * (nd - 1))

    def wspec(a):
        nd = len(a.shape)
        return pl.BlockSpec(tuple(a.shape), lambda i, nd=nd: (0,) * nd)

    out_shape = [jax.ShapeDtypeStruct((rows,) + o.shape[1:], o.dtype) for o in outs[:n_t]]
    out_shape += [jax.ShapeDtypeStruct(o.shape, F32) for o in outs[n_t:]]
    out_specs = [tspec(o) for o in out_shape[:n_t]] + [wspec(o) for o in out_shape[n_t:]]
    return pl.pallas_call(
        body,
        name=name,
        grid=(rows // tile,),
        in_specs=[tspec(a) for a in tiled] + [wspec(a) for a in whole],
        out_specs=out_specs,
        out_shape=out_shape,
        compiler_params=_params(("arbitrary",) if n_acc else ("parallel",)),
    )(*tiled, *whole)


def rowwise(f, rows, consts, params, *, name, tile, tables=()):
    rows, consts, tables, params = tuple(rows), tuple(consts), tuple(tables), tuple(params)
    nr, nc, ntab, npar = len(rows), len(consts), len(tables), len(params)

    @jax.custom_vjp
    def op(rows, consts, tables, params):
        return tuple(_tiled_call(f, rows + consts, tables + params, tile, name + "_fwd", 0))

    def fwd(rows, consts, tables, params):
        return op(rows, consts, tables, params), (rows, consts, tables, params)

    def bwd(res, cts):
        rows, consts, tables, params = res
        ncts = len(cts)

        def g(*args):
            r = args[:nr]
            c = args[nr:nr + nc]
            ct = args[nr + nc:nr + nc + ncts]
            tab = args[nr + nc + ncts:nr + nc + ncts + ntab]
            p = args[nr + nc + ncts + ntab:]
            _, vjp = jax.vjp(lambda *rp: f(*rp[:nr], *c, *tab, *rp[nr:]), *r, *p)
            return tuple(vjp(tuple(ct)))

        outs = _tiled_call(g, rows + consts + tuple(cts), tables + params, tile, name + "_bwd", npar)
        d_rows = tuple(o.astype(r.dtype) for o, r in zip(outs[:nr], rows))
        d_params = tuple(o.astype(p.dtype) for o, p in zip(outs[nr:], params))
        zeros = lambda xs: tuple(jnp.zeros_like(a) for a in xs)
        return d_rows, zeros(consts), zeros(tables), d_params

    op.defvjp(fwd, bwd)
    return op(rows, consts, tables, params)


ATT_TILE = 512
LOG2E = 1.4426950408889634
HOSTED_IN_DQ = 4
_NT = (((1,), (1,)), ((), ()))
_TN = (((0,), (0,)), ((), ()))


def _chunk_mask(row0, col0, shape):
    r = (row0 + lax.broadcasted_iota(jnp.int32, shape, 0)) // ATT_CHUNK
    c = (col0 + lax.broadcasted_iota(jnp.int32, shape, 1)) // ATT_CHUNK
    return c <= r


def _split_rows(shape):
    return shape[0] % SPLIT_ROWS == 0


def _hosted_gather(src_refs, out_refs, send_sems, recv_sems, first, last):
    x, y, c = _place()
    me = 2 * x + y
    chips = _other_chips(x, y)
    n = len(src_refs)

    def rows(t, ref, h):
        if not _split_rows(src_refs[t].shape):
            return ref
        half = src_refs[t].shape[0] // 2
        return ref.at[pl.ds(h * half, half), :]

    def sends():
        return [_rcopy(rows(t, src_refs[t], c), rows(t, out_refs[t].at[me], c), send_sems.at[t, j], recv_sems.at[t, j],
                       (*chip, c)) for t in range(n) for j, chip in enumerate(chips)]

    @pl.when(first)
    def _():
        for cp in sends():
            cp.start()

    @pl.when(last)
    def _():
        for t in range(n):
            for j, (cx, cy) in enumerate(chips):
                got = rows(t, out_refs[t].at[2 * cx + cy], c)
                _rcopy(got, got, send_sems.at[t, j], recv_sems.at[t, j], (cx, cy, c)).wait_recv()
        for cp in sends():
            cp.wait_send()


def _gather_host_args(shards):
    n = len(shards)
    if not n:
        return [], [], [], []
    any_spec = pl.BlockSpec(memory_space=pl.ANY)
    shapes = [jax.ShapeDtypeStruct((N_CHIPS,) + s.shape, s.dtype) for s in shards]
    return [any_spec] * n, [any_spec] * n, shapes, [pltpu.SemaphoreType.DMA((n, 3)), pltpu.SemaphoreType.DMA((n, 3))]


def _attention_fwd(q, k, v, name, shards=()):
    h, s, dq = q.shape
    dv = v.shape[-1]
    t = min(ATT_TILE, s)
    scale = dq ** -0.5
    ng = len(shards)

    def body(*refs):
        q_ref, k_ref, v_ref = refs[:3]
        o_ref, lse_ref = refs[3 + ng:5 + ng]
        k_scr, v_scr = refs[5 + 2 * ng:7 + 2 * ng]
        i = pl.program_id(1)
        if ng:
            hh = pl.program_id(0)
            _hosted_gather(refs[3:3 + ng], refs[5 + ng:5 + 2 * ng], refs[-2], refs[-1],
                           jnp.logical_and(hh == 0, i == 0), jnp.logical_and(hh == h - 1, i == s // t - 1))

        @pl.when(i == 0)
        def _():
            k_scr[...] = k_ref[...].astype(BF16)
            v_scr[...] = v_ref[...].astype(BF16)

        qb = (q_ref[...].astype(F32) * (scale * LOG2E)).astype(BF16)

        def block(j, carry, masked):
            m, l, acc = carry
            off = pl.multiple_of(j * t, t)
            kj = k_scr[pl.ds(off, t), :]
            vj = v_scr[pl.ds(off, t), :]
            sc = lax.dot_general(qb, kj, _NT, preferred_element_type=F32)
            if masked:
                sc = jnp.where(_chunk_mask(i * t, j * t, sc.shape), sc, -jnp.inf)
            m_new = jnp.maximum(m, jnp.max(sc, axis=1, keepdims=True))
            p = jnp.exp2(sc - m_new)
            alpha = jnp.exp2(m - m_new)
            l = alpha * l + jnp.sum(p, axis=1, keepdims=True)
            acc = alpha * acc + jnp.dot(p.astype(BF16), vj, preferred_element_type=F32)
            return m_new, l, acc

        init = (jnp.full((t, 1), -jnp.inf, F32), jnp.zeros((t, 1), F32), jnp.zeros((t, dv), F32))
        carry = lax.fori_loop(0, i, lambda j, c: block(j, c, False), init)
        m, l, acc = block(i, carry, True)
        o_ref[...] = (acc / l).astype(o_ref.dtype)
        lse_ref[...] = m + jnp.log2(l)

    g_in, g_out, g_shapes, g_scratch = _gather_host_args(shards)
    return pl.pallas_call(
        body,
        name=name,
        grid=(h, s // t),
        in_specs=[
            pl.BlockSpec((None, t, dq), lambda hh, i: (hh, i, 0)),
            pl.BlockSpec((None, s, dq), lambda hh, i: (hh, 0, 0)),
            pl.BlockSpec((None, s, dv), lambda hh, i: (hh, 0, 0)),
        ] + g_in,
        out_specs=[
            pl.BlockSpec((None, t, dv), lambda hh, i: (hh, i, 0)),
            pl.BlockSpec((None, t, 1), lambda hh, i: (hh, i, 0)),
        ] + g_out,
        out_shape=[jax.ShapeDtypeStruct((h, s, dv), q.dtype), jax.ShapeDtypeStruct((h, s, 1), F32)] + g_shapes,
        scratch_shapes=[pltpu.VMEM((s, dq), BF16), pltpu.VMEM((s, dv), BF16)] + g_scratch,
        compiler_params=_params(("arbitrary", "arbitrary"), has_side_effects=bool(ng)),
    )(q, k, v, *shards)


def _hosted_exchange(p_refs, b_refs, send_sems, recv_sems, first, last):
    x, y, c = _place()
    me = 2 * x + y
    chips = _other_chips(x, y)
    n = len(p_refs)

    def sends():
        return [_rcopy(p_refs[t].at[2 * cx + cy], b_refs[t].at[me], send_sems.at[t, j], recv_sems.at[t, j], (cx, cy, c))
                for t in range(n) for j, (cx, cy) in enumerate(chips)]

    @pl.when(first)
    def _():
        for cp in sends():
            cp.start()

    @pl.when(last)
    def _():
        for t in range(n):
            for j, (cx, cy) in enumerate(chips):
                got = b_refs[t].at[2 * cx + cy]
                _rcopy(got, got, send_sems.at[t, j], recv_sems.at[t, j], (cx, cy, c)).wait_recv()
        for cp in sends():
            cp.wait_send()


def _host_args(hosted):
    n = len(hosted)
    if not n:
        return [], [], [], []
    any_spec = pl.BlockSpec(memory_space=pl.ANY)
    shapes = [jax.ShapeDtypeStruct(p.shape, p.dtype) for p in hosted]
    return [any_spec] * n, [any_spec] * n, shapes, [pltpu.SemaphoreType.DMA((n, 3)), pltpu.SemaphoreType.DMA((n, 3))]


def _attention_bwd_dq(q, k, v, o, lse, do, name, hosted=()):
    h, s, dq = q.shape
    dv = v.shape[-1]
    t = min(ATT_TILE, s)
    scale = dq ** -0.5
    nh = len(hosted)

    def body(*refs):
        q_ref, k_ref, v_ref, o_ref, lse_ref, do_ref = refs[:6]
        p_refs = refs[6:6 + nh]
        dq_ref, delta_ref = refs[6 + nh:8 + nh]
        b_refs = refs[8 + nh:8 + 2 * nh]
        k_scr, v_scr = refs[8 + 2 * nh:10 + 2 * nh]
        hh, i = pl.program_id(0), pl.program_id(1)
        if nh:
            _hosted_exchange(p_refs, b_refs, refs[-2], refs[-1], jnp.logical_and(hh == 0, i == 0),
                             jnp.logical_and(hh == h - 1, i == s // t - 1))

        @pl.when(i == 0)
        def _():
            k_scr[...] = k_ref[...].astype(BF16)
            v_scr[...] = v_ref[...].astype(BF16)

        qb = (q_ref[...].astype(F32) * (scale * LOG2E)).astype(BF16)
        dof = do_ref[...].astype(F32)
        dob = do_ref[...].astype(BF16)
        lse_v = lse_ref[...]
        delta = jnp.sum(dof * o_ref[...].astype(F32), axis=1, keepdims=True)
        delta_ref[...] = delta

        def block(j, acc, masked):
            off = pl.multiple_of(j * t, t)
            kj = k_scr[pl.ds(off, t), :]
            vj = v_scr[pl.ds(off, t), :]
            sc = lax.dot_general(qb, kj, _NT, preferred_element_type=F32)
            p = jnp.exp2(sc - lse_v)
            if masked:
                p = jnp.where(_chunk_mask(i * t, j * t, sc.shape), p, 0.0)
            dp = lax.dot_general(dob, vj, _NT, preferred_element_type=F32)
            ds = p * (dp - delta)
            return acc + jnp.dot(ds.astype(BF16), kj, preferred_element_type=F32)

        acc = lax.fori_loop(0, i, lambda j, c: block(j, c, False), jnp.zeros((t, dq), F32))
        dq_ref[...] = (block(i, acc, True) * scale).astype(dq_ref.dtype)

    tile = lambda d: pl.BlockSpec((None, t, d), lambda hh, i: (hh, i, 0))
    whole = lambda d: pl.BlockSpec((None, s, d), lambda hh, i: (hh, 0, 0))
    h_in, h_out, h_shapes, h_scratch = _host_args(hosted)
    return pl.pallas_call(
        body,
        name=name,
        grid=(h, s // t),
        in_specs=[tile(dq), whole(dq), whole(dv), tile(dv), tile(1), tile(dv)] + h_in,
        out_specs=[tile(dq), tile(1)] + h_out,
        out_shape=[jax.ShapeDtypeStruct((h, s, dq), q.dtype), jax.ShapeDtypeStruct((h, s, 1), F32)] + h_shapes,
        scratch_shapes=[pltpu.VMEM((s, dq), BF16), pltpu.VMEM((s, dv), BF16)] + h_scratch,
        compiler_params=_params(("arbitrary", "arbitrary"), has_side_effects=bool(nh)),
    )(q, k, v, o, lse, do, *hosted)


def _attention_bwd_dkv(q, k, v, lse, delta, do, name, hosted=()):
    h, s, dq = q.shape
    dv = v.shape[-1]
    t = min(ATT_TILE, s)
    n = s // t
    scale = dq ** -0.5
    nh = len(hosted)

    def body(*refs):
        q_ref, k_ref, v_ref, lse_ref, delta_ref, do_ref = refs[:6]
        p_refs = refs[6:6 + nh]
        dk_ref, dv_ref = refs[6 + nh:8 + nh]
        b_refs = refs[8 + nh:8 + 2 * nh]
        q_scr, do_scr = refs[8 + 2 * nh:10 + 2 * nh]
        hh, j = pl.program_id(0), pl.program_id(1)
        if nh:
            _hosted_exchange(p_refs, b_refs, refs[-2], refs[-1], jnp.logical_and(hh == 0, j == 0),
                             jnp.logical_and(hh == h - 1, j == n - 1))

        @pl.when(j == 0)
        def _():
            q_scr[...] = (q_ref[...].astype(F32) * (scale * LOG2E)).astype(BF16)
            do_scr[...] = do_ref[...].astype(BF16)

        kb = k_ref[...].astype(BF16)
        vb = v_ref[...].astype(BF16)

        def block(i, carry, masked):
            dk, dvv = carry
            off = pl.multiple_of(i * t, t)
            qi = q_scr[pl.ds(off, t), :]
            doi = do_scr[pl.ds(off, t), :]
            sc = lax.dot_general(qi, kb, _NT, preferred_element_type=F32)
            p = jnp.exp2(sc - lse_ref[pl.ds(off, t), :])
            if masked:
                p = jnp.where(_chunk_mask(i * t, j * t, sc.shape), p, 0.0)
            dp = lax.dot_general(doi, vb, _NT, preferred_element_type=F32)
            ds = p * (dp - delta_ref[pl.ds(off, t), :])
            dvv = dvv + lax.dot_general(p.astype(BF16), doi, _TN, preferred_element_type=F32)
            dk = dk + lax.dot_general(ds.astype(BF16), qi, _TN, preferred_element_type=F32)
            return dk, dvv

        carry = block(j, (jnp.zeros((t, dq), F32), jnp.zeros((t, dv), F32)), True)
        dk, dvv = lax.fori_loop(j + 1, n, lambda i, c: block(i, c, False), carry)
        dk_ref[...] = (dk * (1.0 / LOG2E)).astype(dk_ref.dtype)
        dv_ref[...] = dvv.astype(dv_ref.dtype)

    tile = lambda d: pl.BlockSpec((None, t, d), lambda hh, j: (hh, j, 0))
    whole = lambda d: pl.BlockSpec((None, s, d), lambda hh, j: (hh, 0, 0))
    h_in, h_out, h_shapes, h_scratch = _host_args(hosted)
    return pl.pallas_call(
        body,
        name=name,
        grid=(h, n),
        in_specs=[whole(dq), tile(dq), tile(dv), whole(1), whole(1), whole(dv)] + h_in,
        out_specs=[tile(dq), tile(dv)] + h_out,
        out_shape=[jax.ShapeDtypeStruct((h, s, dq), k.dtype), jax.ShapeDtypeStruct((h, s, dv), v.dtype)] + h_shapes,
        scratch_shapes=[pltpu.VMEM((s, dq), BF16), pltpu.VMEM((s, dv), BF16)] + h_scratch,
        compiler_params=_params(("arbitrary", "arbitrary"), has_side_effects=bool(nh)),
    )(q, k, v, lse, delta, do, *hosted)


def attention(q, k, v, tok=(), shards=(), *, name):
    tok, shards = tuple(tok), tuple(shards)
    fwd_name = name + "_fwd" + ("_host" if shards else "")

    @jax.custom_vjp
    def op(q, k, v, tok, shards):
        o, _, *partial = _attention_fwd(q, k, v, fwd_name, shards)
        return o, tok, tuple(partial)

    def fwd(q, k, v, tok, shards):
        o, lse, *partial = _attention_fwd(q, k, v, fwd_name, shards)
        return (o, tok, tuple(partial)), (q, k, v, o, lse, shards)

    def bwd(res, cts):
        q, k, v, o, lse, shards = res
        do, payload, _ = cts
        first = tuple(payload[:HOSTED_IN_DQ])
        rest = tuple(payload[HOSTED_IN_DQ:])
        dq, delta, *got_a = _attention_bwd_dq(q, k, v, o, lse, do, name + "_bwd_dq" + ("_host" if first else ""), first)
        dk, dv, *got_b = _attention_bwd_dkv(q, k, v, lse, delta, do, name + "_bwd_dkv" + ("_host" if rest else ""), rest)
        return dq, dk, dv, tuple(got_a) + tuple(got_b), tuple(jnp.zeros_like(s) for s in shards)

    op.defvjp(fwd, bwd)
    return op(q, k, v, tok, shards)


CONV_HALO = 16


def _conv_tiles(s, c):
    return min(512, s), _pick(c, (512, 256, 128))


def _conv_fwd(x, w, name):
    s, c = x.shape
    ts, tc = _conv_tiles(s, c)
    nb = ts // CONV_HALO

    def body(xc_ref, xp_ref, w_ref, o_ref):
        t = pl.program_id(1)
        prev = jnp.where(t > 0, xp_ref[...].astype(F32), 0.0)
        xe = jnp.concatenate([prev, xc_ref[...].astype(F32)], axis=0)
        wv = w_ref[...]
        acc = jnp.zeros((ts, tc), F32)
        for tap in range(CONV_WIDTH):
            k = CONV_WIDTH - 1 - tap
            sh = xe if k == 0 else pltpu.roll(xe, k, axis=0)
            acc = acc + sh[CONV_HALO:, :] * wv[tap:tap + 1, :]
        o_ref[...] = acc

    return pl.pallas_call(
        body,
        name=name,
        grid=(c // tc, s // ts),
        in_specs=[
            pl.BlockSpec((ts, tc), lambda ci, t: (t, ci)),
            pl.BlockSpec((CONV_HALO, tc), lambda ci, t: (jnp.maximum(t * nb - 1, 0), ci)),
            pl.BlockSpec((CONV_WIDTH, tc), lambda ci, t: (0, ci)),
        ],
        out_specs=pl.BlockSpec((ts, tc), lambda ci, t: (t, ci)),
        out_shape=jax.ShapeDtypeStruct((s, c), F32),
        compiler_params=_params(("parallel", "parallel")),
    )(x, x, w)


def _conv_bwd(x, w, dy, name):
    s, c = x.shape
    ts, tc = _conv_tiles(s, c)
    nb = ts // CONV_HALO
    nt = s // ts

    def body(xc_ref, xp_ref, w_ref, dc_ref, dn_ref, dx_ref, dw_ref):
        t = pl.program_id(1)
        prev = jnp.where(t > 0, xp_ref[...].astype(F32), 0.0)
        xe = jnp.concatenate([prev, xc_ref[...].astype(F32)], axis=0)
        dcur = dc_ref[...]
        nxt = jnp.where(t < nt - 1, dn_ref[...], 0.0)
        de = jnp.concatenate([dcur, nxt], axis=0)
        wv = w_ref[...]
        dx = jnp.zeros((ts, tc), F32)
        dw = jnp.zeros((CONV_WIDTH, tc), F32)
        tap_row = lax.broadcasted_iota(jnp.int32, (CONV_WIDTH, tc), 0)
        for tap in range(CONV_WIDTH):
            k = CONV_WIDTH - 1 - tap
            dsh = de if k == 0 else pltpu.roll(de, ts + CONV_HALO - k, axis=0)
            dx = dx + dsh[:ts, :] * wv[tap:tap + 1, :]
            xsh = xe if k == 0 else pltpu.roll(xe, k, axis=0)
            dwt = jnp.sum(xsh[CONV_HALO:, :] * dcur, axis=0, keepdims=True)
            dw = jnp.where(tap_row == tap, dwt, dw)
        dx_ref[...] = dx.astype(dx_ref.dtype)

        @pl.when(t == 0)
        def _():
            dw_ref[...] = dw

        @pl.when(t > 0)
        def _():
            dw_ref[...] += dw

    return pl.pallas_call(
        body,
        name=name,
        grid=(c // tc, nt),
        in_specs=[
            pl.BlockSpec((ts, tc), lambda ci, t: (t, ci)),
            pl.BlockSpec((CONV_HALO, tc), lambda ci, t: (jnp.maximum(t * nb - 1, 0), ci)),
            pl.BlockSpec((CONV_WIDTH, tc), lambda ci, t: (0, ci)),
            pl.BlockSpec((ts, tc), lambda ci, t: (t, ci)),
            pl.BlockSpec((CONV_HALO, tc), lambda ci, t: (jnp.minimum((t + 1) * nb, s // CONV_HALO - 1), ci)),
        ],
        out_specs=[
            pl.BlockSpec((ts, tc), lambda ci, t: (t, ci)),
            pl.BlockSpec((CONV_WIDTH, tc), lambda ci, t: (0, ci)),
        ],
        out_shape=[jax.ShapeDtypeStruct((s, c), x.dtype), jax.ShapeDtypeStruct((CONV_WIDTH, c), F32)],
        compiler_params=_params(("parallel", "arbitrary")),
    )(x, x, w, dy, dy)


def causal_conv(x, w, *, name):
    @jax.custom_vjp
    def op(x, w):
        return _conv_fwd(x, w, name + "_fwd")

    def fwd(x, w):
        return op(x, w), (x, w)

    def bwd(res, dy):
        x, w = res
        dx, dw = _conv_bwd(x, w, dy, name + "_bwd")
        return dx, dw

    op.defvjp(fwd, bwd)
    return op(x, w)


SSD_T = 128
SSD_R = 8
SSD_GW = SSD_R * SSM_HEADDIM


def _ssd_consts(t):
    r = lax.broadcasted_iota(jnp.int32, (t, t), 0)
    c = lax.broadcasted_iota(jnp.int32, (t, t), 1)
    tril = (c <= r).astype(F32)
    triu = (r <= c).astype(F32)
    head_of_lane = lax.broadcasted_iota(jnp.int32, (SSD_R, SSD_GW), 1) // SSM_HEADDIM
    expand = (head_of_lane == lax.broadcasted_iota(jnp.int32, (SSD_R, SSD_GW), 0)).astype(F32)
    return c <= r, tril, triu, expand


def _hi(a, b):
    return jnp.dot(a, b, precision=HI, preferred_element_type=F32)


def _hi_nt(a, b):
    return lax.dot_general(a, b, _NT, precision=HI, preferred_element_type=F32)


def _bdot(a, b, dn=None):
    if dn is None:
        return jnp.dot(a.astype(BF16), b.astype(BF16), preferred_element_type=F32)
    return lax.dot_general(a.astype(BF16), b.astype(BF16), dn, preferred_element_type=F32)


def _ssd_chunk_common(x_ref, b_ref, c_ref, dtc_ref, dtr_ref, alc_ref, alr_ref, t):
    mask, tril, triu, expand = _ssd_consts(t)
    x, bm, cm = x_ref[...], b_ref[...], c_ref[...]
    dtc, dtr = dtc_ref[...], dtr_ref[...]
    neg_a_c = -jnp.exp(alc_ref[...])
    neg_a_r = -jnp.exp(alr_ref[...])
    acum_c = _hi(tril, dtc * neg_a_c)
    acum_r = _hi(dtr * neg_a_r, triu)
    s_cb = _bdot(cm, bm, _NT)
    return mask, tril, triu, expand, x, bm, cm, dtc, dtr, neg_a_c, neg_a_r, acum_c, acum_r, s_cb


def _head_decay(mask, acum_c, acum_r, h):
    seg = acum_c[:, h:h + 1] - acum_r[h:h + 1, :]
    return jnp.exp(jnp.where(mask, seg, -jnp.inf))


def _ssd_fwd(xbc, dtc, dtr, alc, alr, dexp, name, shards=()):
    s = xbc.shape[0]
    g = SSM_GROUPS
    t = min(SSD_T, s)
    nc = s // t
    n = SSM_STATE
    xblocks = (g * SSD_GW) // n
    ng = len(shards)

    def body(*refs):
        x_ref, b_ref, c_ref, dtc_ref, dtr_ref, alc_ref, alr_ref, d_ref = refs[:8]
        y_ref, hs_ref = refs[8 + ng:10 + ng]
        h_scr = refs[10 + 2 * ng]
        ci = pl.program_id(1)
        if ng:
            gi = pl.program_id(0)
            _hosted_gather(refs[8:8 + ng], refs[10 + ng:10 + 2 * ng], refs[-2], refs[-1],
                           jnp.logical_and(gi == 0, ci == 0), jnp.logical_and(gi == g - 1, ci == nc - 1))

        @pl.when(ci == 0)
        def _():
            h_scr[...] = jnp.zeros_like(h_scr)

        (mask, tril, triu, expand, x, bm, cm, dtc_v, dtr_v, _, _, acum_c, acum_r, s_cb) = _ssd_chunk_common(
            x_ref, b_ref, c_ref, dtc_ref, dtr_ref, alc_ref, alr_ref, t)
        hst = h_scr[...]
        hs_ref[...] = hst
        ch = _bdot(cm, hst)
        y = _hi(jnp.exp(acum_c), expand) * ch + d_ref[...] * x
        half = lax.broadcasted_iota(jnp.int32, (t, 2 * SSM_HEADDIM), 1) // SSM_HEADDIM
        parts = []
        for j in range(SSD_R // 2):
            xp = x[:, j * 128:(j + 1) * 128]
            acc = jnp.zeros((t, 128), F32)
            for hh in range(2):
                h = 2 * j + hh
                m = s_cb * _head_decay(mask, acum_c, acum_r, h) * dtr_v[h:h + 1, :]
                acc = acc + _bdot(m, jnp.where(half == hh, xp, 0.0))
            parts.append(acc)
        y_ref[...] = y + jnp.concatenate(parts, axis=1)
        last = acum_c[t - 1:t, :]
        w_c = jnp.exp(last - acum_c) * dtc_v
        dec = _hi(jnp.broadcast_to(jnp.exp(last), (SSD_R, SSD_R)), expand)[0:1, :]
        h_scr[...] = dec * hst + _bdot(bm, _hi(w_c, expand) * x, _TN)

    g_in, g_out, g_shapes, g_scratch = _gather_host_args(shards)
    return pl.pallas_call(
        body,
        name=name,
        grid=(g, nc),
        in_specs=[
            pl.BlockSpec((t, SSD_GW), lambda gi, ci: (ci, gi)),
            pl.BlockSpec((t, n), lambda gi, ci: (ci, xblocks + gi)),
            pl.BlockSpec((t, n), lambda gi, ci: (ci, xblocks + g + gi)),
            pl.BlockSpec((None, t, SSD_R), lambda gi, ci: (gi, ci, 0)),
            pl.BlockSpec((None, SSD_R, t), lambda gi, ci: (gi, 0, ci)),
            pl.BlockSpec((None, 1, SSD_R), lambda gi, ci: (gi, 0, 0)),
            pl.BlockSpec((None, SSD_R, 1), lambda gi, ci: (gi, 0, 0)),
            pl.BlockSpec((None, 1, SSD_GW), lambda gi, ci: (gi, 0, 0)),
        ] + g_in,
        out_specs=[
            pl.BlockSpec((t, SSD_GW), lambda gi, ci: (ci, gi)),
            pl.BlockSpec((None, None, n, SSD_GW), lambda gi, ci: (gi, ci, 0, 0)),
        ] + g_out,
        out_shape=[jax.ShapeDtypeStruct((s, g * SSD_GW), F32), jax.ShapeDtypeStruct((g, nc, n, SSD_GW), F32)] + g_shapes,
        scratch_shapes=[pltpu.VMEM((n, SSD_GW), F32)] + g_scratch,
        compiler_params=_params(("arbitrary" if ng else "parallel", "arbitrary"), has_side_effects=bool(ng)),
    )(xbc, xbc, xbc, dtc, dtr, alc, alr, dexp, *shards)


def _ssd_bwd(xbc, dtc, dtr, alc, alr, dexp, hs, dy, name):
    s = xbc.shape[0]
    g = SSM_GROUPS
    t = min(SSD_T, s)
    nc = s // t
    n = SSM_STATE
    xblocks = (g * SSD_GW) // n

    def body(x_ref, b_ref, c_ref, dtc_ref, dtr_ref, alc_ref, alr_ref, d_ref, hs_ref, dy_ref,
             dx_ref, db_ref, dc_ref, ddtc_ref, ddtr_ref, dalc_ref, dalr_ref, dd_ref, dh_scr):
        ci = pl.program_id(1)

        @pl.when(ci == 0)
        def _():
            dh_scr[...] = jnp.zeros_like(dh_scr)

        (mask, tril, triu, expand, x, bm, cm, dtc_v, dtr_v, neg_a_c, neg_a_r, acum_c, acum_r, s_cb) = _ssd_chunk_common(
            x_ref, b_ref, c_ref, dtc_ref, dtr_ref, alc_ref, alr_ref, t)
        hst = hs_ref[...]
        dhn = dh_scr[...]
        dy = dy_ref[...]
        ch = _bdot(cm, hst)
        scale_full = _hi(jnp.exp(acum_c), expand)
        sdy = scale_full * dy
        d_c = _bdot(sdy, hst, _NT)
        dh_prev = _bdot(cm, sdy, _TN)
        dacum_c = _hi_nt(sdy * ch, expand)
        dx = d_ref[...] * dy
        dd = jnp.sum(dy * x, axis=0, keepdims=True)
        last = acum_c[t - 1:t, :]
        e_last = jnp.exp(last)
        dec = _hi(jnp.broadcast_to(e_last, (SSD_R, SSD_R)), expand)[0:1, :]
        dh_prev = dh_prev + dec * dhn
        ddec = jnp.sum(hst * dhn, axis=0, keepdims=True)
        dlast = _hi_nt(jnp.broadcast_to(ddec, (SSD_R, SSD_GW)), expand)[0:1, :] * e_last
        w_e = jnp.exp(last - acum_c)
        w_c = w_e * dtc_v
        wfull = _hi(w_c, expand)
        z = _bdot(bm, dhn)
        dx = dx + wfull * z
        dw_c = _hi_nt(x * z, expand)
        ddt_c = dw_c * w_e
        q_c = dw_c * w_c
        dacum_c = dacum_c - q_c
        dlast = dlast + jnp.sum(q_c, axis=0, keepdims=True)
        d_b = _bdot(wfull * x, dhn, _NT)
        half = lax.broadcasted_iota(jnp.int32, (t, 2 * SSM_HEADDIM), 1) // SSM_HEADDIM
        lane8 = lax.broadcasted_iota(jnp.int32, (t, SSD_R), 1)
        row8 = lax.broadcasted_iota(jnp.int32, (SSD_R, t), 0)
        ds_cb = jnp.zeros((t, t), F32)
        dacum_r = jnp.zeros((SSD_R, t), F32)
        ddt_r = jnp.zeros((SSD_R, t), F32)
        parts = []
        for j in range(SSD_R // 2):
            xp = x[:, j * 128:(j + 1) * 128]
            dyp = dy[:, j * 128:(j + 1) * 128]
            dxp = jnp.zeros((t, 128), F32)
            for hh in range(2):
                h = 2 * j + hh
                dts = dtr_v[h:h + 1, :]
                decay = _head_decay(mask, acum_c, acum_r, h)
                sl = s_cb * decay
                m = sl * dts
                xm = jnp.where(half == hh, xp, 0.0)
                dym = jnp.where(half == hh, dyp, 0.0)
                dxp = dxp + _bdot(m, dym, _TN)
                dm = _bdot(dym, xm, _NT)
                ds_cb = ds_cb + dm * decay * dts
                q = dm * m
                dacum_c = dacum_c + jnp.where(lane8 == h, jnp.sum(q, axis=1, keepdims=True), 0.0)
                dacum_r = dacum_r - jnp.where(row8 == h, jnp.sum(q, axis=0, keepdims=True), 0.0)
                ddt_r = ddt_r + jnp.where(row8 == h, jnp.sum(dm * sl, axis=0, keepdims=True), 0.0)
            parts.append(dxp)
        dx_ref[...] = dx + jnp.concatenate(parts, axis=1)
        dc_ref[...] = d_c + _bdot(ds_cb, bm)
        db_ref[...] = d_b + _bdot(ds_cb, cm, _TN)
        row_t = lax.broadcasted_iota(jnp.int32, (t, SSD_R), 0)
        dacum_c = dacum_c + jnp.where(row_t == t - 1, dlast, 0.0)
        da_c = _hi(triu, dacum_c)
        da_r = _hi(dacum_r, tril)
        ddtc_ref[...] = ddt_c + da_c * neg_a_c
        ddtr_ref[...] = ddt_r + da_r * neg_a_r
        dal_c = jnp.sum(da_c * dtc_v, axis=0, keepdims=True) * neg_a_c
        dal_r = jnp.sum(da_r * dtr_v, axis=1, keepdims=True) * neg_a_r
        dh_scr[...] = dh_prev

        @pl.when(ci == 0)
        def _():
            dalc_ref[...] = dal_c
            dalr_ref[...] = dal_r
            dd_ref[...] = dd

        @pl.when(ci > 0)
        def _():
            dalc_ref[...] += dal_c
            dalr_ref[...] += dal_r
            dd_ref[...] += dd

    rev = lambda ci: nc - 1 - ci
    return pl.pallas_call(
        body,
        name=name,
        grid=(g, nc),
        in_specs=[
            pl.BlockSpec((t, SSD_GW), lambda gi, ci: (rev(ci), gi)),
            pl.BlockSpec((t, n), lambda gi, ci: (rev(ci), xblocks + gi)),
            pl.BlockSpec((t, n), lambda gi, ci: (rev(ci), xblocks + g + gi)),
            pl.BlockSpec((None, t, SSD_R), lambda gi, ci: (gi, rev(ci), 0)),
            pl.BlockSpec((None, SSD_R, t), lambda gi, ci: (gi, 0, rev(ci))),
            pl.BlockSpec((None, 1, SSD_R), lambda gi, ci: (gi, 0, 0)),
            pl.BlockSpec((None, SSD_R, 1), lambda gi, ci: (gi, 0, 0)),
            pl.BlockSpec((None, 1, SSD_GW), lambda gi, ci: (gi, 0, 0)),
            pl.BlockSpec((None, None, n, SSD_GW), lambda gi, ci: (gi, rev(ci), 0, 0)),
            pl.BlockSpec((t, SSD_GW), lambda gi, ci: (rev(ci), gi)),
        ],
        out_specs=[
            pl.BlockSpec((t, SSD_GW), lambda gi, ci: (rev(ci), gi)),
            pl.BlockSpec((t, n), lambda gi, ci: (rev(ci), gi)),
            pl.BlockSpec((t, n), lambda gi, ci: (rev(ci), gi)),
            pl.BlockSpec((None, t, SSD_R), lambda gi, ci: (gi, rev(ci), 0)),
            pl.BlockSpec((None, SSD_R, t), lambda gi, ci: (gi, 0, rev(ci))),
            pl.BlockSpec((None, 1, SSD_R), lambda gi, ci: (gi, 0, 0)),
            pl.BlockSpec((None, SSD_R, 1), lambda gi, ci: (gi, 0, 0)),
            pl.BlockSpec((None, 1, SSD_GW), lambda gi, ci: (gi, 0, 0)),
        ],
        out_shape=[
            jax.ShapeDtypeStruct((s, g * SSD_GW), F32),
            jax.ShapeDtypeStruct((s, g * n), F32),
            jax.ShapeDtypeStruct((s, g * n), F32),
            jax.ShapeDtypeStruct((g, s, SSD_R), F32),
            jax.ShapeDtypeStruct((g, SSD_R, s), F32),
            jax.ShapeDtypeStruct((g, 1, SSD_R), F32),
            jax.ShapeDtypeStruct((g, SSD_R, 1), F32),
            jax.ShapeDtypeStruct((g, 1, SSD_GW), F32),
        ],
        scratch_shapes=[pltpu.VMEM((n, SSD_GW), F32)],
        compiler_params=_params(("parallel", "arbitrary")),
    )(xbc, xbc, xbc, dtc, dtr, alc, alr, dexp, hs, dy)


def ssd_core(xbc, dtc, dtr, alc, alr, dexp, shards=(), *, name):
    shards = tuple(shards)
    fwd_name = name + "_fwd" + ("_host" if shards else "")

    @jax.custom_vjp
    def op(xbc, dtc, dtr, alc, alr, dexp, shards):
        y, _, *partial = _ssd_fwd(xbc, dtc, dtr, alc, alr, dexp, fwd_name, shards)
        return y, tuple(partial)

    def fwd(xbc, dtc, dtr, alc, alr, dexp, shards):
        y, hs, *partial = _ssd_fwd(xbc, dtc, dtr, alc, alr, dexp, fwd_name, shards)
        return (y, tuple(partial)), (xbc, dtc, dtr, alc, alr, dexp, hs, shards)

    def bwd(res, cts):
        xbc, dtc, dtr, alc, alr, dexp, hs, shards = res
        dx, db, dc, ddtc, ddtr, dalc, dalr, dd = _ssd_bwd(xbc, dtc, dtr, alc, alr, dexp, hs, cts[0], name + "_bwd")
        return (jnp.concatenate([dx, db, dc], axis=1), ddtc, ddtr, dalc, dalr, dd,
                tuple(jnp.zeros_like(s) for s in shards))

    op.defvjp(fwd, bwd)
    return op(xbc, dtc, dtr, alc, alr, dexp, shards)


def gate_norm(y, z, w):
    z = z.astype(F32)
    return (rms_norm(y * (z * jax.nn.sigmoid(z)), w).astype(BF16),)


def ssd_branch(xbc, z, dt_raw, conv_w, conv_b, dt_bias, a_log, d_skip, norm_w, shards=(), *, name):
    s = xbc.shape[0]
    g = SSM_GROUPS
    conv = causal_conv(xbc, conv_w, name=name + "_conv")
    (xc,) = rowwise(lambda c, b: ((c + b) * jax.nn.sigmoid(c + b),), (conv,), (), (conv_b[None, :],),
                    name=name + "_silu", tile=min(256, s))
    (dt,) = rowwise(lambda r, b: (jax.nn.softplus(r.astype(F32) + b),), (dt_raw,), (), (dt_bias[None, :],),
                    name=name + "_dt", tile=min(512, s))
    dt3 = dt.reshape(s, g, SSD_R)
    y, partial = ssd_core(xc, dt3.transpose(1, 0, 2), dt3.transpose(1, 2, 0), a_log.reshape(g, 1, SSD_R),
                          a_log.reshape(g, SSD_R, 1), jnp.repeat(d_skip, SSM_HEADDIM).reshape(g, 1, SSD_GW), shards,
                          name=name + "_core")
    (out,) = rowwise(gate_norm, (y.reshape(s, g, SSD_GW), z.reshape(s, g, SSD_GW)), (),
                     (norm_w.reshape(g, SSD_GW),), name=name + "_gate", tile=min(128, s))
    return out.reshape(s, g * SSD_GW), partial


def rms_norm(x, w):
    return x * lax.rsqrt(jnp.mean(x * x, axis=-1, keepdims=True) + EPS) * w


def rope_matrix():
    half = QK_ROPE // 2
    j = jnp.arange(QK_DIM)
    src = jnp.where(j < QK_NOPE + half, j + half, j - half)
    sign = jnp.where(j < QK_NOPE, 0.0, jnp.where(j < QK_NOPE + half, -1.0, 1.0))
    return (jnp.arange(QK_DIM)[:, None] == src[None, :]).astype(F32) * sign[None, :]


def rope_tables_full(positions):
    inv_freq = 1.0 / (ROPE_THETA ** (jnp.arange(0, QK_ROPE, 2, dtype=F32) / QK_ROPE))
    ang = positions.astype(F32)[:, None] * inv_freq
    s = positions.shape[0]
    cos = jnp.concatenate([jnp.ones((s, QK_NOPE), F32), jnp.cos(ang), jnp.cos(ang)], axis=-1)
    sin = jnp.concatenate([jnp.zeros((s, QK_NOPE), F32), jnp.sin(ang), jnp.sin(ang)], axis=-1)
    return cos[:, None, :], sin[:, None, :]


def head_norm_rope(x, cos_full, sin_full, rot, w):
    t, h, d = x.shape
    y = rms_norm(x.astype(F32), w)
    partner = jnp.dot(y.reshape(t * h, d), rot, precision=HI, preferred_element_type=F32).reshape(t, h, d)
    return ((y * cos_full + partner * sin_full).astype(BF16),)


def _norm(x, w, *, name, out_dtype=BF16, tile=256):
    (y,) = rowwise(lambda x, w: (rms_norm(x.astype(F32), w).astype(out_dtype),), (x,), (), (w[None, :],), name=name,
                   tile=min(tile, x.shape[0]))
    return y


Q_LORA = 512
KV_LORA = 512
W_IN_PIECES = ("cq", "ckv", "kr", "z", "xbc", "dt", "ga", "gb")


def w_in_widths(d_model):
    d_inner = 2 * d_model
    conv_dim = d_inner + 2 * SSM_GROUPS * SSM_STATE
    return (Q_LORA, KV_LORA, QK_ROPE, d_inner, conv_dim, d_inner // SSM_HEADDIM, d_model, d_model)


LINEAR_NAMES = ("cq", "ckv", "kd", "z", "xbc", "ga", "gb", "w_uq", "w_ukv", "w_o_mla", "w_o_ssm", "w_out", "w_up",
                "w_down", "w_ple_gate", "w_ple")


GATHER_IN_ATTENTION = (0, 6)


def layer_forward(x, p_i, cos_full, sin_full, rot, wb, sm, sinks, tok=(), next_shards=()):
    s, d = x.shape
    in_att = [t for t in range(len(next_shards)) if t in GATHER_IN_ATTENTION]
    in_ssd = [t for t in range(len(next_shards)) if t not in GATHER_IN_ATTENTION]
    lin = lambda a, n: linear(a, wb[n], sinks[n], name="lin_" + n)
    up32 = lambda a: a.astype(F32)
    h = _norm(x, sm["norm_mix_w"], name="norm_mix", out_dtype=F32)
    c_q, c_kv, kd = lin(h, "cq"), lin(h, "ckv"), lin(h, "kd")
    z, xbc, g_a, g_b = lin(h, "z"), lin(h, "xbc"), lin(h, "ga"), lin(h, "gb")
    k_r, dt_raw = kd[:, :QK_ROPE], kd[:, QK_ROPE:]
    q = lin(_norm(c_q, sm["q_a_norm_w"], name="norm_qa"), "w_uq").reshape(s, MLA_HEADS, QK_DIM)
    kv = lin(_norm(c_kv, sm["kv_a_norm_w"], name="norm_kva"), "w_ukv").reshape(s, MLA_HEADS, QK_NOPE + V_DIM)
    k = jnp.concatenate([kv[..., :QK_NOPE], jnp.broadcast_to(k_r[:, None, :], (s, MLA_HEADS, QK_ROPE))], axis=-1)
    v = kv[..., QK_NOPE:]
    tq = min(128, s)
    (q,) = rowwise(head_norm_rope, (q,), (cos_full, sin_full), (sm["q_norm_w"][None, :],), tables=(rot,), name="q_rope", tile=tq)
    (k,) = rowwise(head_norm_rope, (k,), (cos_full, sin_full), (sm["k_norm_w"][None, :],), tables=(rot,), name="k_rope", tile=tq)
    hm = lambda a: a.transpose(1, 0, 2)
    o, tok, part_att = attention(hm(q), hm(k), hm(v), tok, [next_shards[t] for t in in_att], name="attn")
    o = hm(o).reshape(s, MLA_HEADS * V_DIM)
    y_a = lin(o, "w_o_mla")
    y_ssd, part_ssd = ssd_branch(xbc, z, dt_raw, sm["conv_w"], sm["conv_b"], sm["dt_bias"], sm["a_log"], sm["d_skip"],
                                 sm["ssm_norm_w"], [next_shards[t] for t in in_ssd], name="ssd")
    y_b = lin(y_ssd, "w_o_ssm")
    partial = [None] * len(next_shards)
    for t, buf in list(zip(in_att, part_att)) + list(zip(in_ssd, part_ssd)):
        partial[t] = buf
    sig = jax.nn.sigmoid
    tr = min(256, s)
    (merged,) = rowwise(lambda ga, gb, ya, yb: ((sig(up32(ga)) * up32(ya) + sig(up32(gb)) * up32(yb)).astype(BF16),),
                        (g_a, g_b, y_a, y_b), (), (), name="merge", tile=tr)
    mixed = lin(merged, "w_out")
    (x,) = rowwise(lambda x, m: (x + up32(m),), (x, mixed), (), (), name="add_mix", tile=tr)
    up = lin(_norm(x, sm["norm_mlp_w"], name="norm_mlp"), "w_up")
    (act,) = rowwise(lambda u: (jnp.square(jnp.maximum(up32(u), 0.0)).astype(BF16),), (up,), (), (), name="relu2", tile=tr)
    down = lin(act, "w_down")
    (x,) = rowwise(lambda x, m: (x + up32(m),), (x, down), (), (), name="add_mlp", tile=tr)
    pg = lin(_norm(x, sm["ple_norm_w"], name="norm_ple"), "w_ple_gate")
    pe = lin(p_i, "w_ple")
    (x,) = rowwise(lambda x, pe, pg: (x + up32(pe) * sig(up32(pg)),), (x, pe, pg), (), (), name="ple_add", tile=tr)
    return (x, tok), partial


def loss_and_cotangent(y, target):
    s, d = y.shape

    def f(y, t):
        e = y - t
        return e * (1.0 / d), 0.5 * jnp.sum(jnp.sum(e * e, axis=1, keepdims=True) * (1.0 / d), axis=0, keepdims=True)

    dy, part = _tiled_call(f, (y, target), (), min(256, s), "loss", 1)
    return dy, part[0, 0]


ADAM_BLOCK_ELEMS = 256 * 1024


def adamw(w, g, m, v, *, name):
    rows, cols = w.shape
    budget = max(8, ADAM_BLOCK_ELEMS // cols)
    tile = _pick(rows, tuple(t for t in (512, 256, 128, 64, 32, 16, 8) if t <= budget))

    def f(w, g, m, v):
        m = ADAM_B1 * m + (1.0 - ADAM_B1) * g
        v = ADAM_B2 * v + (1.0 - ADAM_B2) * jnp.square(g)
        m_hat = m / (1.0 - ADAM_B1 ** ADAM_STEP)
        v_hat = v / (1.0 - ADAM_B2 ** ADAM_STEP)
        delta = -ADAM_LR * (m_hat / (jnp.sqrt(v_hat) + ADAM_EPS) + ADAM_WD * w)
        return delta, m, v

    return _tiled_call(f, (w, g, m, v), (), tile, name, 0)


MESH_ID = pl.DeviceIdType.MESH
N_CHIPS = 4
_ANY = pl.BlockSpec(memory_space=pl.ANY)


def _place():
    return lax.axis_index("x"), lax.axis_index("y"), lax.axis_index("c")


def _other_chips(x, y):
    return [(1 - x, y), (x, 1 - y), (1 - x, 1 - y)]


def _rcopy(src, dst, send_sem, recv_sem, device):
    return pltpu.make_async_remote_copy(src_ref=src, dst_ref=dst, send_sem=send_sem, recv_sem=recv_sem,
                                        device_id=device, device_id_type=MESH_ID)


def _sems(n, k):
    return pltpu.SemaphoreType.DMA((n, k))


def _comm_call(body, name, ins, out_shapes, scratch, aliases=None):
    return pl.pallas_call(
        body,
        name=name,
        in_specs=[_ANY] * len(ins),
        out_specs=[_ANY] * len(out_shapes),
        out_shape=out_shapes,
        scratch_shapes=scratch,
        input_output_aliases=aliases or {},
        compiler_params=pltpu.CompilerParams(has_side_effects=True),
    )(*ins)


SPLIT_ROWS = 32


def gather_shards(shards, *, name):
    n = len(shards)
    split = [s.shape[0] % SPLIT_ROWS == 0 for s in shards]

    def body(*refs):
        srcs, outs = refs[:n], refs[n:2 * n]
        send_sems, recv_sems = refs[2 * n:]
        x, y, c = _place()
        sibling = (x, y, 1 - c)
        chips = _other_chips(x, y)
        me = 2 * x + y

        def part(t, slot, h):
            if not split[t]:
                return outs[t].at[slot]
            half = shards[t].shape[0] // 2
            return outs[t].at[slot, pl.ds(h * half, half), :]

        def own(t):
            if not split[t]:
                return srcs[t]
            half = shards[t].shape[0] // 2
            return srcs[t].at[pl.ds(c * half, half), :]

        sent = []
        for t in range(n):
            for j, chip in enumerate(chips):
                sent.append(_rcopy(own(t), part(t, me, c), send_sems.at[t, j], recv_sems.at[t, j], (*chip, c)))
                sent[-1].start()
        for t in range(n):
            sent.append(_rcopy(srcs[t], outs[t].at[me], send_sems.at[t, 6], recv_sems.at[t, 6], sibling))
            sent[-1].start()
        for t in range(n):
            for j, (cx, cy) in enumerate(chips):
                got = part(t, 2 * cx + cy, c)
                _rcopy(got, got, send_sems.at[t, j], recv_sems.at[t, j], (cx, cy, c)).wait_recv()
                if split[t]:
                    sent.append(_rcopy(got, got, send_sems.at[t, 3 + j], recv_sems.at[t, 3 + j], sibling))
                    sent[-1].start()
        for t in range(n):
            if split[t]:
                for j, (cx, cy) in enumerate(chips):
                    got = part(t, 2 * cx + cy, 1 - c)
                    _rcopy(got, got, send_sems.at[t, 3 + j], recv_sems.at[t, 3 + j], sibling).wait_recv()
        for t in range(n):
            _rcopy(srcs[t], outs[t].at[me], send_sems.at[t, 6], recv_sems.at[t, 6], sibling).wait_recv()
        for cp in sent:
            cp.wait_send()

    out_shapes = [jax.ShapeDtypeStruct((N_CHIPS,) + s.shape, s.dtype) for s in shards]
    return _comm_call(body, name, shards, out_shapes, [_sems(n, 7), _sems(n, 7)])


def gather_finish(shards, partial, *, name):
    n = len(shards)
    split = [_split_rows(s.shape) for s in shards]

    def body(*refs):
        srcs, outs = refs[:n], refs[2 * n:3 * n]
        send_sems, recv_sems = refs[3 * n:]
        x, y, c = _place()
        sibling = (x, y, 1 - c)
        chips = _other_chips(x, y)
        me = 2 * x + y

        def part(t, slot, h):
            half = shards[t].shape[0] // 2
            return outs[t].at[slot, pl.ds(h * half, half), :]

        sent = []
        for t in range(n):
            if split[t]:
                for j, (cx, cy) in enumerate(chips):
                    got = part(t, 2 * cx + cy, c)
                    sent.append(_rcopy(got, got, send_sems.at[t, j], recv_sems.at[t, j], sibling))
                    sent[-1].start()
            sent.append(_rcopy(srcs[t], outs[t].at[me], send_sems.at[t, 3], recv_sems.at[t, 3], sibling))
            sent[-1].start()
        for t in range(n):
            if split[t]:
                for j, (cx, cy) in enumerate(chips):
                    got = part(t, 2 * cx + cy, 1 - c)
                    _rcopy(got, got, send_sems.at[t, j], recv_sems.at[t, j], sibling).wait_recv()
            _rcopy(srcs[t], outs[t].at[me], send_sems.at[t, 3], recv_sems.at[t, 3], sibling).wait_recv()
        for cp in sent:
            cp.wait_send()

    out_shapes = [jax.ShapeDtypeStruct(p.shape, p.dtype) for p in partial]
    return _comm_call(body, name, list(shards) + list(partial), out_shapes, [_sems(n, 4), _sems(n, 4)],
                      aliases={n + t: t for t in range(n)})


def sibling_take_half(gs, *, name):
    n = len(gs)

    def body(*refs):
        g_refs, a_refs = refs[:n], refs[n:2 * n]
        send_sems, recv_sems = refs[2 * n:]
        x, y, c = _place()
        copies = []
        for t in range(n):
            half = gs[t].shape[1] // 2
            copies.append(_rcopy(g_refs[t].at[:, pl.ds((1 - c) * half, half), :], a_refs[t], send_sems.at[t, 0],
                                 recv_sems.at[t, 0], (x, y, 1 - c)))
            copies[-1].start()
        for cp in copies:
            cp.wait()

    out_shapes = [jax.ShapeDtypeStruct((g.shape[0], g.shape[1] // 2, g.shape[2]), g.dtype) for g in gs]
    return _comm_call(body, name, gs, out_shapes, [_sems(n, 1), _sems(n, 1)])


ELEMWISE_BLOCK_ELEMS = 256 * 1024


def _row_tile(rows, cols):
    budget = max(16, ELEMWISE_BLOCK_ELEMS // cols)
    return _pick(rows, tuple(t for t in (1024, 512, 256, 128, 64, 32, 16) if t <= budget))


def _core_and_chip():
    x, y, c = _place()
    return jnp.stack([c, 2 * x + y]).astype(jnp.int32)


def pair_add(g, a, *, name):
    n, rows, cols = g.shape
    half = rows // 2
    tile = _row_tile(half, cols)
    nb = half // tile

    def body(who_ref, g_ref, a_ref, o_ref):
        o_ref[...] = (g_ref[...] + a_ref[...]).astype(o_ref.dtype)

    return pl.pallas_call(
        body,
        name=name,
        grid_spec=pltpu.PrefetchScalarGridSpec(
            num_scalar_prefetch=1,
            grid=(n, nb),
            in_specs=[
                pl.BlockSpec((None, tile, cols), lambda j, i, who: (j, who[0] * nb + i, 0)),
                pl.BlockSpec((None, tile, cols), lambda j, i, who: (j, i, 0)),
            ],
            out_specs=pl.BlockSpec((None, tile, cols), lambda j, i, who: (j, i, 0)),
        ),
        out_shape=jax.ShapeDtypeStruct((n, half, cols), BF16),
        compiler_params=_params(("parallel", "parallel")),
    )(_core_and_chip(), g, a)


def exchange_chip_slots(ps, *, name):
    n = len(ps)

    def body(*refs):
        p_refs, b_refs = refs[:n], refs[n:2 * n]
        send_sems, recv_sems = refs[2 * n:]
        x, y, c = _place()
        me = 2 * x + y
        chips = _other_chips(x, y)
        sends = []
        for t in range(n):
            for j, (cx, cy) in enumerate(chips):
                sends.append(_rcopy(p_refs[t].at[2 * cx + cy], b_refs[t].at[me], send_sems.at[t, j], recv_sems.at[t, j],
                                    (cx, cy, c)))
                sends[-1].start()
        for t in range(n):
            for j, (cx, cy) in enumerate(chips):
                got = b_refs[t].at[2 * cx + cy]
                _rcopy(got, got, send_sems.at[t, j], recv_sems.at[t, j], (cx, cy, c)).wait_recv()
        for cp in sends:
            cp.wait_send()

    out_shapes = [jax.ShapeDtypeStruct(p.shape, p.dtype) for p in ps]
    return _comm_call(body, name, ps, out_shapes, [_sems(n, 3), _sems(n, 3)])


def chips_add(g, a, b, *, name):
    n, rows, cols = g.shape
    half = rows // 2
    tile = _row_tile(half, cols)
    nb = half // tile

    def body(who_ref, g_ref, a_ref, *rest):
        o_ref = rest[-1]
        acc = g_ref[...] + a_ref[...]
        for b_ref in rest[:-1]:
            acc = acc + b_ref[...].astype(F32)
        o_ref[...] = acc

    other = lambda k: pl.BlockSpec((None, tile, cols), lambda i, who, k=k: ((who[1] + k) % n, i, 0))
    return pl.pallas_call(
        body,
        name=name,
        grid_spec=pltpu.PrefetchScalarGridSpec(
            num_scalar_prefetch=1,
            grid=(nb,),
            in_specs=[
                pl.BlockSpec((None, tile, cols), lambda i, who: (who[1], who[0] * nb + i, 0)),
                pl.BlockSpec((None, tile, cols), lambda i, who: (who[1], i, 0)),
            ] + [other(k) for k in range(1, n)],
            out_specs=pl.BlockSpec((tile, cols), lambda i, who: (who[0] * nb + i, 0)),
        ),
        out_shape=jax.ShapeDtypeStruct((rows, cols), F32),
        compiler_params=_params(("parallel",)),
    )(_core_and_chip(), g, a, *([b] * (n - 1)))


def sibling_join_halves(rs, *, name):
    n = len(rs)

    def body(*refs):
        r_refs = refs[n:2 * n]
        send_sems, recv_sems = refs[2 * n:]
        x, y, c = _place()
        sibling = (x, y, 1 - c)
        copies = []
        for t in range(n):
            half = rs[t].shape[0] // 2
            mine = r_refs[t].at[pl.ds(c * half, half), :]
            copies.append(_rcopy(mine, mine, send_sems.at[t, 0], recv_sems.at[t, 0], sibling))
            copies[-1].start()
        for t in range(n):
            half = rs[t].shape[0] // 2
            got = r_refs[t].at[pl.ds((1 - c) * half, half), :]
            _rcopy(got, got, send_sems.at[t, 0], recv_sems.at[t, 0], sibling).wait_recv()
        for cp in copies:
            cp.wait_send()

    out_shapes = [jax.ShapeDtypeStruct(r.shape, r.dtype) for r in rs]
    return _comm_call(body, name, rs, out_shapes, [_sems(n, 1), _sems(n, 1)], aliases={t: t for t in range(n)})


def reduce_to_owner(gs, *, name):
    gs, a, p = reduce_pairs(gs, name=name)
    return reduce_finish(gs, a, exchange_chip_slots(p, name=name + "_chips"), name=name)


def reduce_pairs(gs, *, name):
    gs = list(gs)
    a = sibling_take_half(gs, name=name + "_pair")
    p = [pair_add(g, ai, name=name + "_pair_add") for g, ai in zip(gs, a)]
    return gs, a, p


def reduce_finish(gs, a, b, *, name):
    f = [chips_add(g, ai, bi, name=name + "_chips_add") for g, ai, bi in zip(gs, a, b)]
    return sibling_join_halves(f, name=name + "_join")


def allreduce_small(v, *, name):
    rows, cols = v.shape

    def body(v_ref, o_ref, buf, send_sems, recv_sems):
        x, y, c = _place()
        me = 4 * x + 2 * y + c
        buf[me] = v_ref[...]
        copies = []
        for k in range(1, 8):
            bx, by, bc = (k >> 2) & 1, (k >> 1) & 1, k & 1
            peer = (x if bx == 0 else 1 - x, y if by == 0 else 1 - y, c if bc == 0 else 1 - c)
            copies.append(_rcopy(v_ref, buf.at[me], send_sems.at[k - 1], recv_sems.at[k - 1], peer))
        for cp in copies:
            cp.start()
        for k in range(1, 8):
            bx, by, bc = (k >> 2) & 1, (k >> 1) & 1, k & 1
            px, py, pc = (x if bx == 0 else 1 - x, y if by == 0 else 1 - y, c if bc == 0 else 1 - c)
            _rcopy(v_ref, buf.at[4 * px + 2 * py + pc], send_sems.at[k - 1], recv_sems.at[k - 1], (px, py, pc)).wait_recv()
        for cp in copies:
            cp.wait_send()
        acc = buf[0]
        for j in range(1, 8):
            acc = acc + buf[j]
        o_ref[...] = acc

    return pl.pallas_call(
        body,
        name=name,
        in_specs=[pl.BlockSpec(memory_space=pltpu.VMEM)],
        out_specs=pl.BlockSpec(memory_space=pltpu.VMEM),
        out_shape=jax.ShapeDtypeStruct((rows, cols), v.dtype),
        scratch_shapes=[pltpu.VMEM((8, rows, cols), v.dtype), pltpu.SemaphoreType.DMA((7,)), pltpu.SemaphoreType.DMA((7,))],
        compiler_params=pltpu.CompilerParams(has_side_effects=True, vmem_limit_bytes=V7X_VMEM_LIMIT),
    )(v)


BIG = (("w_in", 1), ("w_uq", 1), ("w_ukv", 1), ("w_o_mla", 0), ("w_o_ssm", 0), ("w_out", 0), ("w_up", 1),
       ("w_down", 0), ("w_ple_gate", 0), ("w_ple", 1))
SHARDED = BIG + (("conv_w", 1),)
SMALL = ("norm_mix_w", "q_a_norm_w", "kv_a_norm_w", "q_norm_w", "k_norm_w", "conv_b", "dt_bias", "a_log", "d_skip",
         "ssm_norm_w", "norm_mlp_w", "ple_norm_w")
WEIGHTS = ("norm_mix_w", "w_in", "q_a_norm_w", "w_uq", "kv_a_norm_w", "w_ukv", "q_norm_w", "k_norm_w", "w_o_mla", "conv_w",
           "conv_b", "dt_bias", "a_log", "d_skip", "ssm_norm_w", "w_o_ssm", "w_out", "norm_mlp_w", "w_up", "w_down",
           "ple_norm_w", "w_ple_gate", "w_ple")


def _to_rows(flat, cols, row_multiple):
    n = flat.shape[-1]
    rows = -(-n // (cols * row_multiple)) * row_multiple
    pad = [(0, 0)] * (flat.ndim - 1) + [(0, rows * cols - n)]
    return jnp.pad(flat, pad).reshape(flat.shape[:-1] + (rows, cols))


def _w_in_ranges(d_model):
    out, lo = {}, 0
    for n, wd in zip(W_IN_PIECES, w_in_widths(d_model)):
        out[n] = (lo, lo + wd)
        lo += wd
    return out


def w_in_pieces(w3):
    _, k, c = w3.shape
    pc = {}
    for n, (lo, hi) in _w_in_ranges(k).items():
        cuts = [w3[j][:, max(lo, j * c) - j * c:min(hi, (j + 1) * c) - j * c]
                for j in range(N_CHIPS) if max(lo, j * c) < min(hi, (j + 1) * c)]
        pc[n] = cuts[0] if len(cuts) == 1 else jnp.concatenate(cuts, axis=1)
    pc["kd"] = jnp.concatenate([pc.pop("kr"), pc.pop("dt")], axis=1)
    return pc


def w_in_shard_grads(g, k, c):
    g = dict(g)
    g["kr"], g["dt"] = g["kd"][:, :QK_ROPE], g["kd"][:, QK_ROPE:]
    shards = []
    for j in range(N_CHIPS):
        cuts = []
        for n, (lo, hi) in _w_in_ranges(k).items():
            a, b = max(lo, j * c), min(hi, (j + 1) * c)
            if a < b:
                cuts.append(g[n][:, a - lo:b - lo])
        shards.append(jnp.concatenate(cuts, axis=1))
    return jnp.stack(shards)


def kernel(x, p, positions, norm_mix_w, w_in, q_a_norm_w, w_uq, kv_a_norm_w, w_ukv, q_norm_w, k_norm_w, w_o_mla, conv_w, conv_b, dt_bias, a_log, d_skip, ssm_norm_w, w_o_ssm, w_out, norm_mlp_w, w_up, w_down, ple_norm_w, w_ple_gate, w_ple, loss_target, m_norm_mix_w, m_w_in, m_q_a_norm_w, m_w_uq, m_kv_a_norm_w, m_w_ukv, m_q_norm_w, m_k_norm_w, m_w_o_mla, m_conv_w, m_conv_b, m_dt_bias, m_a_log, m_d_skip, m_ssm_norm_w, m_w_o_ssm, m_w_out, m_norm_mlp_w, m_w_up, m_w_down, m_ple_norm_w, m_w_ple_gate, m_w_ple, v_norm_mix_w, v_w_in, v_q_a_norm_w, v_w_uq, v_kv_a_norm_w, v_w_ukv, v_q_norm_w, v_k_norm_w, v_w_o_mla, v_conv_w, v_conv_b, v_dt_bias, v_a_log, v_d_skip, v_ssm_norm_w, v_w_o_ssm, v_w_out, v_norm_mlp_w, v_w_up, v_w_down, v_ple_norm_w, v_w_ple_gate, v_w_ple):
    a = dict(locals())
    x, p, pos, target = a["x"][0], a["p"][:, 0], a["positions"][0], a["loss_target"][0]
    depth = a["w_in"].shape[0]
    shard_shapes = {n: tuple(a[n].shape[1:]) for n, _ in SHARDED}
    cos_full, sin_full = rope_tables_full(pos)
    rot = rope_matrix()

    shards = [[a[n][i].astype(BF16) for n, _ in BIG] + [a["conv_w"][i]] for i in range(depth)]
    tok = tuple(jnp.zeros((N_CHIPS, shard_shapes[n][0] // 2, shard_shapes[n][1]), BF16) for n, _ in BIG)
    layer_vjps = []
    got = gather_shards(shards[0], name="gather_weights")
    for i in range(depth):
        full = dict(zip([n for n, _ in SHARDED], got))
        w_i = w_in_pieces(full["w_in"])
        for n, ax in BIG[1:]:
            w_i[n] = full[n] if ax == 1 else full[n].reshape((-1, full[n].shape[-1]))
        sm_i = {n: a[n][i] for n in SMALL}
        sm_i["conv_w"] = full["conv_w"].transpose(1, 0, 2).reshape(CONV_WIDTH, -1)
        sinks_i = {n: jnp.zeros(w_i[n].shape, F32) for n in LINEAR_NAMES}
        last = i == depth - 1
        f_i = functools.partial(layer_forward, p_i=p[i], cos_full=cos_full, sin_full=sin_full, rot=rot, wb=w_i,
                                next_shards=() if last else shards[i + 1])
        g_i = lambda x, sm, sk, tk, f=f_i: f(x, sm=sm, sinks=sk, tok=tk)
        (x, _), vjp_i, partial = jax.vjp(g_i, x, sm_i, sinks_i, () if last else tok, has_aux=True)
        layer_vjps.append(vjp_i)
        if not last:
            got = gather_finish(shards[i + 1], partial, name="gather_finish")
    dx, loss_part = loss_and_cotangent(x, target)
    loss = lax.psum(loss_part, ("x", "y", "c"))

    per_layer, d_small, pending = [None] * depth, [None] * depth, None
    for i in reversed(range(depth)):
        payload = tuple(pending[2]) if pending is not None else ()
        dx, d_small[i], d_sinks_i, got = layer_vjps[i]((dx, payload))
        if pending is not None:
            per_layer[i + 1] = reduce_finish(pending[0], pending[1], list(got), name="reduce_grads")
        g_i = [w_in_shard_grads(d_sinks_i, *shard_shapes["w_in"])]
        g_i += [d_sinks_i[n].reshape((N_CHIPS,) + shard_shapes[n]) for n, _ in BIG[1:]]
        pending = reduce_pairs(g_i, name="reduce_grads")
    per_layer[0] = reduce_finish(pending[0], pending[1], exchange_chip_slots(pending[2], name="reduce_grads_chips"),
                                 name="reduce_grads")
    grads = {n: jnp.stack([per_layer[i][t] for i in range(depth)]) for t, (n, _) in enumerate(BIG)}

    small_names = SMALL + ("conv_w",)
    flat = jnp.concatenate([d_small[i][n].reshape(-1) for i in range(depth) for n in small_names])
    n_small = flat.shape[0]
    red = allreduce_small(_to_rows(flat, 128, 8), name="reduce_small").reshape(-1)[:n_small]
    per = n_small // depth
    off = 0
    for n in SMALL:
        width = a[n].shape[-1]
        grads[n] = jnp.stack([red[i * per + off:i * per + off + width] for i in range(depth)])
        off += width
    conv_c = shard_shapes["conv_w"][1]
    conv_full = jnp.stack([red[i * per + off:i * per + off + CONV_WIDTH * N_CHIPS * conv_c] for i in range(depth)])
    chip = 2 * lax.axis_index("x") + lax.axis_index("y")
    grads["conv_w"] = lax.dynamic_index_in_dim(conv_full.reshape(depth, CONV_WIDTH, N_CHIPS, conv_c), chip, axis=2,
                                               keepdims=False)

    deltas, new_m, new_v = {}, {}, {}
    two_d = lambda t: t.reshape(-1, t.shape[-1])
    for n in WEIGHTS:
        d, m, v = adamw(two_d(a[n]), two_d(grads[n]), two_d(a["m_" + n]), two_d(a["v_" + n]), name="adamw")
        deltas[n], new_m[n], new_v[n] = d.reshape(a[n].shape), m.reshape(a[n].shape), v.reshape(a[n].shape)

    return (loss, dx[None], *[grads[n].reshape(a[n].shape) for n in WEIGHTS], *[deltas[n] for n in WEIGHTS],
            *[new_m[n] for n in WEIGHTS], *[new_v[n] for n in WEIGHTS])
```

```python
import functools

import jax
import jax.numpy as jnp
from jax import lax
from jax.experimental import pallas as pl
from jax.experimental.pallas import tpu as pltpu

F32 = jnp.float32
BF16 = jnp.bfloat16
HI = lax.Precision.HIGHEST

EPS = 1e-6
MLA_HEADS = 16
QK_NOPE = 128
QK_ROPE = 64
QK_DIM = QK_NOPE + QK_ROPE
V_DIM = 128
ROPE_THETA = 10000.0
ATT_CHUNK = 64
SSM_GROUPS = 8
SSM_HEADDIM = 64
SSM_STATE = 128
CONV_WIDTH = 4
ADAM_LR = 0.001
ADAM_B1 = 0.9
ADAM_B2 = 0.999
ADAM_EPS = 1e-08
ADAM_WD = 0.01
ADAM_STEP = 10

V7X_VMEM_LIMIT = 56 * 1024 * 1024


def _params(sem=None, **kw):
    return pltpu.CompilerParams(dimension_semantics=sem, vmem_limit_bytes=V7X_VMEM_LIMIT, **kw)


def _pick(n, prefs):
    for t in prefs:
        if n % t == 0:
            return t
    return n


MATMUL_OPERAND_BYTES = 24 * 1024 * 1024


def matmul(a, b, *, ta=False, tb=False, out_blocks=0, out_dtype=F32, name):
    m, k = (a.shape[1], a.shape[0]) if ta else a.shape
    blocked = b.ndim == 3
    if blocked:
        nb, rows, c = b.shape
        k2, n = (nb * c, rows) if tb else (rows, nb * c)
    else:
        k2, n = (b.shape[1], b.shape[0]) if tb else b.shape
    assert k == k2, (a.shape, b.shape, ta, tb)
    n_unit = n // out_blocks if out_blocks else (c if blocked and not tb else n)
    k_unit = c if blocked and tb else k
    tm = _pick(m, (1024, 512, 256, 128))
    tn = _pick(n_unit, (1024, 512, 256, 128))
    in_bytes = tm * a.dtype.itemsize + tn * b.dtype.itemsize
    tk = _pick(k_unit, tuple(t for t in (2048, 1024, 512, 256, 128) if 2 * t * in_bytes <= MATMUL_OPERAND_BYTES))
    nk = k // tk
    dn = (((0 if ta else 1,), (1 if tb else 0,)), ((), ()))

    def body(a_ref, b_ref, o_ref, *acc):
        part = lambda: lax.dot_general(a_ref[...].astype(BF16), b_ref[...].astype(BF16), dn, preferred_element_type=F32)
        if nk == 1:
            o_ref[...] = part().astype(o_ref.dtype)
            return
        (acc_ref,) = acc
        kk = pl.program_id(2)

        @pl.when(kk == 0)
        def _():
            acc_ref[...] = jnp.zeros_like(acc_ref)

        acc_ref[...] += part()

        @pl.when(kk == nk - 1)
        def _():
            o_ref[...] = acc_ref[...].astype(o_ref.dtype)

    a_spec = pl.BlockSpec((tk, tm), lambda i, j, kk: (kk, i)) if ta else pl.BlockSpec((tm, tk), lambda i, j, kk: (i, kk))
    if not blocked:
        b_spec = pl.BlockSpec((tn, tk), lambda i, j, kk: (j, kk)) if tb else pl.BlockSpec((tk, tn), lambda i, j, kk: (kk, j))
    elif tb:
        kb = c // tk
        b_spec = pl.BlockSpec((None, tn, tk), lambda i, j, kk: (kk // kb, j, kk % kb))
    else:
        cb = c // tn
        b_spec = pl.BlockSpec((None, tk, tn), lambda i, j, kk: (j // cb, kk, j % cb))
    if out_blocks:
        ob = n_unit // tn
        out_spec = pl.BlockSpec((None, tm, tn), lambda i, j, kk: (j // ob, i, j % ob))
        out_shape = jax.ShapeDtypeStruct((out_blocks, m, n_unit), out_dtype)
    else:
        out_spec = pl.BlockSpec((tm, tn), lambda i, j, kk: (i, j))
        out_shape = jax.ShapeDtypeStruct((m, n), out_dtype)
    return pl.pallas_call(
        body,
        name=name,
        grid=(m // tm, n // tn, nk),
        in_specs=[a_spec, b_spec],
        out_specs=out_spec,
        out_shape=out_shape,
        scratch_shapes=[pltpu.VMEM((tm, tn), F32)] if nk > 1 else [],
        compiler_params=_params(("parallel", "parallel", "arbitrary")),
    )(a, b)


def linear(a, w, sink, *, name, out_dtype=BF16):
    @jax.custom_vjp
    def op(a, w, sink):
        return matmul(a, w, out_dtype=out_dtype, name=name + "_fwd")

    def fwd(a, w, sink):
        return op(a, w, sink), (a, w)

    def bwd(res, ct):
        a, w = res
        da = matmul(ct, w, tb=True, out_dtype=a.dtype, name=name + "_bwd_da")
        dw = matmul(a, ct, ta=True, out_blocks=w.shape[0] if w.ndim == 3 else 0, name=name + "_bwd_dw")
        return da, jnp.zeros_like(w), dw

    op.defvjp(fwd, bwd)
    return op(a, w, sink)


def _tiled_call(fn, tiled, whole, tile, name, n_acc):
    rows = tiled[0].shape[0]
    assert rows % tile == 0
    t_avals = [jax.ShapeDtypeStruct((tile,) + a.shape[1:], a.dtype) for a in tiled]
    w_avals = [jax.ShapeDtypeStruct(a.shape, a.dtype) for a in whole]
    outs = jax.eval_shape(fn, *t_avals, *w_avals)
    n_in = len(tiled) + len(whole)
    n_t = len(outs) - n_acc

    def body(*refs):
        res = fn(*[r[...] for r in refs[:n_in]])
        o_refs = refs[n_in:]
        for r, v in zip(o_refs[:n_t], res[:n_t]):
            r[...] = v.astype(r.dtype)
        if n_acc:
            first = pl.program_id(0) == 0

            @pl.when(first)
            def _():
                for r, v in zip(o_refs[n_t:], res[n_t:]):
                    r[...] = v.astype(F32)

            @pl.when(jnp.logical_not(first))
            def _():
                for r, v in zip(o_refs[n_t:], res[n_t:]):
                    r[...] += v.astype(F32)

    def tspec(a):
        nd = len(a.shape)
        return pl.BlockSpec((tile,) + tuple(a.shape[1:]), lambda i, nd=nd: (i,) + (0,) * (nd - 1))

    def wspec(a):
        nd = len(a.shape)
        return pl.BlockSpec(tuple(a.shape), lambda i, nd=nd: (0,) * nd)

    out_shape = [jax.ShapeDtypeStruct((rows,) + o.shape[1:], o.dtype) for o in outs[:n_t]]
    out_shape += [jax.ShapeDtypeStruct(o.shape, F32) for o in outs[n_t:]]
    out_specs = [tspec(o) for o in out_shape[:n_t]] + [wspec(o) for o in out_shape[n_t:]]
    return pl.pallas_call(
        body,
        name=name,
        grid=(rows // tile,),
        in_specs=[tspec(a) for a in tiled] + [wspec(a) for a in whole],
        out_specs=out_specs,
        out_shape=out_shape,
        compiler_params=_params(("arbitrary",) if n_acc else ("parallel",)),
    )(*tiled, *whole)


def rowwise(f, rows, consts, params, *, name, tile, tables=()):
    rows, consts, tables, params = tuple(rows), tuple(consts), tuple(tables), tuple(params)
    nr, nc, ntab, npar = len(rows), len(consts), len(tables), len(params)

    @jax.custom_vjp
    def op(rows, consts, tables, params):
        return tuple(_tiled_call(f, rows + consts, tables + params, tile, name + "_fwd", 0))

    def fwd(rows, consts, tables, params):
        return op(rows, consts, tables, params), (rows, consts, tables, params)

    def bwd(res, cts):
        rows, consts, tables, params = res
        ncts = len(cts)

        def g(*args):
            r = args[:nr]
            c = args[nr:nr + nc]
            ct = args[nr + nc:nr + nc + ncts]
            tab = args[nr + nc + ncts:nr + nc + ncts + ntab]
            p = args[nr + nc + ncts + ntab:]
            _, vjp = jax.vjp(lambda *rp: f(*rp[:nr], *c, *tab, *rp[nr:]), *r, *p)
            return tuple(vjp(tuple(ct)))

        outs = _tiled_call(g, rows + consts + tuple(cts), tables + params, tile, name + "_bwd", npar)
        d_rows = tuple(o.astype(r.dtype) for o, r in zip(outs[:nr], rows))
        d_params = tuple(o.astype(p.dtype) for o, p in zip(outs[nr:], params))
        zeros = lambda xs: tuple(jnp.zeros_like(a) for a in xs)
        return d_rows, zeros(consts), zeros(tables), d_params

    op.defvjp(fwd, bwd)
    return op(rows, consts, tables, params)


ATT_TILE = 512
LOG2E = 1.4426950408889634
HOSTED_IN_DQ = 4
_NT = (((1,), (1,)), ((), ()))
_TN = (((0,), (0,)), ((), ()))


def _chunk_mask(row0, col0, shape):
    r = (row0 + lax.broadcasted_iota(jnp.int32, shape, 0)) // ATT_CHUNK
    c = (col0 + lax.broadcasted_iota(jnp.int32, shape, 1)) // ATT_CHUNK
    return c <= r


def _split_rows(shape):
    return shape[0] % SPLIT_ROWS == 0


def _hosted_gather(src_refs, out_refs, send_sems, recv_sems, first, last):
    x, y, c = _place()
    me = 2 * x + y
    chips = _other_chips(x, y)
    n = len(src_refs)

    def rows(t, ref, h):
        if not _split_rows(src_refs[t].shape):
            return ref
        half = src_refs[t].shape[0] // 2
        return ref.at[pl.ds(h * half, half), :]

    def sends():
        over_ici = [_rcopy(rows(t, src_refs[t], c), rows(t, out_refs[t].at[me], c), send_sems.at[t, j], recv_sems.at[t, j],
                           (*chip, c)) for t in range(n) for j, chip in enumerate(chips)]
        own_slot = [_rcopy(src_refs[t], out_refs[t].at[me], send_sems.at[t, 3], recv_sems.at[t, 3], (x, y, 1 - c))
                    for t in range(n)]
        return over_ici + own_slot

    @pl.when(first)
    def _():
        for cp in sends():
            cp.start()

    @pl.when(last)
    def _():
        for t in range(n):
            for j, (cx, cy) in enumerate(chips):
                got = rows(t, out_refs[t].at[2 * cx + cy], c)
                _rcopy(got, got, send_sems.at[t, j], recv_sems.at[t, j], (cx, cy, c)).wait_recv()
            _rcopy(src_refs[t], out_refs[t].at[me], send_sems.at[t, 3], recv_sems.at[t, 3], (x, y, 1 - c)).wait_recv()
        for cp in sends():
            cp.wait_send()


def _gather_host_args(shards):
    n = len(shards)
    if not n:
        return [], [], [], []
    any_spec = pl.BlockSpec(memory_space=pl.ANY)
    shapes = [jax.ShapeDtypeStruct((N_CHIPS,) + s.shape, s.dtype) for s in shards]
    return [any_spec] * n, [any_spec] * n, shapes, [pltpu.SemaphoreType.DMA((n, 4)), pltpu.SemaphoreType.DMA((n, 4))]


def _attention_fwd(q, k, v, name, shards=()):
    h, s, dq = q.shape
    dv = v.shape[-1]
    t = min(ATT_TILE, s)
    scale = dq ** -0.5
    ng = len(shards)

    def body(*refs):
        q_ref, k_ref, v_ref = refs[:3]
        o_ref, lse_ref = refs[3 + ng:5 + ng]
        k_scr, v_scr = refs[5 + 2 * ng:7 + 2 * ng]
        i = pl.program_id(1)
        if ng:
            hh = pl.program_id(0)
            _hosted_gather(refs[3:3 + ng], refs[5 + ng:5 + 2 * ng], refs[-2], refs[-1],
                           jnp.logical_and(hh == 0, i == 0), jnp.logical_and(hh == h - 1, i == s // t - 1))

        @pl.when(i == 0)
        def _():
            k_scr[...] = k_ref[...].astype(BF16)
            v_scr[...] = v_ref[...].astype(BF16)

        qb = (q_ref[...].astype(F32) * (scale * LOG2E)).astype(BF16)

        def block(j, carry, masked):
            m, l, acc = carry
            off = pl.multiple_of(j * t, t)
            kj = k_scr[pl.ds(off, t), :]
            vj = v_scr[pl.ds(off, t), :]
            sc = lax.dot_general(qb, kj, _NT, preferred_element_type=F32)
            if masked:
                sc = jnp.where(_chunk_mask(i * t, j * t, sc.shape), sc, -jnp.inf)
            m_new = jnp.maximum(m, jnp.max(sc, axis=1, keepdims=True))
            p = jnp.exp2(sc - m_new)
            alpha = jnp.exp2(m - m_new)
            l = alpha * l + jnp.sum(p, axis=1, keepdims=True)
            acc = alpha * acc + jnp.dot(p.astype(BF16), vj, preferred_element_type=F32)
            return m_new, l, acc

        init = (jnp.full((t, 1), -jnp.inf, F32), jnp.zeros((t, 1), F32), jnp.zeros((t, dv), F32))
        carry = lax.fori_loop(0, i, lambda j, c: block(j, c, False), init)
        m, l, acc = block(i, carry, True)
        o_ref[...] = (acc / l).astype(o_ref.dtype)
        lse_ref[...] = m + jnp.log2(l)

    g_in, g_out, g_shapes, g_scratch = _gather_host_args(shards)
    return pl.pallas_call(
        body,
        name=name,
        grid=(h, s // t),
        in_specs=[
            pl.BlockSpec((None, t, dq), lambda hh, i: (hh, i, 0)),
            pl.BlockSpec((None, s, dq), lambda hh, i: (hh, 0, 0)),
            pl.BlockSpec((None, s, dv), lambda hh, i: (hh, 0, 0)),
        ] + g_in,
        out_specs=[
            pl.BlockSpec((None, t, dv), lambda hh, i: (hh, i, 0)),
            pl.BlockSpec((None, t, 1), lambda hh, i: (hh, i, 0)),
        ] + g_out,
        out_shape=[jax.ShapeDtypeStruct((h, s, dv), q.dtype), jax.ShapeDtypeStruct((h, s, 1), F32)] + g_shapes,
        scratch_shapes=[pltpu.VMEM((s, dq), BF16), pltpu.VMEM((s, dv), BF16)] + g_scratch,
        compiler_params=_params(("arbitrary", "arbitrary"), has_side_effects=bool(ng)),
    )(q, k, v, *shards)


def _hosted_exchange(p_refs, b_refs, send_sems, recv_sems, first, last):
    x, y, c = _place()
    me = 2 * x + y
    chips = _other_chips(x, y)
    n = len(p_refs)

    def sends():
        return [_rcopy(p_refs[t].at[2 * cx + cy], b_refs[t].at[me], send_sems.at[t, j], recv_sems.at[t, j], (cx, cy, c))
                for t in range(n) for j, (cx, cy) in enumerate(chips)]

    @pl.when(first)
    def _():
        for cp in sends():
            cp.start()

    @pl.when(last)
    def _():
        for t in range(n):
            for j, (cx, cy) in enumerate(chips):
                got = b_refs[t].at[2 * cx + cy]
                _rcopy(got, got, send_sems.at[t, j], recv_sems.at[t, j], (cx, cy, c)).wait_recv()
        for cp in sends():
            cp.wait_send()


def _host_args(hosted):
    n = len(hosted)
    if not n:
        return [], [], [], []
    any_spec = pl.BlockSpec(memory_space=pl.ANY)
    shapes = [jax.ShapeDtypeStruct(p.shape, p.dtype) for p in hosted]
    return [any_spec] * n, [any_spec] * n, shapes, [pltpu.SemaphoreType.DMA((n, 3)), pltpu.SemaphoreType.DMA((n, 3))]


def _attention_bwd_dq(q, k, v, o, lse, do, name, hosted=()):
    h, s, dq = q.shape
    dv = v.shape[-1]
    t = min(ATT_TILE, s)
    scale = dq ** -0.5
    nh = len(hosted)

    def body(*refs):
        q_ref, k_ref, v_ref, o_ref, lse_ref, do_ref = refs[:6]
        p_refs = refs[6:6 + nh]
        dq_ref, delta_ref = refs[6 + nh:8 + nh]
        b_refs = refs[8 + nh:8 + 2 * nh]
        k_scr, v_scr = refs[8 + 2 * nh:10 + 2 * nh]
        hh, i = pl.program_id(0), pl.program_id(1)
        if nh:
            _hosted_exchange(p_refs, b_refs, refs[-2], refs[-1], jnp.logical_and(hh == 0, i == 0),
                             jnp.logical_and(hh == h - 1, i == s // t - 1))

        @pl.when(i == 0)
        def _():
            k_scr[...] = k_ref[...].astype(BF16)
            v_scr[...] = v_ref[...].astype(BF16)

        qb = (q_ref[...].astype(F32) * (scale * LOG2E)).astype(BF16)
        dof = do_ref[...].astype(F32)
        dob = do_ref[...].astype(BF16)
        lse_v = lse_ref[...]
        delta = jnp.sum(dof * o_ref[...].astype(F32), axis=1, keepdims=True)
        delta_ref[...] = delta

        def block(j, acc, masked):
            off = pl.multiple_of(j * t, t)
            kj = k_scr[pl.ds(off, t), :]
            vj = v_scr[pl.ds(off, t), :]
            sc = lax.dot_general(qb, kj, _NT, preferred_element_type=F32)
            p = jnp.exp2(sc - lse_v)
            if masked:
                p = jnp.where(_chunk_mask(i * t, j * t, sc.shape), p, 0.0)
            dp = lax.dot_general(dob, vj, _NT, preferred_element_type=F32)
            ds = p * (dp - delta)
            return acc + jnp.dot(ds.astype(BF16), kj, preferred_element_type=F32)

        acc = lax.fori_loop(0, i, lambda j, c: block(j, c, False), jnp.zeros((t, dq), F32))
        dq_ref[...] = (block(i, acc, True) * scale).astype(dq_ref.dtype)

    tile = lambda d: pl.BlockSpec((None, t, d), lambda hh, i: (hh, i, 0))
    whole = lambda d: pl.BlockSpec((None, s, d), lambda hh, i: (hh, 0, 0))
    h_in, h_out, h_shapes, h_scratch = _host_args(hosted)
    return pl.pallas_call(
        body,
        name=name,
        grid=(h, s // t),
        in_specs=[tile(dq), whole(dq), whole(dv), tile(dv), tile(1), tile(dv)] + h_in,
        out_specs=[tile(dq), tile(1)] + h_out,
        out_shape=[jax.ShapeDtypeStruct((h, s, dq), q.dtype), jax.ShapeDtypeStruct((h, s, 1), F32)] + h_shapes,
        scratch_shapes=[pltpu.VMEM((s, dq), BF16), pltpu.VMEM((s, dv), BF16)] + h_scratch,
        compiler_params=_params(("arbitrary", "arbitrary"), has_side_effects=bool(nh)),
    )(q, k, v, o, lse, do, *hosted)


def _attention_bwd_dkv(q, k, v, lse, delta, do, name, hosted=()):
    h, s, dq = q.shape
    dv = v.shape[-1]
    t = min(ATT_TILE, s)
    n = s // t
    scale = dq ** -0.5
    nh = len(hosted)

    def body(*refs):
        q_ref, k_ref, v_ref, lse_ref, delta_ref, do_ref = refs[:6]
        p_refs = refs[6:6 + nh]
        dk_ref, dv_ref = refs[6 + nh:8 + nh]
        b_refs = refs[8 + nh:8 + 2 * nh]
        q_scr, do_scr = refs[8 + 2 * nh:10 + 2 * nh]
        hh, j = pl.program_id(0), pl.program_id(1)
        if nh:
            _hosted_exchange(p_refs, b_refs, refs[-2], refs[-1], jnp.logical_and(hh == 0, j == 0),
                             jnp.logical_and(hh == h - 1, j == n - 1))

        @pl.when(j == 0)
        def _():
            q_scr[...] = (q_ref[...].astype(F32) * (scale * LOG2E)).astype(BF16)
            do_scr[...] = do_ref[...].astype(BF16)

        kb = k_ref[...].astype(BF16)
        vb = v_ref[...].astype(BF16)

        def block(i, carry, masked):
            dk, dvv = carry
            off = pl.multiple_of(i * t, t)
            qi = q_scr[pl.ds(off, t), :]
            doi = do_scr[pl.ds(off, t), :]
            sc = lax.dot_general(qi, kb, _NT, preferred_element_type=F32)
            p = jnp.exp2(sc - lse_ref[pl.ds(off, t), :])
            if masked:
                p = jnp.where(_chunk_mask(i * t, j * t, sc.shape), p, 0.0)
            dp = lax.dot_general(doi, vb, _NT, preferred_element_type=F32)
            ds = p * (dp - delta_ref[pl.ds(off, t), :])
            dvv = dvv + lax.dot_general(p.astype(BF16), doi, _TN, preferred_element_type=F32)
            dk = dk + lax.dot_general(ds.astype(BF16), qi, _TN, preferred_element_type=F32)
            return dk, dvv

        carry = block(j, (jnp.zeros((t, dq), F32), jnp.zeros((t, dv), F32)), True)
        dk, dvv = lax.fori_loop(j + 1, n, lambda i, c: block(i, c, False), carry)
        dk_ref[...] = (dk * (1.0 / LOG2E)).astype(dk_ref.dtype)
        dv_ref[...] = dvv.astype(dv_ref.dtype)

    tile = lambda d: pl.BlockSpec((None, t, d), lambda hh, j: (hh, j, 0))
    whole = lambda d: pl.BlockSpec((None, s, d), lambda hh, j: (hh, 0, 0))
    h_in, h_out, h_shapes, h_scratch = _host_args(hosted)
    return pl.pallas_call(
        body,
        name=name,
        grid=(h, n),
        in_specs=[whole(dq), tile(dq), tile(dv), whole(1), whole(1), whole(dv)] + h_in,
        out_specs=[tile(dq), tile(dv)] + h_out,
        out_shape=[jax.ShapeDtypeStruct((h, s, dq), k.dtype), jax.ShapeDtypeStruct((h, s, dv), v.dtype)] + h_shapes,
        scratch_shapes=[pltpu.VMEM((s, dq), BF16), pltpu.VMEM((s, dv), BF16)] + h_scratch,
        compiler_params=_params(("arbitrary", "arbitrary"), has_side_effects=bool(nh)),
    )(q, k, v, lse, delta, do, *hosted)


def attention(q, k, v, tok=(), shards=(), *, name):
    tok, shards = tuple(tok), tuple(shards)
    fwd_name = name + "_fwd" + ("_host" if shards else "")

    @jax.custom_vjp
    def op(q, k, v, tok, shards):
        o, _, *partial = _attention_fwd(q, k, v, fwd_name, shards)
        return o, tok, tuple(partial)

    def fwd(q, k, v, tok, shards):
        o, lse, *partial = _attention_fwd(q, k, v, fwd_name, shards)
        return (o, tok, tuple(partial)), (q, k, v, o, lse, shards)

    def bwd(res, cts):
        q, k, v, o, lse, shards = res
        do, payload, _ = cts
        first = tuple(payload[:HOSTED_IN_DQ])
        rest = tuple(payload[HOSTED_IN_DQ:])
        dq, delta, *got_a = _attention_bwd_dq(q, k, v, o, lse, do, name + "_bwd_dq" + ("_host" if first else ""), first)
        dk, dv, *got_b = _attention_bwd_dkv(q, k, v, lse, delta, do, name + "_bwd_dkv" + ("_host" if rest else ""), rest)
        return dq, dk, dv, tuple(got_a) + tuple(got_b), tuple(jnp.zeros_like(s) for s in shards)

    op.defvjp(fwd, bwd)
    return op(q, k, v, tok, shards)


CONV_HALO = 16


def _conv_tiles(s, c):
    return min(512, s), _pick(c, (512, 256, 128))


def _conv_fwd(x, w, name):
    s, c = x.shape
    ts, tc = _conv_tiles(s, c)
    nb = ts // CONV_HALO

    def body(xc_ref, xp_ref, w_ref, o_ref):
        t = pl.program_id(1)
        prev = jnp.where(t > 0, xp_ref[...].astype(F32), 0.0)
        xe = jnp.concatenate([prev, xc_ref[...].astype(F32)], axis=0)
        wv = w_ref[...]
        acc = jnp.zeros((ts, tc), F32)
        for tap in range(CONV_WIDTH):
            k = CONV_WIDTH - 1 - tap
            sh = xe if k == 0 else pltpu.roll(xe, k, axis=0)
            acc = acc + sh[CONV_HALO:, :] * wv[tap:tap + 1, :]
        o_ref[...] = acc

    return pl.pallas_call(
        body,
        name=name,
        grid=(c // tc, s // ts),
        in_specs=[
            pl.BlockSpec((ts, tc), lambda ci, t: (t, ci)),
            pl.BlockSpec((CONV_HALO, tc), lambda ci, t: (jnp.maximum(t * nb - 1, 0), ci)),
            pl.BlockSpec((CONV_WIDTH, tc), lambda ci, t: (0, ci)),
        ],
        out_specs=pl.BlockSpec((ts, tc), lambda ci, t: (t, ci)),
        out_shape=jax.ShapeDtypeStruct((s, c), F32),
        compiler_params=_params(("parallel", "parallel")),
    )(x, x, w)


def _conv_bwd(x, w, dy, name):
    s, c = x.shape
    ts, tc = _conv_tiles(s, c)
    nb = ts // CONV_HALO
    nt = s // ts

    def body(xc_ref, xp_ref, w_ref, dc_ref, dn_ref, dx_ref, dw_ref):
        t = pl.program_id(1)
        prev = jnp.where(t > 0, xp_ref[...].astype(F32), 0.0)
        xe = jnp.concatenate([prev, xc_ref[...].astype(F32)], axis=0)
        dcur = dc_ref[...]
        nxt = jnp.where(t < nt - 1, dn_ref[...], 0.0)
        de = jnp.concatenate([dcur, nxt], axis=0)
        wv = w_ref[...]
        dx = jnp.zeros((ts, tc), F32)
        dw = jnp.zeros((CONV_WIDTH, tc), F32)
        tap_row = lax.broadcasted_iota(jnp.int32, (CONV_WIDTH, tc), 0)
        for tap in range(CONV_WIDTH):
            k = CONV_WIDTH - 1 - tap
            dsh = de if k == 0 else pltpu.roll(de, ts + CONV_HALO - k, axis=0)
            dx = dx + dsh[:ts, :] * wv[tap:tap + 1, :]
            xsh = xe if k == 0 else pltpu.roll(xe, k, axis=0)
            dwt = jnp.sum(xsh[CONV_HALO:, :] * dcur, axis=0, keepdims=True)
            dw = jnp.where(tap_row == tap, dwt, dw)
        dx_ref[...] = dx.astype(dx_ref.dtype)

        @pl.when(t == 0)
        def _():
            dw_ref[...] = dw

        @pl.when(t > 0)
        def _():
            dw_ref[...] += dw

    return pl.pallas_call(
        body,
        name=name,
        grid=(c // tc, nt),
        in_specs=[
            pl.BlockSpec((ts, tc), lambda ci, t: (t, ci)),
            pl.BlockSpec((CONV_HALO, tc), lambda ci, t: (jnp.maximum(t * nb - 1, 0), ci)),
            pl.BlockSpec((CONV_WIDTH, tc), lambda ci, t: (0, ci)),
            pl.BlockSpec((ts, tc), lambda ci, t: (t, ci)),
            pl.BlockSpec((CONV_HALO, tc), lambda ci, t: (jnp.minimum((t + 1) * nb, s // CONV_HALO - 1), ci)),
        ],
        out_specs=[
            pl.BlockSpec((ts, tc), lambda ci, t: (t, ci)),
            pl.BlockSpec((CONV_WIDTH, tc), lambda ci, t: (0, ci)),
        ],
        out_shape=[jax.ShapeDtypeStruct((s, c), x.dtype), jax.ShapeDtypeStruct((CONV_WIDTH, c), F32)],
        compiler_params=_params(("parallel", "arbitrary")),
    )(x, x, w, dy, dy)


def causal_conv(x, w, *, name):
    @jax.custom_vjp
    def op(x, w):
        return _conv_fwd(x, w, name + "_fwd")

    def fwd(x, w):
        return op(x, w), (x, w)

    def bwd(res, dy):
        x, w = res
        dx, dw = _conv_bwd(x, w, dy, name + "_bwd")
        return dx, dw

    op.defvjp(fwd, bwd)
    return op(x, w)


SSD_T = 128
SSD_R = 8
SSD_GW = SSD_R * SSM_HEADDIM


def _ssd_consts(t):
    r = lax.broadcasted_iota(jnp.int32, (t, t), 0)
    c = lax.broadcasted_iota(jnp.int32, (t, t), 1)
    tril = (c <= r).astype(F32)
    triu = (r <= c).astype(F32)
    head_of_lane = lax.broadcasted_iota(jnp.int32, (SSD_R, SSD_GW), 1) // SSM_HEADDIM
    expand = (head_of_lane == lax.broadcasted_iota(jnp.int32, (SSD_R, SSD_GW), 0)).astype(F32)
    return c <= r, tril, triu, expand


def _three_bf16(v):
    p1 = v.astype(BF16)
    r1 = v - p1.astype(F32)
    p2 = r1.astype(BF16)
    return p1, p2, (r1 - p2.astype(F32)).astype(BF16)


def _sel_first(sel, v):
    s = sel.astype(BF16)
    return sum(jnp.dot(s, p, preferred_element_type=F32) for p in _three_bf16(v))


def _sel_second(v, sel, dn=None):
    s = sel.astype(BF16)
    if dn is None:
        return sum(jnp.dot(p, s, preferred_element_type=F32) for p in _three_bf16(v))
    return sum(lax.dot_general(p, s, dn, preferred_element_type=F32) for p in _three_bf16(v))


def _bdot(a, b, dn=None):
    if dn is None:
        return jnp.dot(a.astype(BF16), b.astype(BF16), preferred_element_type=F32)
    return lax.dot_general(a.astype(BF16), b.astype(BF16), dn, preferred_element_type=F32)


def _ssd_chunk_common(x_ref, b_ref, c_ref, dtc_ref, dtr_ref, alc_ref, alr_ref, t):
    mask, tril, triu, expand = _ssd_consts(t)
    x, bm, cm = x_ref[...], b_ref[...], c_ref[...]
    dtc, dtr = dtc_ref[...], dtr_ref[...]
    neg_a_c = -jnp.exp(alc_ref[...])
    neg_a_r = -jnp.exp(alr_ref[...])
    acum_c = _sel_first(tril, dtc * neg_a_c)
    acum_r = _sel_second(dtr * neg_a_r, triu)
    s_cb = _bdot(cm, bm, _NT)
    return mask, tril, triu, expand, x, bm, cm, dtc, dtr, neg_a_c, neg_a_r, acum_c, acum_r, s_cb


def _head_decay(mask, acum_c, acum_r, h):
    seg = acum_c[:, h:h + 1] - acum_r[h:h + 1, :]
    return jnp.exp(jnp.where(mask, seg, -jnp.inf))


def _ssd_fwd(xbc, dtc, dtr, alc, alr, dexp, name, shards=()):
    s = xbc.shape[0]
    g = SSM_GROUPS
    t = min(SSD_T, s)
    nc = s // t
    n = SSM_STATE
    xblocks = (g * SSD_GW) // n
    ng = len(shards)

    def body(*refs):
        x_ref, b_ref, c_ref, dtc_ref, dtr_ref, alc_ref, alr_ref, d_ref = refs[:8]
        y_ref, hs_ref = refs[8 + ng:10 + ng]
        h_scr = refs[10 + 2 * ng]
        ci = pl.program_id(1)
        if ng:
            gi = pl.program_id(0)
            _hosted_gather(refs[8:8 + ng], refs[10 + ng:10 + 2 * ng], refs[-2], refs[-1],
                           jnp.logical_and(gi == 0, ci == 0), jnp.logical_and(gi == g - 1, ci == nc - 1))

        @pl.when(ci == 0)
        def _():
            h_scr[...] = jnp.zeros_like(h_scr)

        (mask, tril, triu, expand, x, bm, cm, dtc_v, dtr_v, _, _, acum_c, acum_r, s_cb) = _ssd_chunk_common(
            x_ref, b_ref, c_ref, dtc_ref, dtr_ref, alc_ref, alr_ref, t)
        hst = h_scr[...]
        hs_ref[...] = hst
        ch = _bdot(cm, hst)
        y = _sel_second(jnp.exp(acum_c), expand) * ch + d_ref[...] * x
        half = lax.broadcasted_iota(jnp.int32, (t, 2 * SSM_HEADDIM), 1) // SSM_HEADDIM
        parts = []
        for j in range(SSD_R // 2):
            xp = x[:, j * 128:(j + 1) * 128]
            acc = jnp.zeros((t, 128), F32)
            for hh in range(2):
                h = 2 * j + hh
                m = s_cb * _head_decay(mask, acum_c, acum_r, h) * dtr_v[h:h + 1, :]
                acc = acc + _bdot(m, jnp.where(half == hh, xp, 0.0))
            parts.append(acc)
        y_ref[...] = y + jnp.concatenate(parts, axis=1)
        last = acum_c[t - 1:t, :]
        w_c = jnp.exp(last - acum_c) * dtc_v
        dec = _sel_second(jnp.broadcast_to(jnp.exp(last), (SSD_R, SSD_R)), expand)[0:1, :]
        h_scr[...] = dec * hst + _bdot(bm, _sel_second(w_c, expand) * x, _TN)

    g_in, g_out, g_shapes, g_scratch = _gather_host_args(shards)
    return pl.pallas_call(
        body,
        name=name,
        grid=(g, nc),
        in_specs=[
            pl.BlockSpec((t, SSD_GW), lambda gi, ci: (ci, gi)),
            pl.BlockSpec((t, n), lambda gi, ci: (ci, xblocks + gi)),
            pl.BlockSpec((t, n), lambda gi, ci: (ci, xblocks + g + gi)),
            pl.BlockSpec((None, t, SSD_R), lambda gi, ci: (gi, ci, 0)),
            pl.BlockSpec((None, SSD_R, t), lambda gi, ci: (gi, 0, ci)),
            pl.BlockSpec((None, 1, SSD_R), lambda gi, ci: (gi, 0, 0)),
            pl.BlockSpec((None, SSD_R, 1), lambda gi, ci: (gi, 0, 0)),
            pl.BlockSpec((None, 1, SSD_GW), lambda gi, ci: (gi, 0, 0)),
        ] + g_in,
        out_specs=[
            pl.BlockSpec((t, SSD_GW), lambda gi, ci: (ci, gi)),
            pl.BlockSpec((None, None, n, SSD_GW), lambda gi, ci: (gi, ci, 0, 0)),
        ] + g_out,
        out_shape=[jax.ShapeDtypeStruct((s, g * SSD_GW), F32), jax.ShapeDtypeStruct((g, nc, n, SSD_GW), F32)] + g_shapes,
        scratch_shapes=[pltpu.VMEM((n, SSD_GW), F32)] + g_scratch,
        compiler_params=_params(("arbitrary" if ng else "parallel", "arbitrary"), has_side_effects=bool(ng)),
    )(xbc, xbc, xbc, dtc, dtr, alc, alr, dexp, *shards)


def _hosted_pair_swap(g_refs, a_refs, send_sems, recv_sems, first, last):
    x, y, c = _place()
    n = len(g_refs)

    def copies():
        out = []
        for t in range(n):
            half = g_refs[t].shape[1] // 2
            out.append(_rcopy(g_refs[t].at[:, pl.ds((1 - c) * half, half), :], a_refs[t], send_sems.at[t, 0],
                              recv_sems.at[t, 0], (x, y, 1 - c)))
        return out

    @pl.when(first)
    def _():
        for cp in copies():
            cp.start()

    @pl.when(last)
    def _():
        for cp in copies():
            cp.wait()


def _ssd_bwd(xbc, dtc, dtr, alc, alr, dexp, hs, dy, name, hosted=()):
    s = xbc.shape[0]
    g = SSM_GROUPS
    t = min(SSD_T, s)
    nc = s // t
    n = SSM_STATE
    xblocks = (g * SSD_GW) // n
    nh = len(hosted)

    def body(*refs):
        x_ref, b_ref, c_ref, dtc_ref, dtr_ref, alc_ref, alr_ref, d_ref, hs_ref, dy_ref = refs[:10]
        dx_ref, db_ref, dc_ref, ddtc_ref, ddtr_ref, dalc_ref, dalr_ref, dd_ref = refs[10 + nh:18 + nh]
        dh_scr = refs[18 + 2 * nh]
        ci = pl.program_id(1)
        if nh:
            gi = pl.program_id(0)
            _hosted_pair_swap(refs[10:10 + nh], refs[18 + nh:18 + 2 * nh], refs[-2], refs[-1],
                              jnp.logical_and(gi == 0, ci == 0), jnp.logical_and(gi == g - 1, ci == nc - 1))

        @pl.when(ci == 0)
        def _():
            dh_scr[...] = jnp.zeros_like(dh_scr)

        (mask, tril, triu, expand, x, bm, cm, dtc_v, dtr_v, neg_a_c, neg_a_r, acum_c, acum_r, s_cb) = _ssd_chunk_common(
            x_ref, b_ref, c_ref, dtc_ref, dtr_ref, alc_ref, alr_ref, t)
        hst = hs_ref[...]
        dhn = dh_scr[...]
        dy = dy_ref[...]
        ch = _bdot(cm, hst)
        scale_full = _sel_second(jnp.exp(acum_c), expand)
        sdy = scale_full * dy
        d_c = _bdot(sdy, hst, _NT)
        dh_prev = _bdot(cm, sdy, _TN)
        dacum_c = _sel_second(sdy * ch, expand, _NT)
        dx = d_ref[...] * dy
        dd = jnp.sum(dy * x, axis=0, keepdims=True)
        last = acum_c[t - 1:t, :]
        e_last = jnp.exp(last)
        dec = _sel_second(jnp.broadcast_to(e_last, (SSD_R, SSD_R)), expand)[0:1, :]
        dh_prev = dh_prev + dec * dhn
        ddec = jnp.sum(hst * dhn, axis=0, keepdims=True)
        dlast = _sel_second(jnp.broadcast_to(ddec, (SSD_R, SSD_GW)), expand, _NT)[0:1, :] * e_last
        w_e = jnp.exp(last - acum_c)
        w_c = w_e * dtc_v
        wfull = _sel_second(w_c, expand)
        z = _bdot(bm, dhn)
        dx = dx + wfull * z
        dw_c = _sel_second(x * z, expand, _NT)
        ddt_c = dw_c * w_e
        q_c = dw_c * w_c
        dacum_c = dacum_c - q_c
        dlast = dlast + jnp.sum(q_c, axis=0, keepdims=True)
        d_b = _bdot(wfull * x, dhn, _NT)
        half = lax.broadcasted_iota(jnp.int32, (t, 2 * SSM_HEADDIM), 1) // SSM_HEADDIM
        lane8 = lax.broadcasted_iota(jnp.int32, (t, SSD_R), 1)
        row8 = lax.broadcasted_iota(jnp.int32, (SSD_R, t), 0)
        ds_cb = jnp.zeros((t, t), F32)
        dacum_r = jnp.zeros((SSD_R, t), F32)
        ddt_r = jnp.zeros((SSD_R, t), F32)
        parts = []
        for j in range(SSD_R // 2):
            xp = x[:, j * 128:(j + 1) * 128]
            dyp = dy[:, j * 128:(j + 1) * 128]
            dxp = jnp.zeros((t, 128), F32)
            for hh in range(2):
                h = 2 * j + hh
                dts = dtr_v[h:h + 1, :]
                decay = _head_decay(mask, acum_c, acum_r, h)
                sl = s_cb * decay
                m = sl * dts
                xm = jnp.where(half == hh, xp, 0.0)
                dym = jnp.where(half == hh, dyp, 0.0)
                dxp = dxp + _bdot(m, dym, _TN)
                dm = _bdot(dym, xm, _NT)
                ds_cb = ds_cb + dm * decay * dts
                q = dm * m
                dacum_c = dacum_c + jnp.where(lane8 == h, jnp.sum(q, axis=1, keepdims=True), 0.0)
                dacum_r = dacum_r - jnp.where(row8 == h, jnp.sum(q, axis=0, keepdims=True), 0.0)
                ddt_r = ddt_r + jnp.where(row8 == h, jnp.sum(dm * sl, axis=0, keepdims=True), 0.0)
            parts.append(dxp)
        dx_ref[...] = dx + jnp.concatenate(parts, axis=1)
        dc_ref[...] = d_c + _bdot(ds_cb, bm)
        db_ref[...] = d_b + _bdot(ds_cb, cm, _TN)
        row_t = lax.broadcasted_iota(jnp.int32, (t, SSD_R), 0)
        dacum_c = dacum_c + jnp.where(row_t == t - 1, dlast, 0.0)
        da_c = _sel_first(triu, dacum_c)
        da_r = _sel_second(dacum_r, tril)
        ddtc_ref[...] = ddt_c + da_c * neg_a_c
        ddtr_ref[...] = ddt_r + da_r * neg_a_r
        dal_c = jnp.sum(da_c * dtc_v, axis=0, keepdims=True) * neg_a_c
        dal_r = jnp.sum(da_r * dtr_v, axis=1, keepdims=True) * neg_a_r
        dh_scr[...] = dh_prev

        @pl.when(ci == 0)
        def _():
            dalc_ref[...] = dal_c
            dalr_ref[...] = dal_r
            dd_ref[...] = dd

        @pl.when(ci > 0)
        def _():
            dalc_ref[...] += dal_c
            dalr_ref[...] += dal_r
            dd_ref[...] += dd

    rev = lambda ci: nc - 1 - ci
    any_spec = pl.BlockSpec(memory_space=pl.ANY)
    a_shapes = [jax.ShapeDtypeStruct((gr.shape[0], gr.shape[1] // 2, gr.shape[2]), gr.dtype) for gr in hosted]
    h_scratch = [pltpu.SemaphoreType.DMA((nh, 1)), pltpu.SemaphoreType.DMA((nh, 1))] if nh else []
    return pl.pallas_call(
        body,
        name=name,
        grid=(g, nc),
        in_specs=[
            pl.BlockSpec((t, SSD_GW), lambda gi, ci: (rev(ci), gi)),
            pl.BlockSpec((t, n), lambda gi, ci: (rev(ci), xblocks + gi)),
            pl.BlockSpec((t, n), lambda gi, ci: (rev(ci), xblocks + g + gi)),
            pl.BlockSpec((None, t, SSD_R), lambda gi, ci: (gi, rev(ci), 0)),
            pl.BlockSpec((None, SSD_R, t), lambda gi, ci: (gi, 0, rev(ci))),
            pl.BlockSpec((None, 1, SSD_R), lambda gi, ci: (gi, 0, 0)),
            pl.BlockSpec((None, SSD_R, 1), lambda gi, ci: (gi, 0, 0)),
            pl.BlockSpec((None, 1, SSD_GW), lambda gi, ci: (gi, 0, 0)),
            pl.BlockSpec((None, None, n, SSD_GW), lambda gi, ci: (gi, rev(ci), 0, 0)),
            pl.BlockSpec((t, SSD_GW), lambda gi, ci: (rev(ci), gi)),
        ] + [any_spec] * nh,
        out_specs=[
            pl.BlockSpec((t, SSD_GW), lambda gi, ci: (rev(ci), gi)),
            pl.BlockSpec((t, n), lambda gi, ci: (rev(ci), gi)),
            pl.BlockSpec((t, n), lambda gi, ci: (rev(ci), gi)),
            pl.BlockSpec((None, t, SSD_R), lambda gi, ci: (gi, rev(ci), 0)),
            pl.BlockSpec((None, SSD_R, t), lambda gi, ci: (gi, 0, rev(ci))),
            pl.BlockSpec((None, 1, SSD_R), lambda gi, ci: (gi, 0, 0)),
            pl.BlockSpec((None, SSD_R, 1), lambda gi, ci: (gi, 0, 0)),
            pl.BlockSpec((None, 1, SSD_GW), lambda gi, ci: (gi, 0, 0)),
        ] + [any_spec] * nh,
        out_shape=[
            jax.ShapeDtypeStruct((s, g * SSD_GW), F32),
            jax.ShapeDtypeStruct((s, g * n), F32),
            jax.ShapeDtypeStruct((s, g * n), F32),
            jax.ShapeDtypeStruct((g, s, SSD_R), F32),
            jax.ShapeDtypeStruct((g, SSD_R, s), F32),
            jax.ShapeDtypeStruct((g, 1, SSD_R), F32),
            jax.ShapeDtypeStruct((g, SSD_R, 1), F32),
            jax.ShapeDtypeStruct((g, 1, SSD_GW), F32),
        ] + a_shapes,
        scratch_shapes=[pltpu.VMEM((n, SSD_GW), F32)] + h_scratch,
        compiler_params=_params(("arbitrary" if nh else "parallel", "arbitrary"), has_side_effects=bool(nh)),
    )(xbc, xbc, xbc, dtc, dtr, alc, alr, dexp, hs, dy, *hosted)


def ssd_core(xbc, dtc, dtr, alc, alr, dexp, shards=(), tok_p=(), tok_a=(), *, name):
    shards, tok_p, tok_a = tuple(shards), tuple(tok_p), tuple(tok_a)
    fwd_name = name + "_fwd" + ("_host" if shards else "")

    def tok_g(tok_a):
        return tuple(jnp.zeros((a.shape[0], 2 * a.shape[1], a.shape[2]), a.dtype) for a in tok_a)

    @jax.custom_vjp
    def op(xbc, dtc, dtr, alc, alr, dexp, shards, tok_p, tok_a):
        y, _, *partial = _ssd_fwd(xbc, dtc, dtr, alc, alr, dexp, fwd_name, shards)
        return y, tuple(partial), tok_g(tok_a)

    def fwd(xbc, dtc, dtr, alc, alr, dexp, shards, tok_p, tok_a):
        y, hs, *partial = _ssd_fwd(xbc, dtc, dtr, alc, alr, dexp, fwd_name, shards)
        return (y, tuple(partial), tok_g(tok_a)), (xbc, dtc, dtr, alc, alr, dexp, hs, shards)

    def bwd(res, cts):
        xbc, dtc, dtr, alc, alr, dexp, hs, shards = res
        dy, _, gs = cts
        gs = tuple(gs)
        dx, db, dc, ddtc, ddtr, dalc, dalr, dd, *a = _ssd_bwd(xbc, dtc, dtr, alc, alr, dexp, hs, dy,
                                                              name + "_bwd" + ("_host" if gs else ""), gs)
        p = tuple(pair_add(g, ai, name="reduce_grads_pair_add") for g, ai in zip(gs, a))
        return (jnp.concatenate([dx, db, dc], axis=1), ddtc, ddtr, dalc, dalr, dd,
                tuple(jnp.zeros_like(s) for s in shards), p, tuple(a))

    op.defvjp(fwd, bwd)
    return op(xbc, dtc, dtr, alc, alr, dexp, shards, tok_p, tok_a)


def gate_norm(y, z, w):
    z = z.astype(F32)
    return (rms_norm(y * (z * jax.nn.sigmoid(z)), w).astype(BF16),)


def ssd_branch(xbc, z, dt_raw, conv_w, conv_b, dt_bias, a_log, d_skip, norm_w, shards=(), tok_p=(), tok_a=(), *, name):
    s = xbc.shape[0]
    g = SSM_GROUPS
    conv = causal_conv(xbc, conv_w, name=name + "_conv")
    (xc,) = rowwise(lambda c, b: ((c + b) * jax.nn.sigmoid(c + b),), (conv,), (), (conv_b[None, :],),
                    name=name + "_silu", tile=min(256, s))
    (dt,) = rowwise(lambda r, b: (jax.nn.softplus(r.astype(F32) + b),), (dt_raw,), (), (dt_bias[None, :],),
                    name=name + "_dt", tile=min(512, s))
    dt3 = dt.reshape(s, g, SSD_R)
    y, partial, tok_g = ssd_core(xc, dt3.transpose(1, 0, 2), dt3.transpose(1, 2, 0), a_log.reshape(g, 1, SSD_R),
                                 a_log.reshape(g, SSD_R, 1), jnp.repeat(d_skip, SSM_HEADDIM).reshape(g, 1, SSD_GW),
                                 shards, tok_p, tok_a, name=name + "_core")
    (out,) = rowwise(gate_norm, (y.reshape(s, g, SSD_GW), z.reshape(s, g, SSD_GW)), (),
                     (norm_w.reshape(g, SSD_GW),), name=name + "_gate", tile=min(128, s))
    return out.reshape(s, g * SSD_GW), partial, tok_g


def rms_norm(x, w):
    return x * lax.rsqrt(jnp.mean(x * x, axis=-1, keepdims=True) + EPS) * w


def rope_matrix():
    half = QK_ROPE // 2
    j = jnp.arange(QK_DIM)
    src = jnp.where(j < QK_NOPE + half, j + half, j - half)
    sign = jnp.where(j < QK_NOPE, 0.0, jnp.where(j < QK_NOPE + half, -1.0, 1.0))
    return (jnp.arange(QK_DIM)[:, None] == src[None, :]).astype(F32) * sign[None, :]


def rope_tables_full(positions):
    inv_freq = 1.0 / (ROPE_THETA ** (jnp.arange(0, QK_ROPE, 2, dtype=F32) / QK_ROPE))
    ang = positions.astype(F32)[:, None] * inv_freq
    s = positions.shape[0]
    cos = jnp.concatenate([jnp.ones((s, QK_NOPE), F32), jnp.cos(ang), jnp.cos(ang)], axis=-1)
    sin = jnp.concatenate([jnp.zeros((s, QK_NOPE), F32), jnp.sin(ang), jnp.sin(ang)], axis=-1)
    return cos[:, None, :], sin[:, None, :]


def head_norm_rope(x, cos_full, sin_full, rot, w):
    t, h, d = x.shape
    y = rms_norm(x.astype(F32), w)
    partner = jnp.dot(y.reshape(t * h, d), rot, precision=HI, preferred_element_type=F32).reshape(t, h, d)
    return ((y * cos_full + partner * sin_full).astype(BF16),)


def _norm(x, w, *, name, out_dtype=BF16, tile=256):
    (y,) = rowwise(lambda x, w: (rms_norm(x.astype(F32), w).astype(out_dtype),), (x,), (), (w[None, :],), name=name,
                   tile=min(tile, x.shape[0]))
    return y


Q_LORA = 512
KV_LORA = 512
W_IN_PIECES = ("cq", "ckv", "kr", "z", "xbc", "dt", "ga", "gb")


def w_in_widths(d_model):
    d_inner = 2 * d_model
    conv_dim = d_inner + 2 * SSM_GROUPS * SSM_STATE
    return (Q_LORA, KV_LORA, QK_ROPE, d_inner, conv_dim, d_inner // SSM_HEADDIM, d_model, d_model)


LINEAR_NAMES = ("cq", "ckv", "kd", "z", "xbc", "ga", "gb", "w_uq", "w_ukv", "w_o_mla", "w_o_ssm", "w_out", "w_up",
                "w_down", "w_ple_gate", "w_ple")


GATHER_IN_ATTENTION = (0, 6)


def layer_forward(x, p_i, cos_full, sin_full, rot, wb, sm, sinks, tok=(), tok_a=(), next_shards=()):
    s, d = x.shape
    in_att = [t for t in range(len(next_shards)) if t in GATHER_IN_ATTENTION]
    in_ssd = [t for t in range(len(next_shards)) if t not in GATHER_IN_ATTENTION]
    lin = lambda a, n: linear(a, wb[n], sinks[n], name="lin_" + n)
    up32 = lambda a: a.astype(F32)
    h = _norm(x, sm["norm_mix_w"], name="norm_mix", out_dtype=F32)
    c_q, c_kv, kd = lin(h, "cq"), lin(h, "ckv"), lin(h, "kd")
    z, xbc, g_a, g_b = lin(h, "z"), lin(h, "xbc"), lin(h, "ga"), lin(h, "gb")
    k_r, dt_raw = kd[:, :QK_ROPE], kd[:, QK_ROPE:]
    q = lin(_norm(c_q, sm["q_a_norm_w"], name="norm_qa"), "w_uq").reshape(s, MLA_HEADS, QK_DIM)
    kv = lin(_norm(c_kv, sm["kv_a_norm_w"], name="norm_kva"), "w_ukv").reshape(s, MLA_HEADS, QK_NOPE + V_DIM)
    k = jnp.concatenate([kv[..., :QK_NOPE], jnp.broadcast_to(k_r[:, None, :], (s, MLA_HEADS, QK_ROPE))], axis=-1)
    v = kv[..., QK_NOPE:]
    tq = min(128, s)
    (q,) = rowwise(head_norm_rope, (q,), (cos_full, sin_full), (sm["q_norm_w"][None, :],), tables=(rot,), name="q_rope", tile=tq)
    (k,) = rowwise(head_norm_rope, (k,), (cos_full, sin_full), (sm["k_norm_w"][None, :],), tables=(rot,), name="k_rope", tile=tq)
    hm = lambda a: a.transpose(1, 0, 2)
    o, tok, part_att = attention(hm(q), hm(k), hm(v), tok, [next_shards[t] for t in in_att], name="attn")
    o = hm(o).reshape(s, MLA_HEADS * V_DIM)
    y_a = lin(o, "w_o_mla")
    y_ssd, part_ssd, tok_g = ssd_branch(xbc, z, dt_raw, sm["conv_w"], sm["conv_b"], sm["dt_bias"], sm["a_log"],
                                        sm["d_skip"], sm["ssm_norm_w"], [next_shards[t] for t in in_ssd], tok, tok_a,
                                        name="ssd")
    y_b = lin(y_ssd, "w_o_ssm")
    partial = [None] * len(next_shards)
    for t, buf in list(zip(in_att, part_att)) + list(zip(in_ssd, part_ssd)):
        partial[t] = buf
    sig = jax.nn.sigmoid
    tr = min(256, s)
    (merged,) = rowwise(lambda ga, gb, ya, yb: ((sig(up32(ga)) * up32(ya) + sig(up32(gb)) * up32(yb)).astype(BF16),),
                        (g_a, g_b, y_a, y_b), (), (), name="merge", tile=tr)
    mixed = lin(merged, "w_out")
    (x,) = rowwise(lambda x, m: (x + up32(m),), (x, mixed), (), (), name="add_mix", tile=tr)
    up = lin(_norm(x, sm["norm_mlp_w"], name="norm_mlp"), "w_up")
    (act,) = rowwise(lambda u: (jnp.square(jnp.maximum(up32(u), 0.0)).astype(BF16),), (up,), (), (), name="relu2", tile=tr)
    down = lin(act, "w_down")
    (x,) = rowwise(lambda x, m: (x + up32(m),), (x, down), (), (), name="add_mlp", tile=tr)
    pg = lin(_norm(x, sm["ple_norm_w"], name="norm_ple"), "w_ple_gate")
    pe = lin(p_i, "w_ple")
    (x,) = rowwise(lambda x, pe, pg: (x + up32(pe) * sig(up32(pg)),), (x, pe, pg), (), (), name="ple_add", tile=tr)
    return (x, tok_g), partial


def loss_and_cotangent(y, target):
    s, d = y.shape

    def f(y, t):
        e = y - t
        return e * (1.0 / d), 0.5 * jnp.sum(jnp.sum(e * e, axis=1, keepdims=True) * (1.0 / d), axis=0, keepdims=True)

    dy, part = _tiled_call(f, (y, target), (), min(256, s), "loss", 1)
    return dy, part[0, 0]


ADAM_BLOCK_ELEMS = 256 * 1024


def adamw(w, g, m, v, *, name):
    rows, cols = w.shape
    budget = max(8, ADAM_BLOCK_ELEMS // cols)
    tile = _pick(rows, tuple(t for t in (512, 256, 128, 64, 32, 16, 8) if t <= budget))

    def f(w, g, m, v):
        m = ADAM_B1 * m + (1.0 - ADAM_B1) * g
        v = ADAM_B2 * v + (1.0 - ADAM_B2) * jnp.square(g)
        m_hat = m / (1.0 - ADAM_B1 ** ADAM_STEP)
        v_hat = v / (1.0 - ADAM_B2 ** ADAM_STEP)
        delta = -ADAM_LR * (m_hat / (jnp.sqrt(v_hat) + ADAM_EPS) + ADAM_WD * w)
        return delta, m, v

    return _tiled_call(f, (w, g, m, v), (), tile, name, 0)


MESH_ID = pl.DeviceIdType.MESH
N_CHIPS = 4
_ANY = pl.BlockSpec(memory_space=pl.ANY)


def _place():
    return lax.axis_index("x"), lax.axis_index("y"), lax.axis_index("c")


def _other_chips(x, y):
    return [(1 - x, y), (x, 1 - y), (1 - x, 1 - y)]


def _rcopy(src, dst, send_sem, recv_sem, device):
    return pltpu.make_async_remote_copy(src_ref=src, dst_ref=dst, send_sem=send_sem, recv_sem=recv_sem,
                                        device_id=device, device_id_type=MESH_ID)


def _sems(n, k):
    return pltpu.SemaphoreType.DMA((n, k))


def _comm_call(body, name, ins, out_shapes, scratch, aliases=None):
    return pl.pallas_call(
        body,
        name=name,
        in_specs=[_ANY] * len(ins),
        out_specs=[_ANY] * len(out_shapes),
        out_shape=out_shapes,
        scratch_shapes=scratch,
        input_output_aliases=aliases or {},
        compiler_params=pltpu.CompilerParams(has_side_effects=True),
    )(*ins)


SPLIT_ROWS = 32


def gather_shards(shards, *, name):
    n = len(shards)
    split = [s.shape[0] % SPLIT_ROWS == 0 for s in shards]

    def body(*refs):
        srcs, outs = refs[:n], refs[n:2 * n]
        send_sems, recv_sems = refs[2 * n:]
        x, y, c = _place()
        sibling = (x, y, 1 - c)
        chips = _other_chips(x, y)
        me = 2 * x + y

        def part(t, slot, h):
            if not split[t]:
                return outs[t].at[slot]
            half = shards[t].shape[0] // 2
            return outs[t].at[slot, pl.ds(h * half, half), :]

        def own(t):
            if not split[t]:
                return srcs[t]
            half = shards[t].shape[0] // 2
            return srcs[t].at[pl.ds(c * half, half), :]

        sent = []
        for t in range(n):
            for j, chip in enumerate(chips):
                sent.append(_rcopy(own(t), part(t, me, c), send_sems.at[t, j], recv_sems.at[t, j], (*chip, c)))
                sent[-1].start()
        for t in range(n):
            sent.append(_rcopy(srcs[t], outs[t].at[me], send_sems.at[t, 6], recv_sems.at[t, 6], sibling))
            sent[-1].start()
        for t in range(n):
            for j, (cx, cy) in enumerate(chips):
                got = part(t, 2 * cx + cy, c)
                _rcopy(got, got, send_sems.at[t, j], recv_sems.at[t, j], (cx, cy, c)).wait_recv()
                if split[t]:
                    sent.append(_rcopy(got, got, send_sems.at[t, 3 + j], recv_sems.at[t, 3 + j], sibling))
                    sent[-1].start()
        for t in range(n):
            if split[t]:
                for j, (cx, cy) in enumerate(chips):
                    got = part(t, 2 * cx + cy, 1 - c)
                    _rcopy(got, got, send_sems.at[t, 3 + j], recv_sems.at[t, 3 + j], sibling).wait_recv()
        for t in range(n):
            _rcopy(srcs[t], outs[t].at[me], send_sems.at[t, 6], recv_sems.at[t, 6], sibling).wait_recv()
        for cp in sent:
            cp.wait_send()

    out_shapes = [jax.ShapeDtypeStruct((N_CHIPS,) + s.shape, s.dtype) for s in shards]
    return _comm_call(body, name, shards, out_shapes, [_sems(n, 7), _sems(n, 7)])


def gather_finish(partial, *, name):
    partial = list(partial)
    todo = [t for t, p in enumerate(partial) if _split_rows(p.shape[1:])]
    n = len(todo)

    def body(*refs):
        outs = refs[n:2 * n]
        send_sems, recv_sems = refs[2 * n:]
        x, y, c = _place()
        sibling = (x, y, 1 - c)
        chips = _other_chips(x, y)

        def part(k, slot, h):
            half = partial[todo[k]].shape[1] // 2
            return outs[k].at[slot, pl.ds(h * half, half), :]

        sent = []
        for k in range(n):
            for j, (cx, cy) in enumerate(chips):
                got = part(k, 2 * cx + cy, c)
                sent.append(_rcopy(got, got, send_sems.at[k, j], recv_sems.at[k, j], sibling))
                sent[-1].start()
        for k in range(n):
            for j, (cx, cy) in enumerate(chips):
                got = part(k, 2 * cx + cy, 1 - c)
                _rcopy(got, got, send_sems.at[k, j], recv_sems.at[k, j], sibling).wait_recv()
        for cp in sent:
            cp.wait_send()

    ins = [partial[t] for t in todo]
    done = _comm_call(body, name, ins, [jax.ShapeDtypeStruct(p.shape, p.dtype) for p in ins], [_sems(n, 3), _sems(n, 3)],
                      aliases={k: k for k in range(n)})
    for k, t in enumerate(todo):
        partial[t] = done[k]
    return partial


def sibling_take_half(gs, *, name):
    n = len(gs)

    def body(*refs):
        g_refs, a_refs = refs[:n], refs[n:2 * n]
        send_sems, recv_sems = refs[2 * n:]
        x, y, c = _place()
        copies = []
        for t in range(n):
            half = gs[t].shape[1] // 2
            copies.append(_rcopy(g_refs[t].at[:, pl.ds((1 - c) * half, half), :], a_refs[t], send_sems.at[t, 0],
                                 recv_sems.at[t, 0], (x, y, 1 - c)))
            copies[-1].start()
        for cp in copies:
            cp.wait()

    out_shapes = [jax.ShapeDtypeStruct((g.shape[0], g.shape[1] // 2, g.shape[2]), g.dtype) for g in gs]
    return _comm_call(body, name, gs, out_shapes, [_sems(n, 1), _sems(n, 1)])


ELEMWISE_BLOCK_ELEMS = 256 * 1024


def _row_tile(rows, cols):
    budget = max(16, ELEMWISE_BLOCK_ELEMS // cols)
    return _pick(rows, tuple(t for t in (1024, 512, 256, 128, 64, 32, 16) if t <= budget))


def _core_and_chip():
    x, y, c = _place()
    return jnp.stack([c, 2 * x + y]).astype(jnp.int32)


def pair_add(g, a, *, name):
    n, rows, cols = g.shape
    half = rows // 2
    tile = _row_tile(half, cols)
    nb = half // tile

    def body(who_ref, g_ref, a_ref, o_ref):
        o_ref[...] = (g_ref[...] + a_ref[...]).astype(o_ref.dtype)

    return pl.pallas_call(
        body,
        name=name,
        grid_spec=pltpu.PrefetchScalarGridSpec(
            num_scalar_prefetch=1,
            grid=(n, nb),
            in_specs=[
                pl.BlockSpec((None, tile, cols), lambda j, i, who: (j, who[0] * nb + i, 0)),
                pl.BlockSpec((None, tile, cols), lambda j, i, who: (j, i, 0)),
            ],
            out_specs=pl.BlockSpec((None, tile, cols), lambda j, i, who: (j, i, 0)),
        ),
        out_shape=jax.ShapeDtypeStruct((n, half, cols), BF16),
        compiler_params=_params(("parallel", "parallel")),
    )(_core_and_chip(), g, a)


def exchange_chip_slots(ps, *, name):
    n = len(ps)

    def body(*refs):
        p_refs, b_refs = refs[:n], refs[n:2 * n]
        send_sems, recv_sems = refs[2 * n:]
        x, y, c = _place()
        me = 2 * x + y
        chips = _other_chips(x, y)
        sends = []
        for t in range(n):
            for j, (cx, cy) in enumerate(chips):
                sends.append(_rcopy(p_refs[t].at[2 * cx + cy], b_refs[t].at[me], send_sems.at[t, j], recv_sems.at[t, j],
                                    (cx, cy, c)))
                sends[-1].start()
        for t in range(n):
            for j, (cx, cy) in enumerate(chips):
                got = b_refs[t].at[2 * cx + cy]
                _rcopy(got, got, send_sems.at[t, j], recv_sems.at[t, j], (cx, cy, c)).wait_recv()
        for cp in sends:
            cp.wait_send()

    out_shapes = [jax.ShapeDtypeStruct(p.shape, p.dtype) for p in ps]
    return _comm_call(body, name, ps, out_shapes, [_sems(n, 3), _sems(n, 3)])


def chips_add(g, a, b, *, name):
    n, rows, cols = g.shape
    half = rows // 2
    tile = _row_tile(half, cols)
    nb = half // tile

    def body(who_ref, g_ref, a_ref, *rest):
        o_ref = rest[-1]
        acc = g_ref[...] + a_ref[...]
        for b_ref in rest[:-1]:
            acc = acc + b_ref[...].astype(F32)
        o_ref[...] = acc

    other = lambda k: pl.BlockSpec((None, tile, cols), lambda i, who, k=k: ((who[1] + k) % n, i, 0))
    return pl.pallas_call(
        body,
        name=name,
        grid_spec=pltpu.PrefetchScalarGridSpec(
            num_scalar_prefetch=1,
            grid=(nb,),
            in_specs=[
                pl.BlockSpec((None, tile, cols), lambda i, who: (who[1], who[0] * nb + i, 0)),
                pl.BlockSpec((None, tile, cols), lambda i, who: (who[1], i, 0)),
            ] + [other(k) for k in range(1, n)],
            out_specs=pl.BlockSpec((tile, cols), lambda i, who: (who[0] * nb + i, 0)),
        ),
        out_shape=jax.ShapeDtypeStruct((rows, cols), F32),
        compiler_params=_params(("parallel",)),
    )(_core_and_chip(), g, a, *([b] * (n - 1)))


def sibling_join_halves(rs, *, name):
    n = len(rs)

    def body(*refs):
        r_refs = refs[n:2 * n]
        send_sems, recv_sems = refs[2 * n:]
        x, y, c = _place()
        sibling = (x, y, 1 - c)
        copies = []
        for t in range(n):
            half = rs[t].shape[0] // 2
            mine = r_refs[t].at[pl.ds(c * half, half), :]
            copies.append(_rcopy(mine, mine, send_sems.at[t, 0], recv_sems.at[t, 0], sibling))
            copies[-1].start()
        for t in range(n):
            half = rs[t].shape[0] // 2
            got = r_refs[t].at[pl.ds((1 - c) * half, half), :]
            _rcopy(got, got, send_sems.at[t, 0], recv_sems.at[t, 0], sibling).wait_recv()
        for cp in copies:
            cp.wait_send()

    out_shapes = [jax.ShapeDtypeStruct(r.shape, r.dtype) for r in rs]
    return _comm_call(body, name, rs, out_shapes, [_sems(n, 1), _sems(n, 1)], aliases={t: t for t in range(n)})


def reduce_to_owner(gs, *, name):
    gs, a, p = reduce_pairs(gs, name=name)
    return reduce_finish(gs, a, exchange_chip_slots(p, name=name + "_chips"), name=name)


def reduce_pairs(gs, *, name):
    gs = list(gs)
    a = sibling_take_half(gs, name=name + "_pair")
    p = [pair_add(g, ai, name=name + "_pair_add") for g, ai in zip(gs, a)]
    return gs, a, p


def reduce_finish(gs, a, b, *, name):
    f = [chips_add(g, ai, bi, name=name + "_chips_add") for g, ai, bi in zip(gs, a, b)]
    return sibling_join_halves(f, name=name + "_join")


def allreduce_small(v, *, name):
    rows, cols = v.shape

    def body(v_ref, o_ref, buf, send_sems, recv_sems):
        x, y, c = _place()
        me = 4 * x + 2 * y + c
        buf[me] = v_ref[...]
        copies = []
        for k in range(1, 8):
            bx, by, bc = (k >> 2) & 1, (k >> 1) & 1, k & 1
            peer = (x if bx == 0 else 1 - x, y if by == 0 else 1 - y, c if bc == 0 else 1 - c)
            copies.append(_rcopy(v_ref, buf.at[me], send_sems.at[k - 1], recv_sems.at[k - 1], peer))
        for cp in copies:
            cp.start()
        for k in range(1, 8):
            bx, by, bc = (k >> 2) & 1, (k >> 1) & 1, k & 1
            px, py, pc = (x if bx == 0 else 1 - x, y if by == 0 else 1 - y, c if bc == 0 else 1 - c)
            _rcopy(v_ref, buf.at[4 * px + 2 * py + pc], send_sems.at[k - 1], recv_sems.at[k - 1], (px, py, pc)).wait_recv()
        for cp in copies:
            cp.wait_send()
        acc = buf[0]
        for j in range(1, 8):
            acc = acc + buf[j]
        o_ref[...] = acc

    return pl.pallas_call(
        body,
        name=name,
        in_specs=[pl.BlockSpec(memory_space=pltpu.VMEM)],
        out_specs=pl.BlockSpec(memory_space=pltpu.VMEM),
        out_shape=jax.ShapeDtypeStruct((rows, cols), v.dtype),
        scratch_shapes=[pltpu.VMEM((8, rows, cols), v.dtype), pltpu.SemaphoreType.DMA((7,)), pltpu.SemaphoreType.DMA((7,))],
        compiler_params=pltpu.CompilerParams(has_side_effects=True, vmem_limit_bytes=V7X_VMEM_LIMIT),
    )(v)


BIG = (("w_in", 1), ("w_uq", 1), ("w_ukv", 1), ("w_o_mla", 0), ("w_o_ssm", 0), ("w_out", 0), ("w_up", 1),
       ("w_down", 0), ("w_ple_gate", 0), ("w_ple", 1))
SHARDED = BIG + (("conv_w", 1),)
SMALL = ("norm_mix_w", "q_a_norm_w", "kv_a_norm_w", "q_norm_w", "k_norm_w", "conv_b", "dt_bias", "a_log", "d_skip",
         "ssm_norm_w", "norm_mlp_w", "ple_norm_w")
WEIGHTS = ("norm_mix_w", "w_in", "q_a_norm_w", "w_uq", "kv_a_norm_w", "w_ukv", "q_norm_w", "k_norm_w", "w_o_mla", "conv_w",
           "conv_b", "dt_bias", "a_log", "d_skip", "ssm_norm_w", "w_o_ssm", "w_out", "norm_mlp_w", "w_up", "w_down",
           "ple_norm_w", "w_ple_gate", "w_ple")


def _to_rows(flat, cols, row_multiple):
    n = flat.shape[-1]
    rows = -(-n // (cols * row_multiple)) * row_multiple
    pad = [(0, 0)] * (flat.ndim - 1) + [(0, rows * cols - n)]
    return jnp.pad(flat, pad).reshape(flat.shape[:-1] + (rows, cols))


def _w_in_ranges(d_model):
    out, lo = {}, 0
    for n, wd in zip(W_IN_PIECES, w_in_widths(d_model)):
        out[n] = (lo, lo + wd)
        lo += wd
    return out


def w_in_pieces(w3):
    _, k, c = w3.shape
    pc = {}
    for n, (lo, hi) in _w_in_ranges(k).items():
        cuts = [w3[j][:, max(lo, j * c) - j * c:min(hi, (j + 1) * c) - j * c]
                for j in range(N_CHIPS) if max(lo, j * c) < min(hi, (j + 1) * c)]
        pc[n] = cuts[0] if len(cuts) == 1 else jnp.concatenate(cuts, axis=1)
    pc["kd"] = jnp.concatenate([pc.pop("kr"), pc.pop("dt")], axis=1)
    return pc


def w_in_shard_grads(g, k, c):
    g = dict(g)
    g["kr"], g["dt"] = g["kd"][:, :QK_ROPE], g["kd"][:, QK_ROPE:]
    shards = []
    for j in range(N_CHIPS):
        cuts = []
        for n, (lo, hi) in _w_in_ranges(k).items():
            a, b = max(lo, j * c), min(hi, (j + 1) * c)
            if a < b:
                cuts.append(g[n][:, a - lo:b - lo])
        shards.append(jnp.concatenate(cuts, axis=1))
    return jnp.stack(shards)


def kernel(x, p, positions, norm_mix_w, w_in, q_a_norm_w, w_uq, kv_a_norm_w, w_ukv, q_norm_w, k_norm_w, w_o_mla, conv_w, conv_b, dt_bias, a_log, d_skip, ssm_norm_w, w_o_ssm, w_out, norm_mlp_w, w_up, w_down, ple_norm_w, w_ple_gate, w_ple, loss_target, m_norm_mix_w, m_w_in, m_q_a_norm_w, m_w_uq, m_kv_a_norm_w, m_w_ukv, m_q_norm_w, m_k_norm_w, m_w_o_mla, m_conv_w, m_conv_b, m_dt_bias, m_a_log, m_d_skip, m_ssm_norm_w, m_w_o_ssm, m_w_out, m_norm_mlp_w, m_w_up, m_w_down, m_ple_norm_w, m_w_ple_gate, m_w_ple, v_norm_mix_w, v_w_in, v_q_a_norm_w, v_w_uq, v_kv_a_norm_w, v_w_ukv, v_q_norm_w, v_k_norm_w, v_w_o_mla, v_conv_w, v_conv_b, v_dt_bias, v_a_log, v_d_skip, v_ssm_norm_w, v_w_o_ssm, v_w_out, v_norm_mlp_w, v_w_up, v_w_down, v_ple_norm_w, v_w_ple_gate, v_w_ple):
    a = dict(locals())
    x, p, pos, target = a["x"][0], a["p"][:, 0], a["positions"][0], a["loss_target"][0]
    depth = a["w_in"].shape[0]
    shard_shapes = {n: tuple(a[n].shape[1:]) for n, _ in SHARDED}
    cos_full, sin_full = rope_tables_full(pos)
    rot = rope_matrix()

    shards = [[a[n][i].astype(BF16) for n, _ in BIG] + [a["conv_w"][i]] for i in range(depth)]
    tok = tuple(jnp.zeros((N_CHIPS, shard_shapes[n][0] // 2, shard_shapes[n][1]), BF16) for n, _ in BIG)
    tok_a = tuple(jnp.zeros(t.shape, F32) for t in tok)
    layer_vjps = []
    got = gather_shards(shards[0], name="gather_weights")
    for i in range(depth):
        full = dict(zip([n for n, _ in SHARDED], got))
        w_i = w_in_pieces(full["w_in"])
        for n, ax in BIG[1:]:
            w_i[n] = full[n] if ax == 1 else full[n].reshape((-1, full[n].shape[-1]))
        sm_i = {n: a[n][i] for n in SMALL}
        sm_i["conv_w"] = full["conv_w"].transpose(1, 0, 2).reshape(CONV_WIDTH, -1)
        sinks_i = {n: jnp.zeros(w_i[n].shape, F32) for n in LINEAR_NAMES}
        last = i == depth - 1
        f_i = functools.partial(layer_forward, p_i=p[i], cos_full=cos_full, sin_full=sin_full, rot=rot, wb=w_i,
                                next_shards=() if last else shards[i + 1])
        g_i = lambda x, sm, sk, tk, ta, f=f_i: f(x, sm=sm, sinks=sk, tok=tk, tok_a=ta)
        (x, _), vjp_i, partial = jax.vjp(g_i, x, sm_i, sinks_i, () if last else tok, () if last else tok_a, has_aux=True)
        layer_vjps.append(vjp_i)
        if not last:
            got = gather_finish(partial, name="gather_finish")
    dx, loss_part = loss_and_cotangent(x, target)
    loss = lax.psum(loss_part, ("x", "y", "c"))

    per_layer, d_small, pending = [None] * depth, [None] * depth, None
    for i in reversed(range(depth)):
        dx, d_small[i], d_sinks_i, got_b, got_a = layer_vjps[i]((dx, tuple(pending) if pending is not None else ()))
        if pending is not None:
            per_layer[i + 1] = reduce_finish(pending, list(got_a), list(got_b), name="reduce_grads")
        pending = [w_in_shard_grads(d_sinks_i, *shard_shapes["w_in"])]
        pending += [d_sinks_i[n].reshape((N_CHIPS,) + shard_shapes[n]) for n, _ in BIG[1:]]
    per_layer[0] = reduce_to_owner(pending, name="reduce_grads")
    grads = {n: jnp.stack([per_layer[i][t] for i in range(depth)]) for t, (n, _) in enumerate(BIG)}

    small_names = SMALL + ("conv_w",)
    flat = jnp.concatenate([d_small[i][n].reshape(-1) for i in range(depth) for n in small_names])
    n_small = flat.shape[0]
    red = allreduce_small(_to_rows(flat, 128, 8), name="reduce_small").reshape(-1)[:n_small]
    per = n_small // depth
    off = 0
    for n in SMALL:
        width = a[n].shape[-1]
        grads[n] = jnp.stack([red[i * per + off:i * per + off + width] for i in range(depth)])
        off += width
    conv_c = shard_shapes["conv_w"][1]
    conv_full = jnp.stack([red[i * per + off:i * per + off + CONV_WIDTH * N_CHIPS * conv_c] for i in range(depth)])
    chip = 2 * lax.axis_index("x") + lax.axis_index("y")
    grads["conv_w"] = lax.dynamic_index_in_dim(conv_full.reshape(depth, CONV_WIDTH, N_CHIPS, conv_c), chip, axis=2,
                                               keepdims=False)

    deltas, new_m, new_v = {}, {}, {}
    two_d = lambda t: t.reshape(-1, t.shape[-1])
    for n in WEIGHTS:
        d, m, v = adamw(two_d(a[n]), two_d(grads[n]), two_d(a["m_" + n]), two_d(a["v_" + n]), name="adamw")
        deltas[n], new_m[n], new_v[n] = d.reshape(a[n].shape), m.reshape(a[n].shape), v.reshape(a[n].shape)

    return (loss, dx[None], *[grads[n].reshape(a[n].shape) for n in WEIGHTS], *[deltas[n] for n in WEIGHTS],
            *[new_m[n] for n in WEIGHTS], *[new_v[n] for n in WEIGHTS])
```

```python
import functools

import jax
import jax.numpy as jnp
from jax import lax
from jax.experimental import pallas as pl
from jax.experimental.pallas import tpu as pltpu

F32 = jnp.float32
BF16 = jnp.bfloat16
HI = lax.Precision.HIGHEST

EPS = 1e-6
MLA_HEADS = 16
QK_NOPE = 128
QK_ROPE = 64
QK_DIM = QK_NOPE + QK_ROPE
V_DIM = 128
ROPE_THETA = 10000.0
ATT_CHUNK = 64
SSM_GROUPS = 8
SSM_HEADDIM = 64
SSM_STATE = 128
CONV_WIDTH = 4
ADAM_LR = 0.001
ADAM_B1 = 0.9
ADAM_B2 = 0.999
ADAM_EPS = 1e-08
ADAM_WD = 0.01
ADAM_STEP = 10

V7X_VMEM_LIMIT = 56 * 1024 * 1024


def _params(sem=None, **kw):
    return pltpu.CompilerParams(dimension_semantics=sem, vmem_limit_bytes=V7X_VMEM_LIMIT, **kw)


def _pick(n, prefs):
    for t in prefs:
        if n % t == 0:
            return t
    return n


MATMUL_OPERAND_BYTES = 24 * 1024 * 1024


def matmul(a, b, *, ta=False, tb=False, out_blocks=0, out_dtype=F32, name):
    m, k = (a.shape[1], a.shape[0]) if ta else a.shape
    blocked = b.ndim == 3
    if blocked:
        nb, rows, c = b.shape
        k2, n = (nb * c, rows) if tb else (rows, nb * c)
    else:
        k2, n = (b.shape[1], b.shape[0]) if tb else b.shape
    assert k == k2, (a.shape, b.shape, ta, tb)
    n_unit = n // out_blocks if out_blocks else (c if blocked and not tb else n)
    k_unit = c if blocked and tb else k
    tm = _pick(m, ((2048,) if k <= 512 else ()) + (1024, 512, 256, 128))
    tn = _pick(n_unit, (1024, 768, 512, 384, 256, 128))
    in_bytes = tm * a.dtype.itemsize + tn * b.dtype.itemsize
    tk = _pick(k_unit, tuple(t for t in (2048, 1536, 1024, 768, 512, 384, 256, 128)
                             if 2 * t * in_bytes <= MATMUL_OPERAND_BYTES))
    nk = k // tk
    dn = (((0 if ta else 1,), (1 if tb else 0,)), ((), ()))

    def body(a_ref, b_ref, o_ref, *acc):
        part = lambda: lax.dot_general(a_ref[...].astype(BF16), b_ref[...].astype(BF16), dn, preferred_element_type=F32)
        if nk == 1:
            o_ref[...] = part().astype(o_ref.dtype)
            return
        (acc_ref,) = acc
        kk = pl.program_id(2)

        @pl.when(kk == 0)
        def _():
            acc_ref[...] = jnp.zeros_like(acc_ref)

        acc_ref[...] += part()

        @pl.when(kk == nk - 1)
        def _():
            o_ref[...] = acc_ref[...].astype(o_ref.dtype)

    a_spec = pl.BlockSpec((tk, tm), lambda i, j, kk: (kk, i)) if ta else pl.BlockSpec((tm, tk), lambda i, j, kk: (i, kk))
    if not blocked:
        b_spec = pl.BlockSpec((tn, tk), lambda i, j, kk: (j, kk)) if tb else pl.BlockSpec((tk, tn), lambda i, j, kk: (kk, j))
    elif tb:
        kb = c // tk
        b_spec = pl.BlockSpec((None, tn, tk), lambda i, j, kk: (kk // kb, j, kk % kb))
    else:
        cb = c // tn
        b_spec = pl.BlockSpec((None, tk, tn), lambda i, j, kk: (j // cb, kk, j % cb))
    if out_blocks:
        ob = n_unit // tn
        out_spec = pl.BlockSpec((None, tm, tn), lambda i, j, kk: (j // ob, i, j % ob))
        out_shape = jax.ShapeDtypeStruct((out_blocks, m, n_unit), out_dtype)
    else:
        out_spec = pl.BlockSpec((tm, tn), lambda i, j, kk: (i, j))
        out_shape = jax.ShapeDtypeStruct((m, n), out_dtype)
    return pl.pallas_call(
        body,
        name=name,
        grid=(m // tm, n // tn, nk),
        in_specs=[a_spec, b_spec],
        out_specs=out_spec,
        out_shape=out_shape,
        scratch_shapes=[pltpu.VMEM((tm, tn), F32)] if nk > 1 else [],
        compiler_params=_params(("parallel", "parallel", "arbitrary")),
    )(a, b)


def linear(a, w, sink, *, name, out_dtype=BF16):
    @jax.custom_vjp
    def op(a, w, sink):
        return matmul(a, w, out_dtype=out_dtype, name=name + "_fwd")

    def fwd(a, w, sink):
        return op(a, w, sink), (a, w)

    def bwd(res, ct):
        a, w = res
        da = matmul(ct, w, tb=True, out_dtype=a.dtype, name=name + "_bwd_da")
        dw = matmul(a, ct, ta=True, out_blocks=w.shape[0] if w.ndim == 3 else 0, name=name + "_bwd_dw")
        return da, jnp.zeros_like(w), dw

    op.defvjp(fwd, bwd)
    return op(a, w, sink)


def _tiled_call(fn, tiled, whole, tile, name, n_acc):
    rows = tiled[0].shape[0]
    assert rows % tile == 0
    t_avals = [jax.ShapeDtypeStruct((tile,) + a.shape[1:], a.dtype) for a in tiled]
    w_avals = [jax.ShapeDtypeStruct(a.shape, a.dtype) for a in whole]
    outs = jax.eval_shape(fn, *t_avals, *w_avals)
    n_in = len(tiled) + len(whole)
    n_t = len(outs) - n_acc

    def body(*refs):
        res = fn(*[r[...] for r in refs[:n_in]])
        o_refs = refs[n_in:]
        for r, v in zip(o_refs[:n_t], res[:n_t]):
            r[...] = v.astype(r.dtype)
        if n_acc:
            first = pl.program_id(0) == 0

            @pl.when(first)
            def _():
                for r, v in zip(o_refs[n_t:], res[n_t:]):
                    r[...] = v.astype(F32)

            @pl.when(jnp.logical_not(first))
            def _():
                for r, v in zip(o_refs[n_t:], res[n_t:]):
                    r[...] += v.astype(F32)

    def tspec(a):
        nd = len(a.shape)
        return pl.BlockSpec((tile,) + tuple(a.shape[1:]), lambda i, nd=nd: (i,) + (0,) * (nd - 1))

    def wspec(a):
        nd = len(a.shape)
        return pl.BlockSpec(tuple(a.shape), lambda i, nd=nd: (0,) * nd)

    out_shape = [jax.ShapeDtypeStruct((rows,) + o.shape[1:], o.dtype) for o in outs[:n_t]]
    out_shape += [jax.ShapeDtypeStruct(o.shape, F32) for o in outs[n_t:]]
    out_specs = [tspec(o) for o in out_shape[:n_t]] + [wspec(o) for o in out_shape[n_t:]]
    return pl.pallas_call(
        body,
        name=name,
        grid=(rows // tile,),
        in_specs=[tspec(a) for a in tiled] + [wspec(a) for a in whole],
        out_specs=out_specs,
        out_shape=out_shape,
        compiler_params=_params(("arbitrary",) if n_acc else ("parallel",)),
    )(*tiled, *whole)


def rowwise(f, rows, consts, params, *, name, tile, tables=()):
    rows, consts, tables, params = tuple(rows), tuple(consts), tuple(tables), tuple(params)
    nr, nc, ntab, npar = len(rows), len(consts), len(tables), len(params)

    @jax.custom_vjp
    def op(rows, consts, tables, params):
        return tuple(_tiled_call(f, rows + consts, tables + params, tile, name + "_fwd", 0))

    def fwd(rows, consts, tables, params):
        return op(rows, consts, tables, params), (rows, consts, tables, params)

    def bwd(res, cts):
        rows, consts, tables, params = res
        ncts = len(cts)

        def g(*args):
            r = args[:nr]
            c = args[nr:nr + nc]
            ct = args[nr + nc:nr + nc + ncts]
            tab = args[nr + nc + ncts:nr + nc + ncts + ntab]
            p = args[nr + nc + ncts + ntab:]
            _, vjp = jax.vjp(lambda *rp: f(*rp[:nr], *c, *tab, *rp[nr:]), *r, *p)
            return tuple(vjp(tuple(ct)))

        outs = _tiled_call(g, rows + consts + tuple(cts), tables + params, tile, name + "_bwd", npar)
        d_rows = tuple(o.astype(r.dtype) for o, r in zip(outs[:nr], rows))
        d_params = tuple(o.astype(p.dtype) for o, p in zip(outs[nr:], params))
        zeros = lambda xs: tuple(jnp.zeros_like(a) for a in xs)
        return d_rows, zeros(consts), zeros(tables), d_params

    op.defvjp(fwd, bwd)
    return op(rows, consts, tables, params)


ATT_TILE = 512
LOG2E = 1.4426950408889634
HOSTED_IN_DQ = 4
_NT = (((1,), (1,)), ((), ()))
_TN = (((0,), (0,)), ((), ()))


def _chunk_mask(row0, col0, shape):
    r = (row0 + lax.broadcasted_iota(jnp.int32, shape, 0)) // ATT_CHUNK
    c = (col0 + lax.broadcasted_iota(jnp.int32, shape, 1)) // ATT_CHUNK
    return c <= r


def _split_rows(shape):
    return shape[0] % SPLIT_ROWS == 0


def _hosted_gather(src_refs, out_refs, send_sems, recv_sems, first, last):
    x, y, c = _place()
    me = 2 * x + y
    chips = _other_chips(x, y)
    n = len(src_refs)

    def rows(t, ref, h):
        if not _split_rows(src_refs[t].shape):
            return ref
        half = src_refs[t].shape[0] // 2
        return ref.at[pl.ds(h * half, half), :]

    def sends():
        over_ici = [_rcopy(rows(t, src_refs[t], c), rows(t, out_refs[t].at[me], c), send_sems.at[t, j], recv_sems.at[t, j],
                           (*chip, c)) for t in range(n) for j, chip in enumerate(chips)]
        own_slot = [_rcopy(src_refs[t], out_refs[t].at[me], send_sems.at[t, 3], recv_sems.at[t, 3], (x, y, 1 - c))
                    for t in range(n)]
        return over_ici + own_slot

    @pl.when(first)
    def _():
        for cp in sends():
            cp.start()

    @pl.when(last)
    def _():
        for t in range(n):
            for j, (cx, cy) in enumerate(chips):
                got = rows(t, out_refs[t].at[2 * cx + cy], c)
                _rcopy(got, got, send_sems.at[t, j], recv_sems.at[t, j], (cx, cy, c)).wait_recv()
            _rcopy(src_refs[t], out_refs[t].at[me], send_sems.at[t, 3], recv_sems.at[t, 3], (x, y, 1 - c)).wait_recv()
        for cp in sends():
            cp.wait_send()


def _gather_host_args(shards):
    n = len(shards)
    if not n:
        return [], [], [], []
    any_spec = pl.BlockSpec(memory_space=pl.ANY)
    shapes = [jax.ShapeDtypeStruct((N_CHIPS,) + s.shape, s.dtype) for s in shards]
    return [any_spec] * n, [any_spec] * n, shapes, [pltpu.SemaphoreType.DMA((n, 4)), pltpu.SemaphoreType.DMA((n, 4))]


def _attention_fwd(q, k, v, name, shards=()):
    h, s, dq = q.shape
    dv = v.shape[-1]
    t = min(ATT_TILE, s)
    scale = dq ** -0.5
    ng = len(shards)

    def body(*refs):
        q_ref, k_ref, v_ref = refs[:3]
        o_ref, lse_ref = refs[3 + ng:5 + ng]
        k_scr, v_scr = refs[5 + 2 * ng:7 + 2 * ng]
        i = pl.program_id(1)
        if ng:
            hh = pl.program_id(0)
            _hosted_gather(refs[3:3 + ng], refs[5 + ng:5 + 2 * ng], refs[-2], refs[-1],
                           jnp.logical_and(hh == 0, i == 0), jnp.logical_and(hh == h - 1, i == s // t - 1))

        @pl.when(i == 0)
        def _():
            k_scr[...] = k_ref[...].astype(BF16)
            v_scr[...] = v_ref[...].astype(BF16)

        qb = (q_ref[...].astype(F32) * (scale * LOG2E)).astype(BF16)

        def block(j, carry, masked):
            m, l, acc = carry
            off = pl.multiple_of(j * t, t)
            kj = k_scr[pl.ds(off, t), :]
            vj = v_scr[pl.ds(off, t), :]
            sc = lax.dot_general(qb, kj, _NT, preferred_element_type=F32)
            if masked:
                sc = jnp.where(_chunk_mask(i * t, j * t, sc.shape), sc, -jnp.inf)
            m_new = jnp.maximum(m, jnp.max(sc, axis=1, keepdims=True))
            p = jnp.exp2(sc - m_new)
            alpha = jnp.exp2(m - m_new)
            l = alpha * l + jnp.sum(p, axis=1, keepdims=True)
            acc = alpha * acc + jnp.dot(p.astype(BF16), vj, preferred_element_type=F32)
            return m_new, l, acc

        init = (jnp.full((t, 1), -jnp.inf, F32), jnp.zeros((t, 1), F32), jnp.zeros((t, dv), F32))
        carry = lax.fori_loop(0, i, lambda j, c: block(j, c, False), init)
        m, l, acc = block(i, carry, True)
        o_ref[...] = (acc / l).astype(o_ref.dtype)
        lse_ref[...] = m + jnp.log2(l)

    g_in, g_out, g_shapes, g_scratch = _gather_host_args(shards)
    return pl.pallas_call(
        body,
        name=name,
        grid=(h, s // t),
        in_specs=[
            pl.BlockSpec((None, t, dq), lambda hh, i: (hh, i, 0)),
            pl.BlockSpec((None, s, dq), lambda hh, i: (hh, 0, 0)),
            pl.BlockSpec((None, s, dv), lambda hh, i: (hh, 0, 0)),
        ] + g_in,
        out_specs=[
            pl.BlockSpec((None, t, dv), lambda hh, i: (hh, i, 0)),
            pl.BlockSpec((None, t, 1), lambda hh, i: (hh, i, 0)),
        ] + g_out,
        out_shape=[jax.ShapeDtypeStruct((h, s, dv), q.dtype), jax.ShapeDtypeStruct((h, s, 1), F32)] + g_shapes,
        scratch_shapes=[pltpu.VMEM((s, dq), BF16), pltpu.VMEM((s, dv), BF16)] + g_scratch,
        compiler_params=_params(("arbitrary", "arbitrary"), has_side_effects=bool(ng)),
    )(q, k, v, *shards)


def _hosted_exchange(p_refs, b_refs, send_sems, recv_sems, first, last):
    x, y, c = _place()
    me = 2 * x + y
    chips = _other_chips(x, y)
    n = len(p_refs)

    def sends():
        return [_rcopy(p_refs[t].at[2 * cx + cy], b_refs[t].at[me], send_sems.at[t, j], recv_sems.at[t, j], (cx, cy, c))
                for t in range(n) for j, (cx, cy) in enumerate(chips)]

    @pl.when(first)
    def _():
        for cp in sends():
            cp.start()

    @pl.when(last)
    def _():
        for t in range(n):
            for j, (cx, cy) in enumerate(chips):
                got = b_refs[t].at[2 * cx + cy]
                _rcopy(got, got, send_sems.at[t, j], recv_sems.at[t, j], (cx, cy, c)).wait_recv()
        for cp in sends():
            cp.wait_send()


def _host_args(hosted):
    n = len(hosted)
    if not n:
        return [], [], [], []
    any_spec = pl.BlockSpec(memory_space=pl.ANY)
    shapes = [jax.ShapeDtypeStruct(p.shape, p.dtype) for p in hosted]
    return [any_spec] * n, [any_spec] * n, shapes, [pltpu.SemaphoreType.DMA((n, 3)), pltpu.SemaphoreType.DMA((n, 3))]


def _attention_bwd_dq(q, k, v, o, lse, do, name, hosted=()):
    h, s, dq = q.shape
    dv = v.shape[-1]
    t = min(ATT_TILE, s)
    scale = dq ** -0.5
    nh = len(hosted)

    def body(*refs):
        q_ref, k_ref, v_ref, o_ref, lse_ref, do_ref = refs[:6]
        p_refs = refs[6:6 + nh]
        dq_ref, delta_ref = refs[6 + nh:8 + nh]
        b_refs = refs[8 + nh:8 + 2 * nh]
        k_scr, v_scr = refs[8 + 2 * nh:10 + 2 * nh]
        hh, i = pl.program_id(0), pl.program_id(1)
        if nh:
            _hosted_exchange(p_refs, b_refs, refs[-2], refs[-1], jnp.logical_and(hh == 0, i == 0),
                             jnp.logical_and(hh == h - 1, i == s // t - 1))

        @pl.when(i == 0)
        def _():
            k_scr[...] = k_ref[...].astype(BF16)
            v_scr[...] = v_ref[...].astype(BF16)

        qb = (q_ref[...].astype(F32) * (scale * LOG2E)).astype(BF16)
        dof = do_ref[...].astype(F32)
        dob = do_ref[...].astype(BF16)
        lse_v = lse_ref[...]
        delta = jnp.sum(dof * o_ref[...].astype(F32), axis=1, keepdims=True)
        delta_ref[...] = delta

        def block(j, acc, masked):
            off = pl.multiple_of(j * t, t)
            kj = k_scr[pl.ds(off, t), :]
            vj = v_scr[pl.ds(off, t), :]
            sc = lax.dot_general(qb, kj, _NT, preferred_element_type=F32)
            p = jnp.exp2(sc - lse_v)
            if masked:
                p = jnp.where(_chunk_mask(i * t, j * t, sc.shape), p, 0.0)
            dp = lax.dot_general(dob, vj, _NT, preferred_element_type=F32)
            ds = p * (dp - delta)
            return acc + jnp.dot(ds.astype(BF16), kj, preferred_element_type=F32)

        acc = lax.fori_loop(0, i, lambda j, c: block(j, c, False), jnp.zeros((t, dq), F32))
        dq_ref[...] = (block(i, acc, True) * scale).astype(dq_ref.dtype)

    tile = lambda d: pl.BlockSpec((None, t, d), lambda hh, i: (hh, i, 0))
    whole = lambda d: pl.BlockSpec((None, s, d), lambda hh, i: (hh, 0, 0))
    h_in, h_out, h_shapes, h_scratch = _host_args(hosted)
    return pl.pallas_call(
        body,
        name=name,
        grid=(h, s // t),
        in_specs=[tile(dq), whole(dq), whole(dv), tile(dv), tile(1), tile(dv)] + h_in,
        out_specs=[tile(dq), tile(1)] + h_out,
        out_shape=[jax.ShapeDtypeStruct((h, s, dq), q.dtype), jax.ShapeDtypeStruct((h, s, 1), F32)] + h_shapes,
        scratch_shapes=[pltpu.VMEM((s, dq), BF16), pltpu.VMEM((s, dv), BF16)] + h_scratch,
        compiler_params=_params(("arbitrary", "arbitrary"), has_side_effects=bool(nh)),
    )(q, k, v, o, lse, do, *hosted)


def _attention_bwd_dkv(q, k, v, lse, delta, do, name, hosted=()):
    h, s, dq = q.shape
    dv = v.shape[-1]
    t = min(ATT_TILE, s)
    n = s // t
    scale = dq ** -0.5
    nh = len(hosted)

    def body(*refs):
        q_ref, k_ref, v_ref, lse_ref, delta_ref, do_ref = refs[:6]
        p_refs = refs[6:6 + nh]
        dk_ref, dv_ref = refs[6 + nh:8 + nh]
        b_refs = refs[8 + nh:8 + 2 * nh]
        q_scr, do_scr = refs[8 + 2 * nh:10 + 2 * nh]
        hh, j = pl.program_id(0), pl.program_id(1)
        if nh:
            _hosted_exchange(p_refs, b_refs, refs[-2], refs[-1], jnp.logical_and(hh == 0, j == 0),
                             jnp.logical_and(hh == h - 1, j == n - 1))

        @pl.when(j == 0)
        def _():
            q_scr[...] = (q_ref[...].astype(F32) * (scale * LOG2E)).astype(BF16)
            do_scr[...] = do_ref[...].astype(BF16)

        kb = k_ref[...].astype(BF16)
        vb = v_ref[...].astype(BF16)

        def block(i, carry, masked):
            dk, dvv = carry
            off = pl.multiple_of(i * t, t)
            qi = q_scr[pl.ds(off, t), :]
            doi = do_scr[pl.ds(off, t), :]
            sc = lax.dot_general(qi, kb, _NT, preferred_element_type=F32)
            p = jnp.exp2(sc - lse_ref[pl.ds(off, t), :])
            if masked:
                p = jnp.where(_chunk_mask(i * t, j * t, sc.shape), p, 0.0)
            dp = lax.dot_general(doi, vb, _NT, preferred_element_type=F32)
            ds = p * (dp - delta_ref[pl.ds(off, t), :])
            dvv = dvv + lax.dot_general(p.astype(BF16), doi, _TN, preferred_element_type=F32)
            dk = dk + lax.dot_general(ds.astype(BF16), qi, _TN, preferred_element_type=F32)
            return dk, dvv

        carry = block(j, (jnp.zeros((t, dq), F32), jnp.zeros((t, dv), F32)), True)
        dk, dvv = lax.fori_loop(j + 1, n, lambda i, c: block(i, c, False), carry)
        dk_ref[...] = (dk * (1.0 / LOG2E)).astype(dk_ref.dtype)
        dv_ref[...] = dvv.astype(dv_ref.dtype)

    tile = lambda d: pl.BlockSpec((None, t, d), lambda hh, j: (hh, j, 0))
    whole = lambda d: pl.BlockSpec((None, s, d), lambda hh, j: (hh, 0, 0))
    h_in, h_out, h_shapes, h_scratch = _host_args(hosted)
    return pl.pallas_call(
        body,
        name=name,
        grid=(h, n),
        in_specs=[whole(dq), tile(dq), tile(dv), whole(1), whole(1), whole(dv)] + h_in,
        out_specs=[tile(dq), tile(dv)] + h_out,
        out_shape=[jax.ShapeDtypeStruct((h, s, dq), k.dtype), jax.ShapeDtypeStruct((h, s, dv), v.dtype)] + h_shapes,
        scratch_shapes=[pltpu.VMEM((s, dq), BF16), pltpu.VMEM((s, dv), BF16)] + h_scratch,
        compiler_params=_params(("arbitrary", "arbitrary"), has_side_effects=bool(nh)),
    )(q, k, v, lse, delta, do, *hosted)


def attention(q, k, v, tok=(), shards=(), *, name):
    tok, shards = tuple(tok), tuple(shards)
    fwd_name = name + "_fwd" + ("_host" if shards else "")

    @jax.custom_vjp
    def op(q, k, v, tok, shards):
        o, _, *partial = _attention_fwd(q, k, v, fwd_name, shards)
        return o, tok, tuple(partial)

    def fwd(q, k, v, tok, shards):
        o, lse, *partial = _attention_fwd(q, k, v, fwd_name, shards)
        return (o, tok, tuple(partial)), (q, k, v, o, lse, shards)

    def bwd(res, cts):
        q, k, v, o, lse, shards = res
        do, payload, _ = cts
        first = tuple(payload[:HOSTED_IN_DQ])
        rest = tuple(payload[HOSTED_IN_DQ:])
        dq, delta, *got_a = _attention_bwd_dq(q, k, v, o, lse, do, name + "_bwd_dq" + ("_host" if first else ""), first)
        dk, dv, *got_b = _attention_bwd_dkv(q, k, v, lse, delta, do, name + "_bwd_dkv" + ("_host" if rest else ""), rest)
        return dq, dk, dv, tuple(got_a) + tuple(got_b), tuple(jnp.zeros_like(s) for s in shards)

    op.defvjp(fwd, bwd)
    return op(q, k, v, tok, shards)


CONV_HALO = 16


def _conv_tiles(s, c):
    return min(512, s), _pick(c, (512, 256, 128))


def _conv_fwd(x, w, name):
    s, c = x.shape
    ts, tc = _conv_tiles(s, c)
    nb = ts // CONV_HALO

    def body(xc_ref, xp_ref, w_ref, o_ref):
        t = pl.program_id(1)
        prev = jnp.where(t > 0, xp_ref[...].astype(F32), 0.0)
        xe = jnp.concatenate([prev, xc_ref[...].astype(F32)], axis=0)
        wv = w_ref[...]
        acc = jnp.zeros((ts, tc), F32)
        for tap in range(CONV_WIDTH):
            k = CONV_WIDTH - 1 - tap
            sh = xe if k == 0 else pltpu.roll(xe, k, axis=0)
            acc = acc + sh[CONV_HALO:, :] * wv[tap:tap + 1, :]
        o_ref[...] = acc

    return pl.pallas_call(
        body,
        name=name,
        grid=(c // tc, s // ts),
        in_specs=[
            pl.BlockSpec((ts, tc), lambda ci, t: (t, ci)),
            pl.BlockSpec((CONV_HALO, tc), lambda ci, t: (jnp.maximum(t * nb - 1, 0), ci)),
            pl.BlockSpec((CONV_WIDTH, tc), lambda ci, t: (0, ci)),
        ],
        out_specs=pl.BlockSpec((ts, tc), lambda ci, t: (t, ci)),
        out_shape=jax.ShapeDtypeStruct((s, c), F32),
        compiler_params=_params(("parallel", "parallel")),
    )(x, x, w)


def _conv_bwd(x, w, dy, name):
    s, c = x.shape
    ts, tc = _conv_tiles(s, c)
    nb = ts // CONV_HALO
    nt = s // ts

    def body(xc_ref, xp_ref, w_ref, dc_ref, dn_ref, dx_ref, dw_ref):
        t = pl.program_id(1)
        prev = jnp.where(t > 0, xp_ref[...].astype(F32), 0.0)
        xe = jnp.concatenate([prev, xc_ref[...].astype(F32)], axis=0)
        dcur = dc_ref[...]
        nxt = jnp.where(t < nt - 1, dn_ref[...], 0.0)
        de = jnp.concatenate([dcur, nxt], axis=0)
        wv = w_ref[...]
        dx = jnp.zeros((ts, tc), F32)
        dw = jnp.zeros((CONV_WIDTH, tc), F32)
        tap_row = lax.broadcasted_iota(jnp.int32, (CONV_WIDTH, tc), 0)
        for tap in range(CONV_WIDTH):
            k = CONV_WIDTH - 1 - tap
            dsh = de if k == 0 else pltpu.roll(de, ts + CONV_HALO - k, axis=0)
            dx = dx + dsh[:ts, :] * wv[tap:tap + 1, :]
            xsh = xe if k == 0 else pltpu.roll(xe, k, axis=0)
            dwt = jnp.sum(xsh[CONV_HALO:, :] * dcur, axis=0, keepdims=True)
            dw = jnp.where(tap_row == tap, dwt, dw)
        dx_ref[...] = dx.astype(dx_ref.dtype)

        @pl.when(t == 0)
        def _():
            dw_ref[...] = dw

        @pl.when(t > 0)
        def _():
            dw_ref[...] += dw

    return pl.pallas_call(
        body,
        name=name,
        grid=(c // tc, nt),
        in_specs=[
            pl.BlockSpec((ts, tc), lambda ci, t: (t, ci)),
            pl.BlockSpec((CONV_HALO, tc), lambda ci, t: (jnp.maximum(t * nb - 1, 0), ci)),
            pl.BlockSpec((CONV_WIDTH, tc), lambda ci, t: (0, ci)),
            pl.BlockSpec((ts, tc), lambda ci, t: (t, ci)),
            pl.BlockSpec((CONV_HALO, tc), lambda ci, t: (jnp.minimum((t + 1) * nb, s // CONV_HALO - 1), ci)),
        ],
        out_specs=[
            pl.BlockSpec((ts, tc), lambda ci, t: (t, ci)),
            pl.BlockSpec((CONV_WIDTH, tc), lambda ci, t: (0, ci)),
        ],
        out_shape=[jax.ShapeDtypeStruct((s, c), x.dtype), jax.ShapeDtypeStruct((CONV_WIDTH, c), F32)],
        compiler_params=_params(("parallel", "arbitrary")),
    )(x, x, w, dy, dy)


def causal_conv(x, w, *, name):
    @jax.custom_vjp
    def op(x, w):
        return _conv_fwd(x, w, name + "_fwd")

    def fwd(x, w):
        return op(x, w), (x, w)

    def bwd(res, dy):
        x, w = res
        dx, dw = _conv_bwd(x, w, dy, name + "_bwd")
        return dx, dw

    op.defvjp(fwd, bwd)
    return op(x, w)


SSD_T = 128
SSD_R = 8
SSD_GW = SSD_R * SSM_HEADDIM


def _ssd_consts(t):
    r = lax.broadcasted_iota(jnp.int32, (t, t), 0)
    c = lax.broadcasted_iota(jnp.int32, (t, t), 1)
    tril = (c <= r).astype(F32)
    triu = (r <= c).astype(F32)
    head_of_lane = lax.broadcasted_iota(jnp.int32, (SSD_R, SSD_GW), 1) // SSM_HEADDIM
    expand = (head_of_lane == lax.broadcasted_iota(jnp.int32, (SSD_R, SSD_GW), 0)).astype(F32)
    return c <= r, tril, triu, expand


def _three_bf16(v):
    p1 = v.astype(BF16)
    r1 = v - p1.astype(F32)
    p2 = r1.astype(BF16)
    return p1, p2, (r1 - p2.astype(F32)).astype(BF16)


def _sel_first(sel, v):
    s = sel.astype(BF16)
    return sum(jnp.dot(s, p, preferred_element_type=F32) for p in _three_bf16(v))


def _sel_second(v, sel, dn=None):
    s = sel.astype(BF16)
    if dn is None:
        return sum(jnp.dot(p, s, preferred_element_type=F32) for p in _three_bf16(v))
    return sum(lax.dot_general(p, s, dn, preferred_element_type=F32) for p in _three_bf16(v))


def _bdot(a, b, dn=None):
    if dn is None:
        return jnp.dot(a.astype(BF16), b.astype(BF16), preferred_element_type=F32)
    return lax.dot_general(a.astype(BF16), b.astype(BF16), dn, preferred_element_type=F32)


def _ssd_chunk_common(x_ref, b_ref, c_ref, dtc_ref, dtr_ref, alc_ref, alr_ref, t):
    mask, tril, triu, expand = _ssd_consts(t)
    x, bm, cm = x_ref[...], b_ref[...], c_ref[...]
    dtc, dtr = dtc_ref[...], dtr_ref[...]
    neg_a_c = -jnp.exp(alc_ref[...])
    neg_a_r = -jnp.exp(alr_ref[...])
    acum_c = _sel_first(tril, dtc * neg_a_c)
    acum_r = _sel_second(dtr * neg_a_r, triu)
    s_cb = _bdot(cm, bm, _NT)
    return mask, tril, triu, expand, x, bm, cm, dtc, dtr, neg_a_c, neg_a_r, acum_c, acum_r, s_cb


def _head_decay(mask, acum_c, acum_r, h):
    seg = acum_c[:, h:h + 1] - acum_r[h:h + 1, :]
    return jnp.exp(jnp.where(mask, seg, -jnp.inf))


def _ssd_fwd(xbc, dtc, dtr, alc, alr, dexp, name, shards=()):
    s = xbc.shape[0]
    g = SSM_GROUPS
    t = min(SSD_T, s)
    nc = s // t
    n = SSM_STATE
    xblocks = (g * SSD_GW) // n
    ng = len(shards)

    def body(*refs):
        x_ref, b_ref, c_ref, dtc_ref, dtr_ref, alc_ref, alr_ref, d_ref = refs[:8]
        y_ref, hs_ref = refs[8 + ng:10 + ng]
        h_scr = refs[10 + 2 * ng]
        ci = pl.program_id(1)
        if ng:
            gi = pl.program_id(0)
            _hosted_gather(refs[8:8 + ng], refs[10 + ng:10 + 2 * ng], refs[-2], refs[-1],
                           jnp.logical_and(gi == 0, ci == 0), jnp.logical_and(gi == g - 1, ci == nc - 1))

        @pl.when(ci == 0)
        def _():
            h_scr[...] = jnp.zeros_like(h_scr)

        (mask, tril, triu, expand, x, bm, cm, dtc_v, dtr_v, _, _, acum_c, acum_r, s_cb) = _ssd_chunk_common(
            x_ref, b_ref, c_ref, dtc_ref, dtr_ref, alc_ref, alr_ref, t)
        hst = h_scr[...]
        hs_ref[...] = hst
        ch = _bdot(cm, hst)
        y = _sel_second(jnp.exp(acum_c), expand) * ch + d_ref[...] * x
        half = lax.broadcasted_iota(jnp.int32, (t, 2 * SSM_HEADDIM), 1) // SSM_HEADDIM
        parts = []
        for j in range(SSD_R // 2):
            xp = x[:, j * 128:(j + 1) * 128]
            acc = jnp.zeros((t, 128), F32)
            for hh in range(2):
                h = 2 * j + hh
                m = s_cb * _head_decay(mask, acum_c, acum_r, h) * dtr_v[h:h + 1, :]
                acc = acc + _bdot(m, jnp.where(half == hh, xp, 0.0))
            parts.append(acc)
        y_ref[...] = y + jnp.concatenate(parts, axis=1)
        last = acum_c[t - 1:t, :]
        w_c = jnp.exp(last - acum_c) * dtc_v
        dec = _sel_second(jnp.broadcast_to(jnp.exp(last), (SSD_R, SSD_R)), expand)[0:1, :]
        h_scr[...] = dec * hst + _bdot(bm, _sel_second(w_c, expand) * x, _TN)

    g_in, g_out, g_shapes, g_scratch = _gather_host_args(shards)
    return pl.pallas_call(
        body,
        name=name,
        grid=(g, nc),
        in_specs=[
            pl.BlockSpec((t, SSD_GW), lambda gi, ci: (ci, gi)),
            pl.BlockSpec((t, n), lambda gi, ci: (ci, xblocks + gi)),
            pl.BlockSpec((t, n), lambda gi, ci: (ci, xblocks + g + gi)),
            pl.BlockSpec((None, t, SSD_R), lambda gi, ci: (gi, ci, 0)),
            pl.BlockSpec((None, SSD_R, t), lambda gi, ci: (gi, 0, ci)),
            pl.BlockSpec((None, 1, SSD_R), lambda gi, ci: (gi, 0, 0)),
            pl.BlockSpec((None, SSD_R, 1), lambda gi, ci: (gi, 0, 0)),
            pl.BlockSpec((None, 1, SSD_GW), lambda gi, ci: (gi, 0, 0)),
        ] + g_in,
        out_specs=[
            pl.BlockSpec((t, SSD_GW), lambda gi, ci: (ci, gi)),
            pl.BlockSpec((None, None, n, SSD_GW), lambda gi, ci: (gi, ci, 0, 0)),
        ] + g_out,
        out_shape=[jax.ShapeDtypeStruct((s, g * SSD_GW), F32), jax.ShapeDtypeStruct((g, nc, n, SSD_GW), F32)] + g_shapes,
        scratch_shapes=[pltpu.VMEM((n, SSD_GW), F32)] + g_scratch,
        compiler_params=_params(("arbitrary" if ng else "parallel", "arbitrary"), has_side_effects=bool(ng)),
    )(xbc, xbc, xbc, dtc, dtr, alc, alr, dexp, *shards)


def _hosted_pair_swap(g_refs, a_refs, send_sems, recv_sems, first, last):
    x, y, c = _place()
    n = len(g_refs)

    def copies():
        out = []
        for t in range(n):
            half = g_refs[t].shape[1] // 2
            out.append(_rcopy(g_refs[t].at[:, pl.ds((1 - c) * half, half), :], a_refs[t], send_sems.at[t, 0],
                              recv_sems.at[t, 0], (x, y, 1 - c)))
        return out

    @pl.when(first)
    def _():
        for cp in copies():
            cp.start()

    @pl.when(last)
    def _():
        for cp in copies():
            cp.wait()


def _ssd_bwd(xbc, dtc, dtr, alc, alr, dexp, hs, dy, name, hosted=()):
    s = xbc.shape[0]
    g = SSM_GROUPS
    t = min(SSD_T, s)
    nc = s // t
    n = SSM_STATE
    xblocks = (g * SSD_GW) // n
    nh = len(hosted)

    def body(*refs):
        x_ref, b_ref, c_ref, dtc_ref, dtr_ref, alc_ref, alr_ref, d_ref, hs_ref, dy_ref = refs[:10]
        dx_ref, db_ref, dc_ref, ddtc_ref, ddtr_ref, dalc_ref, dalr_ref, dd_ref = refs[10 + nh:18 + nh]
        dh_scr = refs[18 + 2 * nh]
        ci = pl.program_id(1)
        if nh:
            gi = pl.program_id(0)
            _hosted_pair_swap(refs[10:10 + nh], refs[18 + nh:18 + 2 * nh], refs[-2], refs[-1],
                              jnp.logical_and(gi == 0, ci == 0), jnp.logical_and(gi == g - 1, ci == nc - 1))

        @pl.when(ci == 0)
        def _():
            dh_scr[...] = jnp.zeros_like(dh_scr)

        (mask, tril, triu, expand, x, bm, cm, dtc_v, dtr_v, neg_a_c, neg_a_r, acum_c, acum_r, s_cb) = _ssd_chunk_common(
            x_ref, b_ref, c_ref, dtc_ref, dtr_ref, alc_ref, alr_ref, t)
        hst = hs_ref[...]
        dhn = dh_scr[...]
        dy = dy_ref[...]
        ch = _bdot(cm, hst)
        scale_full = _sel_second(jnp.exp(acum_c), expand)
        sdy = scale_full * dy
        d_c = _bdot(sdy, hst, _NT)
        dh_prev = _bdot(cm, sdy, _TN)
        dacum_c = _sel_second(sdy * ch, expand, _NT)
        dx = d_ref[...] * dy
        dd = jnp.sum(dy * x, axis=0, keepdims=True)
        last = acum_c[t - 1:t, :]
        e_last = jnp.exp(last)
        dec = _sel_second(jnp.broadcast_to(e_last, (SSD_R, SSD_R)), expand)[0:1, :]
        dh_prev = dh_prev + dec * dhn
        ddec = jnp.sum(hst * dhn, axis=0, keepdims=True)
        dlast = _sel_second(jnp.broadcast_to(ddec, (SSD_R, SSD_GW)), expand, _NT)[0:1, :] * e_last
        w_e = jnp.exp(last - acum_c)
        w_c = w_e * dtc_v
        wfull = _sel_second(w_c, expand)
        z = _bdot(bm, dhn)
        dx = dx + wfull * z
        dw_c = _sel_second(x * z, expand, _NT)
        ddt_c = dw_c * w_e
        q_c = dw_c * w_c
        dacum_c = dacum_c - q_c
        dlast = dlast + jnp.sum(q_c, axis=0, keepdims=True)
        d_b = _bdot(wfull * x, dhn, _NT)
        half = lax.broadcasted_iota(jnp.int32, (t, 2 * SSM_HEADDIM), 1) // SSM_HEADDIM
        lane8 = lax.broadcasted_iota(jnp.int32, (t, SSD_R), 1)
        row8 = lax.broadcasted_iota(jnp.int32, (SSD_R, t), 0)
        ds_cb = jnp.zeros((t, t), F32)
        dacum_r = jnp.zeros((SSD_R, t), F32)
        ddt_r = jnp.zeros((SSD_R, t), F32)
        parts = []
        for j in range(SSD_R // 2):
            xp = x[:, j * 128:(j + 1) * 128]
            dyp = dy[:, j * 128:(j + 1) * 128]
            dxp = jnp.zeros((t, 128), F32)
            for hh in range(2):
                h = 2 * j + hh
                dts = dtr_v[h:h + 1, :]
                decay = _head_decay(mask, acum_c, acum_r, h)
                sl = s_cb * decay
                m = sl * dts
                xm = jnp.where(half == hh, xp, 0.0)
                dym = jnp.where(half == hh, dyp, 0.0)
                dxp = dxp + _bdot(m, dym, _TN)
                dm = _bdot(dym, xm, _NT)
                ds_cb = ds_cb + dm * decay * dts
                q = dm * m
                dacum_c = dacum_c + jnp.where(lane8 == h, jnp.sum(q, axis=1, keepdims=True), 0.0)
                dacum_r = dacum_r - jnp.where(row8 == h, jnp.sum(q, axis=0, keepdims=True), 0.0)
                ddt_r = ddt_r + jnp.where(row8 == h, jnp.sum(dm * sl, axis=0, keepdims=True), 0.0)
            parts.append(dxp)
        dx_ref[...] = dx + jnp.concatenate(parts, axis=1)
        dc_ref[...] = d_c + _bdot(ds_cb, bm)
        db_ref[...] = d_b + _bdot(ds_cb, cm, _TN)
        row_t = lax.broadcasted_iota(jnp.int32, (t, SSD_R), 0)
        dacum_c = dacum_c + jnp.where(row_t == t - 1, dlast, 0.0)
        da_c = _sel_first(triu, dacum_c)
        da_r = _sel_second(dacum_r, tril)
        ddtc_ref[...] = ddt_c + da_c * neg_a_c
        ddtr_ref[...] = ddt_r + da_r * neg_a_r
        dal_c = jnp.sum(da_c * dtc_v, axis=0, keepdims=True) * neg_a_c
        dal_r = jnp.sum(da_r * dtr_v, axis=1, keepdims=True) * neg_a_r
        dh_scr[...] = dh_prev

        @pl.when(ci == 0)
        def _():
            dalc_ref[...] = dal_c
            dalr_ref[...] = dal_r
            dd_ref[...] = dd

        @pl.when(ci > 0)
        def _():
            dalc_ref[...] += dal_c
            dalr_ref[...] += dal_r
            dd_ref[...] += dd

    rev = lambda ci: nc - 1 - ci
    any_spec = pl.BlockSpec(memory_space=pl.ANY)
    a_shapes = [jax.ShapeDtypeStruct((gr.shape[0], gr.shape[1] // 2, gr.shape[2]), gr.dtype) for gr in hosted]
    h_scratch = [pltpu.SemaphoreType.DMA((nh, 1)), pltpu.SemaphoreType.DMA((nh, 1))] if nh else []
    return pl.pallas_call(
        body,
        name=name,
        grid=(g, nc),
        in_specs=[
            pl.BlockSpec((t, SSD_GW), lambda gi, ci: (rev(ci), gi)),
            pl.BlockSpec((t, n), lambda gi, ci: (rev(ci), xblocks + gi)),
            pl.BlockSpec((t, n), lambda gi, ci: (rev(ci), xblocks + g + gi)),
            pl.BlockSpec((None, t, SSD_R), lambda gi, ci: (gi, rev(ci), 0)),
            pl.BlockSpec((None, SSD_R, t), lambda gi, ci: (gi, 0, rev(ci))),
            pl.BlockSpec((None, 1, SSD_R), lambda gi, ci: (gi, 0, 0)),
            pl.BlockSpec((None, SSD_R, 1), lambda gi, ci: (gi, 0, 0)),
            pl.BlockSpec((None, 1, SSD_GW), lambda gi, ci: (gi, 0, 0)),
            pl.BlockSpec((None, None, n, SSD_GW), lambda gi, ci: (gi, rev(ci), 0, 0)),
            pl.BlockSpec((t, SSD_GW), lambda gi, ci: (rev(ci), gi)),
        ] + [any_spec] * nh,
        out_specs=[
            pl.BlockSpec((t, SSD_GW), lambda gi, ci: (rev(ci), gi)),
            pl.BlockSpec((t, n), lambda gi, ci: (rev(ci), gi)),
            pl.BlockSpec((t, n), lambda gi, ci: (rev(ci), gi)),
            pl.BlockSpec((None, t, SSD_R), lambda gi, ci: (gi, rev(ci), 0)),
            pl.BlockSpec((None, SSD_R, t), lambda gi, ci: (gi, 0, rev(ci))),
            pl.BlockSpec((None, 1, SSD_R), lambda gi, ci: (gi, 0, 0)),
            pl.BlockSpec((None, SSD_R, 1), lambda gi, ci: (gi, 0, 0)),
            pl.BlockSpec((None, 1, SSD_GW), lambda gi, ci: (gi, 0, 0)),
        ] + [any_spec] * nh,
        out_shape=[
            jax.ShapeDtypeStruct((s, g * SSD_GW), F32),
            jax.ShapeDtypeStruct((s, g * n), F32),
            jax.ShapeDtypeStruct((s, g * n), F32),
            jax.ShapeDtypeStruct((g, s, SSD_R), F32),
            jax.ShapeDtypeStruct((g, SSD_R, s), F32),
            jax.ShapeDtypeStruct((g, 1, SSD_R), F32),
            jax.ShapeDtypeStruct((g, SSD_R, 1), F32),
            jax.ShapeDtypeStruct((g, 1, SSD_GW), F32),
        ] + a_shapes,
        scratch_shapes=[pltpu.VMEM((n, SSD_GW), F32)] + h_scratch,
        compiler_params=_params(("arbitrary" if nh else "parallel", "arbitrary"), has_side_effects=bool(nh)),
    )(xbc, xbc, xbc, dtc, dtr, alc, alr, dexp, hs, dy, *hosted)


def ssd_core(xbc, dtc, dtr, alc, alr, dexp, shards=(), tok_p=(), tok_a=(), *, name):
    shards, tok_p, tok_a = tuple(shards), tuple(tok_p), tuple(tok_a)
    fwd_name = name + "_fwd" + ("_host" if shards else "")

    def tok_g(tok_a):
        return tuple(jnp.zeros((a.shape[0], 2 * a.shape[1], a.shape[2]), a.dtype) for a in tok_a)

    @jax.custom_vjp
    def op(xbc, dtc, dtr, alc, alr, dexp, shards, tok_p, tok_a):
        y, _, *partial = _ssd_fwd(xbc, dtc, dtr, alc, alr, dexp, fwd_name, shards)
        return y, tuple(partial), tok_g(tok_a)

    def fwd(xbc, dtc, dtr, alc, alr, dexp, shards, tok_p, tok_a):
        y, hs, *partial = _ssd_fwd(xbc, dtc, dtr, alc, alr, dexp, fwd_name, shards)
        return (y, tuple(partial), tok_g(tok_a)), (xbc, dtc, dtr, alc, alr, dexp, hs, shards)

    def bwd(res, cts):
        xbc, dtc, dtr, alc, alr, dexp, hs, shards = res
        dy, _, gs = cts
        gs = tuple(gs)
        dx, db, dc, ddtc, ddtr, dalc, dalr, dd, *a = _ssd_bwd(xbc, dtc, dtr, alc, alr, dexp, hs, dy,
                                                              name + "_bwd" + ("_host" if gs else ""), gs)
        p = tuple(pair_add(g, ai, name="reduce_grads_pair_add") for g, ai in zip(gs, a))
        return (jnp.concatenate([dx, db, dc], axis=1), ddtc, ddtr, dalc, dalr, dd,
                tuple(jnp.zeros_like(s) for s in shards), p, tuple(a))

    op.defvjp(fwd, bwd)
    return op(xbc, dtc, dtr, alc, alr, dexp, shards, tok_p, tok_a)


def gate_norm(y, z, w):
    z = z.astype(F32)
    return (rms_norm(y * (z * jax.nn.sigmoid(z)), w).astype(BF16),)


def ssd_branch(xbc, z, dt_raw, conv_w, conv_b, dt_bias, a_log, d_skip, norm_w, shards=(), tok_p=(), tok_a=(), *, name):
    s = xbc.shape[0]
    g = SSM_GROUPS
    conv = causal_conv(xbc, conv_w, name=name + "_conv")
    (xc,) = rowwise(lambda c, b: ((c + b) * jax.nn.sigmoid(c + b),), (conv,), (), (conv_b[None, :],),
                    name=name + "_silu", tile=min(256, s))
    (dt,) = rowwise(lambda r, b: (jax.nn.softplus(r.astype(F32) + b),), (dt_raw,), (), (dt_bias[None, :],),
                    name=name + "_dt", tile=min(512, s))
    dt3 = dt.reshape(s, g, SSD_R)
    y, partial, tok_g = ssd_core(xc, dt3.transpose(1, 0, 2), dt3.transpose(1, 2, 0), a_log.reshape(g, 1, SSD_R),
                                 a_log.reshape(g, SSD_R, 1), jnp.repeat(d_skip, SSM_HEADDIM).reshape(g, 1, SSD_GW),
                                 shards, tok_p, tok_a, name=name + "_core")
    (out,) = rowwise(gate_norm, (y.reshape(s, g, SSD_GW), z.reshape(s, g, SSD_GW)), (),
                     (norm_w.reshape(g, SSD_GW),), name=name + "_gate", tile=min(128, s))
    return out.reshape(s, g * SSD_GW), partial, tok_g


def rms_norm(x, w):
    return x * lax.rsqrt(jnp.mean(x * x, axis=-1, keepdims=True) + EPS) * w


def rope_matrix():
    half = QK_ROPE // 2
    j = jnp.arange(QK_DIM)
    src = jnp.where(j < QK_NOPE + half, j + half, j - half)
    sign = jnp.where(j < QK_NOPE, 0.0, jnp.where(j < QK_NOPE + half, -1.0, 1.0))
    return (jnp.arange(QK_DIM)[:, None] == src[None, :]).astype(F32) * sign[None, :]


def rope_tables_full(positions):
    inv_freq = 1.0 / (ROPE_THETA ** (jnp.arange(0, QK_ROPE, 2, dtype=F32) / QK_ROPE))
    ang = positions.astype(F32)[:, None] * inv_freq
    s = positions.shape[0]
    cos = jnp.concatenate([jnp.ones((s, QK_NOPE), F32), jnp.cos(ang), jnp.cos(ang)], axis=-1)
    sin = jnp.concatenate([jnp.zeros((s, QK_NOPE), F32), jnp.sin(ang), jnp.sin(ang)], axis=-1)
    return cos[:, None, :], sin[:, None, :]


def head_norm_rope(x, cos_full, sin_full, rot, w):
    t, h, d = x.shape
    y = rms_norm(x.astype(F32), w)
    partner = jnp.dot(y.reshape(t * h, d), rot, precision=HI, preferred_element_type=F32).reshape(t, h, d)
    return ((y * cos_full + partner * sin_full).astype(BF16),)


def _norm(x, w, *, name, out_dtype=BF16, tile=256):
    (y,) = rowwise(lambda x, w: (rms_norm(x.astype(F32), w).astype(out_dtype),), (x,), (), (w[None, :],), name=name,
                   tile=min(tile, x.shape[0]))
    return y


Q_LORA = 512
KV_LORA = 512
W_IN_PIECES = ("cq", "ckv", "kr", "z", "xbc", "dt", "ga", "gb")


def w_in_widths(d_model):
    d_inner = 2 * d_model
    conv_dim = d_inner + 2 * SSM_GROUPS * SSM_STATE
    return (Q_LORA, KV_LORA, QK_ROPE, d_inner, conv_dim, d_inner // SSM_HEADDIM, d_model, d_model)


LINEAR_NAMES = ("cq", "ckv", "kd", "z", "xbc", "ga", "gb", "w_uq", "w_ukv", "w_o_mla", "w_o_ssm", "w_out", "w_up",
                "w_down", "w_ple_gate", "w_ple")


GATHER_IN_ATTENTION = (0, 6)


def layer_forward(x, p_i, cos_full, sin_full, rot, wb, sm, sinks, tok=(), tok_a=(), next_shards=()):
    s, d = x.shape
    in_att = [t for t in range(len(next_shards)) if t in GATHER_IN_ATTENTION]
    in_ssd = [t for t in range(len(next_shards)) if t not in GATHER_IN_ATTENTION]
    lin = lambda a, n: linear(a, wb[n], sinks[n], name="lin_" + n)
    up32 = lambda a: a.astype(F32)
    h = _norm(x, sm["norm_mix_w"], name="norm_mix", out_dtype=F32)
    c_q, c_kv, kd = lin(h, "cq"), lin(h, "ckv"), lin(h, "kd")
    z, xbc, g_a, g_b = lin(h, "z"), lin(h, "xbc"), lin(h, "ga"), lin(h, "gb")
    k_r, dt_raw = kd[:, :QK_ROPE], kd[:, QK_ROPE:]
    q = lin(_norm(c_q, sm["q_a_norm_w"], name="norm_qa"), "w_uq").reshape(s, MLA_HEADS, QK_DIM)
    kv = lin(_norm(c_kv, sm["kv_a_norm_w"], name="norm_kva"), "w_ukv").reshape(s, MLA_HEADS, QK_NOPE + V_DIM)
    k = jnp.concatenate([kv[..., :QK_NOPE], jnp.broadcast_to(k_r[:, None, :], (s, MLA_HEADS, QK_ROPE))], axis=-1)
    v = kv[..., QK_NOPE:]
    tq = min(128, s)
    (q,) = rowwise(head_norm_rope, (q,), (cos_full, sin_full), (sm["q_norm_w"][None, :],), tables=(rot,), name="q_rope", tile=tq)
    (k,) = rowwise(head_norm_rope, (k,), (cos_full, sin_full), (sm["k_norm_w"][None, :],), tables=(rot,), name="k_rope", tile=tq)
    hm = lambda a: a.transpose(1, 0, 2)
    o, tok, part_att = attention(hm(q), hm(k), hm(v), tok, [next_shards[t] for t in in_att], name="attn")
    o = hm(o).reshape(s, MLA_HEADS * V_DIM)
    y_a = lin(o, "w_o_mla")
    y_ssd, part_ssd, tok_g = ssd_branch(xbc, z, dt_raw, sm["conv_w"], sm["conv_b"], sm["dt_bias"], sm["a_log"],
                                        sm["d_skip"], sm["ssm_norm_w"], [next_shards[t] for t in in_ssd], tok, tok_a,
                                        name="ssd")
    y_b = lin(y_ssd, "w_o_ssm")
    partial = [None] * len(next_shards)
    for t, buf in list(zip(in_att, part_att)) + list(zip(in_ssd, part_ssd)):
        partial[t] = buf
    sig = jax.nn.sigmoid
    tr = min(256, s)
    (merged,) = rowwise(lambda ga, gb, ya, yb: ((sig(up32(ga)) * up32(ya) + sig(up32(gb)) * up32(yb)).astype(BF16),),
                        (g_a, g_b, y_a, y_b), (), (), name="merge", tile=tr)
    mixed = lin(merged, "w_out")
    (x,) = rowwise(lambda x, m: (x + up32(m),), (x, mixed), (), (), name="add_mix", tile=tr)
    up = lin(_norm(x, sm["norm_mlp_w"], name="norm_mlp"), "w_up")
    (act,) = rowwise(lambda u: (jnp.square(jnp.maximum(up32(u), 0.0)).astype(BF16),), (up,), (), (), name="relu2", tile=tr)
    down = lin(act, "w_down")
    (x,) = rowwise(lambda x, m: (x + up32(m),), (x, down), (), (), name="add_mlp", tile=tr)
    pg = lin(_norm(x, sm["ple_norm_w"], name="norm_ple"), "w_ple_gate")
    pe = lin(p_i, "w_ple")
    (x,) = rowwise(lambda x, pe, pg: (x + up32(pe) * sig(up32(pg)),), (x, pe, pg), (), (), name="ple_add", tile=tr)
    return (x, tok_g), partial


def loss_and_cotangent(y, target):
    s, d = y.shape

    def f(y, t):
        e = y - t
        return e * (1.0 / d), 0.5 * jnp.sum(jnp.sum(e * e, axis=1, keepdims=True) * (1.0 / d), axis=0, keepdims=True)

    dy, part = _tiled_call(f, (y, target), (), min(256, s), "loss", 1)
    return dy, part[0, 0]


ADAM_BLOCK_ELEMS = 256 * 1024


def adamw(w, g, m, v, *, name):
    rows, cols = w.shape
    budget = max(8, ADAM_BLOCK_ELEMS // cols)
    tile = _pick(rows, tuple(t for t in (512, 256, 128, 64, 32, 16, 8) if t <= budget))

    def f(w, g, m, v):
        m = ADAM_B1 * m + (1.0 - ADAM_B1) * g
        v = ADAM_B2 * v + (1.0 - ADAM_B2) * jnp.square(g)
        m_hat = m / (1.0 - ADAM_B1 ** ADAM_STEP)
        v_hat = v / (1.0 - ADAM_B2 ** ADAM_STEP)
        delta = -ADAM_LR * (m_hat / (jnp.sqrt(v_hat) + ADAM_EPS) + ADAM_WD * w)
        return delta, m, v

    return _tiled_call(f, (w, g, m, v), (), tile, name, 0)


MESH_ID = pl.DeviceIdType.MESH
N_CHIPS = 4
_ANY = pl.BlockSpec(memory_space=pl.ANY)


def _place():
    return lax.axis_index("x"), lax.axis_index("y"), lax.axis_index("c")


def _other_chips(x, y):
    return [(1 - x, y), (x, 1 - y), (1 - x, 1 - y)]


def _rcopy(src, dst, send_sem, recv_sem, device):
    return pltpu.make_async_remote_copy(src_ref=src, dst_ref=dst, send_sem=send_sem, recv_sem=recv_sem,
                                        device_id=device, device_id_type=MESH_ID)


def _sems(n, k):
    return pltpu.SemaphoreType.DMA((n, k))


def _comm_call(body, name, ins, out_shapes, scratch, aliases=None):
    return pl.pallas_call(
        body,
        name=name,
        in_specs=[_ANY] * len(ins),
        out_specs=[_ANY] * len(out_shapes),
        out_shape=out_shapes,
        scratch_shapes=scratch,
        input_output_aliases=aliases or {},
        compiler_params=pltpu.CompilerParams(has_side_effects=True),
    )(*ins)


SPLIT_ROWS = 32


def gather_shards(shards, *, name):
    n = len(shards)
    split = [s.shape[0] % SPLIT_ROWS == 0 for s in shards]

    def body(*refs):
        srcs, outs = refs[:n], refs[n:2 * n]
        send_sems, recv_sems = refs[2 * n:]
        x, y, c = _place()
        sibling = (x, y, 1 - c)
        chips = _other_chips(x, y)
        me = 2 * x + y

        def part(t, slot, h):
            if not split[t]:
                return outs[t].at[slot]
            half = shards[t].shape[0] // 2
            return outs[t].at[slot, pl.ds(h * half, half), :]

        def own(t):
            if not split[t]:
                return srcs[t]
            half = shards[t].shape[0] // 2
            return srcs[t].at[pl.ds(c * half, half), :]

        sent = []
        for t in range(n):
            for j, chip in enumerate(chips):
                sent.append(_rcopy(own(t), part(t, me, c), send_sems.at[t, j], recv_sems.at[t, j], (*chip, c)))
                sent[-1].start()
        for t in range(n):
            sent.append(_rcopy(srcs[t], outs[t].at[me], send_sems.at[t, 6], recv_sems.at[t, 6], sibling))
            sent[-1].start()
        for t in range(n):
            for j, (cx, cy) in enumerate(chips):
                got = part(t, 2 * cx + cy, c)
                _rcopy(got, got, send_sems.at[t, j], recv_sems.at[t, j], (cx, cy, c)).wait_recv()
                if split[t]:
                    sent.append(_rcopy(got, got, send_sems.at[t, 3 + j], recv_sems.at[t, 3 + j], sibling))
                    sent[-1].start()
        for t in range(n):
            if split[t]:
                for j, (cx, cy) in enumerate(chips):
                    got = part(t, 2 * cx + cy, 1 - c)
                    _rcopy(got, got, send_sems.at[t, 3 + j], recv_sems.at[t, 3 + j], sibling).wait_recv()
        for t in range(n):
            _rcopy(srcs[t], outs[t].at[me], send_sems.at[t, 6], recv_sems.at[t, 6], sibling).wait_recv()
        for cp in sent:
            cp.wait_send()

    out_shapes = [jax.ShapeDtypeStruct((N_CHIPS,) + s.shape, s.dtype) for s in shards]
    return _comm_call(body, name, shards, out_shapes, [_sems(n, 7), _sems(n, 7)])


def gather_finish(partial, *, name):
    partial = list(partial)
    todo = [t for t, p in enumerate(partial) if _split_rows(p.shape[1:])]
    n = len(todo)

    def body(*refs):
        outs = refs[n:2 * n]
        send_sems, recv_sems = refs[2 * n:]
        x, y, c = _place()
        sibling = (x, y, 1 - c)
        chips = _other_chips(x, y)

        def part(k, slot, h):
            half = partial[todo[k]].shape[1] // 2
            return outs[k].at[slot, pl.ds(h * half, half), :]

        sent = []
        for k in range(n):
            for j, (cx, cy) in enumerate(chips):
                got = part(k, 2 * cx + cy, c)
                sent.append(_rcopy(got, got, send_sems.at[k, j], recv_sems.at[k, j], sibling))
                sent[-1].start()
        for k in range(n):
            for j, (cx, cy) in enumerate(chips):
                got = part(k, 2 * cx + cy, 1 - c)
                _rcopy(got, got, send_sems.at[k, j], recv_sems.at[k, j], sibling).wait_recv()
        for cp in sent:
            cp.wait_send()

    ins = [partial[t] for t in todo]
    done = _comm_call(body, name, ins, [jax.ShapeDtypeStruct(p.shape, p.dtype) for p in ins], [_sems(n, 3), _sems(n, 3)],
                      aliases={k: k for k in range(n)})
    for k, t in enumerate(todo):
        partial[t] = done[k]
    return partial


def sibling_take_half(gs, *, name):
    n = len(gs)

    def body(*refs):
        g_refs, a_refs = refs[:n], refs[n:2 * n]
        send_sems, recv_sems = refs[2 * n:]
        x, y, c = _place()
        copies = []
        for t in range(n):
            half = gs[t].shape[1] // 2
            copies.append(_rcopy(g_refs[t].at[:, pl.ds((1 - c) * half, half), :], a_refs[t], send_sems.at[t, 0],
                                 recv_sems.at[t, 0], (x, y, 1 - c)))
            copies[-1].start()
        for cp in copies:
            cp.wait()

    out_shapes = [jax.ShapeDtypeStruct((g.shape[0], g.shape[1] // 2, g.shape[2]), g.dtype) for g in gs]
    return _comm_call(body, name, gs, out_shapes, [_sems(n, 1), _sems(n, 1)])


ELEMWISE_BLOCK_ELEMS = 256 * 1024


def _row_tile(rows, cols):
    budget = max(16, ELEMWISE_BLOCK_ELEMS // cols)
    return _pick(rows, tuple(t for t in (1024, 512, 256, 128, 64, 32, 16) if t <= budget))


def _core_and_chip():
    x, y, c = _place()
    return jnp.stack([c, 2 * x + y]).astype(jnp.int32)


def pair_add(g, a, *, name):
    n, rows, cols = g.shape
    half = rows // 2
    tile = _row_tile(half, cols)
    nb = half // tile

    def body(who_ref, g_ref, a_ref, o_ref):
        o_ref[...] = (g_ref[...] + a_ref[...]).astype(o_ref.dtype)

    return pl.pallas_call(
        body,
        name=name,
        grid_spec=pltpu.PrefetchScalarGridSpec(
            num_scalar_prefetch=1,
            grid=(n, nb),
            in_specs=[
                pl.BlockSpec((None, tile, cols), lambda j, i, who: (j, who[0] * nb + i, 0)),
                pl.BlockSpec((None, tile, cols), lambda j, i, who: (j, i, 0)),
            ],
            out_specs=pl.BlockSpec((None, tile, cols), lambda j, i, who: (j, i, 0)),
        ),
        out_shape=jax.ShapeDtypeStruct((n, half, cols), BF16),
        compiler_params=_params(("parallel", "parallel")),
    )(_core_and_chip(), g, a)


def exchange_chip_slots(ps, *, name):
    n = len(ps)

    def body(*refs):
        p_refs, b_refs = refs[:n], refs[n:2 * n]
        send_sems, recv_sems = refs[2 * n:]
        x, y, c = _place()
        me = 2 * x + y
        chips = _other_chips(x, y)
        sends = []
        for t in range(n):
            for j, (cx, cy) in enumerate(chips):
                sends.append(_rcopy(p_refs[t].at[2 * cx + cy], b_refs[t].at[me], send_sems.at[t, j], recv_sems.at[t, j],
                                    (cx, cy, c)))
                sends[-1].start()
        for t in range(n):
            for j, (cx, cy) in enumerate(chips):
                got = b_refs[t].at[2 * cx + cy]
                _rcopy(got, got, send_sems.at[t, j], recv_sems.at[t, j], (cx, cy, c)).wait_recv()
        for cp in sends:
            cp.wait_send()

    out_shapes = [jax.ShapeDtypeStruct(p.shape, p.dtype) for p in ps]
    return _comm_call(body, name, ps, out_shapes, [_sems(n, 3), _sems(n, 3)])


def chips_add(g, a, b, *, name):
    n, rows, cols = g.shape
    half = rows // 2
    tile = _row_tile(half, cols)
    nb = half // tile

    def body(who_ref, g_ref, a_ref, *rest):
        o_ref = rest[-1]
        acc = g_ref[...] + a_ref[...]
        for b_ref in rest[:-1]:
            acc = acc + b_ref[...].astype(F32)
        o_ref[...] = acc

    other = lambda k: pl.BlockSpec((None, tile, cols), lambda i, who, k=k: ((who[1] + k) % n, i, 0))
    return pl.pallas_call(
        body,
        name=name,
        grid_spec=pltpu.PrefetchScalarGridSpec(
            num_scalar_prefetch=1,
            grid=(nb,),
            in_specs=[
                pl.BlockSpec((None, tile, cols), lambda i, who: (who[1], who[0] * nb + i, 0)),
                pl.BlockSpec((None, tile, cols), lambda i, who: (who[1], i, 0)),
            ] + [other(k) for k in range(1, n)],
            out_specs=pl.BlockSpec((tile, cols), lambda i, who: (who[0] * nb + i, 0)),
        ),
        out_shape=jax.ShapeDtypeStruct((rows, cols), F32),
        compiler_params=_params(("parallel",)),
    )(_core_and_chip(), g, a, *([b] * (n - 1)))


def sibling_join_halves(rs, *, name):
    n = len(rs)

    def body(*refs):
        r_refs = refs[n:2 * n]
        send_sems, recv_sems = refs[2 * n:]
        x, y, c = _place()
        sibling = (x, y, 1 - c)
        copies = []
        for t in range(n):
            half = rs[t].shape[0] // 2
            mine = r_refs[t].at[pl.ds(c * half, half), :]
            copies.append(_rcopy(mine, mine, send_sems.at[t, 0], recv_sems.at[t, 0], sibling))
            copies[-1].start()
        for t in range(n):
            half = rs[t].shape[0] // 2
            got = r_refs[t].at[pl.ds((1 - c) * half, half), :]
            _rcopy(got, got, send_sems.at[t, 0], recv_sems.at[t, 0], sibling).wait_recv()
        for cp in copies:
            cp.wait_send()

    out_shapes = [jax.ShapeDtypeStruct(r.shape, r.dtype) for r in rs]
    return _comm_call(body, name, rs, out_shapes, [_sems(n, 1), _sems(n, 1)], aliases={t: t for t in range(n)})


def reduce_to_owner(gs, *, name):
    gs, a, p = reduce_pairs(gs, name=name)
    return reduce_finish(gs, a, exchange_chip_slots(p, name=name + "_chips"), name=name)


def reduce_pairs(gs, *, name):
    gs = list(gs)
    a = sibling_take_half(gs, name=name + "_pair")
    p = [pair_add(g, ai, name=name + "_pair_add") for g, ai in zip(gs, a)]
    return gs, a, p


def reduce_finish(gs, a, b, *, name):
    f = [chips_add(g, ai, bi, name=name + "_chips_add") for g, ai, bi in zip(gs, a, b)]
    return sibling_join_halves(f, name=name + "_join")


def allreduce_small(v, *, name):
    rows, cols = v.shape

    def body(v_ref, o_ref, buf, send_sems, recv_sems):
        x, y, c = _place()
        me = 4 * x + 2 * y + c
        buf[me] = v_ref[...]
        copies = []
        for k in range(1, 8):
            bx, by, bc = (k >> 2) & 1, (k >> 1) & 1, k & 1
            peer = (x if bx == 0 else 1 - x, y if by == 0 else 1 - y, c if bc == 0 else 1 - c)
            copies.append(_rcopy(v_ref, buf.at[me], send_sems.at[k - 1], recv_sems.at[k - 1], peer))
        for cp in copies:
            cp.start()
        for k in range(1, 8):
            bx, by, bc = (k >> 2) & 1, (k >> 1) & 1, k & 1
            px, py, pc = (x if bx == 0 else 1 - x, y if by == 0 else 1 - y, c if bc == 0 else 1 - c)
            _rcopy(v_ref, buf.at[4 * px + 2 * py + pc], send_sems.at[k - 1], recv_sems.at[k - 1], (px, py, pc)).wait_recv()
        for cp in copies:
            cp.wait_send()
        acc = buf[0]
        for j in range(1, 8):
            acc = acc + buf[j]
        o_ref[...] = acc

    return pl.pallas_call(
        body,
        name=name,
        in_specs=[pl.BlockSpec(memory_space=pltpu.VMEM)],
        out_specs=pl.BlockSpec(memory_space=pltpu.VMEM),
        out_shape=jax.ShapeDtypeStruct((rows, cols), v.dtype),
        scratch_shapes=[pltpu.VMEM((8, rows, cols), v.dtype), pltpu.SemaphoreType.DMA((7,)), pltpu.SemaphoreType.DMA((7,))],
        compiler_params=pltpu.CompilerParams(has_side_effects=True, vmem_limit_bytes=V7X_VMEM_LIMIT),
    )(v)


BIG = (("w_in", 1), ("w_uq", 1), ("w_ukv", 1), ("w_o_mla", 0), ("w_o_ssm", 0), ("w_out", 0), ("w_up", 1),
       ("w_down", 0), ("w_ple_gate", 0), ("w_ple", 1))
SHARDED = BIG + (("conv_w", 1),)
SMALL = ("norm_mix_w", "q_a_norm_w", "kv_a_norm_w", "q_norm_w", "k_norm_w", "conv_b", "dt_bias", "a_log", "d_skip",
         "ssm_norm_w", "norm_mlp_w", "ple_norm_w")
WEIGHTS = ("norm_mix_w", "w_in", "q_a_norm_w", "w_uq", "kv_a_norm_w", "w_ukv", "q_norm_w", "k_norm_w", "w_o_mla", "conv_w",
           "conv_b", "dt_bias", "a_log", "d_skip", "ssm_norm_w", "w_o_ssm", "w_out", "norm_mlp_w", "w_up", "w_down",
           "ple_norm_w", "w_ple_gate", "w_ple")


def _to_rows(flat, cols, row_multiple):
    n = flat.shape[-1]
    rows = -(-n // (cols * row_multiple)) * row_multiple
    pad = [(0, 0)] * (flat.ndim - 1) + [(0, rows * cols - n)]
    return jnp.pad(flat, pad).reshape(flat.shape[:-1] + (rows, cols))


def _w_in_ranges(d_model):
    out, lo = {}, 0
    for n, wd in zip(W_IN_PIECES, w_in_widths(d_model)):
        out[n] = (lo, lo + wd)
        lo += wd
    return out


def w_in_pieces(w3):
    _, k, c = w3.shape
    pc = {}
    for n, (lo, hi) in _w_in_ranges(k).items():
        cuts = [w3[j][:, max(lo, j * c) - j * c:min(hi, (j + 1) * c) - j * c]
                for j in range(N_CHIPS) if max(lo, j * c) < min(hi, (j + 1) * c)]
        pc[n] = cuts[0] if len(cuts) == 1 else jnp.concatenate(cuts, axis=1)
    pc["kd"] = jnp.concatenate([pc.pop("kr"), pc.pop("dt")], axis=1)
    return pc


def w_in_shard_grads(g, k, c):
    g = dict(g)
    g["kr"], g["dt"] = g["kd"][:, :QK_ROPE], g["kd"][:, QK_ROPE:]
    shards = []
    for j in range(N_CHIPS):
        cuts = []
        for n, (lo, hi) in _w_in_ranges(k).items():
            a, b = max(lo, j * c), min(hi, (j + 1) * c)
            if a < b:
                cuts.append(g[n][:, a - lo:b - lo])
        shards.append(jnp.concatenate(cuts, axis=1))
    return jnp.stack(shards)


def kernel(x, p, positions, norm_mix_w, w_in, q_a_norm_w, w_uq, kv_a_norm_w, w_ukv, q_norm_w, k_norm_w, w_o_mla, conv_w, conv_b, dt_bias, a_log, d_skip, ssm_norm_w, w_o_ssm, w_out, norm_mlp_w, w_up, w_down, ple_norm_w, w_ple_gate, w_ple, loss_target, m_norm_mix_w, m_w_in, m_q_a_norm_w, m_w_uq, m_kv_a_norm_w, m_w_ukv, m_q_norm_w, m_k_norm_w, m_w_o_mla, m_conv_w, m_conv_b, m_dt_bias, m_a_log, m_d_skip, m_ssm_norm_w, m_w_o_ssm, m_w_out, m_norm_mlp_w, m_w_up, m_w_down, m_ple_norm_w, m_w_ple_gate, m_w_ple, v_norm_mix_w, v_w_in, v_q_a_norm_w, v_w_uq, v_kv_a_norm_w, v_w_ukv, v_q_norm_w, v_k_norm_w, v_w_o_mla, v_conv_w, v_conv_b, v_dt_bias, v_a_log, v_d_skip, v_ssm_norm_w, v_w_o_ssm, v_w_out, v_norm_mlp_w, v_w_up, v_w_down, v_ple_norm_w, v_w_ple_gate, v_w_ple):
    a = dict(locals())
    x, p, pos, target = a["x"][0], a["p"][:, 0], a["positions"][0], a["loss_target"][0]
    depth = a["w_in"].shape[0]
    shard_shapes = {n: tuple(a[n].shape[1:]) for n, _ in SHARDED}
    cos_full, sin_full = rope_tables_full(pos)
    rot = rope_matrix()

    shards = [[a[n][i].astype(BF16) for n, _ in BIG] + [a["conv_w"][i]] for i in range(depth)]
    tok = tuple(jnp.zeros((N_CHIPS, shard_shapes[n][0] // 2, shard_shapes[n][1]), BF16) for n, _ in BIG)
    tok_a = tuple(jnp.zeros(t.shape, F32) for t in tok)
    layer_vjps = []
    got = gather_shards(shards[0], name="gather_weights")
    for i in range(depth):
        full = dict(zip([n for n, _ in SHARDED], got))
        w_i = w_in_pieces(full["w_in"])
        for n, ax in BIG[1:]:
            w_i[n] = full[n] if ax == 1 else full[n].reshape((-1, full[n].shape[-1]))
        sm_i = {n: a[n][i] for n in SMALL}
        sm_i["conv_w"] = full["conv_w"].transpose(1, 0, 2).reshape(CONV_WIDTH, -1)
        sinks_i = {n: jnp.zeros(w_i[n].shape, F32) for n in LINEAR_NAMES}
        last = i == depth - 1
        f_i = functools.partial(layer_forward, p_i=p[i], cos_full=cos_full, sin_full=sin_full, rot=rot, wb=w_i,
                                next_shards=() if last else shards[i + 1])
        g_i = lambda x, sm, sk, tk, ta, f=f_i: f(x, sm=sm, sinks=sk, tok=tk, tok_a=ta)
        (x, _), vjp_i, partial = jax.vjp(g_i, x, sm_i, sinks_i, () if last else tok, () if last else tok_a, has_aux=True)
        layer_vjps.append(vjp_i)
        if not last:
            got = gather_finish(partial, name="gather_finish")
    dx, loss_part = loss_and_cotangent(x, target)
    loss = lax.psum(loss_part, ("x", "y", "c"))

    per_layer, d_small, pending = [None] * depth, [None] * depth, None
    for i in reversed(range(depth)):
        dx, d_small[i], d_sinks_i, got_b, got_a = layer_vjps[i]((dx, tuple(pending) if pending is not None else ()))
        if pending is not None:
            per_layer[i + 1] = reduce_finish(pending, list(got_a), list(got_b), name="reduce_grads")
        pending = [w_in_shard_grads(d_sinks_i, *shard_shapes["w_in"])]
        pending += [d_sinks_i[n].reshape((N_CHIPS,) + shard_shapes[n]) for n, _ in BIG[1:]]
    per_layer[0] = reduce_to_owner(pending, name="reduce_grads")
    grads = {n: jnp.stack([per_layer[i][t] for i in range(depth)]) for t, (n, _) in enumerate(BIG)}

    small_names = SMALL + ("conv_w",)
    flat = jnp.concatenate([d_small[i][n].reshape(-1) for i in range(depth) for n in small_names])
    n_small = flat.shape[0]
    red = allreduce_small(_to_rows(flat, 128, 8), name="reduce_small").reshape(-1)[:n_small]
    per = n_small // depth
    off = 0
    for n in SMALL:
        width = a[n].shape[-1]
        grads[n] = jnp.stack([red[i * per + off:i * per + off + width] for i in range(depth)])
        off += width
    conv_c = shard_shapes["conv_w"][1]
    conv_full = jnp.stack([red[i * per + off:i * per + off + CONV_WIDTH * N_CHIPS * conv_c] for i in range(depth)])
    chip = 2 * lax.axis_index("x") + lax.axis_index("y")
    grads["conv_w"] = lax.dynamic_index_in_dim(conv_full.reshape(depth, CONV_WIDTH, N_CHIPS, conv_c), chip, axis=2,
                                               keepdims=False)

    deltas, new_m, new_v = {}, {}, {}
    two_d = lambda t: t.reshape(-1, t.shape[-1])
    for n in WEIGHTS:
        d, m, v = adamw(two_d(a[n]), two_d(grads[n]), two_d(a["m_" + n]), two_d(a["v_" + n]), name="adamw")
        deltas[n], new_m[n], new_v[n] = d.reshape(a[n].shape), m.reshape(a[n].shape), v.reshape(a[n].shape)

    return (loss, dx[None], *[grads[n].reshape(a[n].shape) for n in WEIGHTS], *[deltas[n] for n in WEIGHTS],
            *[new_m[n] for n in WEIGHTS], *[new_v[n] for n in WEIGHTS])
```

```python
import functools

import jax
import jax.numpy as jnp
from jax import lax
from jax.experimental import pallas as pl
from jax.experimental.pallas import tpu as pltpu

F32 = jnp.float32
BF16 = jnp.bfloat16
HI = lax.Precision.HIGHEST

EPS = 1e-6
MLA_HEADS = 16
QK_NOPE = 128
QK_ROPE = 64
QK_DIM = QK_NOPE + QK_ROPE
V_DIM = 128
ROPE_THETA = 10000.0
ATT_CHUNK = 64
SSM_GROUPS = 8
SSM_HEADDIM = 64
SSM_STATE = 128
CONV_WIDTH = 4
ADAM_LR = 0.001
ADAM_B1 = 0.9
ADAM_B2 = 0.999
ADAM_EPS = 1e-08
ADAM_WD = 0.01
ADAM_STEP = 10

V7X_VMEM_LIMIT = 56 * 1024 * 1024


def _params(sem=None, **kw):
    return pltpu.CompilerParams(dimension_semantics=sem, vmem_limit_bytes=V7X_VMEM_LIMIT, **kw)


def _pick(n, prefs):
    for t in prefs:
        if n % t == 0:
            return t
    return n


MATMUL_OPERAND_BYTES = 24 * 1024 * 1024


def matmul(a, b, *, ta=False, tb=False, out_blocks=0, out_dtype=F32, name):
    m, k = (a.shape[1], a.shape[0]) if ta else a.shape
    blocked = b.ndim == 3
    if blocked:
        nb, rows, c = b.shape
        k2, n = (nb * c, rows) if tb else (rows, nb * c)
    else:
        k2, n = (b.shape[1], b.shape[0]) if tb else b.shape
    assert k == k2, (a.shape, b.shape, ta, tb)
    n_unit = n // out_blocks if out_blocks else (c if blocked and not tb else n)
    k_unit = c if blocked and tb else k
    tm = _pick(m, ((2048,) if k <= 512 else ()) + (1024, 512, 256, 128))
    tn = _pick(n_unit, (1024, 768, 512, 384, 256, 128))
    in_bytes = tm * a.dtype.itemsize + tn * b.dtype.itemsize
    tk = _pick(k_unit, tuple(t for t in (2048, 1536, 1024, 768, 512, 384, 256, 128)
                             if 2 * t * in_bytes <= MATMUL_OPERAND_BYTES))
    nk = k // tk
    dn = (((0 if ta else 1,), (1 if tb else 0,)), ((), ()))

    def body(a_ref, b_ref, o_ref, *acc):
        part = lambda: lax.dot_general(a_ref[...].astype(BF16), b_ref[...].astype(BF16), dn, preferred_element_type=F32)
        if nk == 1:
            o_ref[...] = part().astype(o_ref.dtype)
            return
        (acc_ref,) = acc
        kk = pl.program_id(2)

        @pl.when(kk == 0)
        def _():
            acc_ref[...] = jnp.zeros_like(acc_ref)

        acc_ref[...] += part()

        @pl.when(kk == nk - 1)
        def _():
            o_ref[...] = acc_ref[...].astype(o_ref.dtype)

    a_spec = pl.BlockSpec((tk, tm), lambda i, j, kk: (kk, i)) if ta else pl.BlockSpec((tm, tk), lambda i, j, kk: (i, kk))
    if not blocked:
        b_spec = pl.BlockSpec((tn, tk), lambda i, j, kk: (j, kk)) if tb else pl.BlockSpec((tk, tn), lambda i, j, kk: (kk, j))
    elif tb:
        kb = c // tk
        b_spec = pl.BlockSpec((None, tn, tk), lambda i, j, kk: (kk // kb, j, kk % kb))
    else:
        cb = c // tn
        b_spec = pl.BlockSpec((None, tk, tn), lambda i, j, kk: (j // cb, kk, j % cb))
    if out_blocks:
        ob = n_unit // tn
        out_spec = pl.BlockSpec((None, tm, tn), lambda i, j, kk: (j // ob, i, j % ob))
        out_shape = jax.ShapeDtypeStruct((out_blocks, m, n_unit), out_dtype)
    else:
        out_spec = pl.BlockSpec((tm, tn), lambda i, j, kk: (i, j))
        out_shape = jax.ShapeDtypeStruct((m, n), out_dtype)
    return pl.pallas_call(
        body,
        name=name,
        grid=(m // tm, n // tn, nk),
        in_specs=[a_spec, b_spec],
        out_specs=out_spec,
        out_shape=out_shape,
        scratch_shapes=[pltpu.VMEM((tm, tn), F32)] if nk > 1 else [],
        compiler_params=_params(("parallel", "parallel", "arbitrary")),
    )(a, b)


def linear(a, w, sink, *, name, out_dtype=BF16):
    @jax.custom_vjp
    def op(a, w, sink):
        return matmul(a, w, out_dtype=out_dtype, name=name + "_fwd")

    def fwd(a, w, sink):
        return op(a, w, sink), (a, w)

    def bwd(res, ct):
        a, w = res
        da = matmul(ct, w, tb=True, out_dtype=a.dtype, name=name + "_bwd_da")
        dw = matmul(a, ct, ta=True, out_blocks=w.shape[0] if w.ndim == 3 else 0, name=name + "_bwd_dw")
        return da, jnp.zeros_like(w), dw

    op.defvjp(fwd, bwd)
    return op(a, w, sink)


def _tiled_call(fn, tiled, whole, tile, name, n_acc):
    rows = tiled[0].shape[0]
    assert rows % tile == 0
    t_avals = [jax.ShapeDtypeStruct((tile,) + a.shape[1:], a.dtype) for a in tiled]
    w_avals = [jax.ShapeDtypeStruct(a.shape, a.dtype) for a in whole]
    outs = jax.eval_shape(fn, *t_avals, *w_avals)
    n_in = len(tiled) + len(whole)
    n_t = len(outs) - n_acc

    def body(*refs):
        res = fn(*[r[...] for r in refs[:n_in]])
        o_refs = refs[n_in:]
        for r, v in zip(o_refs[:n_t], res[:n_t]):
            r[...] = v.astype(r.dtype)
        if n_acc:
            first = pl.program_id(0) == 0

            @pl.when(first)
            def _():
                for r, v in zip(o_refs[n_t:], res[n_t:]):
                    r[...] = v.astype(F32)

            @pl.when(jnp.logical_not(first))
            def _():
                for r, v in zip(o_refs[n_t:], res[n_t:]):
                    r[...] += v.astype(F32)

    def tspec(a):
        nd = len(a.shape)
        return pl.BlockSpec((tile,) + tuple(a.shape[1:]), lambda i, nd=nd: (i,) + (0,) * (nd - 1))

    def wspec(a):
        nd = len(a.shape)
        return pl.BlockSpec(tuple(a.shape), lambda i, nd=nd: (0,) * nd)

    out_shape = [jax.ShapeDtypeStruct((rows,) + o.shape[1:], o.dtype) for o in outs[:n_t]]
    out_shape += [jax.ShapeDtypeStruct(o.shape, F32) for o in outs[n_t:]]
    out_specs = [tspec(o) for o in out_shape[:n_t]] + [wspec(o) for o in out_shape[n_t:]]
    return pl.pallas_call(
        body,
        name=name,
        grid=(rows // tile,),
        in_specs=[tspec(a) for a in tiled] + [wspec(a) for a in whole],
        out_specs=out_specs,
        out_shape=out_shape,
        compiler_params=_params(("arbitrary",) if n_acc else ("parallel",)),
    )(*tiled, *whole)


def rowwise(f, rows, consts, params, *, name, tile, tables=()):
    rows, consts, tables, params = tuple(rows), tuple(consts), tuple(tables), tuple(params)
    nr, nc, ntab, npar = len(rows), len(consts), len(tables), len(params)

    @jax.custom_vjp
    def op(rows, consts, tables, params):
        return tuple(_tiled_call(f, rows + consts, tables + params, tile, name + "_fwd", 0))

    def fwd(rows, consts, tables, params):
        return op(rows, consts, tables, params), (rows, consts, tables, params)

    def bwd(res, cts):
        rows, consts, tables, params = res
        ncts = len(cts)

        def g(*args):
            r = args[:nr]
            c = args[nr:nr + nc]
            ct = args[nr + nc:nr + nc + ncts]
            tab = args[nr + nc + ncts:nr + nc + ncts + ntab]
            p = args[nr + nc + ncts + ntab:]
            _, vjp = jax.vjp(lambda *rp: f(*rp[:nr], *c, *tab, *rp[nr:]), *r, *p)
            return tuple(vjp(tuple(ct)))

        outs = _tiled_call(g, rows + consts + tuple(cts), tables + params, tile, name + "_bwd", npar)
        d_rows = tuple(o.astype(r.dtype) for o, r in zip(outs[:nr], rows))
        d_params = tuple(o.astype(p.dtype) for o, p in zip(outs[nr:], params))
        zeros = lambda xs: tuple(jnp.zeros_like(a) for a in xs)
        return d_rows, zeros(consts), zeros(tables), d_params

    op.defvjp(fwd, bwd)
    return op(rows, consts, tables, params)


ATT_TILE = 512
LOG2E = 1.4426950408889634
HOSTED_IN_DQ = 4
_NT = (((1,), (1,)), ((), ()))
_TN = (((0,), (0,)), ((), ()))


def _chunk_mask(row0, col0, shape):
    r = (row0 + lax.broadcasted_iota(jnp.int32, shape, 0)) // ATT_CHUNK
    c = (col0 + lax.broadcasted_iota(jnp.int32, shape, 1)) // ATT_CHUNK
    return c <= r


def _split_rows(shape):
    return shape[0] % SPLIT_ROWS == 0


def _hosted_gather(src_refs, out_refs, send_sems, recv_sems, first, last):
    x, y, c = _place()
    me = 2 * x + y
    chips = _other_chips(x, y)
    n = len(src_refs)

    def rows(t, ref, h):
        if not _split_rows(src_refs[t].shape):
            return ref
        half = src_refs[t].shape[0] // 2
        return ref.at[pl.ds(h * half, half), :]

    def sends():
        over_ici = [_rcopy(rows(t, src_refs[t], c), rows(t, out_refs[t].at[me], c), send_sems.at[t, j], recv_sems.at[t, j],
                           (*chip, c)) for t in range(n) for j, chip in enumerate(chips)]
        own_slot = [_rcopy(src_refs[t], out_refs[t].at[me], send_sems.at[t, 3], recv_sems.at[t, 3], (x, y, 1 - c))
                    for t in range(n)]
        return over_ici + own_slot

    @pl.when(first)
    def _():
        for cp in sends():
            cp.start()

    @pl.when(last)
    def _():
        for t in range(n):
            for j, (cx, cy) in enumerate(chips):
                got = rows(t, out_refs[t].at[2 * cx + cy], c)
                _rcopy(got, got, send_sems.at[t, j], recv_sems.at[t, j], (cx, cy, c)).wait_recv()
            _rcopy(src_refs[t], out_refs[t].at[me], send_sems.at[t, 3], recv_sems.at[t, 3], (x, y, 1 - c)).wait_recv()
        for cp in sends():
            cp.wait_send()


def _gather_host_args(shards):
    n = len(shards)
    if not n:
        return [], [], [], []
    any_spec = pl.BlockSpec(memory_space=pl.ANY)
    shapes = [jax.ShapeDtypeStruct((N_CHIPS,) + s.shape, s.dtype) for s in shards]
    return [any_spec] * n, [any_spec] * n, shapes, [pltpu.SemaphoreType.DMA((n, 4)), pltpu.SemaphoreType.DMA((n, 4))]


def _attention_fwd(q, k, v, name, shards=()):
    h, s, dq = q.shape
    dv = v.shape[-1]
    t = min(ATT_TILE, s)
    scale = dq ** -0.5
    ng = len(shards)

    def body(*refs):
        q_ref, k_ref, v_ref = refs[:3]
        o_ref, lse_ref = refs[3 + ng:5 + ng]
        k_scr, v_scr = refs[5 + 2 * ng:7 + 2 * ng]
        i = pl.program_id(1)
        if ng:
            hh = pl.program_id(0)
            _hosted_gather(refs[3:3 + ng], refs[5 + ng:5 + 2 * ng], refs[-2], refs[-1],
                           jnp.logical_and(hh == 0, i == 0), jnp.logical_and(hh == h - 1, i == s // t - 1))

        k_src, v_src = k_ref, v_ref
        if k_ref.dtype != BF16 or v_ref.dtype != BF16:
            k_src, v_src = k_scr, v_scr

            @pl.when(i == 0)
            def _():
                k_scr[...] = k_ref[...].astype(BF16)
                v_scr[...] = v_ref[...].astype(BF16)

        qb = (q_ref[...].astype(F32) * (scale * LOG2E)).astype(BF16)

        def block(j, carry, masked):
            m, l, acc = carry
            off = pl.multiple_of(j * t, t)
            kj = k_src[pl.ds(off, t), :]
            vj = v_src[pl.ds(off, t), :]
            sc = lax.dot_general(qb, kj, _NT, preferred_element_type=F32)
            if masked:
                sc = jnp.where(_chunk_mask(i * t, j * t, sc.shape), sc, -jnp.inf)
            m_new = jnp.maximum(m, jnp.max(sc, axis=1, keepdims=True))
            p = jnp.exp2(sc - m_new)
            alpha = jnp.exp2(m - m_new)
            l = alpha * l + jnp.sum(p, axis=1, keepdims=True)
            acc = alpha * acc + jnp.dot(p.astype(BF16), vj, preferred_element_type=F32)
            return m_new, l, acc

        init = (jnp.full((t, 1), -jnp.inf, F32), jnp.zeros((t, 1), F32), jnp.zeros((t, dv), F32))
        carry = lax.fori_loop(0, i, lambda j, c: block(j, c, False), init)
        m, l, acc = block(i, carry, True)
        o_ref[...] = (acc / l).astype(o_ref.dtype)
        lse_ref[...] = m + jnp.log2(l)

    g_in, g_out, g_shapes, g_scratch = _gather_host_args(shards)
    return pl.pallas_call(
        body,
        name=name,
        grid=(h, s // t),
        in_specs=[
            pl.BlockSpec((None, t, dq), lambda hh, i: (hh, i, 0)),
            pl.BlockSpec((None, s, dq), lambda hh, i: (hh, 0, 0)),
            pl.BlockSpec((None, s, dv), lambda hh, i: (hh, 0, 0)),
        ] + g_in,
        out_specs=[
            pl.BlockSpec((None, t, dv), lambda hh, i: (hh, i, 0)),
            pl.BlockSpec((None, t, 1), lambda hh, i: (hh, i, 0)),
        ] + g_out,
        out_shape=[jax.ShapeDtypeStruct((h, s, dv), q.dtype), jax.ShapeDtypeStruct((h, s, 1), F32)] + g_shapes,
        scratch_shapes=[pltpu.VMEM((s, dq), BF16), pltpu.VMEM((s, dv), BF16)] + g_scratch,
        compiler_params=_params(("arbitrary", "arbitrary"), has_side_effects=bool(ng)),
    )(q, k, v, *shards)


def _hosted_exchange(p_refs, b_refs, send_sems, recv_sems, first, last):
    x, y, c = _place()
    me = 2 * x + y
    chips = _other_chips(x, y)
    n = len(p_refs)

    def sends():
        return [_rcopy(p_refs[t].at[2 * cx + cy], b_refs[t].at[me], send_sems.at[t, j], recv_sems.at[t, j], (cx, cy, c))
                for t in range(n) for j, (cx, cy) in enumerate(chips)]

    @pl.when(first)
    def _():
        for cp in sends():
            cp.start()

    @pl.when(last)
    def _():
        for t in range(n):
            for j, (cx, cy) in enumerate(chips):
                got = b_refs[t].at[2 * cx + cy]
                _rcopy(got, got, send_sems.at[t, j], recv_sems.at[t, j], (cx, cy, c)).wait_recv()
        for cp in sends():
            cp.wait_send()


def _host_args(hosted):
    n = len(hosted)
    if not n:
        return [], [], [], []
    any_spec = pl.BlockSpec(memory_space=pl.ANY)
    shapes = [jax.ShapeDtypeStruct(p.shape, p.dtype) for p in hosted]
    return [any_spec] * n, [any_spec] * n, shapes, [pltpu.SemaphoreType.DMA((n, 3)), pltpu.SemaphoreType.DMA((n, 3))]


def _attention_bwd_dq(q, k, v, o, lse, do, name, hosted=()):
    h, s, dq = q.shape
    dv = v.shape[-1]
    t = min(ATT_TILE, s)
    scale = dq ** -0.5
    nh = len(hosted)

    def body(*refs):
        q_ref, k_ref, v_ref, o_ref, lse_ref, do_ref = refs[:6]
        p_refs = refs[6:6 + nh]
        dq_ref, delta_ref = refs[6 + nh:8 + nh]
        b_refs = refs[8 + nh:8 + 2 * nh]
        k_scr, v_scr = refs[8 + 2 * nh:10 + 2 * nh]
        hh, i = pl.program_id(0), pl.program_id(1)
        if nh:
            _hosted_exchange(p_refs, b_refs, refs[-2], refs[-1], jnp.logical_and(hh == 0, i == 0),
                             jnp.logical_and(hh == h - 1, i == s // t - 1))

        k_src, v_src = k_ref, v_ref
        if k_ref.dtype != BF16 or v_ref.dtype != BF16:
            k_src, v_src = k_scr, v_scr

            @pl.when(i == 0)
            def _():
                k_scr[...] = k_ref[...].astype(BF16)
                v_scr[...] = v_ref[...].astype(BF16)

        qb = (q_ref[...].astype(F32) * (scale * LOG2E)).astype(BF16)
        dof = do_ref[...].astype(F32)
        dob = do_ref[...].astype(BF16)
        lse_v = lse_ref[...]
        delta = jnp.sum(dof * o_ref[...].astype(F32), axis=1, keepdims=True)
        delta_ref[...] = delta

        def block(j, acc, masked):
            off = pl.multiple_of(j * t, t)
            kj = k_src[pl.ds(off, t), :]
            vj = v_src[pl.ds(off, t), :]
            sc = lax.dot_general(qb, kj, _NT, preferred_element_type=F32)
            p = jnp.exp2(sc - lse_v)
            if masked:
                p = jnp.where(_chunk_mask(i * t, j * t, sc.shape), p, 0.0)
            dp = lax.dot_general(dob, vj, _NT, preferred_element_type=F32)
            ds = p * (dp - delta)
            return acc + jnp.dot(ds.astype(BF16), kj, preferred_element_type=F32)

        acc = lax.fori_loop(0, i, lambda j, c: block(j, c, False), jnp.zeros((t, dq), F32))
        dq_ref[...] = (block(i, acc, True) * scale).astype(dq_ref.dtype)

    tile = lambda d: pl.BlockSpec((None, t, d), lambda hh, i: (hh, i, 0))
    whole = lambda d: pl.BlockSpec((None, s, d), lambda hh, i: (hh, 0, 0))
    h_in, h_out, h_shapes, h_scratch = _host_args(hosted)
    return pl.pallas_call(
        body,
        name=name,
        grid=(h, s // t),
        in_specs=[tile(dq), whole(dq), whole(dv), tile(dv), tile(1), tile(dv)] + h_in,
        out_specs=[tile(dq), tile(1)] + h_out,
        out_shape=[jax.ShapeDtypeStruct((h, s, dq), q.dtype), jax.ShapeDtypeStruct((h, s, 1), F32)] + h_shapes,
        scratch_shapes=[pltpu.VMEM((s, dq), BF16), pltpu.VMEM((s, dv), BF16)] + h_scratch,
        compiler_params=_params(("arbitrary", "arbitrary"), has_side_effects=bool(nh)),
    )(q, k, v, o, lse, do, *hosted)


def _attention_bwd_dkv(q, k, v, lse, delta, do, name, hosted=()):
    h, s, dq = q.shape
    dv = v.shape[-1]
    t = min(ATT_TILE, s)
    n = s // t
    scale = dq ** -0.5
    nh = len(hosted)

    def body(*refs):
        q_ref, k_ref, v_ref, lse_ref, delta_ref, do_ref = refs[:6]
        p_refs = refs[6:6 + nh]
        dk_ref, dv_ref = refs[6 + nh:8 + nh]
        b_refs = refs[8 + nh:8 + 2 * nh]
        q_scr, do_scr = refs[8 + 2 * nh:10 + 2 * nh]
        hh, j = pl.program_id(0), pl.program_id(1)
        if nh:
            _hosted_exchange(p_refs, b_refs, refs[-2], refs[-1], jnp.logical_and(hh == 0, j == 0),
                             jnp.logical_and(hh == h - 1, j == n - 1))

        do_src = do_scr if do_ref.dtype != BF16 else do_ref

        @pl.when(j == 0)
        def _():
            q_scr[...] = (q_ref[...].astype(F32) * (scale * LOG2E)).astype(BF16)
            if do_ref.dtype != BF16:
                do_scr[...] = do_ref[...].astype(BF16)

        kb = k_ref[...].astype(BF16)
        vb = v_ref[...].astype(BF16)

        def block(i, carry, masked):
            dk, dvv = carry
            off = pl.multiple_of(i * t, t)
            qi = q_scr[pl.ds(off, t), :]
            doi = do_src[pl.ds(off, t), :]
            sc = lax.dot_general(qi, kb, _NT, preferred_element_type=F32)
            p = jnp.exp2(sc - lse_ref[pl.ds(off, t), :])
            if masked:
                p = jnp.where(_chunk_mask(i * t, j * t, sc.shape), p, 0.0)
            dp = lax.dot_general(doi, vb, _NT, preferred_element_type=F32)
            ds = p * (dp - delta_ref[pl.ds(off, t), :])
            dvv = dvv + lax.dot_general(p.astype(BF16), doi, _TN, preferred_element_type=F32)
            dk = dk + lax.dot_general(ds.astype(BF16), qi, _TN, preferred_element_type=F32)
            return dk, dvv

        carry = block(j, (jnp.zeros((t, dq), F32), jnp.zeros((t, dv), F32)), True)
        dk, dvv = lax.fori_loop(j + 1, n, lambda i, c: block(i, c, False), carry)
        dk_ref[...] = (dk * (1.0 / LOG2E)).astype(dk_ref.dtype)
        dv_ref[...] = dvv.astype(dv_ref.dtype)

    tile = lambda d: pl.BlockSpec((None, t, d), lambda hh, j: (hh, j, 0))
    whole = lambda d: pl.BlockSpec((None, s, d), lambda hh, j: (hh, 0, 0))
    h_in, h_out, h_shapes, h_scratch = _host_args(hosted)
    return pl.pallas_call(
        body,
        name=name,
        grid=(h, n),
        in_specs=[whole(dq), tile(dq), tile(dv), whole(1), whole(1), whole(dv)] + h_in,
        out_specs=[tile(dq), tile(dv)] + h_out,
        out_shape=[jax.ShapeDtypeStruct((h, s, dq), k.dtype), jax.ShapeDtypeStruct((h, s, dv), v.dtype)] + h_shapes,
        scratch_shapes=[pltpu.VMEM((s, dq), BF16), pltpu.VMEM((s, dv), BF16)] + h_scratch,
        compiler_params=_params(("arbitrary", "arbitrary"), has_side_effects=bool(nh)),
    )(q, k, v, lse, delta, do, *hosted)


def attention(q, k, v, tok=(), shards=(), *, name):
    tok, shards = tuple(tok), tuple(shards)
    fwd_name = name + "_fwd" + ("_host" if shards else "")

    @jax.custom_vjp
    def op(q, k, v, tok, shards):
        o, _, *partial = _attention_fwd(q, k, v, fwd_name, shards)
        return o, tok, tuple(partial)

    def fwd(q, k, v, tok, shards):
        o, lse, *partial = _attention_fwd(q, k, v, fwd_name, shards)
        return (o, tok, tuple(partial)), (q, k, v, o, lse, shards)

    def bwd(res, cts):
        q, k, v, o, lse, shards = res
        do, payload, _ = cts
        first = tuple(payload[:HOSTED_IN_DQ])
        rest = tuple(payload[HOSTED_IN_DQ:])
        dq, delta, *got_a = _attention_bwd_dq(q, k, v, o, lse, do, name + "_bwd_dq" + ("_host" if first else ""), first)
        dk, dv, *got_b = _attention_bwd_dkv(q, k, v, lse, delta, do, name + "_bwd_dkv" + ("_host" if rest else ""), rest)
        return dq, dk, dv, tuple(got_a) + tuple(got_b), tuple(jnp.zeros_like(s) for s in shards)

    op.defvjp(fwd, bwd)
    return op(q, k, v, tok, shards)


CONV_HALO = 16


def _conv_tiles(s, c):
    return min(512, s), _pick(c, (512, 256, 128))


def _conv_fwd(x, w, name):
    s, c = x.shape
    ts, tc = _conv_tiles(s, c)
    nb = ts // CONV_HALO

    def body(xc_ref, xp_ref, w_ref, o_ref):
        t = pl.program_id(1)
        prev = jnp.where(t > 0, xp_ref[...].astype(F32), 0.0)
        xe = jnp.concatenate([prev, xc_ref[...].astype(F32)], axis=0)
        wv = w_ref[...]
        acc = jnp.zeros((ts, tc), F32)
        for tap in range(CONV_WIDTH):
            k = CONV_WIDTH - 1 - tap
            sh = xe if k == 0 else pltpu.roll(xe, k, axis=0)
            acc = acc + sh[CONV_HALO:, :] * wv[tap:tap + 1, :]
        o_ref[...] = acc

    return pl.pallas_call(
        body,
        name=name,
        grid=(c // tc, s // ts),
        in_specs=[
            pl.BlockSpec((ts, tc), lambda ci, t: (t, ci)),
            pl.BlockSpec((CONV_HALO, tc), lambda ci, t: (jnp.maximum(t * nb - 1, 0), ci)),
            pl.BlockSpec((CONV_WIDTH, tc), lambda ci, t: (0, ci)),
        ],
        out_specs=pl.BlockSpec((ts, tc), lambda ci, t: (t, ci)),
        out_shape=jax.ShapeDtypeStruct((s, c), F32),
        compiler_params=_params(("parallel", "parallel")),
    )(x, x, w)


def _conv_bwd(x, w, dy, name):
    s, c = x.shape
    ts, tc = _conv_tiles(s, c)
    nb = ts // CONV_HALO
    nt = s // ts

    def body(xc_ref, xp_ref, w_ref, dc_ref, dn_ref, dx_ref, dw_ref):
        t = pl.program_id(1)
        prev = jnp.where(t > 0, xp_ref[...].astype(F32), 0.0)
        xe = jnp.concatenate([prev, xc_ref[...].astype(F32)], axis=0)
        dcur = dc_ref[...]
        nxt = jnp.where(t < nt - 1, dn_ref[...], 0.0)
        de = jnp.concatenate([dcur, nxt], axis=0)
        wv = w_ref[...]
        dx = jnp.zeros((ts, tc), F32)
        dw = jnp.zeros((CONV_WIDTH, tc), F32)
        tap_row = lax.broadcasted_iota(jnp.int32, (CONV_WIDTH, tc), 0)
        for tap in range(CONV_WIDTH):
            k = CONV_WIDTH - 1 - tap
            dsh = de if k == 0 else pltpu.roll(de, ts + CONV_HALO - k, axis=0)
            dx = dx + dsh[:ts, :] * wv[tap:tap + 1, :]
            xsh = xe if k == 0 else pltpu.roll(xe, k, axis=0)
            dwt = jnp.sum(xsh[CONV_HALO:, :] * dcur, axis=0, keepdims=True)
            dw = jnp.where(tap_row == tap, dwt, dw)
        dx_ref[...] = dx.astype(dx_ref.dtype)

        @pl.when(t == 0)
        def _():
            dw_ref[...] = dw

        @pl.when(t > 0)
        def _():
            dw_ref[...] += dw

    return pl.pallas_call(
        body,
        name=name,
        grid=(c // tc, nt),
        in_specs=[
            pl.BlockSpec((ts, tc), lambda ci, t: (t, ci)),
            pl.BlockSpec((CONV_HALO, tc), lambda ci, t: (jnp.maximum(t * nb - 1, 0), ci)),
            pl.BlockSpec((CONV_WIDTH, tc), lambda ci, t: (0, ci)),
            pl.BlockSpec((ts, tc), lambda ci, t: (t, ci)),
            pl.BlockSpec((CONV_HALO, tc), lambda ci, t: (jnp.minimum((t + 1) * nb, s // CONV_HALO - 1), ci)),
        ],
        out_specs=[
            pl.BlockSpec((ts, tc), lambda ci, t: (t, ci)),
            pl.BlockSpec((CONV_WIDTH, tc), lambda ci, t: (0, ci)),
        ],
        out_shape=[jax.ShapeDtypeStruct((s, c), x.dtype), jax.ShapeDtypeStruct((CONV_WIDTH, c), F32)],
        compiler_params=_params(("parallel", "arbitrary")),
    )(x, x, w, dy, dy)


def causal_conv(x, w, *, name):
    @jax.custom_vjp
    def op(x, w):
        return _conv_fwd(x, w, name + "_fwd")

    def fwd(x, w):
        return op(x, w), (x, w)

    def bwd(res, dy):
        x, w = res
        dx, dw = _conv_bwd(x, w, dy, name + "_bwd")
        return dx, dw

    op.defvjp(fwd, bwd)
    return op(x, w)


SSD_T = 128
SSD_R = 8
SSD_GW = SSD_R * SSM_HEADDIM


def _ssd_consts(t):
    r = lax.broadcasted_iota(jnp.int32, (t, t), 0)
    c = lax.broadcasted_iota(jnp.int32, (t, t), 1)
    tril = (c <= r).astype(F32)
    triu = (r <= c).astype(F32)
    head_of_lane = lax.broadcasted_iota(jnp.int32, (SSD_R, SSD_GW), 1) // SSM_HEADDIM
    expand = (head_of_lane == lax.broadcasted_iota(jnp.int32, (SSD_R, SSD_GW), 0)).astype(F32)
    return c <= r, tril, triu, expand


def _three_bf16(v):
    p1 = v.astype(BF16)
    r1 = v - p1.astype(F32)
    p2 = r1.astype(BF16)
    return p1, p2, (r1 - p2.astype(F32)).astype(BF16)


def _sel_first(sel, v):
    s = sel.astype(BF16)
    return sum(jnp.dot(s, p, preferred_element_type=F32) for p in _three_bf16(v))


def _sel_second(v, sel, dn=None):
    s = sel.astype(BF16)
    if dn is None:
        return sum(jnp.dot(p, s, preferred_element_type=F32) for p in _three_bf16(v))
    return sum(lax.dot_general(p, s, dn, preferred_element_type=F32) for p in _three_bf16(v))


def _bdot(a, b, dn=None):
    if dn is None:
        return jnp.dot(a.astype(BF16), b.astype(BF16), preferred_element_type=F32)
    return lax.dot_general(a.astype(BF16), b.astype(BF16), dn, preferred_element_type=F32)


def _ssd_chunk_common(x_ref, b_ref, c_ref, dtc_ref, dtr_ref, alc_ref, alr_ref, t):
    mask, tril, triu, expand = _ssd_consts(t)
    x, bm, cm = x_ref[...], b_ref[...], c_ref[...]
    dtc, dtr = dtc_ref[...], dtr_ref[...]
    neg_a_c = -jnp.exp(alc_ref[...])
    neg_a_r = -jnp.exp(alr_ref[...])
    acum_c = _sel_first(tril, dtc * neg_a_c)
    acum_r = _sel_second(dtr * neg_a_r, triu)
    s_cb = _bdot(cm, bm, _NT)
    return mask, tril, triu, expand, x, bm, cm, dtc, dtr, neg_a_c, neg_a_r, acum_c, acum_r, s_cb


def _head_decay(mask, acum_c, acum_r, h):
    seg = acum_c[:, h:h + 1] - acum_r[h:h + 1, :]
    return jnp.exp(jnp.where(mask, seg, -jnp.inf))


def _ssd_fwd(xbc, dtc, dtr, alc, alr, dexp, name, shards=()):
    s = xbc.shape[0]
    g = SSM_GROUPS
    t = min(SSD_T, s)
    nc = s // t
    n = SSM_STATE
    xblocks = (g * SSD_GW) // n
    ng = len(shards)

    def body(*refs):
        x_ref, b_ref, c_ref, dtc_ref, dtr_ref, alc_ref, alr_ref, d_ref = refs[:8]
        y_ref, hs_ref = refs[8 + ng:10 + ng]
        h_scr = refs[10 + 2 * ng]
        ci = pl.program_id(1)
        if ng:
            gi = pl.program_id(0)
            _hosted_gather(refs[8:8 + ng], refs[10 + ng:10 + 2 * ng], refs[-2], refs[-1],
                           jnp.logical_and(gi == 0, ci == 0), jnp.logical_and(gi == g - 1, ci == nc - 1))

        @pl.when(ci == 0)
        def _():
            h_scr[...] = jnp.zeros_like(h_scr)

        (mask, tril, triu, expand, x, bm, cm, dtc_v, dtr_v, _, _, acum_c, acum_r, s_cb) = _ssd_chunk_common(
            x_ref, b_ref, c_ref, dtc_ref, dtr_ref, alc_ref, alr_ref, t)
        hst = h_scr[...]
        hs_ref[...] = hst
        ch = _bdot(cm, hst)
        y = _sel_second(jnp.exp(acum_c), expand) * ch + d_ref[...] * x
        half = lax.broadcasted_iota(jnp.int32, (t, 2 * SSM_HEADDIM), 1) // SSM_HEADDIM
        parts = []
        for j in range(SSD_R // 2):
            xp = x[:, j * 128:(j + 1) * 128]
            acc = jnp.zeros((t, 128), F32)
            for hh in range(2):
                h = 2 * j + hh
                m = s_cb * _head_decay(mask, acum_c, acum_r, h) * dtr_v[h:h + 1, :]
                acc = acc + _bdot(m, jnp.where(half == hh, xp, 0.0))
            parts.append(acc)
        y_ref[...] = y + jnp.concatenate(parts, axis=1)
        last = acum_c[t - 1:t, :]
        w_c = jnp.exp(last - acum_c) * dtc_v
        dec = _sel_second(jnp.broadcast_to(jnp.exp(last), (SSD_R, SSD_R)), expand)[0:1, :]
        h_scr[...] = dec * hst + _bdot(bm, _sel_second(w_c, expand) * x, _TN)

    g_in, g_out, g_shapes, g_scratch = _gather_host_args(shards)
    return pl.pallas_call(
        body,
        name=name,
        grid=(g, nc),
        in_specs=[
            pl.BlockSpec((t, SSD_GW), lambda gi, ci: (ci, gi)),
            pl.BlockSpec((t, n), lambda gi, ci: (ci, xblocks + gi)),
            pl.BlockSpec((t, n), lambda gi, ci: (ci, xblocks + g + gi)),
            pl.BlockSpec((None, t, SSD_R), lambda gi, ci: (gi, ci, 0)),
            pl.BlockSpec((None, SSD_R, t), lambda gi, ci: (gi, 0, ci)),
            pl.BlockSpec((None, 1, SSD_R), lambda gi, ci: (gi, 0, 0)),
            pl.BlockSpec((None, SSD_R, 1), lambda gi, ci: (gi, 0, 0)),
            pl.BlockSpec((None, 1, SSD_GW), lambda gi, ci: (gi, 0, 0)),
        ] + g_in,
        out_specs=[
            pl.BlockSpec((t, SSD_GW), lambda gi, ci: (ci, gi)),
            pl.BlockSpec((None, None, n, SSD_GW), lambda gi, ci: (gi, ci, 0, 0)),
        ] + g_out,
        out_shape=[jax.ShapeDtypeStruct((s, g * SSD_GW), F32), jax.ShapeDtypeStruct((g, nc, n, SSD_GW), F32)] + g_shapes,
        scratch_shapes=[pltpu.VMEM((n, SSD_GW), F32)] + g_scratch,
        compiler_params=_params(("arbitrary" if ng else "parallel", "arbitrary"), has_side_effects=bool(ng)),
    )(xbc, xbc, xbc, dtc, dtr, alc, alr, dexp, *shards)


def _hosted_pair_swap(g_refs, a_refs, send_sems, recv_sems, first, last):
    x, y, c = _place()
    n = len(g_refs)

    def copies():
        out = []
        for t in range(n):
            half = g_refs[t].shape[1] // 2
            out.append(_rcopy(g_refs[t].at[:, pl.ds((1 - c) * half, half), :], a_refs[t], send_sems.at[t, 0],
                              recv_sems.at[t, 0], (x, y, 1 - c)))
        return out

    @pl.when(first)
    def _():
        for cp in copies():
            cp.start()

    @pl.when(last)
    def _():
        for cp in copies():
            cp.wait()


def _ssd_bwd(xbc, dtc, dtr, alc, alr, dexp, hs, dy, name, hosted=()):
    s = xbc.shape[0]
    g = SSM_GROUPS
    t = min(SSD_T, s)
    nc = s // t
    n = SSM_STATE
    xblocks = (g * SSD_GW) // n
    nh = len(hosted)

    def body(*refs):
        x_ref, b_ref, c_ref, dtc_ref, dtr_ref, alc_ref, alr_ref, d_ref, hs_ref, dy_ref = refs[:10]
        dx_ref, db_ref, dc_ref, ddtc_ref, ddtr_ref, dalc_ref, dalr_ref, dd_ref = refs[10 + nh:18 + nh]
        dh_scr = refs[18 + 2 * nh]
        ci = pl.program_id(1)
        if nh:
            gi = pl.program_id(0)
            _hosted_pair_swap(refs[10:10 + nh], refs[18 + nh:18 + 2 * nh], refs[-2], refs[-1],
                              jnp.logical_and(gi == 0, ci == 0), jnp.logical_and(gi == g - 1, ci == nc - 1))

        @pl.when(ci == 0)
        def _():
            dh_scr[...] = jnp.zeros_like(dh_scr)

        (mask, tril, triu, expand, x, bm, cm, dtc_v, dtr_v, neg_a_c, neg_a_r, acum_c, acum_r, s_cb) = _ssd_chunk_common(
            x_ref, b_ref, c_ref, dtc_ref, dtr_ref, alc_ref, alr_ref, t)
        hst = hs_ref[...]
        dhn = dh_scr[...]
        dy = dy_ref[...]
        ch = _bdot(cm, hst)
        scale_full = _sel_second(jnp.exp(acum_c), expand)
        sdy = scale_full * dy
        d_c = _bdot(sdy, hst, _NT)
        dh_prev = _bdot(cm, sdy, _TN)
        dacum_c = _sel_second(sdy * ch, expand, _NT)
        dx = d_ref[...] * dy
        dd = jnp.sum(dy * x, axis=0, keepdims=True)
        last = acum_c[t - 1:t, :]
        e_last = jnp.exp(last)
        dec = _sel_second(jnp.broadcast_to(e_last, (SSD_R, SSD_R)), expand)[0:1, :]
        dh_prev = dh_prev + dec * dhn
        ddec = jnp.sum(hst * dhn, axis=0, keepdims=True)
        dlast = _sel_second(jnp.broadcast_to(ddec, (SSD_R, SSD_GW)), expand, _NT)[0:1, :] * e_last
        w_e = jnp.exp(last - acum_c)
        w_c = w_e * dtc_v
        wfull = _sel_second(w_c, expand)
        z = _bdot(bm, dhn)
        dx = dx + wfull * z
        dw_c = _sel_second(x * z, expand, _NT)
        ddt_c = dw_c * w_e
        q_c = dw_c * w_c
        dacum_c = dacum_c - q_c
        dlast = dlast + jnp.sum(q_c, axis=0, keepdims=True)
        d_b = _bdot(wfull * x, dhn, _NT)
        half = lax.broadcasted_iota(jnp.int32, (t, 2 * SSM_HEADDIM), 1) // SSM_HEADDIM
        lane8 = lax.broadcasted_iota(jnp.int32, (t, SSD_R), 1)
        row8 = lax.broadcasted_iota(jnp.int32, (SSD_R, t), 0)
        ds_cb = jnp.zeros((t, t), F32)
        dacum_r = jnp.zeros((SSD_R, t), F32)
        ddt_r = jnp.zeros((SSD_R, t), F32)
        parts = []
        for j in range(SSD_R // 2):
            xp = x[:, j * 128:(j + 1) * 128]
            dyp = dy[:, j * 128:(j + 1) * 128]
            dxp = jnp.zeros((t, 128), F32)
            for hh in range(2):
                h = 2 * j + hh
                dts = dtr_v[h:h + 1, :]
                decay = _head_decay(mask, acum_c, acum_r, h)
                sl = s_cb * decay
                m = sl * dts
                xm = jnp.where(half == hh, xp, 0.0)
                dym = jnp.where(half == hh, dyp, 0.0)
                dxp = dxp + _bdot(m, dym, _TN)
                dm = _bdot(dym, xm, _NT)
                ds_cb = ds_cb + dm * decay * dts
                q = dm * m
                dacum_c = dacum_c + jnp.where(lane8 == h, jnp.sum(q, axis=1, keepdims=True), 0.0)
                dacum_r = dacum_r - jnp.where(row8 == h, jnp.sum(q, axis=0, keepdims=True), 0.0)
                ddt_r = ddt_r + jnp.where(row8 == h, jnp.sum(dm * sl, axis=0, keepdims=True), 0.0)
            parts.append(dxp)
        dx_ref[...] = dx + jnp.concatenate(parts, axis=1)
        dc_ref[...] = d_c + _bdot(ds_cb, bm)
        db_ref[...] = d_b + _bdot(ds_cb, cm, _TN)
        row_t = lax.broadcasted_iota(jnp.int32, (t, SSD_R), 0)
        dacum_c = dacum_c + jnp.where(row_t == t - 1, dlast, 0.0)
        da_c = _sel_first(triu, dacum_c)
        da_r = _sel_second(dacum_r, tril)
        ddtc_ref[...] = ddt_c + da_c * neg_a_c
        ddtr_ref[...] = ddt_r + da_r * neg_a_r
        dal_c = jnp.sum(da_c * dtc_v, axis=0, keepdims=True) * neg_a_c
        dal_r = jnp.sum(da_r * dtr_v, axis=1, keepdims=True) * neg_a_r
        dh_scr[...] = dh_prev

        @pl.when(ci == 0)
        def _():
            dalc_ref[...] = dal_c
            dalr_ref[...] = dal_r
            dd_ref[...] = dd

        @pl.when(ci > 0)
        def _():
            dalc_ref[...] += dal_c
            dalr_ref[...] += dal_r
            dd_ref[...] += dd

    rev = lambda ci: nc - 1 - ci
    any_spec = pl.BlockSpec(memory_space=pl.ANY)
    a_shapes = [jax.ShapeDtypeStruct((gr.shape[0], gr.shape[1] // 2, gr.shape[2]), gr.dtype) for gr in hosted]
    h_scratch = [pltpu.SemaphoreType.DMA((nh, 1)), pltpu.SemaphoreType.DMA((nh, 1))] if nh else []
    return pl.pallas_call(
        body,
        name=name,
        grid=(g, nc),
        in_specs=[
            pl.BlockSpec((t, SSD_GW), lambda gi, ci: (rev(ci), gi)),
            pl.BlockSpec((t, n), lambda gi, ci: (rev(ci), xblocks + gi)),
            pl.BlockSpec((t, n), lambda gi, ci: (rev(ci), xblocks + g + gi)),
            pl.BlockSpec((None, t, SSD_R), lambda gi, ci: (gi, rev(ci), 0)),
            pl.BlockSpec((None, SSD_R, t), lambda gi, ci: (gi, 0, rev(ci))),
            pl.BlockSpec((None, 1, SSD_R), lambda gi, ci: (gi, 0, 0)),
            pl.BlockSpec((None, SSD_R, 1), lambda gi, ci: (gi, 0, 0)),
            pl.BlockSpec((None, 1, SSD_GW), lambda gi, ci: (gi, 0, 0)),
            pl.BlockSpec((None, None, n, SSD_GW), lambda gi, ci: (gi, rev(ci), 0, 0)),
            pl.BlockSpec((t, SSD_GW), lambda gi, ci: (rev(ci), gi)),
        ] + [any_spec] * nh,
        out_specs=[
            pl.BlockSpec((t, SSD_GW), lambda gi, ci: (rev(ci), gi)),
            pl.BlockSpec((t, n), lambda gi, ci: (rev(ci), gi)),
            pl.BlockSpec((t, n), lambda gi, ci: (rev(ci), gi)),
            pl.BlockSpec((None, t, SSD_R), lambda gi, ci: (gi, rev(ci), 0)),
            pl.BlockSpec((None, SSD_R, t), lambda gi, ci: (gi, 0, rev(ci))),
            pl.BlockSpec((None, 1, SSD_R), lambda gi, ci: (gi, 0, 0)),
            pl.BlockSpec((None, SSD_R, 1), lambda gi, ci: (gi, 0, 0)),
            pl.BlockSpec((None, 1, SSD_GW), lambda gi, ci: (gi, 0, 0)),
        ] + [any_spec] * nh,
        out_shape=[
            jax.ShapeDtypeStruct((s, g * SSD_GW), F32),
            jax.ShapeDtypeStruct((s, g * n), F32),
            jax.ShapeDtypeStruct((s, g * n), F32),
            jax.ShapeDtypeStruct((g, s, SSD_R), F32),
            jax.ShapeDtypeStruct((g, SSD_R, s), F32),
            jax.ShapeDtypeStruct((g, 1, SSD_R), F32),
            jax.ShapeDtypeStruct((g, SSD_R, 1), F32),
            jax.ShapeDtypeStruct((g, 1, SSD_GW), F32),
        ] + a_shapes,
        scratch_shapes=[pltpu.VMEM((n, SSD_GW), F32)] + h_scratch,
        compiler_params=_params(("arbitrary" if nh else "parallel", "arbitrary"), has_side_effects=bool(nh)),
    )(xbc, xbc, xbc, dtc, dtr, alc, alr, dexp, hs, dy, *hosted)


def ssd_core(xbc, dtc, dtr, alc, alr, dexp, shards=(), tok_p=(), tok_a=(), *, name):
    shards, tok_p, tok_a = tuple(shards), tuple(tok_p), tuple(tok_a)
    fwd_name = name + "_fwd" + ("_host" if shards else "")

    def tok_g(tok_a):
        return tuple(jnp.zeros((a.shape[0], 2 * a.shape[1], a.shape[2]), a.dtype) for a in tok_a)

    @jax.custom_vjp
    def op(xbc, dtc, dtr, alc, alr, dexp, shards, tok_p, tok_a):
        y, _, *partial = _ssd_fwd(xbc, dtc, dtr, alc, alr, dexp, fwd_name, shards)
        return y, tuple(partial), tok_g(tok_a)

    def fwd(xbc, dtc, dtr, alc, alr, dexp, shards, tok_p, tok_a):
        y, hs, *partial = _ssd_fwd(xbc, dtc, dtr, alc, alr, dexp, fwd_name, shards)
        return (y, tuple(partial), tok_g(tok_a)), (xbc, dtc, dtr, alc, alr, dexp, hs, shards)

    def bwd(res, cts):
        xbc, dtc, dtr, alc, alr, dexp, hs, shards = res
        dy, _, gs = cts
        gs = tuple(gs)
        dx, db, dc, ddtc, ddtr, dalc, dalr, dd, *a = _ssd_bwd(xbc, dtc, dtr, alc, alr, dexp, hs, dy,
                                                              name + "_bwd" + ("_host" if gs else ""), gs)
        p = tuple(pair_add(g, ai, name="reduce_grads_pair_add") for g, ai in zip(gs, a))
        return (jnp.concatenate([dx, db, dc], axis=1), ddtc, ddtr, dalc, dalr, dd,
                tuple(jnp.zeros_like(s) for s in shards), p, tuple(a))

    op.defvjp(fwd, bwd)
    return op(xbc, dtc, dtr, alc, alr, dexp, shards, tok_p, tok_a)


def gate_norm(y, z, w):
    z = z.astype(F32)
    return (rms_norm(y * (z * jax.nn.sigmoid(z)), w).astype(BF16),)


def ssd_branch(xbc, z, dt_raw, conv_w, conv_b, dt_bias, a_log, d_skip, norm_w, shards=(), tok_p=(), tok_a=(), *, name):
    s = xbc.shape[0]
    g = SSM_GROUPS
    conv = causal_conv(xbc, conv_w, name=name + "_conv")
    (xc,) = rowwise(lambda c, b: ((c + b) * jax.nn.sigmoid(c + b),), (conv,), (), (conv_b[None, :],),
                    name=name + "_silu", tile=min(256, s))
    (dt,) = rowwise(lambda r, b: (jax.nn.softplus(r.astype(F32) + b),), (dt_raw,), (), (dt_bias[None, :],),
                    name=name + "_dt", tile=min(512, s))
    dt3 = dt.reshape(s, g, SSD_R)
    y, partial, tok_g = ssd_core(xc, dt3.transpose(1, 0, 2), dt3.transpose(1, 2, 0), a_log.reshape(g, 1, SSD_R),
                                 a_log.reshape(g, SSD_R, 1), jnp.repeat(d_skip, SSM_HEADDIM).reshape(g, 1, SSD_GW),
                                 shards, tok_p, tok_a, name=name + "_core")
    (out,) = rowwise(gate_norm, (y.reshape(s, g, SSD_GW), z.reshape(s, g, SSD_GW)), (),
                     (norm_w.reshape(g, SSD_GW),), name=name + "_gate", tile=min(128, s))
    return out.reshape(s, g * SSD_GW), partial, tok_g


def rms_norm(x, w):
    return x * lax.rsqrt(jnp.mean(x * x, axis=-1, keepdims=True) + EPS) * w


def rope_matrix():
    half = QK_ROPE // 2
    j = jnp.arange(QK_DIM)
    src = jnp.where(j < QK_NOPE + half, j + half, j - half)
    sign = jnp.where(j < QK_NOPE, 0.0, jnp.where(j < QK_NOPE + half, -1.0, 1.0))
    return (jnp.arange(QK_DIM)[:, None] == src[None, :]).astype(F32) * sign[None, :]


def rope_tables_full(positions):
    inv_freq = 1.0 / (ROPE_THETA ** (jnp.arange(0, QK_ROPE, 2, dtype=F32) / QK_ROPE))
    ang = positions.astype(F32)[:, None] * inv_freq
    s = positions.shape[0]
    cos = jnp.concatenate([jnp.ones((s, QK_NOPE), F32), jnp.cos(ang), jnp.cos(ang)], axis=-1)
    sin = jnp.concatenate([jnp.zeros((s, QK_NOPE), F32), jnp.sin(ang), jnp.sin(ang)], axis=-1)
    return cos[:, None, :], sin[:, None, :]


def head_norm_rope(x, cos_full, sin_full, rot, w):
    t, h, d = x.shape
    y = rms_norm(x.astype(F32), w)
    partner = jnp.dot(y.reshape(t * h, d), rot, precision=HI, preferred_element_type=F32).reshape(t, h, d)
    return ((y * cos_full + partner * sin_full).astype(BF16),)


def _norm(x, w, *, name, out_dtype=BF16, tile=256):
    (y,) = rowwise(lambda x, w: (rms_norm(x.astype(F32), w).astype(out_dtype),), (x,), (), (w[None, :],), name=name,
                   tile=min(tile, x.shape[0]))
    return y


Q_LORA = 512
KV_LORA = 512
W_IN_PIECES = ("cq", "ckv", "kr", "z", "xbc", "dt", "ga", "gb")


def w_in_widths(d_model):
    d_inner = 2 * d_model
    conv_dim = d_inner + 2 * SSM_GROUPS * SSM_STATE
    return (Q_LORA, KV_LORA, QK_ROPE, d_inner, conv_dim, d_inner // SSM_HEADDIM, d_model, d_model)


LINEAR_NAMES = ("cq", "ckv", "kd", "z", "xbc", "ga", "gb", "w_uq", "w_ukv", "w_o_mla", "w_o_ssm", "w_out", "w_up",
                "w_down", "w_ple_gate", "w_ple")


GATHER_IN_ATTENTION = (0, 6)


def layer_forward(x, p_i, cos_full, sin_full, rot, wb, sm, sinks, tok=(), tok_a=(), next_shards=()):
    s, d = x.shape
    in_att = [t for t in range(len(next_shards)) if t in GATHER_IN_ATTENTION]
    in_ssd = [t for t in range(len(next_shards)) if t not in GATHER_IN_ATTENTION]
    lin = lambda a, n: linear(a, wb[n], sinks[n], name="lin_" + n)
    up32 = lambda a: a.astype(F32)
    h = _norm(x, sm["norm_mix_w"], name="norm_mix", out_dtype=F32)
    c_q, c_kv, kd = lin(h, "cq"), lin(h, "ckv"), lin(h, "kd")
    z, xbc, g_a, g_b = lin(h, "z"), lin(h, "xbc"), lin(h, "ga"), lin(h, "gb")
    k_r, dt_raw = kd[:, :QK_ROPE], kd[:, QK_ROPE:]
    q = lin(_norm(c_q, sm["q_a_norm_w"], name="norm_qa"), "w_uq").reshape(s, MLA_HEADS, QK_DIM)
    kv = lin(_norm(c_kv, sm["kv_a_norm_w"], name="norm_kva"), "w_ukv").reshape(s, MLA_HEADS, QK_NOPE + V_DIM)
    k = jnp.concatenate([kv[..., :QK_NOPE], jnp.broadcast_to(k_r[:, None, :], (s, MLA_HEADS, QK_ROPE))], axis=-1)
    v = kv[..., QK_NOPE:]
    tq = min(128, s)
    (q,) = rowwise(head_norm_rope, (q,), (cos_full, sin_full), (sm["q_norm_w"][None, :],), tables=(rot,), name="q_rope", tile=tq)
    (k,) = rowwise(head_norm_rope, (k,), (cos_full, sin_full), (sm["k_norm_w"][None, :],), tables=(rot,), name="k_rope", tile=tq)
    hm = lambda a: a.transpose(1, 0, 2)
    o, tok, part_att = attention(hm(q), hm(k), hm(v), tok, [next_shards[t] for t in in_att], name="attn")
    o = hm(o).reshape(s, MLA_HEADS * V_DIM)
    y_a = lin(o, "w_o_mla")
    y_ssd, part_ssd, tok_g = ssd_branch(xbc, z, dt_raw, sm["conv_w"], sm["conv_b"], sm["dt_bias"], sm["a_log"],
                                        sm["d_skip"], sm["ssm_norm_w"], [next_shards[t] for t in in_ssd], tok, tok_a,
                                        name="ssd")
    y_b = lin(y_ssd, "w_o_ssm")
    partial = [None] * len(next_shards)
    for t, buf in list(zip(in_att, part_att)) + list(zip(in_ssd, part_ssd)):
        partial[t] = buf
    sig = jax.nn.sigmoid
    tr = min(256, s)
    (merged,) = rowwise(lambda ga, gb, ya, yb: ((sig(up32(ga)) * up32(ya) + sig(up32(gb)) * up32(yb)).astype(BF16),),
                        (g_a, g_b, y_a, y_b), (), (), name="merge", tile=tr)
    mixed = lin(merged, "w_out")
    (x,) = rowwise(lambda x, m: (x + up32(m),), (x, mixed), (), (), name="add_mix", tile=tr)
    up = lin(_norm(x, sm["norm_mlp_w"], name="norm_mlp"), "w_up")
    (act,) = rowwise(lambda u: (jnp.square(jnp.maximum(up32(u), 0.0)).astype(BF16),), (up,), (), (), name="relu2", tile=tr)
    down = lin(act, "w_down")
    (x,) = rowwise(lambda x, m: (x + up32(m),), (x, down), (), (), name="add_mlp", tile=tr)
    pg = lin(_norm(x, sm["ple_norm_w"], name="norm_ple"), "w_ple_gate")
    pe = lin(p_i, "w_ple")
    (x,) = rowwise(lambda x, pe, pg: (x + up32(pe) * sig(up32(pg)),), (x, pe, pg), (), (), name="ple_add", tile=tr)
    return (x, tok_g), partial


def loss_and_cotangent(y, target):
    s, d = y.shape

    def f(y, t):
        e = y - t
        return e * (1.0 / d), 0.5 * jnp.sum(jnp.sum(e * e, axis=1, keepdims=True) * (1.0 / d), axis=0, keepdims=True)

    dy, part = _tiled_call(f, (y, target), (), min(256, s), "loss", 1)
    return dy, part[0, 0]


ADAM_BLOCK_ELEMS = 256 * 1024


def adamw(w, g, m, v, *, name):
    rows, cols = w.shape
    budget = max(8, ADAM_BLOCK_ELEMS // cols)
    tile = _pick(rows, tuple(t for t in (512, 256, 128, 64, 32, 16, 8) if t <= budget))

    def f(w, g, m, v):
        m = ADAM_B1 * m + (1.0 - ADAM_B1) * g
        v = ADAM_B2 * v + (1.0 - ADAM_B2) * jnp.square(g)
        m_hat = m / (1.0 - ADAM_B1 ** ADAM_STEP)
        v_hat = v / (1.0 - ADAM_B2 ** ADAM_STEP)
        delta = -ADAM_LR * (m_hat / (jnp.sqrt(v_hat) + ADAM_EPS) + ADAM_WD * w)
        return delta, m, v

    return _tiled_call(f, (w, g, m, v), (), tile, name, 0)


MESH_ID = pl.DeviceIdType.MESH
N_CHIPS = 4
_ANY = pl.BlockSpec(memory_space=pl.ANY)


def _place():
    return lax.axis_index("x"), lax.axis_index("y"), lax.axis_index("c")


def _other_chips(x, y):
    return [(1 - x, y), (x, 1 - y), (1 - x, 1 - y)]


def _rcopy(src, dst, send_sem, recv_sem, device):
    return pltpu.make_async_remote_copy(src_ref=src, dst_ref=dst, send_sem=send_sem, recv_sem=recv_sem,
                                        device_id=device, device_id_type=MESH_ID)


def _sems(n, k):
    return pltpu.SemaphoreType.DMA((n, k))


def _comm_call(body, name, ins, out_shapes, scratch, aliases=None):
    return pl.pallas_call(
        body,
        name=name,
        in_specs=[_ANY] * len(ins),
        out_specs=[_ANY] * len(out_shapes),
        out_shape=out_shapes,
        scratch_shapes=scratch,
        input_output_aliases=aliases or {},
        compiler_params=pltpu.CompilerParams(has_side_effects=True),
    )(*ins)


SPLIT_ROWS = 32


def gather_shards(shards, *, name):
    n = len(shards)
    split = [s.shape[0] % SPLIT_ROWS == 0 for s in shards]

    def body(*refs):
        srcs, outs = refs[:n], refs[n:2 * n]
        send_sems, recv_sems = refs[2 * n:]
        x, y, c = _place()
        sibling = (x, y, 1 - c)
        chips = _other_chips(x, y)
        me = 2 * x + y

        def part(t, slot, h):
            if not split[t]:
                return outs[t].at[slot]
            half = shards[t].shape[0] // 2
            return outs[t].at[slot, pl.ds(h * half, half), :]

        def own(t):
            if not split[t]:
                return srcs[t]
            half = shards[t].shape[0] // 2
            return srcs[t].at[pl.ds(c * half, half), :]

        sent = []
        for t in range(n):
            for j, chip in enumerate(chips):
                sent.append(_rcopy(own(t), part(t, me, c), send_sems.at[t, j], recv_sems.at[t, j], (*chip, c)))
                sent[-1].start()
        for t in range(n):
            sent.append(_rcopy(srcs[t], outs[t].at[me], send_sems.at[t, 6], recv_sems.at[t, 6], sibling))
            sent[-1].start()
        for t in range(n):
            for j, (cx, cy) in enumerate(chips):
                got = part(t, 2 * cx + cy, c)
                _rcopy(got, got, send_sems.at[t, j], recv_sems.at[t, j], (cx, cy, c)).wait_recv()
                if split[t]:
                    sent.append(_rcopy(got, got, send_sems.at[t, 3 + j], recv_sems.at[t, 3 + j], sibling))
                    sent[-1].start()
        for t in range(n):
            if split[t]:
                for j, (cx, cy) in enumerate(chips):
                    got = part(t, 2 * cx + cy, 1 - c)
                    _rcopy(got, got, send_sems.at[t, 3 + j], recv_sems.at[t, 3 + j], sibling).wait_recv()
        for t in range(n):
            _rcopy(srcs[t], outs[t].at[me], send_sems.at[t, 6], recv_sems.at[t, 6], sibling).wait_recv()
        for cp in sent:
            cp.wait_send()

    out_shapes = [jax.ShapeDtypeStruct((N_CHIPS,) + s.shape, s.dtype) for s in shards]
    return _comm_call(body, name, shards, out_shapes, [_sems(n, 7), _sems(n, 7)])


def gather_finish(partial, *, name):
    partial = list(partial)
    todo = [t for t, p in enumerate(partial) if _split_rows(p.shape[1:])]
    n = len(todo)

    def body(*refs):
        outs = refs[n:2 * n]
        send_sems, recv_sems = refs[2 * n:]
        x, y, c = _place()
        sibling = (x, y, 1 - c)
        chips = _other_chips(x, y)

        def part(k, slot, h):
            half = partial[todo[k]].shape[1] // 2
            return outs[k].at[slot, pl.ds(h * half, half), :]

        sent = []
        for k in range(n):
            for j, (cx, cy) in enumerate(chips):
                got = part(k, 2 * cx + cy, c)
                sent.append(_rcopy(got, got, send_sems.at[k, j], recv_sems.at[k, j], sibling))
                sent[-1].start()
        for k in range(n):
            for j, (cx, cy) in enumerate(chips):
                got = part(k, 2 * cx + cy, 1 - c)
                _rcopy(got, got, send_sems.at[k, j], recv_sems.at[k, j], sibling).wait_recv()
        for cp in sent:
            cp.wait_send()

    ins = [partial[t] for t in todo]
    done = _comm_call(body, name, ins, [jax.ShapeDtypeStruct(p.shape, p.dtype) for p in ins], [_sems(n, 3), _sems(n, 3)],
                      aliases={k: k for k in range(n)})
    for k, t in enumerate(todo):
        partial[t] = done[k]
    return partial


def sibling_take_half(gs, *, name):
    n = len(gs)

    def body(*refs):
        g_refs, a_refs = refs[:n], refs[n:2 * n]
        send_sems, recv_sems = refs[2 * n:]
        x, y, c = _place()
        copies = []
        for t in range(n):
            half = gs[t].shape[1] // 2
            copies.append(_rcopy(g_refs[t].at[:, pl.ds((1 - c) * half, half), :], a_refs[t], send_sems.at[t, 0],
                                 recv_sems.at[t, 0], (x, y, 1 - c)))
            copies[-1].start()
        for cp in copies:
            cp.wait()

    out_shapes = [jax.ShapeDtypeStruct((g.shape[0], g.shape[1] // 2, g.shape[2]), g.dtype) for g in gs]
    return _comm_call(body, name, gs, out_shapes, [_sems(n, 1), _sems(n, 1)])


ELEMWISE_BLOCK_ELEMS = 256 * 1024


def _row_tile(rows, cols):
    budget = max(16, ELEMWISE_BLOCK_ELEMS // cols)
    return _pick(rows, tuple(t for t in (1024, 512, 256, 128, 64, 32, 16) if t <= budget))


def _core_and_chip():
    x, y, c = _place()
    return jnp.stack([c, 2 * x + y]).astype(jnp.int32)


def pair_add(g, a, *, name):
    n, rows, cols = g.shape
    half = rows // 2
    tile = _row_tile(half, cols)
    nb = half // tile

    def body(who_ref, g_ref, a_ref, o_ref):
        o_ref[...] = (g_ref[...] + a_ref[...]).astype(o_ref.dtype)

    return pl.pallas_call(
        body,
        name=name,
        grid_spec=pltpu.PrefetchScalarGridSpec(
            num_scalar_prefetch=1,
            grid=(n, nb),
            in_specs=[
                pl.BlockSpec((None, tile, cols), lambda j, i, who: (j, who[0] * nb + i, 0)),
                pl.BlockSpec((None, tile, cols), lambda j, i, who: (j, i, 0)),
            ],
            out_specs=pl.BlockSpec((None, tile, cols), lambda j, i, who: (j, i, 0)),
        ),
        out_shape=jax.ShapeDtypeStruct((n, half, cols), BF16),
        compiler_params=_params(("parallel", "parallel")),
    )(_core_and_chip(), g, a)


def exchange_chip_slots(ps, *, name):
    n = len(ps)

    def body(*refs):
        p_refs, b_refs = refs[:n], refs[n:2 * n]
        send_sems, recv_sems = refs[2 * n:]
        x, y, c = _place()
        me = 2 * x + y
        chips = _other_chips(x, y)
        sends = []
        for t in range(n):
            for j, (cx, cy) in enumerate(chips):
                sends.append(_rcopy(p_refs[t].at[2 * cx + cy], b_refs[t].at[me], send_sems.at[t, j], recv_sems.at[t, j],
                                    (cx, cy, c)))
                sends[-1].start()
        for t in range(n):
            for j, (cx, cy) in enumerate(chips):
                got = b_refs[t].at[2 * cx + cy]
                _rcopy(got, got, send_sems.at[t, j], recv_sems.at[t, j], (cx, cy, c)).wait_recv()
        for cp in sends:
            cp.wait_send()

    out_shapes = [jax.ShapeDtypeStruct(p.shape, p.dtype) for p in ps]
    return _comm_call(body, name, ps, out_shapes, [_sems(n, 3), _sems(n, 3)])


def chips_add(g, a, b, *, name):
    n, rows, cols = g.shape
    half = rows // 2
    tile = _row_tile(half, cols)
    nb = half // tile

    def body(who_ref, g_ref, a_ref, *rest):
        o_ref = rest[-1]
        acc = g_ref[...] + a_ref[...]
        for b_ref in rest[:-1]:
            acc = acc + b_ref[...].astype(F32)
        o_ref[...] = acc

    other = lambda k: pl.BlockSpec((None, tile, cols), lambda i, who, k=k: ((who[1] + k) % n, i, 0))
    return pl.pallas_call(
        body,
        name=name,
        grid_spec=pltpu.PrefetchScalarGridSpec(
            num_scalar_prefetch=1,
            grid=(nb,),
            in_specs=[
                pl.BlockSpec((None, tile, cols), lambda i, who: (who[1], who[0] * nb + i, 0)),
                pl.BlockSpec((None, tile, cols), lambda i, who: (who[1], i, 0)),
            ] + [other(k) for k in range(1, n)],
            out_specs=pl.BlockSpec((tile, cols), lambda i, who: (who[0] * nb + i, 0)),
        ),
        out_shape=jax.ShapeDtypeStruct((rows, cols), F32),
        compiler_params=_params(("parallel",)),
    )(_core_and_chip(), g, a, *([b] * (n - 1)))


def sibling_join_halves(rs, *, name):
    n = len(rs)

    def body(*refs):
        r_refs = refs[n:2 * n]
        send_sems, recv_sems = refs[2 * n:]
        x, y, c = _place()
        sibling = (x, y, 1 - c)
        copies = []
        for t in range(n):
            half = rs[t].shape[0] // 2
            mine = r_refs[t].at[pl.ds(c * half, half), :]
            copies.append(_rcopy(mine, mine, send_sems.at[t, 0], recv_sems.at[t, 0], sibling))
            copies[-1].start()
        for t in range(n):
            half = rs[t].shape[0] // 2
            got = r_refs[t].at[pl.ds((1 - c) * half, half), :]
            _rcopy(got, got, send_sems.at[t, 0], recv_sems.at[t, 0], sibling).wait_recv()
        for cp in copies:
            cp.wait_send()

    out_shapes = [jax.ShapeDtypeStruct(r.shape, r.dtype) for r in rs]
    return _comm_call(body, name, rs, out_shapes, [_sems(n, 1), _sems(n, 1)], aliases={t: t for t in range(n)})


def reduce_to_owner(gs, *, name):
    gs, a, p = reduce_pairs(gs, name=name)
    return reduce_finish(gs, a, exchange_chip_slots(p, name=name + "_chips"), name=name)


def reduce_pairs(gs, *, name):
    gs = list(gs)
    a = sibling_take_half(gs, name=name + "_pair")
    p = [pair_add(g, ai, name=name + "_pair_add") for g, ai in zip(gs, a)]
    return gs, a, p


def reduce_finish(gs, a, b, *, name):
    f = [chips_add(g, ai, bi, name=name + "_chips_add") for g, ai, bi in zip(gs, a, b)]
    return sibling_join_halves(f, name=name + "_join")


def allreduce_small(v, *, name):
    rows, cols = v.shape

    def body(v_ref, o_ref, buf, send_sems, recv_sems):
        x, y, c = _place()
        me = 4 * x + 2 * y + c
        buf[me] = v_ref[...]
        copies = []
        for k in range(1, 8):
            bx, by, bc = (k >> 2) & 1, (k >> 1) & 1, k & 1
            peer = (x if bx == 0 else 1 - x, y if by == 0 else 1 - y, c if bc == 0 else 1 - c)
            copies.append(_rcopy(v_ref, buf.at[me], send_sems.at[k - 1], recv_sems.at[k - 1], peer))
        for cp in copies:
            cp.start()
        for k in range(1, 8):
            bx, by, bc = (k >> 2) & 1, (k >> 1) & 1, k & 1
            px, py, pc = (x if bx == 0 else 1 - x, y if by == 0 else 1 - y, c if bc == 0 else 1 - c)
            _rcopy(v_ref, buf.at[4 * px + 2 * py + pc], send_sems.at[k - 1], recv_sems.at[k - 1], (px, py, pc)).wait_recv()
        for cp in copies:
            cp.wait_send()
        acc = buf[0]
        for j in range(1, 8):
            acc = acc + buf[j]
        o_ref[...] = acc

    return pl.pallas_call(
        body,
        name=name,
        in_specs=[pl.BlockSpec(memory_space=pltpu.VMEM)],
        out_specs=pl.BlockSpec(memory_space=pltpu.VMEM),
        out_shape=jax.ShapeDtypeStruct((rows, cols), v.dtype),
        scratch_shapes=[pltpu.VMEM((8, rows, cols), v.dtype), pltpu.SemaphoreType.DMA((7,)), pltpu.SemaphoreType.DMA((7,))],
        compiler_params=pltpu.CompilerParams(has_side_effects=True, vmem_limit_bytes=V7X_VMEM_LIMIT),
    )(v)


BIG = (("w_in", 1), ("w_uq", 1), ("w_ukv", 1), ("w_o_mla", 0), ("w_o_ssm", 0), ("w_out", 0), ("w_up", 1),
       ("w_down", 0), ("w_ple_gate", 0), ("w_ple", 1))
SHARDED = BIG + (("conv_w", 1),)
SMALL = ("norm_mix_w", "q_a_norm_w", "kv_a_norm_w", "q_norm_w", "k_norm_w", "conv_b", "dt_bias", "a_log", "d_skip",
         "ssm_norm_w", "norm_mlp_w", "ple_norm_w")
WEIGHTS = ("norm_mix_w", "w_in", "q_a_norm_w", "w_uq", "kv_a_norm_w", "w_ukv", "q_norm_w", "k_norm_w", "w_o_mla", "conv_w",
           "conv_b", "dt_bias", "a_log", "d_skip", "ssm_norm_w", "w_o_ssm", "w_out", "norm_mlp_w", "w_up", "w_down",
           "ple_norm_w", "w_ple_gate", "w_ple")


def _to_rows(flat, cols, row_multiple):
    n = flat.shape[-1]
    rows = -(-n // (cols * row_multiple)) * row_multiple
    pad = [(0, 0)] * (flat.ndim - 1) + [(0, rows * cols - n)]
    return jnp.pad(flat, pad).reshape(flat.shape[:-1] + (rows, cols))


def _w_in_ranges(d_model):
    out, lo = {}, 0
    for n, wd in zip(W_IN_PIECES, w_in_widths(d_model)):
        out[n] = (lo, lo + wd)
        lo += wd
    return out


def w_in_pieces(w3):
    _, k, c = w3.shape
    pc = {}
    for n, (lo, hi) in _w_in_ranges(k).items():
        cuts = [w3[j][:, max(lo, j * c) - j * c:min(hi, (j + 1) * c) - j * c]
                for j in range(N_CHIPS) if max(lo, j * c) < min(hi, (j + 1) * c)]
        pc[n] = cuts[0] if len(cuts) == 1 else jnp.concatenate(cuts, axis=1)
    pc["kd"] = jnp.concatenate([pc.pop("kr"), pc.pop("dt")], axis=1)
    return pc


def w_in_shard_grads(g, k, c):
    g = dict(g)
    g["kr"], g["dt"] = g["kd"][:, :QK_ROPE], g["kd"][:, QK_ROPE:]
    shards = []
    for j in range(N_CHIPS):
        cuts = []
        for n, (lo, hi) in _w_in_ranges(k).items():
            a, b = max(lo, j * c), min(hi, (j + 1) * c)
            if a < b:
                cuts.append(g[n][:, a - lo:b - lo])
        shards.append(jnp.concatenate(cuts, axis=1))
    return jnp.stack(shards)


def kernel(x, p, positions, norm_mix_w, w_in, q_a_norm_w, w_uq, kv_a_norm_w, w_ukv, q_norm_w, k_norm_w, w_o_mla, conv_w, conv_b, dt_bias, a_log, d_skip, ssm_norm_w, w_o_ssm, w_out, norm_mlp_w, w_up, w_down, ple_norm_w, w_ple_gate, w_ple, loss_target, m_norm_mix_w, m_w_in, m_q_a_norm_w, m_w_uq, m_kv_a_norm_w, m_w_ukv, m_q_norm_w, m_k_norm_w, m_w_o_mla, m_conv_w, m_conv_b, m_dt_bias, m_a_log, m_d_skip, m_ssm_norm_w, m_w_o_ssm, m_w_out, m_norm_mlp_w, m_w_up, m_w_down, m_ple_norm_w, m_w_ple_gate, m_w_ple, v_norm_mix_w, v_w_in, v_q_a_norm_w, v_w_uq, v_kv_a_norm_w, v_w_ukv, v_q_norm_w, v_k_norm_w, v_w_o_mla, v_conv_w, v_conv_b, v_dt_bias, v_a_log, v_d_skip, v_ssm_norm_w, v_w_o_ssm, v_w_out, v_norm_mlp_w, v_w_up, v_w_down, v_ple_norm_w, v_w_ple_gate, v_w_ple):
    a = dict(locals())
    x, p, pos, target = a["x"][0], a["p"][:, 0], a["positions"][0], a["loss_target"][0]
    depth = a["w_in"].shape[0]
    shard_shapes = {n: tuple(a[n].shape[1:]) for n, _ in SHARDED}
    cos_full, sin_full = rope_tables_full(pos)
    rot = rope_matrix()

    shards = [[a[n][i].astype(BF16) for n, _ in BIG] + [a["conv_w"][i]] for i in range(depth)]
    tok = tuple(jnp.zeros((N_CHIPS, shard_shapes[n][0] // 2, shard_shapes[n][1]), BF16) for n, _ in BIG)
    tok_a = tuple(jnp.zeros(t.shape, F32) for t in tok)
    layer_vjps = []
    got = gather_shards(shards[0], name="gather_weights")
    for i in range(depth):
        full = dict(zip([n for n, _ in SHARDED], got))
        w_i = w_in_pieces(full["w_in"])
        for n, ax in BIG[1:]:
            w_i[n] = full[n] if ax == 1 else full[n].reshape((-1, full[n].shape[-1]))
        sm_i = {n: a[n][i] for n in SMALL}
        sm_i["conv_w"] = full["conv_w"].transpose(1, 0, 2).reshape(CONV_WIDTH, -1)
        sinks_i = {n: jnp.zeros(w_i[n].shape, F32) for n in LINEAR_NAMES}
        last = i == depth - 1
        f_i = functools.partial(layer_forward, p_i=p[i], cos_full=cos_full, sin_full=sin_full, rot=rot, wb=w_i,
                                next_shards=() if last else shards[i + 1])
        g_i = lambda x, sm, sk, tk, ta, f=f_i: f(x, sm=sm, sinks=sk, tok=tk, tok_a=ta)
        (x, _), vjp_i, partial = jax.vjp(g_i, x, sm_i, sinks_i, () if last else tok, () if last else tok_a, has_aux=True)
        layer_vjps.append(vjp_i)
        if not last:
            got = gather_finish(partial, name="gather_finish")
    dx, loss_part = loss_and_cotangent(x, target)
    loss = lax.psum(loss_part, ("x", "y", "c"))

    per_layer, d_small, pending = [None] * depth, [None] * depth, None
    for i in reversed(range(depth)):
        dx, d_small[i], d_sinks_i, got_b, got_a = layer_vjps[i]((dx, tuple(pending) if pending is not None else ()))
        if pending is not None:
            per_layer[i + 1] = reduce_finish(pending, list(got_a), list(got_b), name="reduce_grads")
        pending = [w_in_shard_grads(d_sinks_i, *shard_shapes["w_in"])]
        pending += [d_sinks_i[n].reshape((N_CHIPS,) + shard_shapes[n]) for n, _ in BIG[1:]]
    per_layer[0] = reduce_to_owner(pending, name="reduce_grads")
    grads = {n: jnp.stack([per_layer[i][t] for i in range(depth)]) for t, (n, _) in enumerate(BIG)}

    small_names = SMALL + ("conv_w",)
    flat = jnp.concatenate([d_small[i][n].reshape(-1) for i in range(depth) for n in small_names])
    n_small = flat.shape[0]
    red = allreduce_small(_to_rows(flat, 128, 8), name="reduce_small").reshape(-1)[:n_small]
    per = n_small // depth
    off = 0
    for n in SMALL:
        width = a[n].shape[-1]
        grads[n] = jnp.stack([red[i * per + off:i * per + off + width] for i in range(depth)])
        off += width
    conv_c = shard_shapes["conv_w"][1]
    conv_full = jnp.stack([red[i * per + off:i * per + off + CONV_WIDTH * N_CHIPS * conv_c] for i in range(depth)])
    chip = 2 * lax.axis_index("x") + lax.axis_index("y")
    grads["conv_w"] = lax.dynamic_index_in_dim(conv_full.reshape(depth, CONV_WIDTH, N_CHIPS, conv_c), chip, axis=2,
                                               keepdims=False)

    deltas, new_m, new_v = {}, {}, {}
    two_d = lambda t: t.reshape(-1, t.shape[-1])
    for n in WEIGHTS:
        d, m, v = adamw(two_d(a[n]), two_d(grads[n]), two_d(a["m_" + n]), two_d(a["v_" + n]), name="adamw")
        deltas[n], new_m[n], new_v[n] = d.reshape(a[n].shape), m.reshape(a[n].shape), v.reshape(a[n].shape)

    return (loss, dx[None], *[grads[n].reshape(a[n].shape) for n in WEIGHTS], *[deltas[n] for n in WEIGHTS],
            *[new_m[n] for n in WEIGHTS], *[new_v[n] for n in WEIGHTS])
```
